```python
import math
import jax, jax.numpy as jnp
from jax import lax
import numpy as np

D_MODEL = 1024
BATCH = 4
SEQ = 8192
DEPTH = 1
DEC_BATCH = 128
DEC_SEQ = 1
PAST_LEN = 16384
PAGE_SIZE = 128

D_MIX = D_MODEL
HEAD_DIM = 64
N_Q_HEADS = (D_MIX // 2) // HEAD_DIM
N_KV_HEADS = 2
Q_PER_KV = N_Q_HEADS // N_KV_HEADS
WINDOW = 128
LRU_WIDTH = D_MIX - N_Q_HEADS * HEAD_DIM
LRU_BLOCKS = 8
LRU_BLOCK_DIM = LRU_WIDTH // LRU_BLOCKS
CONV_WIDTH = 4
RG_C = 8.0
N_EXPERTS = 32
TOP_K = 4
D_FF = D_MODEL
SWIGLU_LIMIT = 7.0
SWIGLU_ALPHA = 1.702
MOE_BLOCK = 128
LN_EPS = 1e-5
DN_ALPHA = (2 * DEPTH) ** 0.25
DN_BETA = (8 * DEPTH) ** -0.25

Q_COLS = N_Q_HEADS * HEAD_DIM
KV_COLS = N_KV_HEADS * HEAD_DIM
D_IN = Q_COLS + 2 * KV_COLS + 2 * LRU_WIDTH
NEG_INF = -1e30

kernel_name = 'hymba_swa_rglru_moe_step'


def layer_norm(x, g, b):
    xf = x.astype(jnp.float32)
    mu = jnp.mean(xf, axis=-1, keepdims=True)
    var = jnp.mean(jnp.square(xf - mu), axis=-1, keepdims=True)
    y = (xf - mu) * lax.rsqrt(var + LN_EPS) * g.astype(jnp.float32) + b.astype(jnp.float32)
    return y.astype(x.dtype)


def alibi_slopes():
    return jnp.exp2(-8.0 * jnp.arange(1, N_Q_HEADS + 1, dtype=jnp.float32) / N_Q_HEADS)


def sliding_window_attention(q, k, v, k_prev, v_prev, prev_valid, sinks):
    B, S = q.shape[0], q.shape[1]
    bq = WINDOW if S >= WINDOW else S
    nb = -(-S // bq)
    pad = nb * bq - S
    padw = ((0, 0), (0, pad), (0, 0), (0, 0))
    q = jnp.pad(q, padw)
    k_ext = jnp.concatenate([k_prev.astype(k.dtype), jnp.pad(k, padw)], axis=1)
    v_ext = jnp.concatenate([v_prev.astype(v.dtype), jnp.pad(v, padw)], axis=1)
    kb = bq + WINDOW
    key_idx = (jnp.arange(nb) * bq)[:, None] + jnp.arange(kb)[None, :]
    k_blk = k_ext[:, key_idx]
    v_blk = v_ext[:, key_idx]
    q_blk = q.reshape(B, nb, bq, N_KV_HEADS, Q_PER_KV, HEAD_DIM)
    scores = jnp.einsum('bnqhgd,bnkhd->bnhgqk', q_blk, k_blk).astype(jnp.float32) * (HEAD_DIM ** -0.5)
    t_pos = (jnp.arange(nb) * bq)[:, None] + jnp.arange(bq)[None, :]
    s_pos = key_idx - WINDOW
    dist = t_pos[:, :, None] - s_pos[:, None, :]
    valid = (dist >= 0) & (dist <= WINDOW)
    if not prev_valid:
        valid = valid & (s_pos[:, None, :] >= 0)
    slopes = alibi_slopes().reshape(N_KV_HEADS, Q_PER_KV)
    alibi = -slopes[None, :, :, None, None] * dist[:, None, None].astype(jnp.float32)
    scores = jnp.where(valid[:, None, None], scores + alibi, NEG_INF)
    sink = sinks.astype(jnp.float32).reshape(N_KV_HEADS, Q_PER_KV)[None, None, :, :, None, None]
    m = jnp.maximum(jnp.max(scores, axis=-1, keepdims=True), sink)
    p = jnp.exp(scores - m)
    probs = p / (jnp.sum(p, axis=-1, keepdims=True) + jnp.exp(sink - m))
    out = jnp.einsum('bnhgqk,bnkhd->bnqhgd', probs.astype(v.dtype), v_blk)
    return out.reshape(B, nb * bq, Q_COLS)[:, :S]


def rglru_branch(u_x, u_gate, conv_prev, h0, conv_w, conv_b, w_a, b_a, w_i, b_i, lam):
    B, S = u_x.shape[0], u_x.shape[1]
    ext = jnp.concatenate([conv_prev.astype(u_x.dtype), u_x], axis=1)
    xc = conv_b
    for tap in range(CONV_WIDTH):
        xc = xc + ext[:, tap:tap + S] * conv_w[tap]
    new_conv = ext[:, -(CONV_WIDTH - 1):]
    xh = xc.reshape(B, S, LRU_BLOCKS, LRU_BLOCK_DIM)
    r = jax.nn.sigmoid(jnp.einsum('bshi,hij->bshj', xh, w_a) + b_a).reshape(B, S, LRU_WIDTH)
    gi = jax.nn.sigmoid(jnp.einsum('bshi,hij->bshj', xh, w_i) + b_i).reshape(B, S, LRU_WIDTH)
    log_a = -RG_C * r.astype(jnp.float32) * jax.nn.softplus(-lam.astype(jnp.float32))
    a = jnp.exp(log_a)
    mult = jnp.sqrt(jnp.maximum(-jnp.expm1(2.0 * log_a), 0.0))
    b = mult * (gi * xc).astype(jnp.float32)
    b = b.at[:, 0].add(a[:, 0] * h0.astype(jnp.float32))

    def combine(left, right):
        a_l, b_l = left
        a_r, b_r = right
        return a_l * a_r, a_r * b_l + b_r

    _, h = lax.associative_scan(combine, (a, b), axis=1)
    y = h.astype(u_x.dtype) * jax.nn.gelu(u_gate)
    return y, h[:, -1].astype(h0.dtype), new_conv


def moe_ffn(x, router_w, router_b, w_gu, b_gu, w_dn, b_dn):
    T = x.shape[0]
    logits = (x @ router_w).astype(jnp.float32) + router_b.astype(jnp.float32)
    top_v, top_e = lax.top_k(logits, TOP_K)
    gates = jax.nn.softmax(top_v, axis=-1)
    M = T * TOP_K
    e_flat = top_e.reshape(M)
    tok_flat = jnp.broadcast_to(jnp.arange(T, dtype=jnp.int32)[:, None], (T, TOP_K)).reshape(M)
    g_flat = gates.reshape(M)
    order = jnp.argsort(e_flat)
    e_sorted = e_flat[order]
    counts = jnp.bincount(e_flat, length=N_EXPERTS)
    padded = (counts + MOE_BLOCK - 1) // MOE_BLOCK * MOE_BLOCK
    pad_end = jnp.cumsum(padded)
    pad_start = pad_end - padded
    start = jnp.cumsum(counts) - counts
    dest = pad_start[e_sorted] + (jnp.arange(M) - start[e_sorted])
    n_blocks = -(-M // MOE_BLOCK) + N_EXPERTS
    P = n_blocks * MOE_BLOCK
    slot_tok = jnp.zeros((P,), jnp.int32).at[dest].set(tok_flat[order])
    slot_gate = jnp.zeros((P,), jnp.float32).at[dest].set(g_flat[order])
    block_e = jnp.minimum(jnp.searchsorted(pad_end, jnp.arange(n_blocks) * MOE_BLOCK, side='right'),
                          N_EXPERTS - 1)
    xb = x[slot_tok].reshape(n_blocks, MOE_BLOCK, x.shape[1])

    def expert_block(args):
        xe, e = args
        hgu = xe @ w_gu[e] + b_gu[e]
        glu = jnp.minimum(hgu[:, :D_FF], SWIGLU_LIMIT)
        lin = jnp.clip(hgu[:, D_FF:], -SWIGLU_LIMIT, SWIGLU_LIMIT)
        act = glu * jax.nn.sigmoid(SWIGLU_ALPHA * glu) * (lin + 1.0)
        return act @ w_dn[e] + b_dn[e]

    yb = lax.map(expert_block, (xb, block_e)).reshape(P, x.shape[1])
    return jnp.zeros_like(x).at[slot_tok].add(yb * slot_gate[:, None].astype(yb.dtype))


def decoder_layer(x, win_k, win_v, lru_h, conv_buf, prev_valid,
                  w_in, b_in, sinks, conv_w, conv_b, w_a, b_a, w_i, b_i, lam,
                  w_out, b_out, ln1_g, ln1_b, router_w, router_b, w_gu, b_gu, w_dn, b_dn, ln2_g, ln2_b):
    B, S, D = x.shape
    proj = x @ w_in + b_in
    q, k, v, u_x, u_gate = jnp.split(
        proj, [Q_COLS, Q_COLS + KV_COLS, Q_COLS + 2 * KV_COLS, Q_COLS + 2 * KV_COLS + LRU_WIDTH], axis=-1)
    q = q.reshape(B, S, N_Q_HEADS, HEAD_DIM)
    k = k.reshape(B, S, N_KV_HEADS, HEAD_DIM)
    v = v.reshape(B, S, N_KV_HEADS, HEAD_DIM)
    attn = sliding_window_attention(q, k, v, win_k, win_v, prev_valid, sinks)
    rnn, new_h, new_conv = rglru_branch(u_x, u_gate, conv_buf, lru_h, conv_w, conv_b, w_a, b_a, w_i, b_i, lam)
    mix = jnp.concatenate([attn, rnn], axis=-1) @ w_out + b_out
    h = layer_norm(DN_ALPHA * x + mix, ln1_g, ln1_b)
    ff = moe_ffn(h.reshape(B * S, D), router_w, router_b, w_gu, b_gu, w_dn, b_dn).reshape(B, S, D)
    y = layer_norm(DN_ALPHA * h + ff, ln2_g, ln2_b)
    new_k = jnp.concatenate([win_k.astype(k.dtype), k], axis=1)[:, -WINDOW:]
    new_v = jnp.concatenate([win_v.astype(v.dtype), v], axis=1)[:, -WINDOW:]
    return y, new_k, new_v, new_h, new_conv


def setup_inputs(seed: int = 0) -> dict:
    key = jax.random.key(seed)
    ks = jax.random.split(key, 32)
    f32 = jnp.float32

    def nrm(k, shape, scale):
        return jax.random.normal(k, shape, f32) * scale

    win_len = min(WINDOW, PAST_LEN)
    col_scale = jnp.concatenate([jnp.ones((Q_COLS + KV_COLS,), f32), jnp.full((KV_COLS,), DN_BETA, f32),
                                 jnp.ones((2 * LRU_WIDTH,), f32)])
    a0 = jax.random.uniform(ks[14], (DEPTH, LRU_WIDTH), f32, minval=0.9, maxval=0.999)
    sig = a0 ** (1.0 / RG_C)
    lru_lambda = jnp.log(sig) - jnp.log1p(-sig)
    return {
        'x_prompt': nrm(ks[0], (BATCH, SEQ, D_MODEL), 1.0),
        'x_sample': nrm(ks[1], (DEC_BATCH, DEC_SEQ, D_MODEL), 1.0),
        'cache_win_k': nrm(ks[2], (DEPTH, DEC_BATCH, win_len, N_KV_HEADS, HEAD_DIM), 1.0),
        'cache_win_v': nrm(ks[3], (DEPTH, DEC_BATCH, win_len, N_KV_HEADS, HEAD_DIM), DN_BETA),
        'state_lru_h': nrm(ks[4], (DEPTH, DEC_BATCH, LRU_WIDTH), 0.5),
        'state_conv': nrm(ks[5], (DEPTH, DEC_BATCH, CONV_WIDTH - 1, LRU_WIDTH), 1.0),
        'w_in': nrm(ks[6], (DEPTH, D_MODEL, D_IN), D_MODEL ** -0.5) * col_scale,
        'b_in': nrm(ks[7], (DEPTH, D_IN), 0.01),
        'attn_sinks': nrm(ks[8], (DEPTH, N_Q_HEADS), 0.5),
        'conv_w': nrm(ks[9], (DEPTH, CONV_WIDTH, LRU_WIDTH), CONV_WIDTH ** -0.5),
        'conv_b': nrm(ks[10], (DEPTH, LRU_WIDTH), 0.01),
        'lru_w_a': nrm(ks[11], (DEPTH, LRU_BLOCKS, LRU_BLOCK_DIM, LRU_BLOCK_DIM), LRU_BLOCK_DIM ** -0.5),
        'lru_b_a': nrm(ks[12], (DEPTH, LRU_BLOCKS, LRU_BLOCK_DIM), 0.01),
        'lru_w_i': nrm(ks[13], (DEPTH, LRU_BLOCKS, LRU_BLOCK_DIM, LRU_BLOCK_DIM), LRU_BLOCK_DIM ** -0.5),
        'lru_b_i': nrm(ks[15], (DEPTH, LRU_BLOCKS, LRU_BLOCK_DIM), 0.01),
        'lru_lambda': lru_lambda,
        'w_out': nrm(ks[16], (DEPTH, D_MIX, D_MODEL), DN_BETA * D_MIX ** -0.5),
        'b_out': nrm(ks[17], (DEPTH, D_MODEL), 0.01),
        'ln1_g': 1.0 + nrm(ks[18], (DEPTH, D_MODEL), 0.02),
        'ln1_b': nrm(ks[19], (DEPTH, D_MODEL), 0.02),
        'router_w': nrm(ks[20], (DEPTH, D_MODEL, N_EXPERTS), D_MODEL ** -0.5),
        'router_b': nrm(ks[21], (DEPTH, N_EXPERTS), 0.01),
        'w_gate_up': nrm(ks[22], (DEPTH, N_EXPERTS, D_MODEL, 2 * D_FF), DN_BETA * D_MODEL ** -0.5),
        'b_gate_up': nrm(ks[23], (DEPTH, N_EXPERTS, 2 * D_FF), 0.01),
        'w_down': nrm(ks[24], (DEPTH, N_EXPERTS, D_FF, D_MODEL), DN_BETA * D_FF ** -0.5),
        'b_down': nrm(ks[25], (DEPTH, N_EXPERTS, D_MODEL), 0.01),
        'ln2_g': 1.0 + nrm(ks[26], (DEPTH, D_MODEL), 0.02),
        'ln2_b': nrm(ks[27], (DEPTH, D_MODEL), 0.02),
    }


def reference(x_prompt, x_sample, cache_win_k, cache_win_v, state_lru_h, state_conv,
              w_in, b_in, attn_sinks, conv_w, conv_b, lru_w_a, lru_b_a, lru_w_i, lru_b_i, lru_lambda,
              w_out, b_out, ln1_g, ln1_b, router_w, router_b, w_gate_up, b_gate_up, w_down, b_down,
              ln2_g, ln2_b):
    yp, ys = x_prompt, x_sample
    bp = x_prompt.shape[0]
    pk, pv, ph, pc = [], [], [], []
    sk, sv, sh, sc = [], [], [], []
    for l in range(DEPTH):
        lw = (w_in[l], b_in[l], attn_sinks[l], conv_w[l], conv_b[l], lru_w_a[l], lru_b_a[l],
              lru_w_i[l], lru_b_i[l], lru_lambda[l], w_out[l], b_out[l], ln1_g[l], ln1_b[l],
              router_w[l], router_b[l], w_gate_up[l], b_gate_up[l], w_down[l], b_down[l], ln2_g[l], ln2_b[l])
        zero_kv = jnp.zeros((bp, WINDOW, N_KV_HEADS, HEAD_DIM), x_prompt.dtype)
        zero_h = jnp.zeros((bp, LRU_WIDTH), state_lru_h.dtype)
        zero_conv = jnp.zeros((bp, CONV_WIDTH - 1, LRU_WIDTH), x_prompt.dtype)
        yp, k1, v1, h1, c1 = decoder_layer(yp, zero_kv, zero_kv, zero_h, zero_conv, False, *lw)
        ys, k2, v2, h2, c2 = decoder_layer(ys, cache_win_k[l], cache_win_v[l], state_lru_h[l], state_conv[l],
                                           True, *lw)
        pk.append(k1); pv.append(v1); ph.append(h1); pc.append(c1)
        sk.append(k2); sv.append(v2); sh.append(h2); sc.append(c2)
    return (yp, ys, jnp.stack(pk), jnp.stack(pv), jnp.stack(ph), jnp.stack(pc),
            jnp.stack(sk), jnp.stack(sv), jnp.stack(sh), jnp.stack(sc))
```

```python
import functools

import jax
import jax.numpy as jnp
import numpy as np
from jax import lax
from jax.experimental import pallas as pl
from jax.experimental.pallas import tpu as pltpu

F32 = jnp.float32
BF16 = jnp.bfloat16

D_MODEL = 1024
N_Q_HEADS = 8
N_KV_HEADS = 2
HEAD_DIM = 64
WINDOW = 128
Q_COLS = N_Q_HEADS * HEAD_DIM
KV_COLS = N_KV_HEADS * HEAD_DIM
LRU_WIDTH = 512
LRU_BLOCKS = 8
CONV_WIDTH = 4
RG_C = 8.0
N_EXPERTS = 32
TOP_K = 4
D_FF = 1024
SWIGLU_LIMIT = 7.0
SWIGLU_ALPHA = 1.702
LN_EPS = 1e-5
DN_ALPHA = 2.0 ** 0.25
NEG_INF = -1e30
D_IN = Q_COLS + 2 * KV_COLS + 2 * LRU_WIDTH

LANES = 128
SUBLANES = 8
MXU_DIM = 256

TILE = 256
SEG_ALIGN = SUBLANES
ROW_BLOCK = 256
HEAD_GROUPS = N_Q_HEADS // N_KV_HEADS
VMEM_LIMIT = 56 * 1024 * 1024


def _layer_norm(z, g, b):
    mu = jnp.mean(z, axis=-1, keepdims=True)
    zc = z - mu
    var = jnp.mean(zc * zc, axis=-1, keepdims=True)
    return zc * lax.rsqrt(var + LN_EPS) * g + b


def _softplus(x):
    return jnp.maximum(x, 0.0) + jnp.log1p(jnp.exp(-jnp.abs(x)))


def _gelu_tanh(x):
    c = np.float32(np.sqrt(2.0 / np.pi))
    return 0.5 * x * (1.0 + jnp.tanh(c * (x + 0.044715 * (x * x * x))))


def _rglru_gates(xc, wa_ref, ba_ref, wi_ref, bi_ref, lam_ref):
    xcb = xc.astype(BF16)
    half = LRU_WIDTH // 2
    pre_a = jnp.concatenate(
        [jnp.dot(xcb[:, :half], wa_ref[0], preferred_element_type=F32),
         jnp.dot(xcb[:, half:], wa_ref[1], preferred_element_type=F32)], axis=1)
    pre_i = jnp.concatenate(
        [jnp.dot(xcb[:, :half], wi_ref[0], preferred_element_type=F32),
         jnp.dot(xcb[:, half:], wi_ref[1], preferred_element_type=F32)], axis=1)
    r = jax.nn.sigmoid(pre_a + ba_ref[...])
    gi = jax.nn.sigmoid(pre_i + bi_ref[...])
    log_a = (-RG_C) * r * _softplus(-lam_ref[...])
    a = jnp.exp(log_a)
    t = jnp.tanh(log_a)
    mult = jnp.sqrt(jnp.maximum(-2.0 * t / (1.0 - t), 0.0))
    return a, mult * (gi * xc)


def _route(h1, rw_ref, rb_ref):
    n = h1.shape[0]
    logits = lax.dot_general(rw_ref[...], h1.astype(BF16), (((1,), (1,)), ((), ())),
                             preferred_element_type=F32) + rb_ref[...]
    eidx = lax.broadcasted_iota(jnp.int32, (N_EXPERTS, n), 0)
    vals = logits
    top_v, top_e = [], []
    for _ in range(TOP_K):
        m = jnp.max(vals, axis=0, keepdims=True)
        idx = jnp.min(jnp.where(vals == m, eidx, N_EXPERTS), axis=0, keepdims=True)
        top_v.append(m)
        top_e.append(idx)
        vals = jnp.where(eidx == idx, -jnp.inf, vals)
    ex = [jnp.exp(v - top_v[0]) for v in top_v]
    den = ex[0] + ex[1] + ex[2] + ex[3]
    gates = jnp.concatenate([e / den for e in ex], axis=0)
    te = jnp.concatenate(top_e, axis=0)
    mem = jnp.zeros((N_EXPERTS, n), jnp.int32)
    for idx in top_e:
        mem = mem + jnp.where(eidx == idx, 1, 0)
    return te, gates, jnp.sum(mem, axis=1, keepdims=True)


def _post_mix(x, mix_in, w_out_ref, b_out_ref, g_ref, b_ref):
    mix = jnp.dot(mix_in.astype(BF16), w_out_ref[...], preferred_element_type=F32) + b_out_ref[...]
    return _layer_norm(DN_ALPHA * x + mix, g_ref[...], b_ref[...])


def _mixer_prompt_kernel(sinks_ref, x_ref, w_in_ref, b_in_ref, conv_w_ref, conv_b_ref,
                         wa_ref, ba_ref, wi_ref, bi_ref, lam_ref, w_out_ref, b_out_ref,
                         ln_g_ref, ln_b_ref, rw_ref, rb_ref,
                         h1_ref, te_ref, tg_ref, cnt_ref, wk_ref, wv_ref, lh_ref, cv_ref,
                         kext, vext, uext, hcar, attn_buf):
    j = pl.program_id(1)
    nj = pl.num_programs(1)
    R = TILE

    @pl.when(j == 0)
    def _():
        kext[0:WINDOW, :] = jnp.zeros((WINDOW, KV_COLS), F32)
        vext[0:WINDOW, :] = jnp.zeros((WINDOW, KV_COLS), F32)
        uext[0:SUBLANES, :] = jnp.zeros((SUBLANES, LRU_WIDTH), F32)
        hcar[...] = jnp.zeros((1, LRU_WIDTH), F32)

    x = x_ref[0]
    proj = jnp.dot(x.astype(BF16), w_in_ref[...], preferred_element_type=F32) + b_in_ref[...]
    q = proj[:, :Q_COLS]
    kext[WINDOW:WINDOW + R, :] = proj[:, Q_COLS:Q_COLS + KV_COLS]
    vext[WINDOW:WINDOW + R, :] = proj[:, Q_COLS + KV_COLS:Q_COLS + 2 * KV_COLS]
    ux = proj[:, Q_COLS + 2 * KV_COLS:Q_COLS + 2 * KV_COLS + LRU_WIDTH]
    ug = proj[:, Q_COLS + 2 * KV_COLS + LRU_WIDTH:]
    uext[SUBLANES:SUBLANES + R, :] = ux

    row2 = lax.broadcasted_iota(jnp.int32, (2 * WINDOW, 2 * WINDOW), 0)
    col2 = lax.broadcasted_iota(jnp.int32, (2 * WINDOW, 2 * WINDOW), 1)
    dist = (row2 & (WINDOW - 1)) - col2 + WINDOW
    valid = (dist >= 0) & (dist <= WINDOW)
    distf = dist.astype(F32)
    upper = row2 >= WINDOW
    lane = lax.broadcasted_iota(jnp.int32, (WINDOW, LANES), 1)
    lo_lane = lane < HEAD_DIM
    for s in range(R // WINDOW):
        kk = kext[s * WINDOW:s * WINDOW + 2 * WINDOW, :].astype(BF16)
        vv = vext[s * WINDOW:s * WINDOW + 2 * WINDOW, :].astype(BF16)
        ok = valid
        if s == 0:
            ok = ok & ((col2 >= WINDOW) | (j > 0))
        for g in range(HEAD_GROUPS):
            qg = q[s * WINDOW:(s + 1) * WINDOW, g * LANES:(g + 1) * LANES]
            q2 = jnp.concatenate([jnp.where(lo_lane, qg, 0.0), jnp.where(lo_lane, 0.0, qg)], axis=0)
            sc = lax.dot_general(q2.astype(BF16), kk, (((1,), (1,)), ((), ())),
                                 preferred_element_type=F32) * (HEAD_DIM ** -0.5)
            slope = jnp.where(upper, 2.0 ** -(g + HEAD_GROUPS + 1), 2.0 ** -(g + 1))
            sc = jnp.where(ok, sc - slope * distf, NEG_INF)
            sink = jnp.where(upper[:, 0:1], sinks_ref[g + HEAD_GROUPS], sinks_ref[g])
            m = jnp.maximum(jnp.max(sc, axis=-1, keepdims=True), sink)
            p = jnp.exp(sc - m)
            den = jnp.sum(p, axis=-1, keepdims=True) + jnp.exp(sink - m)
            o = jnp.dot(p.astype(BF16), vv, preferred_element_type=F32) / den
            attn_buf[s * WINDOW:(s + 1) * WINDOW, g * LANES:(g + 1) * LANES] = jnp.where(
                lo_lane, o[:WINDOW], o[WINDOW:])

    xc = conv_b_ref[...]
    for tap in range(CONV_WIDTH):
        off = SUBLANES - (CONV_WIDTH - 1) + tap
        xc = xc + uext[off:off + R, :] * conv_w_ref[tap:tap + 1, :]
    a, bv = _rglru_gates(xc, wa_ref, ba_ref, wi_ref, bi_ref, lam_ref)
    rows = lax.broadcasted_iota(jnp.int32, (R, LRU_WIDTH), 0)
    d = 1
    while d < R:
        keep = rows >= d
        a_s = pltpu.roll(a, d, 0)
        b_s = pltpu.roll(bv, d, 0)
        bv = jnp.where(keep, a * b_s + bv, bv)
        a = jnp.where(keep, a * a_s, a)
        d *= 2
    h = a * hcar[...] + bv
    hcar[...] = h[R - 1:R, :]
    rnn = h * _gelu_tanh(ug)

    mix_in = jnp.concatenate([attn_buf[...], rnn], axis=1)
    h1 = _post_mix(x, mix_in, w_out_ref, b_out_ref, ln_g_ref, ln_b_ref)
    h1_ref[0] = h1
    te, tg, cnt = _route(h1, rw_ref, rb_ref)
    te_ref[0] = te
    tg_ref[0] = tg
    cnt_ref[0] = cnt

    @pl.when(j == nj - 1)
    def _():
        wk_ref[0] = kext[R:R + WINDOW, :]
        wv_ref[0] = vext[R:R + WINDOW, :]
        lh_ref[0] = h[R - 1:R, :]
        cv_ref[0] = uext[SUBLANES + R - (CONV_WIDTH - 1):SUBLANES + R, :]

    kext[0:WINDOW, :] = kext[R:R + WINDOW, :]
    vext[0:WINDOW, :] = vext[R:R + WINDOW, :]
    uext[0:SUBLANES, :] = uext[R:R + SUBLANES, :]


def _const_spec(shape):
    return pl.BlockSpec(shape, lambda *_: (0,) * len(shape))


def _mixer_prompt(x, sinks, wts):
    B, S, _ = x.shape
    nj = S // TILE
    n_tiles = B * nj
    tile_idx = lambda b, j, *_: (b * nj + j, 0, 0)
    batch_idx = lambda b, j, *_: (b, 0, 0)
    in_specs = [pl.BlockSpec((1, TILE, D_MODEL), lambda b, j, *_: (b, j, 0))]
    in_specs += [_const_spec(w.shape) for w in wts]
    out_shape = (
        jax.ShapeDtypeStruct((B, S, D_MODEL), F32),
        jax.ShapeDtypeStruct((n_tiles, TOP_K, TILE), jnp.int32),
        jax.ShapeDtypeStruct((n_tiles, TOP_K, TILE), F32),
        jax.ShapeDtypeStruct((n_tiles, N_EXPERTS, 1), jnp.int32),
        jax.ShapeDtypeStruct((B, WINDOW, KV_COLS), F32),
        jax.ShapeDtypeStruct((B, WINDOW, KV_COLS), F32),
        jax.ShapeDtypeStruct((B, 1, LRU_WIDTH), F32),
        jax.ShapeDtypeStruct((B, CONV_WIDTH - 1, LRU_WIDTH), F32),
    )
    out_specs = (
        pl.BlockSpec((1, TILE, D_MODEL), lambda b, j, *_: (b, j, 0)),
        pl.BlockSpec((1, TOP_K, TILE), tile_idx),
        pl.BlockSpec((1, TOP_K, TILE), tile_idx),
        pl.BlockSpec((1, N_EXPERTS, 1), tile_idx),
        pl.BlockSpec((1, WINDOW, KV_COLS), batch_idx),
        pl.BlockSpec((1, WINDOW, KV_COLS), batch_idx),
        pl.BlockSpec((1, 1, LRU_WIDTH), batch_idx),
        pl.BlockSpec((1, CONV_WIDTH - 1, LRU_WIDTH), batch_idx),
    )
    scratch = [
        pltpu.VMEM((TILE + WINDOW, KV_COLS), F32),
        pltpu.VMEM((TILE + WINDOW, KV_COLS), F32),
        pltpu.VMEM((TILE + SUBLANES, LRU_WIDTH), F32),
        pltpu.VMEM((1, LRU_WIDTH), F32),
        pltpu.VMEM((TILE, Q_COLS), F32),
    ]
    return pl.pallas_call(
        _mixer_prompt_kernel,
        grid_spec=pltpu.PrefetchScalarGridSpec(
            num_scalar_prefetch=1, grid=(B, nj), in_specs=in_specs, out_specs=out_specs,
            scratch_shapes=scratch),
        out_shape=out_shape,
        compiler_params=pltpu.CompilerParams(
            dimension_semantics=("arbitrary", "arbitrary"), vmem_limit_bytes=VMEM_LIMIT),
        name="mixer_prompt",
    )(sinks, x, *wts)


SEQ_CHUNK = 16


def _mixer_sample_kernel(sinks_ref, x_ref, ck_ref, cv_ref, h0_ref, cprev_ref,
                         w_in_ref, b_in_ref, conv_w_ref, conv_b_ref,
                         wa_ref, ba_ref, wi_ref, bi_ref, lam_ref, w_out_ref, b_out_ref,
                         ln_g_ref, ln_b_ref, rw_ref, rb_ref,
                         h1_ref, te_ref, tg_ref, cnt_ref, wk_ref, wv_ref, lh_ref, cnew_ref,
                         proj_s, attn_s, attn_c):
    c = pl.program_id(0)
    nc = pl.num_programs(0)
    nseq = x_ref.shape[0]

    @pl.when(c == 0)
    def _():
        proj_s[...] = jnp.dot(x_ref[...].astype(BF16), w_in_ref[...],
                              preferred_element_type=F32) + b_in_ref[...]

    sub = lax.broadcasted_iota(jnp.int32, (N_Q_HEADS, LANES), 0)
    lane = lax.broadcasted_iota(jnp.int32, (N_Q_HEADS, LANES), 1)
    own_half = (lane < HEAD_DIM) == (sub < HEAD_GROUPS)
    sub1 = sub[:, 0:1]
    slope = jnp.zeros((N_Q_HEADS, 1), F32)
    sink = jnp.zeros((N_Q_HEADS, 1), F32)
    for hd in range(N_Q_HEADS):
        slope = jnp.where(sub1 == hd, 2.0 ** -(hd + 1), slope)
        sink = jnp.where(sub1 == hd, sinks_ref[hd], sink)
    dist = (WINDOW - lax.broadcasted_iota(jnp.int32, (1, WINDOW), 1)).astype(F32)
    lo_row = lax.broadcasted_iota(jnp.int32, (1, LANES), 1) < HEAD_DIM
    scale = HEAD_DIM ** -0.5

    for i in range(SEQ_CHUNK):
        b = c * SEQ_CHUNK + i
        prow = proj_s[pl.ds(b, 1), :]
        q8 = jnp.zeros((N_Q_HEADS, LANES), F32)
        for g in range(HEAD_GROUPS):
            qg = jnp.broadcast_to(prow[:, g * LANES:(g + 1) * LANES], (N_Q_HEADS, LANES))
            q8 = jnp.where(((sub & (HEAD_GROUPS - 1)) == g) & own_half, qg, q8)
        k_new = prow[:, Q_COLS:Q_COLS + KV_COLS]
        v_new = prow[:, Q_COLS + KV_COLS:Q_COLS + 2 * KV_COLS]
        kb = ck_ref[i]
        vb = cv_ref[i]
        sc = lax.dot_general(q8.astype(BF16), kb.astype(BF16), (((1,), (1,)), ((), ())),
                             preferred_element_type=F32) * scale - slope * dist
        sc_new = jnp.sum(q8 * k_new, axis=-1, keepdims=True) * scale
        m = jnp.maximum(jnp.maximum(jnp.max(sc, axis=-1, keepdims=True), sc_new), sink)
        p = jnp.exp(sc - m)
        p_new = jnp.exp(sc_new - m)
        den = jnp.sum(p, axis=-1, keepdims=True) + p_new + jnp.exp(sink - m)
        o = (jnp.dot(p.astype(BF16), vb.astype(BF16), preferred_element_type=F32)
             + p_new * v_new) / den
        for g in range(HEAD_GROUPS):
            attn_c[i:i + 1, g * LANES:(g + 1) * LANES] = jnp.where(
                lo_row, o[g:g + 1, :], o[g + HEAD_GROUPS:g + HEAD_GROUPS + 1, :])
        wk_ref[i, 0:WINDOW - 1, :] = ck_ref[i, 1:WINDOW, :]
        wk_ref[i, WINDOW - 1:WINDOW, :] = k_new
        wv_ref[i, 0:WINDOW - 1, :] = cv_ref[i, 1:WINDOW, :]
        wv_ref[i, WINDOW - 1:WINDOW, :] = v_new
    attn_s[pl.ds(pl.multiple_of(c * SEQ_CHUNK, SEQ_CHUNK), SEQ_CHUNK), :] = attn_c[...]

    @pl.when(c == nc - 1)
    def _():
        x = x_ref[...]
        ux = proj_s[:, Q_COLS + 2 * KV_COLS:Q_COLS + 2 * KV_COLS + LRU_WIDTH]
        ug = proj_s[:, Q_COLS + 2 * KV_COLS + LRU_WIDTH:]
        xc = conv_b_ref[...]
        for tap in range(CONV_WIDTH - 1):
            xc = xc + cprev_ref[tap] * conv_w_ref[tap:tap + 1, :]
        xc = xc + ux * conv_w_ref[CONV_WIDTH - 1:CONV_WIDTH, :]
        a, bv = _rglru_gates(xc, wa_ref, ba_ref, wi_ref, bi_ref, lam_ref)
        h = a * h0_ref[...] + bv
        rnn = h * _gelu_tanh(ug)
        mix_in = jnp.concatenate([attn_s[...], rnn], axis=1)
        h1 = _post_mix(x, mix_in, w_out_ref, b_out_ref, ln_g_ref, ln_b_ref)
        h1_ref[...] = h1
        te, tg, cnt = _route(h1, rw_ref, rb_ref)
        cnt_ref[0] = cnt
        te_ref[0] = jnp.concatenate([te, jnp.full((TOP_K, TILE - nseq), -1, jnp.int32)], axis=1)
        tg_ref[0] = jnp.concatenate([tg, jnp.zeros((TOP_K, TILE - nseq), F32)], axis=1)
        lh_ref[...] = h
        for tap in range(1, CONV_WIDTH - 1):
            cnew_ref[tap - 1] = cprev_ref[tap]
        cnew_ref[CONV_WIDTH - 2] = ux


def _mixer_sample(x, ck, cv, h0, cprev, sinks, wts):
    nseq = x.shape[0]
    assert nseq % SEQ_CHUNK == 0 and nseq <= TILE and nseq % LANES == 0
    nc = nseq // SEQ_CHUNK
    chunk_idx = lambda c, *_: (c, 0, 0)
    in_specs = [
        _const_spec((nseq, D_MODEL)),
        pl.BlockSpec((SEQ_CHUNK, WINDOW, KV_COLS), chunk_idx),
        pl.BlockSpec((SEQ_CHUNK, WINDOW, KV_COLS), chunk_idx),
        _const_spec((nseq, LRU_WIDTH)),
        _const_spec((CONV_WIDTH - 1, nseq, LRU_WIDTH)),
    ] + [_const_spec(w.shape) for w in wts]
    out_shape = (
        jax.ShapeDtypeStruct((nseq, D_MODEL), F32),
        jax.ShapeDtypeStruct((1, TOP_K, TILE), jnp.int32),
        jax.ShapeDtypeStruct((1, TOP_K, TILE), F32),
        jax.ShapeDtypeStruct((1, N_EXPERTS, 1), jnp.int32),
        jax.ShapeDtypeStruct((nseq, WINDOW, KV_COLS), F32),
        jax.ShapeDtypeStruct((nseq, WINDOW, KV_COLS), F32),
        jax.ShapeDtypeStruct((nseq, LRU_WIDTH), F32),
        jax.ShapeDtypeStruct((CONV_WIDTH - 1, nseq, LRU_WIDTH), F32),
    )
    out_specs = (
        _const_spec((nseq, D_MODEL)),
        _const_spec((1, TOP_K, TILE)),
        _const_spec((1, TOP_K, TILE)),
        _const_spec((1, N_EXPERTS, 1)),
        pl.BlockSpec((SEQ_CHUNK, WINDOW, KV_COLS), chunk_idx),
        pl.BlockSpec((SEQ_CHUNK, WINDOW, KV_COLS), chunk_idx),
        _const_spec((nseq, LRU_WIDTH)),
        _const_spec((CONV_WIDTH - 1, nseq, LRU_WIDTH)),
    )
    scratch = [pltpu.VMEM((nseq, D_IN), F32), pltpu.VMEM((nseq, Q_COLS), F32),
               pltpu.VMEM((SEQ_CHUNK, Q_COLS), F32)]
    return pl.pallas_call(
        _mixer_sample_kernel,
        grid_spec=pltpu.PrefetchScalarGridSpec(
            num_scalar_prefetch=1, grid=(nc,), in_specs=in_specs, out_specs=out_specs,
            scratch_shapes=scratch),
        out_shape=out_shape,
        compiler_params=pltpu.CompilerParams(
            dimension_semantics=("arbitrary",), vmem_limit_bytes=VMEM_LIMIT),
        name="mixer_sample",
    )(sinks, x, ck, cv, h0, cprev, *wts)


SEG_BITS = 6
TILE_ROWS = 1280
XB_COLS = D_MODEL + LANES
assert TILE_ROWS >= TOP_K * TILE + N_EXPERTS * (SEG_ALIGN - 1) and TILE_ROWS % LANES == 0
assert SEG_ALIGN << (SEG_BITS - 1) == TILE


def _segment_copies(tile, e, seg_rows_ref, seg_start_ref, seg_off_ref, make_copy, act):
    idx = tile * N_EXPERTS + e
    rows = seg_rows_ref[idx]
    start = seg_start_ref[idx]
    off = seg_off_ref[idx]
    for bit in reversed(range(SEG_BITS)):
        size = SEG_ALIGN << bit

        @pl.when((rows & size) != 0)
        def _():
            done = rows & ~(2 * size - 1)
            act(make_copy(pl.multiple_of(start + done, SEG_ALIGN),
                          pl.multiple_of(off + done, SEG_ALIGN), size))


def _tile_tokens(i, n_prompt_tiles, hp_ref, hs_ref):
    hs = hs_ref[...]
    hs_tile = jnp.concatenate([hs, jnp.zeros((TILE - hs.shape[0], D_MODEL), F32)], axis=0)
    return jnp.where(i < n_prompt_tiles, hp_ref[...], hs_tile)


def _dispatch_kernel(seg_rows_ref, seg_start_ref, seg_off_ref,
                     hp_ref, hs_ref, te_ref, tg_ref, segv_ref,
                     dest_ref, xb_ref, ybuf, sem, *, n_prompt_tiles):
    i = pl.program_id(0)
    n = pl.num_programs(0)
    slot = i % 2

    x = _tile_tokens(i, n_prompt_tiles, hp_ref, hs_ref).astype(BF16)
    te = te_ref[0]
    tg = tg_ref[0]
    eidx = lax.broadcasted_iota(jnp.int32, (N_EXPERTS, TILE), 0)
    mem = jnp.zeros((N_EXPERTS, TILE), F32)
    for k in range(TOP_K):
        mem = mem + jnp.where(eidx == te[k:k + 1, :], 1.0, 0.0)
    tr = lax.broadcasted_iota(jnp.int32, (TILE, TILE), 0)
    tc = lax.broadcasted_iota(jnp.int32, (TILE, TILE), 1)
    before = jnp.where(tr < tc, 1.0, 0.0).astype(BF16)
    pos = jnp.dot(mem.astype(BF16), before, preferred_element_type=F32)
    base = pos.astype(jnp.int32) + segv_ref[0]
    dests = []
    for k in range(TOP_K):
        d = jnp.sum(jnp.where(eidx == te[k:k + 1, :], base, 0), axis=0, keepdims=True)
        dests.append(jnp.where(te[k:k + 1, :] >= 0, d, -1))
    dest_ref[0] = jnp.concatenate(dests, axis=0)

    rid = lax.broadcasted_iota(jnp.int32, (TILE_ROWS, TILE), 0)
    onehot = jnp.zeros((TILE_ROWS, TILE), F32)
    gsel = jnp.zeros((TILE_ROWS, TILE), F32)
    for k in range(TOP_K):
        hit = rid == dests[k]
        onehot = jnp.where(hit, 1.0, onehot)
        gsel = jnp.where(hit, tg[k:k + 1, :], gsel)
    ybuf[slot, :, 0:D_MODEL] = jnp.dot(onehot.astype(BF16), x, preferred_element_type=F32)
    ybuf[slot, :, D_MODEL:XB_COLS] = jnp.broadcast_to(
        jnp.sum(gsel, axis=-1, keepdims=True), (TILE_ROWS, LANES))

    def copy_for(s):
        def make(src_row, dst_row, size):
            return pltpu.make_async_copy(ybuf.at[s, pl.ds(src_row, size), :],
                                         xb_ref.at[pl.ds(dst_row, size), :], sem.at[s])
        return make

    def start_all(e, carry):
        _segment_copies(i, e, seg_rows_ref, seg_start_ref, seg_off_ref, copy_for(slot),
                        lambda cp: cp.start())
        return carry

    lax.fori_loop(0, N_EXPERTS, start_all, 0)

    @pl.when(i > 0)
    def _():
        def wait_prev(e, carry):
            _segment_copies(i - 1, e, seg_rows_ref, seg_start_ref, seg_off_ref, copy_for(1 - slot),
                            lambda cp: cp.wait())
            return carry
        lax.fori_loop(0, N_EXPERTS, wait_prev, 0)

    @pl.when(i == n - 1)
    def _():
        def wait_own(e, carry):
            _segment_copies(i, e, seg_rows_ref, seg_start_ref, seg_off_ref, copy_for(slot),
                            lambda cp: cp.wait())
            return carry
        lax.fori_loop(0, N_EXPERTS, wait_own, 0)


def _dispatch(seg_rows, seg_start, seg_off, h1p, h1s, te, tg, segv, n_rows):
    n_tiles = te.shape[0]
    n_prompt_tiles = h1p.shape[0] // TILE
    tile_idx = lambda i, *_: (i, 0, 0)
    in_specs = [
        pl.BlockSpec((TILE, D_MODEL), lambda i, *_: (jnp.minimum(i, n_prompt_tiles - 1), 0)),
        _const_spec(h1s.shape),
        pl.BlockSpec((1, TOP_K, TILE), tile_idx),
        pl.BlockSpec((1, TOP_K, TILE), tile_idx),
        pl.BlockSpec((1, N_EXPERTS, 1), tile_idx),
    ]
    out_shape = (jax.ShapeDtypeStruct((n_tiles, TOP_K, TILE), jnp.int32),
                 jax.ShapeDtypeStruct((n_rows, XB_COLS), F32))
    out_specs = (pl.BlockSpec((1, TOP_K, TILE), tile_idx), pl.BlockSpec(memory_space=pl.ANY))
    return pl.pallas_call(
        functools.partial(_dispatch_kernel, n_prompt_tiles=n_prompt_tiles),
        grid_spec=pltpu.PrefetchScalarGridSpec(
            num_scalar_prefetch=3, grid=(n_tiles,), in_specs=in_specs, out_specs=out_specs,
            scratch_shapes=[pltpu.VMEM((2, TILE_ROWS, XB_COLS), F32), pltpu.SemaphoreType.DMA((2,))]),
        out_shape=out_shape,
        compiler_params=pltpu.CompilerParams(
            dimension_semantics=("arbitrary",), vmem_limit_bytes=VMEM_LIMIT),
        name="moe_dispatch",
    )(seg_rows, seg_start, seg_off, h1p, h1s, te, tg, segv)


def _experts_kernel(be_ref, nb_ref, xb_ref, wgu_ref, bgu_ref, wdn_ref, bdn_ref, yb_ref, wgu_s, wdn_s):
    b = pl.program_id(0)

    @pl.when(b < nb_ref[0])
    def _():
        @pl.when((b == 0) | (be_ref[b] != be_ref[jnp.maximum(b - 1, 0)]))
        def _():
            wgu_s[...] = wgu_ref[0].astype(BF16)
            wdn_s[...] = wdn_ref[0].astype(BF16)

        xg = xb_ref[...]
        hgu = jnp.dot(xg[:, :D_MODEL].astype(BF16), wgu_s[...], preferred_element_type=F32) + bgu_ref[0]
        glu = jnp.minimum(hgu[:, :D_FF], SWIGLU_LIMIT)
        lin = jnp.clip(hgu[:, D_FF:], -SWIGLU_LIMIT, SWIGLU_LIMIT)
        act = glu * jax.nn.sigmoid(SWIGLU_ALPHA * glu) * (lin + 1.0)
        y = jnp.dot(act.astype(BF16), wdn_s[...], preferred_element_type=F32) + bdn_ref[0]
        yb_ref[...] = y * xg[:, D_MODEL:D_MODEL + 1]


def _experts(block_expert, n_used, xb, w_gu, b_gu, w_dn, b_dn):
    n_blocks = xb.shape[0] // ROW_BLOCK
    row_idx = lambda b, be, nb: (jnp.minimum(b, nb[0] - 1), 0)
    exp_idx = lambda b, be, nb: (be[b], 0, 0)
    in_specs = [
        pl.BlockSpec((ROW_BLOCK, XB_COLS), row_idx),
        pl.BlockSpec((1, D_MODEL, 2 * D_FF), exp_idx),
        pl.BlockSpec((1, 1, 2 * D_FF), exp_idx),
        pl.BlockSpec((1, D_FF, D_MODEL), exp_idx),
        pl.BlockSpec((1, 1, D_MODEL), exp_idx),
    ]
    return pl.pallas_call(
        _experts_kernel,
        grid_spec=pltpu.PrefetchScalarGridSpec(
            num_scalar_prefetch=2, grid=(n_blocks,), in_specs=in_specs,
            out_specs=pl.BlockSpec((ROW_BLOCK, D_MODEL), row_idx),
            scratch_shapes=[pltpu.VMEM((D_MODEL, 2 * D_FF), BF16), pltpu.VMEM((D_FF, D_MODEL), BF16)]),
        out_shape=jax.ShapeDtypeStruct((xb.shape[0], D_MODEL), F32),
        compiler_params=pltpu.CompilerParams(
            dimension_semantics=("arbitrary",), vmem_limit_bytes=VMEM_LIMIT),
        name="moe_experts",
    )(block_expert, n_used, xb, w_gu, b_gu[:, None, :], w_dn, b_dn[:, None, :])


def _combine_kernel(seg_rows_ref, seg_start_ref, seg_off_ref,
                    hp_ref, hs_ref, dest_ref, g_ref, b_ref, yb_ref,
                    yp_ref, ys_ref, ybuf, sem, *, n_prompt_tiles):
    i = pl.program_id(0)
    n = pl.num_programs(0)
    slot = i % 2

    def copy_for(s):
        def make(buf_row, hbm_row, size):
            return pltpu.make_async_copy(yb_ref.at[pl.ds(hbm_row, size), :],
                                         ybuf.at[s, pl.ds(buf_row, size), :], sem.at[s])
        return make

    def gather_tile(tile, s, act):
        def body(e, carry):
            _segment_copies(tile, e, seg_rows_ref, seg_start_ref, seg_off_ref, copy_for(s), act)
            return carry
        lax.fori_loop(0, N_EXPERTS, body, 0)

    @pl.when(i == 0)
    def _():
        ybuf[...] = jnp.zeros(ybuf.shape, F32)
        gather_tile(i, slot, lambda cp: cp.start())

    @pl.when(i + 1 < n)
    def _():
        gather_tile(i + 1, 1 - slot, lambda cp: cp.start())

    gather_tile(i, slot, lambda cp: cp.wait())

    dest = dest_ref[0]
    rid = lax.broadcasted_iota(jnp.int32, (TILE, TILE_ROWS), 1)
    onehot = jnp.zeros((TILE, TILE_ROWS), F32)
    for k in range(TOP_K):
        onehot = jnp.where(rid == dest[:, k:k + 1], 1.0, onehot)
    ff = jnp.dot(onehot.astype(BF16), ybuf[slot].astype(BF16), preferred_element_type=F32)
    h1 = _tile_tokens(i, n_prompt_tiles, hp_ref, hs_ref)
    y = _layer_norm(DN_ALPHA * h1 + ff, g_ref[...], b_ref[...])

    @pl.when(i < n_prompt_tiles)
    def _():
        yp_ref[...] = y

    @pl.when(i >= n_prompt_tiles)
    def _():
        ys_ref[...] = y[:ys_ref.shape[0]]


def _combine(seg_rows, seg_start, seg_off, h1p, h1s, dest_t, ln_g, ln_b, yb):
    n_tiles = dest_t.shape[0]
    n_prompt_tiles = h1p.shape[0] // TILE
    prompt_idx = lambda i, *_: (jnp.minimum(i, n_prompt_tiles - 1), 0)
    in_specs = [
        pl.BlockSpec((TILE, D_MODEL), prompt_idx),
        _const_spec(h1s.shape),
        pl.BlockSpec((1, TILE, TOP_K), lambda i, *_: (i, 0, 0)),
        _const_spec(ln_g.shape),
        _const_spec(ln_b.shape),
        pl.BlockSpec(memory_space=pl.ANY),
    ]
    out_shape = (jax.ShapeDtypeStruct(h1p.shape, F32), jax.ShapeDtypeStruct(h1s.shape, F32))
    out_specs = (pl.BlockSpec((TILE, D_MODEL), prompt_idx), _const_spec(h1s.shape))
    return pl.pallas_call(
        functools.partial(_combine_kernel, n_prompt_tiles=n_prompt_tiles),
        grid_spec=pltpu.PrefetchScalarGridSpec(
            num_scalar_prefetch=3, grid=(n_tiles,), in_specs=in_specs, out_specs=out_specs,
            scratch_shapes=[pltpu.VMEM((2, TILE_ROWS, D_MODEL), F32), pltpu.SemaphoreType.DMA((2,))]),
        out_shape=out_shape,
        compiler_params=pltpu.CompilerParams(
            dimension_semantics=("arbitrary",), vmem_limit_bytes=VMEM_LIMIT),
        name="moe_combine",
    )(seg_rows, seg_start, seg_off, h1p, h1s, dest_t, ln_g, ln_b, yb)


def _moe_layout(cnt):
    n_tiles = cnt.shape[0]
    seg_rows = (cnt + SEG_ALIGN - 1) // SEG_ALIGN * SEG_ALIGN
    seg_start = jnp.cumsum(seg_rows, axis=1) - seg_rows
    exp_rows = jnp.sum(seg_rows, axis=0)
    exp_blocks = (exp_rows + ROW_BLOCK - 1) // ROW_BLOCK
    blocks_end = jnp.cumsum(exp_blocks)
    exp_base = (blocks_end - exp_blocks) * ROW_BLOCK
    seg_off = exp_base[None, :] + jnp.cumsum(seg_rows, axis=0) - seg_rows
    max_rows = n_tiles * (TOP_K * TILE + N_EXPERTS * (SEG_ALIGN - 1)) + N_EXPERTS * (ROW_BLOCK - SEG_ALIGN)
    n_blocks = -(-max_rows // ROW_BLOCK)
    n_used = blocks_end[-1]
    blk = jnp.minimum(jnp.arange(n_blocks, dtype=jnp.int32), n_used - 1)
    block_expert = jnp.minimum(jnp.searchsorted(blocks_end, blk, side="right"), N_EXPERTS - 1)
    flat = lambda a: a.reshape(-1).astype(jnp.int32)
    return (flat(seg_rows), flat(seg_start), flat(seg_off), seg_start[:, :, None].astype(jnp.int32),
            block_expert.astype(jnp.int32), n_used.reshape(1).astype(jnp.int32), n_blocks * ROW_BLOCK)


def _prep_weights(w_in, b_in, conv_w, conv_b, lru_w_a, lru_b_a, lru_w_i, lru_b_i, lru_lambda,
                  w_out, b_out, ln1_g, ln1_b, router_w, router_b):
    qperm = np.concatenate([np.r_[g * HEAD_DIM:(g + 1) * HEAD_DIM,
                                  (g + HEAD_GROUPS) * HEAD_DIM:(g + HEAD_GROUPS + 1) * HEAD_DIM]
                            for g in range(HEAD_GROUPS)])
    in_perm = np.concatenate([qperm, np.arange(Q_COLS, D_IN)])
    out_perm = np.concatenate([qperm, np.arange(Q_COLS, D_MODEL)])

    def diag_tiles(w):
        per = MXU_DIM // (LRU_WIDTH // LRU_BLOCKS)
        w4 = w.reshape(LRU_BLOCKS // per, per, LRU_WIDTH // LRU_BLOCKS, LRU_WIDTH // LRU_BLOCKS)
        t = jnp.einsum("taij,ab->taibj", w4, jnp.eye(per, dtype=w.dtype))
        return t.reshape(LRU_BLOCKS // per, MXU_DIM, MXU_DIM).astype(BF16)

    return (
        w_in[0][:, in_perm].astype(BF16), b_in[0][in_perm][None],
        conv_w[0], conv_b[0][None],
        diag_tiles(lru_w_a[0]), lru_b_a[0].reshape(1, LRU_WIDTH),
        diag_tiles(lru_w_i[0]), lru_b_i[0].reshape(1, LRU_WIDTH),
        lru_lambda[0][None],
        w_out[0][out_perm].astype(BF16), b_out[0][None],
        ln1_g[0][None], ln1_b[0][None],
        router_w[0].T.astype(BF16), router_b[0][:, None],
    )


def kernel(x_prompt, x_sample, cache_win_k, cache_win_v, state_lru_h, state_conv, w_in, b_in, attn_sinks, conv_w, conv_b, lru_w_a, lru_b_a, lru_w_i, lru_b_i, lru_lambda, w_out, b_out, ln1_g, ln1_b, router_w, router_b, w_gate_up, b_gate_up, w_down, b_down, ln2_g, ln2_b):
    assert w_in.shape[0] == 1, "single-layer step"
    B, S, _ = x_prompt.shape
    nseq = x_sample.shape[0]
    assert x_sample.shape[1] == 1 and S % TILE == 0
    wts = _prep_weights(w_in, b_in, conv_w, conv_b, lru_w_a, lru_b_a, lru_w_i, lru_b_i, lru_lambda,
                        w_out, b_out, ln1_g, ln1_b, router_w, router_b)
    sinks = attn_sinks[0]

    h1p, te_p, tg_p, cnt_p, pk, pv, ph, pc = _mixer_prompt(x_prompt, sinks, wts)
    h1s, te_s, tg_s, cnt_s, sk, sv, sh, sc = _mixer_sample(
        x_sample.reshape(nseq, D_MODEL),
        cache_win_k[0].reshape(nseq, WINDOW, KV_COLS), cache_win_v[0].reshape(nseq, WINDOW, KV_COLS),
        state_lru_h[0], jnp.transpose(state_conv[0], (1, 0, 2)), sinks, wts)

    te = jnp.concatenate([te_p, te_s], axis=0)
    tg = jnp.concatenate([tg_p, tg_s], axis=0)
    cnt = jnp.concatenate([cnt_p, cnt_s], axis=0)[:, :, 0]
    seg_rows, seg_start, seg_off, segv, block_expert, n_used, n_rows = _moe_layout(cnt)

    h1p2 = h1p.reshape(B * S, D_MODEL)
    dest, xb = _dispatch(seg_rows, seg_start, seg_off, h1p2, h1s, te, tg, segv, n_rows)
    yb = _experts(block_expert, n_used, xb, w_gate_up[0], b_gate_up[0], w_down[0], b_down[0])
    yp, ys = _combine(seg_rows, seg_start, seg_off, h1p2, h1s, jnp.transpose(dest, (0, 2, 1)),
                      ln2_g[0][None], ln2_b[0][None], yb)

    kv_shape = (N_KV_HEADS, HEAD_DIM)
    return (
        yp.reshape(B, S, D_MODEL), ys.reshape(nseq, 1, D_MODEL),
        pk.reshape(1, B, WINDOW, *kv_shape), pv.reshape(1, B, WINDOW, *kv_shape),
        ph.reshape(1, B, LRU_WIDTH), pc[None],
        sk.reshape(1, nseq, WINDOW, *kv_shape), sv.reshape(1, nseq, WINDOW, *kv_shape),
        sh[None], jnp.transpose(sc, (1, 0, 2))[None],
    )
```

```python
import functools

import jax
import jax.numpy as jnp
import numpy as np
from jax import lax
from jax.experimental import pallas as pl
from jax.experimental.pallas import tpu as pltpu

F32 = jnp.float32
BF16 = jnp.bfloat16

D_MODEL = 1024
N_Q_HEADS = 8
N_KV_HEADS = 2
HEAD_DIM = 64
WINDOW = 128
Q_COLS = N_Q_HEADS * HEAD_DIM
KV_COLS = N_KV_HEADS * HEAD_DIM
LRU_WIDTH = 512
LRU_BLOCKS = 8
CONV_WIDTH = 4
RG_C = 8.0
N_EXPERTS = 32
TOP_K = 4
D_FF = 1024
SWIGLU_LIMIT = 7.0
SWIGLU_ALPHA = 1.702
LN_EPS = 1e-5
DN_ALPHA = 2.0 ** 0.25
NEG_INF = -1e30
D_IN = Q_COLS + 2 * KV_COLS + 2 * LRU_WIDTH

LANES = 128
SUBLANES = 8
MXU_DIM = 256

TILE = 256
SEG_ALIGN = SUBLANES
ROW_BLOCK = 256
HEAD_GROUPS = N_Q_HEADS // N_KV_HEADS
VMEM_LIMIT = 56 * 1024 * 1024


def _layer_norm(z, g, b):
    mu = jnp.mean(z, axis=-1, keepdims=True)
    zc = z - mu
    var = jnp.mean(zc * zc, axis=-1, keepdims=True)
    return zc * lax.rsqrt(var + LN_EPS) * g + b


def _softplus(x):
    return jnp.maximum(x, 0.0) + jnp.log1p(jnp.exp(-jnp.abs(x)))


def _gelu_tanh(x):
    c = np.float32(np.sqrt(2.0 / np.pi))
    return 0.5 * x * (1.0 + jnp.tanh(c * (x + 0.044715 * (x * x * x))))


def _rglru_gates(xc, wa_ref, ba_ref, wi_ref, bi_ref, lam_ref):
    xcb = xc.astype(BF16)
    half = LRU_WIDTH // 2
    pre_a = jnp.concatenate(
        [jnp.dot(xcb[:, :half], wa_ref[0], preferred_element_type=F32),
         jnp.dot(xcb[:, half:], wa_ref[1], preferred_element_type=F32)], axis=1)
    pre_i = jnp.concatenate(
        [jnp.dot(xcb[:, :half], wi_ref[0], preferred_element_type=F32),
         jnp.dot(xcb[:, half:], wi_ref[1], preferred_element_type=F32)], axis=1)
    r = jax.nn.sigmoid(pre_a + ba_ref[...])
    gi = jax.nn.sigmoid(pre_i + bi_ref[...])
    log_a = (-RG_C) * r * _softplus(-lam_ref[...])
    a = jnp.exp(log_a)
    t = jnp.tanh(log_a)
    mult = jnp.sqrt(jnp.maximum(-2.0 * t / (1.0 - t), 0.0))
    return a, mult * (gi * xc)


def _route(h1, rw_ref, rb_ref):
    n = h1.shape[0]
    logits = lax.dot_general(rw_ref[...], h1.astype(BF16), (((1,), (1,)), ((), ())),
                             preferred_element_type=F32) + rb_ref[...]
    eidx = lax.broadcasted_iota(jnp.int32, (N_EXPERTS, n), 0)
    vals = logits
    top_v, top_e = [], []
    for _ in range(TOP_K):
        m = jnp.max(vals, axis=0, keepdims=True)
        idx = jnp.min(jnp.where(vals == m, eidx, N_EXPERTS), axis=0, keepdims=True)
        top_v.append(m)
        top_e.append(idx)
        vals = jnp.where(eidx == idx, -jnp.inf, vals)
    ex = [jnp.exp(v - top_v[0]) for v in top_v]
    den = ex[0] + ex[1] + ex[2] + ex[3]
    gates = jnp.concatenate([e / den for e in ex], axis=0)
    te = jnp.concatenate(top_e, axis=0)
    mem = jnp.zeros((N_EXPERTS, n), jnp.int32)
    for idx in top_e:
        mem = mem + jnp.where(eidx == idx, 1, 0)
    return te, gates, jnp.sum(mem, axis=1, keepdims=True)


def _post_mix(x, mix_in, w_out_ref, b_out_ref, g_ref, b_ref):
    mix = jnp.dot(mix_in.astype(BF16), w_out_ref[...], preferred_element_type=F32) + b_out_ref[...]
    return _layer_norm(DN_ALPHA * x + mix, g_ref[...], b_ref[...])


def _mixer_prompt_kernel(sinks_ref, x_ref, w_in_ref, b_in_ref, conv_w_ref, conv_b_ref,
                         wa_ref, ba_ref, wi_ref, bi_ref, lam_ref, w_out_ref, b_out_ref,
                         ln_g_ref, ln_b_ref, rw_ref, rb_ref,
                         h1_ref, te_ref, tg_ref, cnt_ref, wk_ref, wv_ref, lh_ref, cv_ref,
                         kext, vext, uext, hcar, attn_buf):
    j = pl.program_id(1)
    nj = pl.num_programs(1)
    R = TILE

    @pl.when(j == 0)
    def _():
        kext[0:WINDOW, :] = jnp.zeros((WINDOW, KV_COLS), F32)
        vext[0:WINDOW, :] = jnp.zeros((WINDOW, KV_COLS), F32)
        uext[0:SUBLANES, :] = jnp.zeros((SUBLANES, LRU_WIDTH), F32)
        hcar[...] = jnp.zeros((1, LRU_WIDTH), F32)

    x = x_ref[0]
    proj = jnp.dot(x.astype(BF16), w_in_ref[...], preferred_element_type=F32) + b_in_ref[...]
    q = proj[:, :Q_COLS]
    kext[WINDOW:WINDOW + R, :] = proj[:, Q_COLS:Q_COLS + KV_COLS]
    vext[WINDOW:WINDOW + R, :] = proj[:, Q_COLS + KV_COLS:Q_COLS + 2 * KV_COLS]
    ux = proj[:, Q_COLS + 2 * KV_COLS:Q_COLS + 2 * KV_COLS + LRU_WIDTH]
    ug = proj[:, Q_COLS + 2 * KV_COLS + LRU_WIDTH:]
    uext[SUBLANES:SUBLANES + R, :] = ux

    row2 = lax.broadcasted_iota(jnp.int32, (2 * WINDOW, 2 * WINDOW), 0)
    col2 = lax.broadcasted_iota(jnp.int32, (2 * WINDOW, 2 * WINDOW), 1)
    dist = (row2 & (WINDOW - 1)) - col2 + WINDOW
    valid = (dist >= 0) & (dist <= WINDOW)
    distf = dist.astype(F32)
    upper = row2 >= WINDOW
    lane = lax.broadcasted_iota(jnp.int32, (WINDOW, LANES), 1)
    lo_lane = lane < HEAD_DIM
    for s in range(R // WINDOW):
        kk = kext[s * WINDOW:s * WINDOW + 2 * WINDOW, :].astype(BF16)
        vv = vext[s * WINDOW:s * WINDOW + 2 * WINDOW, :].astype(BF16)
        ok = valid
        if s == 0:
            ok = ok & ((col2 >= WINDOW) | (j > 0))
        for g in range(HEAD_GROUPS):
            qg = q[s * WINDOW:(s + 1) * WINDOW, g * LANES:(g + 1) * LANES]
            q2 = jnp.concatenate([jnp.where(lo_lane, qg, 0.0), jnp.where(lo_lane, 0.0, qg)], axis=0)
            sc = lax.dot_general(q2.astype(BF16), kk, (((1,), (1,)), ((), ())),
                                 preferred_element_type=F32) * (HEAD_DIM ** -0.5)
            slope = jnp.where(upper, 2.0 ** -(g + HEAD_GROUPS + 1), 2.0 ** -(g + 1))
            sc = jnp.where(ok, sc - slope * distf, NEG_INF)
            sink = jnp.where(upper[:, 0:1], sinks_ref[g + HEAD_GROUPS], sinks_ref[g])
            m = jnp.maximum(jnp.max(sc, axis=-1, keepdims=True), sink)
            p = jnp.exp(sc - m)
            den = jnp.sum(p, axis=-1, keepdims=True) + jnp.exp(sink - m)
            o = jnp.dot(p.astype(BF16), vv, preferred_element_type=F32) / den
            attn_buf[s * WINDOW:(s + 1) * WINDOW, g * LANES:(g + 1) * LANES] = jnp.where(
                lo_lane, o[:WINDOW], o[WINDOW:])

    xc = conv_b_ref[...]
    for tap in range(CONV_WIDTH):
        off = SUBLANES - (CONV_WIDTH - 1) + tap
        xc = xc + uext[off:off + R, :] * conv_w_ref[tap:tap + 1, :]
    a, bv = _rglru_gates(xc, wa_ref, ba_ref, wi_ref, bi_ref, lam_ref)
    rows = lax.broadcasted_iota(jnp.int32, (R, LRU_WIDTH), 0)
    d = 1
    while d < R:
        keep = rows >= d
        a_s = pltpu.roll(a, d, 0)
        b_s = pltpu.roll(bv, d, 0)
        bv = jnp.where(keep, a * b_s + bv, bv)
        a = jnp.where(keep, a * a_s, a)
        d *= 2
    h = a * hcar[...] + bv
    hcar[...] = h[R - 1:R, :]
    rnn = h * _gelu_tanh(ug)

    mix_in = jnp.concatenate([attn_buf[...], rnn], axis=1)
    h1 = _post_mix(x, mix_in, w_out_ref, b_out_ref, ln_g_ref, ln_b_ref)
    h1_ref[0] = h1
    te, tg, cnt = _route(h1, rw_ref, rb_ref)
    te_ref[0] = te
    tg_ref[0] = tg
    cnt_ref[0] = cnt

    @pl.when(j == nj - 1)
    def _():
        wk_ref[0] = kext[R:R + WINDOW, :]
        wv_ref[0] = vext[R:R + WINDOW, :]
        lh_ref[0] = h[R - 1:R, :]
        cv_ref[0] = uext[SUBLANES + R - (CONV_WIDTH - 1):SUBLANES + R, :]

    kext[0:WINDOW, :] = kext[R:R + WINDOW, :]
    vext[0:WINDOW, :] = vext[R:R + WINDOW, :]
    uext[0:SUBLANES, :] = uext[R:R + SUBLANES, :]


def _const_spec(shape):
    return pl.BlockSpec(shape, lambda *_: (0,) * len(shape))


def _mixer_prompt(x, sinks, wts):
    B, S, _ = x.shape
    nj = S // TILE
    n_tiles = B * nj
    tile_idx = lambda b, j, *_: (b * nj + j, 0, 0)
    batch_idx = lambda b, j, *_: (b, 0, 0)
    in_specs = [pl.BlockSpec((1, TILE, D_MODEL), lambda b, j, *_: (b, j, 0))]
    in_specs += [_const_spec(w.shape) for w in wts]
    out_shape = (
        jax.ShapeDtypeStruct((B, S, D_MODEL), F32),
        jax.ShapeDtypeStruct((n_tiles, TOP_K, TILE), jnp.int32),
        jax.ShapeDtypeStruct((n_tiles, TOP_K, TILE), F32),
        jax.ShapeDtypeStruct((n_tiles, N_EXPERTS, 1), jnp.int32),
        jax.ShapeDtypeStruct((B, WINDOW, KV_COLS), F32),
        jax.ShapeDtypeStruct((B, WINDOW, KV_COLS), F32),
        jax.ShapeDtypeStruct((B, 1, LRU_WIDTH), F32),
        jax.ShapeDtypeStruct((B, CONV_WIDTH - 1, LRU_WIDTH), F32),
    )
    out_specs = (
        pl.BlockSpec((1, TILE, D_MODEL), lambda b, j, *_: (b, j, 0)),
        pl.BlockSpec((1, TOP_K, TILE), tile_idx),
        pl.BlockSpec((1, TOP_K, TILE), tile_idx),
        pl.BlockSpec((1, N_EXPERTS, 1), tile_idx),
        pl.BlockSpec((1, WINDOW, KV_COLS), batch_idx),
        pl.BlockSpec((1, WINDOW, KV_COLS), batch_idx),
        pl.BlockSpec((1, 1, LRU_WIDTH), batch_idx),
        pl.BlockSpec((1, CONV_WIDTH - 1, LRU_WIDTH), batch_idx),
    )
    scratch = [
        pltpu.VMEM((TILE + WINDOW, KV_COLS), F32),
        pltpu.VMEM((TILE + WINDOW, KV_COLS), F32),
        pltpu.VMEM((TILE + SUBLANES, LRU_WIDTH), F32),
        pltpu.VMEM((1, LRU_WIDTH), F32),
        pltpu.VMEM((TILE, Q_COLS), F32),
    ]
    return pl.pallas_call(
        _mixer_prompt_kernel,
        grid_spec=pltpu.PrefetchScalarGridSpec(
            num_scalar_prefetch=1, grid=(B, nj), in_specs=in_specs, out_specs=out_specs,
            scratch_shapes=scratch),
        out_shape=out_shape,
        compiler_params=pltpu.CompilerParams(
            dimension_semantics=("arbitrary", "arbitrary"), vmem_limit_bytes=VMEM_LIMIT),
        name="mixer_prompt",
    )(sinks, x, *wts)


SEQ_CHUNK = 16


def _mixer_sample_kernel(sinks_ref, x_ref, ck_ref, cv_ref, h0_ref, cprev_ref,
                         w_in_ref, b_in_ref, conv_w_ref, conv_b_ref,
                         wa_ref, ba_ref, wi_ref, bi_ref, lam_ref, w_out_ref, b_out_ref,
                         ln_g_ref, ln_b_ref, rw_ref, rb_ref,
                         h1_ref, te_ref, tg_ref, cnt_ref, wk_ref, wv_ref, lh_ref, cnew_ref,
                         proj_s, attn_s, attn_c):
    c = pl.program_id(0)
    nc = pl.num_programs(0)
    nseq = x_ref.shape[0]

    @pl.when(c == 0)
    def _():
        proj_s[...] = jnp.dot(x_ref[...].astype(BF16), w_in_ref[...],
                              preferred_element_type=F32) + b_in_ref[...]

    sub = lax.broadcasted_iota(jnp.int32, (N_Q_HEADS, LANES), 0)
    lane = lax.broadcasted_iota(jnp.int32, (N_Q_HEADS, LANES), 1)
    own_half = (lane < HEAD_DIM) == (sub < HEAD_GROUPS)
    sub1 = sub[:, 0:1]
    slope = jnp.zeros((N_Q_HEADS, 1), F32)
    sink = jnp.zeros((N_Q_HEADS, 1), F32)
    for hd in range(N_Q_HEADS):
        slope = jnp.where(sub1 == hd, 2.0 ** -(hd + 1), slope)
        sink = jnp.where(sub1 == hd, sinks_ref[hd], sink)
    dist = (WINDOW - lax.broadcasted_iota(jnp.int32, (1, WINDOW), 1)).astype(F32)
    lo_row = lax.broadcasted_iota(jnp.int32, (1, LANES), 1) < HEAD_DIM
    scale = HEAD_DIM ** -0.5

    for i in range(SEQ_CHUNK):
        b = c * SEQ_CHUNK + i
        prow = proj_s[pl.ds(b, 1), :]
        q8 = jnp.zeros((N_Q_HEADS, LANES), F32)
        for g in range(HEAD_GROUPS):
            qg = jnp.broadcast_to(prow[:, g * LANES:(g + 1) * LANES], (N_Q_HEADS, LANES))
            q8 = jnp.where(((sub & (HEAD_GROUPS - 1)) == g) & own_half, qg, q8)
        k_new = prow[:, Q_COLS:Q_COLS + KV_COLS]
        v_new = prow[:, Q_COLS + KV_COLS:Q_COLS + 2 * KV_COLS]
        kb = ck_ref[i]
        vb = cv_ref[i]
        sc = lax.dot_general(q8.astype(BF16), kb.astype(BF16), (((1,), (1,)), ((), ())),
                             preferred_element_type=F32) * scale - slope * dist
        sc_new = jnp.sum(q8 * k_new, axis=-1, keepdims=True) * scale
        m = jnp.maximum(jnp.maximum(jnp.max(sc, axis=-1, keepdims=True), sc_new), sink)
        p = jnp.exp(sc - m)
        p_new = jnp.exp(sc_new - m)
        den = jnp.sum(p, axis=-1, keepdims=True) + p_new + jnp.exp(sink - m)
        o = (jnp.dot(p.astype(BF16), vb.astype(BF16), preferred_element_type=F32)
             + p_new * v_new) / den
        for g in range(HEAD_GROUPS):
            attn_c[i:i + 1, g * LANES:(g + 1) * LANES] = jnp.where(
                lo_row, o[g:g + 1, :], o[g + HEAD_GROUPS:g + HEAD_GROUPS + 1, :])
        wk_ref[i, 0:WINDOW - 1, :] = ck_ref[i, 1:WINDOW, :]
        wk_ref[i, WINDOW - 1:WINDOW, :] = k_new
        wv_ref[i, 0:WINDOW - 1, :] = cv_ref[i, 1:WINDOW, :]
        wv_ref[i, WINDOW - 1:WINDOW, :] = v_new
    attn_s[pl.ds(pl.multiple_of(c * SEQ_CHUNK, SEQ_CHUNK), SEQ_CHUNK), :] = attn_c[...]

    @pl.when(c == nc - 1)
    def _():
        x = x_ref[...]
        ux = proj_s[:, Q_COLS + 2 * KV_COLS:Q_COLS + 2 * KV_COLS + LRU_WIDTH]
        ug = proj_s[:, Q_COLS + 2 * KV_COLS + LRU_WIDTH:]
        xc = conv_b_ref[...]
        for tap in range(CONV_WIDTH - 1):
            xc = xc + cprev_ref[tap] * conv_w_ref[tap:tap + 1, :]
        xc = xc + ux * conv_w_ref[CONV_WIDTH - 1:CONV_WIDTH, :]
        a, bv = _rglru_gates(xc, wa_ref, ba_ref, wi_ref, bi_ref, lam_ref)
        h = a * h0_ref[...] + bv
        rnn = h * _gelu_tanh(ug)
        mix_in = jnp.concatenate([attn_s[...], rnn], axis=1)
        h1 = _post_mix(x, mix_in, w_out_ref, b_out_ref, ln_g_ref, ln_b_ref)
        h1_ref[...] = h1
        te, tg, cnt = _route(h1, rw_ref, rb_ref)
        cnt_ref[0] = cnt
        te_ref[0] = jnp.concatenate([te, jnp.full((TOP_K, TILE - nseq), -1, jnp.int32)], axis=1)
        tg_ref[0] = jnp.concatenate([tg, jnp.zeros((TOP_K, TILE - nseq), F32)], axis=1)
        lh_ref[...] = h
        for tap in range(1, CONV_WIDTH - 1):
            cnew_ref[tap - 1] = cprev_ref[tap]
        cnew_ref[CONV_WIDTH - 2] = ux


def _mixer_sample(x, ck, cv, h0, cprev, sinks, wts):
    nseq = x.shape[0]
    assert nseq % SEQ_CHUNK == 0 and nseq <= TILE and nseq % LANES == 0
    nc = nseq // SEQ_CHUNK
    chunk_idx = lambda c, *_: (c, 0, 0)
    in_specs = [
        _const_spec((nseq, D_MODEL)),
        pl.BlockSpec((SEQ_CHUNK, WINDOW, KV_COLS), chunk_idx),
        pl.BlockSpec((SEQ_CHUNK, WINDOW, KV_COLS), chunk_idx),
        _const_spec((nseq, LRU_WIDTH)),
        _const_spec((CONV_WIDTH - 1, nseq, LRU_WIDTH)),
    ] + [_const_spec(w.shape) for w in wts]
    out_shape = (
        jax.ShapeDtypeStruct((nseq, D_MODEL), F32),
        jax.ShapeDtypeStruct((1, TOP_K, TILE), jnp.int32),
        jax.ShapeDtypeStruct((1, TOP_K, TILE), F32),
        jax.ShapeDtypeStruct((1, N_EXPERTS, 1), jnp.int32),
        jax.ShapeDtypeStruct((nseq, WINDOW, KV_COLS), F32),
        jax.ShapeDtypeStruct((nseq, WINDOW, KV_COLS), F32),
        jax.ShapeDtypeStruct((nseq, LRU_WIDTH), F32),
        jax.ShapeDtypeStruct((CONV_WIDTH - 1, nseq, LRU_WIDTH), F32),
    )
    out_specs = (
        _const_spec((nseq, D_MODEL)),
        _const_spec((1, TOP_K, TILE)),
        _const_spec((1, TOP_K, TILE)),
        _const_spec((1, N_EXPERTS, 1)),
        pl.BlockSpec((SEQ_CHUNK, WINDOW, KV_COLS), chunk_idx),
        pl.BlockSpec((SEQ_CHUNK, WINDOW, KV_COLS), chunk_idx),
        _const_spec((nseq, LRU_WIDTH)),
        _const_spec((CONV_WIDTH - 1, nseq, LRU_WIDTH)),
    )
    scratch = [pltpu.VMEM((nseq, D_IN), F32), pltpu.VMEM((nseq, Q_COLS), F32),
               pltpu.VMEM((SEQ_CHUNK, Q_COLS), F32)]
    return pl.pallas_call(
        _mixer_sample_kernel,
        grid_spec=pltpu.PrefetchScalarGridSpec(
            num_scalar_prefetch=1, grid=(nc,), in_specs=in_specs, out_specs=out_specs,
            scratch_shapes=scratch),
        out_shape=out_shape,
        compiler_params=pltpu.CompilerParams(
            dimension_semantics=("arbitrary",), vmem_limit_bytes=VMEM_LIMIT),
        name="mixer_sample",
    )(sinks, x, ck, cv, h0, cprev, *wts)


SEG_BITS = 6
TILE_ROWS = 1280
TOTAL_BITS = 8
assert TILE_ROWS >= TOP_K * TILE + N_EXPERTS * (SEG_ALIGN - 1) and TILE_ROWS % LANES == 0
assert SEG_ALIGN << (SEG_BITS - 1) == TILE and TILE_ROWS < SEG_ALIGN << TOTAL_BITS


def _wait_rows(total, make_copy):
    for bit in range(TOTAL_BITS):
        size = SEG_ALIGN << bit

        @pl.when((total & size) != 0)
        def _():
            make_copy(0, 0, size).wait()


def _segment_copies(tile, e, seg_rows_ref, seg_start_ref, seg_off_ref, make_copy, act):
    idx = tile * N_EXPERTS + e
    rows = seg_rows_ref[idx]
    start = seg_start_ref[idx]
    off = seg_off_ref[idx]
    for bit in reversed(range(SEG_BITS)):
        size = SEG_ALIGN << bit

        @pl.when((rows & size) != 0)
        def _():
            done = rows & ~(2 * size - 1)
            act(make_copy(pl.multiple_of(start + done, SEG_ALIGN),
                          pl.multiple_of(off + done, SEG_ALIGN), size))


def _tile_tokens(i, n_prompt_tiles, hp_ref, hs_ref):
    hs = hs_ref[...]
    hs_tile = jnp.concatenate([hs, jnp.zeros((TILE - hs.shape[0], D_MODEL), F32)], axis=0)
    return jnp.where(i < n_prompt_tiles, hp_ref[...], hs_tile)


def _dispatch_kernel(seg_rows_ref, seg_start_ref, seg_off_ref, tile_rows_ref, nb_ref,
                     hp_ref, hs_ref, te_ref, segv_ref,
                     dest_ref, xb_ref, ybuf, zbuf, sem, *, n_prompt_tiles):
    i = pl.program_id(0)
    n = pl.num_programs(0)
    slot = i % 2
    n_blocks = xb_ref.shape[0] // ROW_BLOCK

    def copy_for(s):
        def make(src_row, dst_row, size):
            return pltpu.make_async_copy(ybuf.at[s, pl.ds(src_row, size), :],
                                         xb_ref.at[pl.ds(dst_row, size), :], sem.at[s])
        return make

    def zero_copy(src_row, dst_row, size):
        return pltpu.make_async_copy(zbuf.at[pl.ds(src_row, size), :],
                                     xb_ref.at[pl.ds(dst_row, size), :], sem.at[2])

    def zero_block(b):
        return zero_copy(0, pl.multiple_of(b * ROW_BLOCK, ROW_BLOCK), ROW_BLOCK)

    def zero_fill(act):
        def tails(e, carry):
            _segment_copies(n, e, seg_rows_ref, seg_start_ref, seg_off_ref, zero_copy, act)
            return carry
        lax.fori_loop(0, N_EXPERTS, tails, 0)

        def blocks(b, carry):
            act(zero_block(b))
            return carry
        lax.fori_loop(nb_ref[0], n_blocks, blocks, 0)

    @pl.when(i == 0)
    def _():
        zbuf[...] = jnp.zeros(zbuf.shape, F32)
        zero_fill(lambda cp: cp.start())

    x = _tile_tokens(i, n_prompt_tiles, hp_ref, hs_ref).astype(BF16)
    te = te_ref[0]
    eidx = lax.broadcasted_iota(jnp.int32, (N_EXPERTS, TILE), 0)
    mem = jnp.zeros((N_EXPERTS, TILE), F32)
    for k in range(TOP_K):
        mem = mem + jnp.where(eidx == te[k:k + 1, :], 1.0, 0.0)
    tr = lax.broadcasted_iota(jnp.int32, (TILE, TILE), 0)
    tc = lax.broadcasted_iota(jnp.int32, (TILE, TILE), 1)
    before = jnp.where(tr < tc, 1.0, 0.0).astype(BF16)
    pos = jnp.dot(mem.astype(BF16), before, preferred_element_type=F32)
    base = pos.astype(jnp.int32) + segv_ref[0]
    dests = []
    for k in range(TOP_K):
        d = jnp.sum(jnp.where(eidx == te[k:k + 1, :], base, 0), axis=0, keepdims=True)
        dests.append(jnp.where(te[k:k + 1, :] >= 0, d, -1))
    dest_ref[0] = jnp.concatenate(dests, axis=0)

    rid = lax.broadcasted_iota(jnp.int32, (TILE_ROWS, TILE), 0)
    onehot = jnp.zeros((TILE_ROWS, TILE), F32)
    for k in range(TOP_K):
        onehot = jnp.where(rid == dests[k], 1.0, onehot)
    ybuf[slot] = jnp.dot(onehot.astype(BF16), x, preferred_element_type=F32)

    def start_all(e, carry):
        _segment_copies(i, e, seg_rows_ref, seg_start_ref, seg_off_ref, copy_for(slot),
                        lambda cp: cp.start())
        return carry

    lax.fori_loop(0, N_EXPERTS, start_all, 0)

    @pl.when(i > 0)
    def _():
        _wait_rows(tile_rows_ref[i - 1], copy_for(1 - slot))

    @pl.when(i == n - 1)
    def _():
        _wait_rows(tile_rows_ref[i], copy_for(slot))
        zero_fill(lambda cp: cp.wait())


def _dispatch(layout, h1p, h1s, te, segv, n_rows):
    n_tiles = te.shape[0]
    n_prompt_tiles = h1p.shape[0] // TILE
    tile_idx = lambda i, *_: (i, 0, 0)
    in_specs = [
        pl.BlockSpec((TILE, D_MODEL), lambda i, *_: (jnp.minimum(i, n_prompt_tiles - 1), 0)),
        _const_spec(h1s.shape),
        pl.BlockSpec((1, TOP_K, TILE), tile_idx),
        pl.BlockSpec((1, N_EXPERTS, 1), tile_idx),
    ]
    out_shape = (jax.ShapeDtypeStruct((n_tiles, TOP_K, TILE), jnp.int32),
                 jax.ShapeDtypeStruct((n_rows, D_MODEL), F32))
    out_specs = (pl.BlockSpec((1, TOP_K, TILE), tile_idx), pl.BlockSpec(memory_space=pl.ANY))
    return pl.pallas_call(
        functools.partial(_dispatch_kernel, n_prompt_tiles=n_prompt_tiles),
        grid_spec=pltpu.PrefetchScalarGridSpec(
            num_scalar_prefetch=len(layout), grid=(n_tiles,), in_specs=in_specs, out_specs=out_specs,
            scratch_shapes=[pltpu.VMEM((2, TILE_ROWS, D_MODEL), F32), pltpu.VMEM((ROW_BLOCK, D_MODEL), F32),
                            pltpu.SemaphoreType.DMA((3,))]),
        out_shape=out_shape,
        compiler_params=pltpu.CompilerParams(
            dimension_semantics=("arbitrary",), vmem_limit_bytes=VMEM_LIMIT),
        name="moe_dispatch",
    )(*layout, h1p, h1s, te, segv)


def _experts_kernel(be_ref, nb_ref, xb_ref, wgu_ref, bgu_ref, wdn_ref, bdn_ref, yb_ref, wgu_s, wdn_s):
    b = pl.program_id(0)

    @pl.when(b < nb_ref[0])
    def _():
        @pl.when((b == 0) | (be_ref[b] != be_ref[jnp.maximum(b - 1, 0)]))
        def _():
            wgu_s[...] = wgu_ref[0].astype(BF16)
            wdn_s[...] = wdn_ref[0].astype(BF16)

        hgu = jnp.dot(xb_ref[...].astype(BF16), wgu_s[...], preferred_element_type=F32) + bgu_ref[0]
        glu = jnp.minimum(hgu[:, :D_FF], SWIGLU_LIMIT)
        lin = jnp.clip(hgu[:, D_FF:], -SWIGLU_LIMIT, SWIGLU_LIMIT)
        act = glu * jax.nn.sigmoid(SWIGLU_ALPHA * glu) * (lin + 1.0)
        yb_ref[...] = jnp.dot(act.astype(BF16), wdn_s[...], preferred_element_type=F32) + bdn_ref[0]

    @pl.when(b >= nb_ref[0])
    def _():
        yb_ref[...] = jnp.zeros(yb_ref.shape, F32)


def _experts(block_expert, n_used, xb, w_gu, b_gu, w_dn, b_dn):
    n_blocks = xb.shape[0] // ROW_BLOCK
    row_idx = lambda b, be, nb: (jnp.minimum(b, nb[0] - 1), 0)
    exp_idx = lambda b, be, nb: (be[b], 0, 0)
    in_specs = [
        pl.BlockSpec((ROW_BLOCK, D_MODEL), row_idx),
        pl.BlockSpec((1, D_MODEL, 2 * D_FF), exp_idx),
        pl.BlockSpec((1, 1, 2 * D_FF), exp_idx),
        pl.BlockSpec((1, D_FF, D_MODEL), exp_idx),
        pl.BlockSpec((1, 1, D_MODEL), exp_idx),
    ]
    return pl.pallas_call(
        _experts_kernel,
        grid_spec=pltpu.PrefetchScalarGridSpec(
            num_scalar_prefetch=2, grid=(n_blocks,), in_specs=in_specs,
            out_specs=pl.BlockSpec((ROW_BLOCK, D_MODEL), lambda b, be, nb: (b, 0)),
            scratch_shapes=[pltpu.VMEM((D_MODEL, 2 * D_FF), BF16), pltpu.VMEM((D_FF, D_MODEL), BF16)]),
        out_shape=jax.ShapeDtypeStruct((xb.shape[0], D_MODEL), F32),
        compiler_params=pltpu.CompilerParams(
            dimension_semantics=("arbitrary",), vmem_limit_bytes=VMEM_LIMIT),
        name="moe_experts",
    )(block_expert, n_used, xb, w_gu, b_gu[:, None, :], w_dn, b_dn[:, None, :])


def _combine_kernel(seg_rows_ref, seg_start_ref, seg_off_ref, tile_rows_ref, nb_ref,
                    hp_ref, hs_ref, dest_ref, gate_ref, g_ref, b_ref, yb_ref,
                    yp_ref, ys_ref, ybuf, sem, *, n_prompt_tiles):
    i = pl.program_id(0)
    n = pl.num_programs(0)
    slot = i % 2

    def copy_for(s):
        def make(buf_row, hbm_row, size):
            return pltpu.make_async_copy(yb_ref.at[pl.ds(hbm_row, size), :],
                                         ybuf.at[s, pl.ds(buf_row, size), :], sem.at[s])
        return make

    def gather_tile(tile, s, act):
        def body(e, carry):
            _segment_copies(tile, e, seg_rows_ref, seg_start_ref, seg_off_ref, copy_for(s), act)
            return carry
        lax.fori_loop(0, N_EXPERTS, body, 0)

    @pl.when(i == 0)
    def _():
        ybuf[...] = jnp.zeros(ybuf.shape, F32)
        gather_tile(i, slot, lambda cp: cp.start())

    @pl.when(i + 1 < n)
    def _():
        gather_tile(i + 1, 1 - slot, lambda cp: cp.start())

    _wait_rows(tile_rows_ref[i], copy_for(slot))

    dest = dest_ref[0]
    gate = gate_ref[0]
    rid = lax.broadcasted_iota(jnp.int32, (TILE, TILE_ROWS), 1)
    weights = jnp.zeros((TILE, TILE_ROWS), F32)
    for k in range(TOP_K):
        weights = jnp.where(rid == dest[:, k:k + 1], gate[:, k:k + 1], weights)
    ff = jnp.dot(weights.astype(BF16), ybuf[slot].astype(BF16), preferred_element_type=F32)
    h1 = _tile_tokens(i, n_prompt_tiles, hp_ref, hs_ref)
    y = _layer_norm(DN_ALPHA * h1 + ff, g_ref[...], b_ref[...])

    @pl.when(i < n_prompt_tiles)
    def _():
        yp_ref[...] = y

    @pl.when(i >= n_prompt_tiles)
    def _():
        ys_ref[...] = y[:ys_ref.shape[0]]


def _combine(layout, h1p, h1s, dest_t, gate_t, ln_g, ln_b, yb):
    n_tiles = dest_t.shape[0]
    n_prompt_tiles = h1p.shape[0] // TILE
    prompt_idx = lambda i, *_: (jnp.minimum(i, n_prompt_tiles - 1), 0)
    in_specs = [
        pl.BlockSpec((TILE, D_MODEL), prompt_idx),
        _const_spec(h1s.shape),
        pl.BlockSpec((1, TILE, TOP_K), lambda i, *_: (i, 0, 0)),
        pl.BlockSpec((1, TILE, TOP_K), lambda i, *_: (i, 0, 0)),
        _const_spec(ln_g.shape),
        _const_spec(ln_b.shape),
        pl.BlockSpec(memory_space=pl.ANY),
    ]
    out_shape = (jax.ShapeDtypeStruct(h1p.shape, F32), jax.ShapeDtypeStruct(h1s.shape, F32))
    out_specs = (pl.BlockSpec((TILE, D_MODEL), prompt_idx), _const_spec(h1s.shape))
    return pl.pallas_call(
        functools.partial(_combine_kernel, n_prompt_tiles=n_prompt_tiles),
        grid_spec=pltpu.PrefetchScalarGridSpec(
            num_scalar_prefetch=len(layout), grid=(n_tiles,), in_specs=in_specs, out_specs=out_specs,
            scratch_shapes=[pltpu.VMEM((2, TILE_ROWS, D_MODEL), F32), pltpu.SemaphoreType.DMA((2,))]),
        out_shape=out_shape,
        compiler_params=pltpu.CompilerParams(
            dimension_semantics=("arbitrary",), vmem_limit_bytes=VMEM_LIMIT),
        name="moe_combine",
    )(*layout, h1p, h1s, dest_t, gate_t, ln_g, ln_b, yb)


def _moe_layout(cnt):
    n_tiles = cnt.shape[0]
    seg_rows = (cnt + SEG_ALIGN - 1) // SEG_ALIGN * SEG_ALIGN
    seg_start = jnp.cumsum(seg_rows, axis=1) - seg_rows
    exp_rows = jnp.sum(seg_rows, axis=0)
    exp_blocks = (exp_rows + ROW_BLOCK - 1) // ROW_BLOCK
    blocks_end = jnp.cumsum(exp_blocks)
    exp_base = (blocks_end - exp_blocks) * ROW_BLOCK
    seg_off = exp_base[None, :] + jnp.cumsum(seg_rows, axis=0) - seg_rows
    max_rows = n_tiles * (TOP_K * TILE + N_EXPERTS * (SEG_ALIGN - 1)) + N_EXPERTS * (ROW_BLOCK - SEG_ALIGN)
    n_blocks = -(-max_rows // ROW_BLOCK)
    n_used = blocks_end[-1]
    blk = jnp.minimum(jnp.arange(n_blocks, dtype=jnp.int32), n_used - 1)
    block_expert = jnp.minimum(jnp.sum(blocks_end[None, :] <= blk[:, None], axis=1), N_EXPERTS - 1)
    zero = jnp.zeros((1, N_EXPERTS), seg_rows.dtype)
    tail_rows = (exp_blocks * ROW_BLOCK - exp_rows)[None, :]
    tail_off = (exp_base + exp_rows)[None, :]
    flat = lambda *a: jnp.concatenate(a, axis=0).reshape(-1).astype(jnp.int32)
    layout = (flat(seg_rows, tail_rows), flat(seg_start, zero), flat(seg_off, tail_off),
              jnp.sum(seg_rows, axis=1).astype(jnp.int32), n_used.reshape(1).astype(jnp.int32))
    return (layout, seg_start[:, :, None].astype(jnp.int32), block_expert.astype(jnp.int32),
            n_blocks * ROW_BLOCK)


def _prep_weights(w_in, b_in, conv_w, conv_b, lru_w_a, lru_b_a, lru_w_i, lru_b_i, lru_lambda,
                  w_out, b_out, ln1_g, ln1_b, router_w, router_b):
    qperm = np.concatenate([np.r_[g * HEAD_DIM:(g + 1) * HEAD_DIM,
                                  (g + HEAD_GROUPS) * HEAD_DIM:(g + HEAD_GROUPS + 1) * HEAD_DIM]
                            for g in range(HEAD_GROUPS)])
    in_perm = np.concatenate([qperm, np.arange(Q_COLS, D_IN)])
    out_perm = np.concatenate([qperm, np.arange(Q_COLS, D_MODEL)])

    def diag_tiles(w):
        per = MXU_DIM // (LRU_WIDTH // LRU_BLOCKS)
        w4 = w.reshape(LRU_BLOCKS // per, per, LRU_WIDTH // LRU_BLOCKS, LRU_WIDTH // LRU_BLOCKS)
        t = jnp.einsum("taij,ab->taibj", w4, jnp.eye(per, dtype=w.dtype))
        return t.reshape(LRU_BLOCKS // per, MXU_DIM, MXU_DIM).astype(BF16)

    return (
        w_in[0][:, in_perm].astype(BF16), b_in[0][in_perm][None],
        conv_w[0], conv_b[0][None],
        diag_tiles(lru_w_a[0]), lru_b_a[0].reshape(1, LRU_WIDTH),
        diag_tiles(lru_w_i[0]), lru_b_i[0].reshape(1, LRU_WIDTH),
        lru_lambda[0][None],
        w_out[0][out_perm].astype(BF16), b_out[0][None],
        ln1_g[0][None], ln1_b[0][None],
        router_w[0].T.astype(BF16), router_b[0][:, None],
    )


def kernel(x_prompt, x_sample, cache_win_k, cache_win_v, state_lru_h, state_conv, w_in, b_in, attn_sinks, conv_w, conv_b, lru_w_a, lru_b_a, lru_w_i, lru_b_i, lru_lambda, w_out, b_out, ln1_g, ln1_b, router_w, router_b, w_gate_up, b_gate_up, w_down, b_down, ln2_g, ln2_b):
    assert w_in.shape[0] == 1, "single-layer step"
    B, S, _ = x_prompt.shape
    nseq = x_sample.shape[0]
    assert x_sample.shape[1] == 1 and S % TILE == 0
    wts = _prep_weights(w_in, b_in, conv_w, conv_b, lru_w_a, lru_b_a, lru_w_i, lru_b_i, lru_lambda,
                        w_out, b_out, ln1_g, ln1_b, router_w, router_b)
    sinks = attn_sinks[0]

    h1p, te_p, tg_p, cnt_p, pk, pv, ph, pc = _mixer_prompt(x_prompt, sinks, wts)
    h1s, te_s, tg_s, cnt_s, sk, sv, sh, sc = _mixer_sample(
        x_sample.reshape(nseq, D_MODEL),
        cache_win_k[0].reshape(nseq, WINDOW, KV_COLS), cache_win_v[0].reshape(nseq, WINDOW, KV_COLS),
        state_lru_h[0], jnp.transpose(state_conv[0], (1, 0, 2)), sinks, wts)

    te = jnp.concatenate([te_p, te_s], axis=0)
    tg = jnp.concatenate([tg_p, tg_s], axis=0)
    cnt = jnp.concatenate([cnt_p, cnt_s], axis=0)[:, :, 0]
    layout, segv, block_expert, n_rows = _moe_layout(cnt)

    h1p2 = h1p.reshape(B * S, D_MODEL)
    dest, xb = _dispatch(layout, h1p2, h1s, te, segv, n_rows)
    yb = _experts(block_expert, layout[-1], xb, w_gate_up[0], b_gate_up[0], w_down[0], b_down[0])
    yp, ys = _combine(layout, h1p2, h1s, jnp.transpose(dest, (0, 2, 1)), jnp.transpose(tg, (0, 2, 1)),
                      ln2_g[0][None], ln2_b[0][None], yb)

    kv_shape = (N_KV_HEADS, HEAD_DIM)
    return (
        yp.reshape(B, S, D_MODEL), ys.reshape(nseq, 1, D_MODEL),
        pk.reshape(1, B, WINDOW, *kv_shape), pv.reshape(1, B, WINDOW, *kv_shape),
        ph.reshape(1, B, LRU_WIDTH), pc[None],
        sk.reshape(1, nseq, WINDOW, *kv_shape), sv.reshape(1, nseq, WINDOW, *kv_shape),
        sh[None], jnp.transpose(sc, (1, 0, 2))[None],
    )
```

```python
import functools

import jax
import jax.numpy as jnp
import numpy as np
from jax import lax
from jax.experimental import pallas as pl
from jax.experimental.pallas import tpu as pltpu

F32 = jnp.float32
BF16 = jnp.bfloat16

D_MODEL = 1024
N_Q_HEADS = 8
N_KV_HEADS = 2
HEAD_DIM = 64
WINDOW = 128
Q_COLS = N_Q_HEADS * HEAD_DIM
KV_COLS = N_KV_HEADS * HEAD_DIM
LRU_WIDTH = 512
LRU_BLOCKS = 8
CONV_WIDTH = 4
RG_C = 8.0
N_EXPERTS = 32
TOP_K = 4
D_FF = 1024
SWIGLU_LIMIT = 7.0
SWIGLU_ALPHA = 1.702
LN_EPS = 1e-5
DN_ALPHA = 2.0 ** 0.25
NEG_INF = -1e30
LOG2E = 1.4426950408889634
D_IN = Q_COLS + 2 * KV_COLS + 2 * LRU_WIDTH

LANES = 128
SUBLANES = 8
MXU_DIM = 256

TILE = 256
SEG_ALIGN = SUBLANES
ROW_BLOCK = 512
HEAD_GROUPS = N_Q_HEADS // N_KV_HEADS
VMEM_LIMIT = 56 * 1024 * 1024


def _layer_norm(z, g, b):
    mu = jnp.mean(z, axis=-1, keepdims=True)
    zc = z - mu
    var = jnp.mean(zc * zc, axis=-1, keepdims=True)
    return zc * lax.rsqrt(var + LN_EPS) * g + b


def _softplus(x):
    return jnp.maximum(x, 0.0) + jnp.log1p(jnp.exp(-jnp.abs(x)))


def _gelu_tanh(x):
    c = np.float32(np.sqrt(2.0 / np.pi))
    return 0.5 * x * (1.0 + jnp.tanh(c * (x + 0.044715 * (x * x * x))))


def _rglru_gates(xc, wa_ref, ba_ref, wi_ref, bi_ref, lam_ref):
    xcb = xc.astype(BF16)
    half = LRU_WIDTH // 2
    pre_a = jnp.concatenate(
        [jnp.dot(xcb[:, :half], wa_ref[0], preferred_element_type=F32),
         jnp.dot(xcb[:, half:], wa_ref[1], preferred_element_type=F32)], axis=1)
    pre_i = jnp.concatenate(
        [jnp.dot(xcb[:, :half], wi_ref[0], preferred_element_type=F32),
         jnp.dot(xcb[:, half:], wi_ref[1], preferred_element_type=F32)], axis=1)
    r = jax.nn.sigmoid(pre_a + ba_ref[...])
    gi = jax.nn.sigmoid(pre_i + bi_ref[...])
    log_a = (-RG_C) * r * _softplus(-lam_ref[...])
    a = jnp.exp(log_a)
    t = jnp.tanh(log_a)
    mult = jnp.sqrt(jnp.maximum(-2.0 * t / (1.0 - t), 0.0))
    return a, mult * (gi * xc)


def _route(h1, rw_ref, rb_ref):
    n = h1.shape[0]
    logits = lax.dot_general(rw_ref[...], h1.astype(BF16), (((1,), (1,)), ((), ())),
                             preferred_element_type=F32) + rb_ref[...]
    eidx = lax.broadcasted_iota(jnp.int32, (N_EXPERTS, n), 0)
    vals = logits
    top_v, top_e = [], []
    for _ in range(TOP_K):
        m = jnp.max(vals, axis=0, keepdims=True)
        idx = jnp.min(jnp.where(vals == m, eidx, N_EXPERTS), axis=0, keepdims=True)
        top_v.append(m)
        top_e.append(idx)
        vals = jnp.where(eidx == idx, -jnp.inf, vals)
    ex = [jnp.exp(v - top_v[0]) for v in top_v]
    den = ex[0] + ex[1] + ex[2] + ex[3]
    gates = jnp.concatenate([e / den for e in ex], axis=0)
    te = jnp.concatenate(top_e, axis=0)
    mem = jnp.zeros((N_EXPERTS, n), jnp.int32)
    for idx in top_e:
        mem = mem + jnp.where(eidx == idx, 1, 0)
    return te, gates, jnp.sum(mem, axis=1, keepdims=True)


def _post_mix(x, mix_in, w_out_ref, b_out_ref, g_ref, b_ref):
    mix = jnp.dot(mix_in.astype(BF16), w_out_ref[...], preferred_element_type=F32) + b_out_ref[...]
    return _layer_norm(DN_ALPHA * x + mix, g_ref[...], b_ref[...])


SEQ_PAIR = 2


def _bias_tables(bias_s):
    row2 = lax.broadcasted_iota(jnp.int32, (2 * WINDOW, 2 * WINDOW), 0)
    col2 = lax.broadcasted_iota(jnp.int32, (2 * WINDOW, 2 * WINDOW), 1)
    dist = (row2 & (WINDOW - 1)) - col2 + WINDOW
    valid = (dist >= 0) & (dist <= WINDOW)
    distf = dist.astype(F32)
    for g in range(HEAD_GROUPS):
        slope = jnp.where(row2 >= WINDOW, 2.0 ** -(g + HEAD_GROUPS + 1), 2.0 ** -(g + 1))
        bias = jnp.where(valid, (-LOG2E) * (slope * distf), NEG_INF)
        bias_s[g, 0] = bias
        bias_s[g, 1] = jnp.where(col2 >= WINDOW, bias, NEG_INF)


def _mixer_prompt_kernel(sinks_ref, x_ref, *rest):
    wts = rest[:15]
    h1_ref, te_ref, tg_ref, cnt_ref, wk_ref, wv_ref, lh_ref, cv_ref = rest[15:23]
    kext, vext, uext, hcar, attn_buf, bias_s = rest[23:]
    j = pl.program_id(1)
    nj = pl.num_programs(1)
    R = TILE

    @pl.when(j == 0)
    def _():
        for bb in range(SEQ_PAIR):
            kext[bb, 0:WINDOW, :] = jnp.zeros((WINDOW, KV_COLS), F32)
            vext[bb, 0:WINDOW, :] = jnp.zeros((WINDOW, KV_COLS), F32)
            uext[bb, 0:SUBLANES, :] = jnp.zeros((SUBLANES, LRU_WIDTH), F32)
            hcar[bb] = jnp.zeros((1, LRU_WIDTH), F32)

    @pl.when((pl.program_id(0) == 0) & (j == 0))
    def _():
        _bias_tables(bias_s)

    one = lambda ref, bb: ref.at[pl.ds(bb, 1)]
    h_last = [
        _mixer_tile(j, sinks_ref, one(x_ref, bb), wts,
                    one(h1_ref, bb), te_ref.at[bb], tg_ref.at[bb], cnt_ref.at[bb],
                    kext.at[bb], vext.at[bb], uext.at[bb], hcar.at[bb], attn_buf.at[bb], bias_s)
        for bb in range(SEQ_PAIR)]

    @pl.when(j == nj - 1)
    def _():
        for bb in range(SEQ_PAIR):
            wk_ref[bb] = kext[bb, R:R + WINDOW, :]
            wv_ref[bb] = vext[bb, R:R + WINDOW, :]
            lh_ref[bb] = h_last[bb]
            cv_ref[bb] = uext[bb, SUBLANES + R - (CONV_WIDTH - 1):SUBLANES + R, :]

    for bb in range(SEQ_PAIR):
        kext[bb, 0:WINDOW, :] = kext[bb, R:R + WINDOW, :]
        vext[bb, 0:WINDOW, :] = vext[bb, R:R + WINDOW, :]
        uext[bb, 0:SUBLANES, :] = uext[bb, R:R + SUBLANES, :]


def _mixer_tile(j, sinks_ref, x_ref, wts, h1_ref, te_ref, tg_ref, cnt_ref,
                kext, vext, uext, hcar, attn_buf, bias_s):
    (w_in_ref, b_in_ref, conv_w_ref, conv_b_ref, wa_ref, ba_ref, wi_ref, bi_ref, lam_ref,
     w_out_ref, b_out_ref, ln_g_ref, ln_b_ref, rw_ref, rb_ref) = wts
    R = TILE
    x = x_ref[0]
    proj = jnp.dot(x.astype(BF16), w_in_ref[...], preferred_element_type=F32) + b_in_ref[...]
    q = proj[:, :Q_COLS]
    kext[WINDOW:WINDOW + R, :] = proj[:, Q_COLS:Q_COLS + KV_COLS]
    vext[WINDOW:WINDOW + R, :] = proj[:, Q_COLS + KV_COLS:Q_COLS + 2 * KV_COLS]
    ux = proj[:, Q_COLS + 2 * KV_COLS:Q_COLS + 2 * KV_COLS + LRU_WIDTH]
    ug = proj[:, Q_COLS + 2 * KV_COLS + LRU_WIDTH:]
    uext[SUBLANES:SUBLANES + R, :] = ux

    upper = lax.broadcasted_iota(jnp.int32, (2 * WINDOW, 1), 0) >= WINDOW
    lo_lane = lax.broadcasted_iota(jnp.int32, (WINDOW, LANES), 1) < HEAD_DIM
    qs = q * (HEAD_DIM ** -0.5 * LOG2E)
    for s in range(R // WINDOW):
        kk = kext[s * WINDOW:s * WINDOW + 2 * WINDOW, :].astype(BF16)
        vv = vext[s * WINDOW:s * WINDOW + 2 * WINDOW, :].astype(BF16)
        table = jnp.where(j == 0, 1, 0) if s == 0 else 0
        for g in range(HEAD_GROUPS):
            qg = qs[s * WINDOW:(s + 1) * WINDOW, g * LANES:(g + 1) * LANES]
            q2 = jnp.concatenate([jnp.where(lo_lane, qg, 0.0), jnp.where(lo_lane, 0.0, qg)], axis=0)
            sc = lax.dot_general(q2.astype(BF16), kk, (((1,), (1,)), ((), ())),
                                 preferred_element_type=F32) + bias_s[g, table]
            sink = jnp.where(upper, sinks_ref[g + HEAD_GROUPS], sinks_ref[g]) * LOG2E
            m = jnp.maximum(jnp.max(sc, axis=-1, keepdims=True), sink)
            p = jnp.exp2(sc - m)
            den = jnp.sum(p, axis=-1, keepdims=True) + jnp.exp2(sink - m)
            o = jnp.dot(p.astype(BF16), vv, preferred_element_type=F32) / den
            attn_buf[s * WINDOW:(s + 1) * WINDOW, g * LANES:(g + 1) * LANES] = jnp.where(
                lo_lane, o[:WINDOW], o[WINDOW:])

    xc = conv_b_ref[...]
    for tap in range(CONV_WIDTH):
        off = SUBLANES - (CONV_WIDTH - 1) + tap
        xc = xc + uext[off:off + R, :] * conv_w_ref[tap:tap + 1, :]
    a, bv = _rglru_gates(xc, wa_ref, ba_ref, wi_ref, bi_ref, lam_ref)
    groups = R // SUBLANES
    a3 = a.reshape(groups, SUBLANES, LRU_WIDTH)
    b3 = bv.reshape(groups, SUBLANES, LRU_WIDTH)
    sub = lax.broadcasted_iota(jnp.int32, (groups, SUBLANES, LRU_WIDTH), 1)
    d = 1
    while d < SUBLANES:
        keep = sub >= d
        a_prev = jnp.where(keep, pltpu.roll(a3, d, 1), 1.0)
        b_prev = jnp.where(keep, pltpu.roll(b3, d, 1), 0.0)
        b3 = a3 * b_prev + b3
        a3 = a3 * a_prev
        d *= 2
    h_prev = hcar[...]
    h_groups = []
    for c in range(groups):
        hc = a3[c] * h_prev + b3[c]
        h_groups.append(hc)
        h_prev = hc[SUBLANES - 1:SUBLANES, :]
    h = jnp.concatenate(h_groups, axis=0)
    hcar[...] = h_prev
    rnn = h * _gelu_tanh(ug)

    mix_in = jnp.concatenate([attn_buf[...], rnn], axis=1)
    h1 = _post_mix(x, mix_in, w_out_ref, b_out_ref, ln_g_ref, ln_b_ref)
    h1_ref[0] = h1
    te, tg, cnt = _route(h1, rw_ref, rb_ref)
    te_ref[0] = te
    tg_ref[0] = tg
    cnt_ref[0] = cnt
    return h_prev


def _const_spec(shape):
    return pl.BlockSpec(shape, lambda *_: (0,) * len(shape))


def _mixer_prompt(x, sinks, wts):
    B, S, _ = x.shape
    assert B % SEQ_PAIR == 0 and S % TILE == 0
    nj = S // TILE
    P = SEQ_PAIR
    tile_idx = lambda b, j, *_: (b, j, 0, 0)
    batch_idx = lambda b, j, *_: (b, 0, 0)
    in_specs = [pl.BlockSpec((P, TILE, D_MODEL), lambda b, j, *_: (b, j, 0))]
    in_specs += [_const_spec(w.shape) for w in wts]
    out_shape = (
        jax.ShapeDtypeStruct((B, S, D_MODEL), F32),
        jax.ShapeDtypeStruct((B, nj, TOP_K, TILE), jnp.int32),
        jax.ShapeDtypeStruct((B, nj, TOP_K, TILE), F32),
        jax.ShapeDtypeStruct((B, nj, N_EXPERTS, 1), jnp.int32),
        jax.ShapeDtypeStruct((B, WINDOW, KV_COLS), F32),
        jax.ShapeDtypeStruct((B, WINDOW, KV_COLS), F32),
        jax.ShapeDtypeStruct((B, 1, LRU_WIDTH), F32),
        jax.ShapeDtypeStruct((B, CONV_WIDTH - 1, LRU_WIDTH), F32),
    )
    out_specs = (
        pl.BlockSpec((P, TILE, D_MODEL), lambda b, j, *_: (b, j, 0)),
        pl.BlockSpec((P, 1, TOP_K, TILE), tile_idx),
        pl.BlockSpec((P, 1, TOP_K, TILE), tile_idx),
        pl.BlockSpec((P, 1, N_EXPERTS, 1), tile_idx),
        pl.BlockSpec((P, WINDOW, KV_COLS), batch_idx),
        pl.BlockSpec((P, WINDOW, KV_COLS), batch_idx),
        pl.BlockSpec((P, 1, LRU_WIDTH), batch_idx),
        pl.BlockSpec((P, CONV_WIDTH - 1, LRU_WIDTH), batch_idx),
    )
    scratch = [
        pltpu.VMEM((P, TILE + WINDOW, KV_COLS), F32),
        pltpu.VMEM((P, TILE + WINDOW, KV_COLS), F32),
        pltpu.VMEM((P, TILE + SUBLANES, LRU_WIDTH), F32),
        pltpu.VMEM((P, 1, LRU_WIDTH), F32),
        pltpu.VMEM((P, TILE, Q_COLS), F32),
        pltpu.VMEM((HEAD_GROUPS, 2, 2 * WINDOW, 2 * WINDOW), F32),
    ]
    return pl.pallas_call(
        _mixer_prompt_kernel,
        grid_spec=pltpu.PrefetchScalarGridSpec(
            num_scalar_prefetch=1, grid=(B // P, nj), in_specs=in_specs, out_specs=out_specs,
            scratch_shapes=scratch),
        out_shape=out_shape,
        compiler_params=pltpu.CompilerParams(
            dimension_semantics=("arbitrary", "arbitrary"), vmem_limit_bytes=VMEM_LIMIT),
        name="mixer_prompt",
    )(sinks, x, *wts)


SEQ_CHUNK = 16


def _mixer_sample_kernel(sinks_ref, x_ref, ck_ref, cv_ref, h0_ref, cprev_ref,
                         w_in_ref, b_in_ref, conv_w_ref, conv_b_ref,
                         wa_ref, ba_ref, wi_ref, bi_ref, lam_ref, w_out_ref, b_out_ref,
                         ln_g_ref, ln_b_ref, rw_ref, rb_ref,
                         h1_ref, te_ref, tg_ref, cnt_ref, wk_ref, wv_ref, lh_ref, cnew_ref,
                         proj_s, attn_s, attn_c):
    c = pl.program_id(0)
    nc = pl.num_programs(0)
    nseq = x_ref.shape[0]

    @pl.when(c == 0)
    def _():
        proj_s[...] = jnp.dot(x_ref[...].astype(BF16), w_in_ref[...],
                              preferred_element_type=F32) + b_in_ref[...]

    sub = lax.broadcasted_iota(jnp.int32, (N_Q_HEADS, LANES), 0)
    lane = lax.broadcasted_iota(jnp.int32, (N_Q_HEADS, LANES), 1)
    own_half = (lane < HEAD_DIM) == (sub < HEAD_GROUPS)
    sub1 = sub[:, 0:1]
    slope = jnp.zeros((N_Q_HEADS, 1), F32)
    sink = jnp.zeros((N_Q_HEADS, 1), F32)
    for hd in range(N_Q_HEADS):
        slope = jnp.where(sub1 == hd, 2.0 ** -(hd + 1), slope)
        sink = jnp.where(sub1 == hd, sinks_ref[hd], sink)
    dist = (WINDOW - lax.broadcasted_iota(jnp.int32, (1, WINDOW), 1)).astype(F32)
    lo_row = lax.broadcasted_iota(jnp.int32, (1, LANES), 1) < HEAD_DIM
    scale = HEAD_DIM ** -0.5

    for i in range(SEQ_CHUNK):
        b = c * SEQ_CHUNK + i
        prow = proj_s[pl.ds(b, 1), :]
        q8 = jnp.zeros((N_Q_HEADS, LANES), F32)
        for g in range(HEAD_GROUPS):
            qg = jnp.broadcast_to(prow[:, g * LANES:(g + 1) * LANES], (N_Q_HEADS, LANES))
            q8 = jnp.where(((sub & (HEAD_GROUPS - 1)) == g) & own_half, qg, q8)
        k_new = prow[:, Q_COLS:Q_COLS + KV_COLS]
        v_new = prow[:, Q_COLS + KV_COLS:Q_COLS + 2 * KV_COLS]
        kb = ck_ref[i]
        vb = cv_ref[i]
        sc = lax.dot_general(q8.astype(BF16), kb.astype(BF16), (((1,), (1,)), ((), ())),
                             preferred_element_type=F32) * scale - slope * dist
        sc_new = jnp.sum(q8 * k_new, axis=-1, keepdims=True) * scale
        m = jnp.maximum(jnp.maximum(jnp.max(sc, axis=-1, keepdims=True), sc_new), sink)
        p = jnp.exp(sc - m)
        p_new = jnp.exp(sc_new - m)
        den = jnp.sum(p, axis=-1, keepdims=True) + p_new + jnp.exp(sink - m)
        o = (jnp.dot(p.astype(BF16), vb.astype(BF16), preferred_element_type=F32)
             + p_new * v_new) / den
        for g in range(HEAD_GROUPS):
            attn_c[i:i + 1, g * LANES:(g + 1) * LANES] = jnp.where(
                lo_row, o[g:g + 1, :], o[g + HEAD_GROUPS:g + HEAD_GROUPS + 1, :])
        wk_ref[i, 0:WINDOW - 1, :] = ck_ref[i, 1:WINDOW, :]
        wk_ref[i, WINDOW - 1:WINDOW, :] = k_new
        wv_ref[i, 0:WINDOW - 1, :] = cv_ref[i, 1:WINDOW, :]
        wv_ref[i, WINDOW - 1:WINDOW, :] = v_new
    attn_s[pl.ds(pl.multiple_of(c * SEQ_CHUNK, SEQ_CHUNK), SEQ_CHUNK), :] = attn_c[...]

    @pl.when(c == nc - 1)
    def _():
        x = x_ref[...]
        ux = proj_s[:, Q_COLS + 2 * KV_COLS:Q_COLS + 2 * KV_COLS + LRU_WIDTH]
        ug = proj_s[:, Q_COLS + 2 * KV_COLS + LRU_WIDTH:]
        xc = conv_b_ref[...]
        for tap in range(CONV_WIDTH - 1):
            xc = xc + cprev_ref[tap] * conv_w_ref[tap:tap + 1, :]
        xc = xc + ux * conv_w_ref[CONV_WIDTH - 1:CONV_WIDTH, :]
        a, bv = _rglru_gates(xc, wa_ref, ba_ref, wi_ref, bi_ref, lam_ref)
        h = a * h0_ref[...] + bv
        rnn = h * _gelu_tanh(ug)
        mix_in = jnp.concatenate([attn_s[...], rnn], axis=1)
        h1 = _post_mix(x, mix_in, w_out_ref, b_out_ref, ln_g_ref, ln_b_ref)
        h1_ref[...] = h1
        te, tg, cnt = _route(h1, rw_ref, rb_ref)
        cnt_ref[0] = cnt
        te_ref[0] = jnp.concatenate([te, jnp.full((TOP_K, TILE - nseq), -1, jnp.int32)], axis=1)
        tg_ref[0] = jnp.concatenate([tg, jnp.zeros((TOP_K, TILE - nseq), F32)], axis=1)
        lh_ref[...] = h
        for tap in range(1, CONV_WIDTH - 1):
            cnew_ref[tap - 1] = cprev_ref[tap]
        cnew_ref[CONV_WIDTH - 2] = ux


def _mixer_sample(x, ck, cv, h0, cprev, sinks, wts):
    nseq = x.shape[0]
    assert nseq % SEQ_CHUNK == 0 and nseq <= TILE and nseq % LANES == 0
    nc = nseq // SEQ_CHUNK
    chunk_idx = lambda c, *_: (c, 0, 0)
    in_specs = [
        _const_spec((nseq, D_MODEL)),
        pl.BlockSpec((SEQ_CHUNK, WINDOW, KV_COLS), chunk_idx),
        pl.BlockSpec((SEQ_CHUNK, WINDOW, KV_COLS), chunk_idx),
        _const_spec((nseq, LRU_WIDTH)),
        _const_spec((CONV_WIDTH - 1, nseq, LRU_WIDTH)),
    ] + [_const_spec(w.shape) for w in wts]
    out_shape = (
        jax.ShapeDtypeStruct((nseq, D_MODEL), F32),
        jax.ShapeDtypeStruct((1, TOP_K, TILE), jnp.int32),
        jax.ShapeDtypeStruct((1, TOP_K, TILE), F32),
        jax.ShapeDtypeStruct((1, N_EXPERTS, 1), jnp.int32),
        jax.ShapeDtypeStruct((nseq, WINDOW, KV_COLS), F32),
        jax.ShapeDtypeStruct((nseq, WINDOW, KV_COLS), F32),
        jax.ShapeDtypeStruct((nseq, LRU_WIDTH), F32),
        jax.ShapeDtypeStruct((CONV_WIDTH - 1, nseq, LRU_WIDTH), F32),
    )
    out_specs = (
        _const_spec((nseq, D_MODEL)),
        _const_spec((1, TOP_K, TILE)),
        _const_spec((1, TOP_K, TILE)),
        _const_spec((1, N_EXPERTS, 1)),
        pl.BlockSpec((SEQ_CHUNK, WINDOW, KV_COLS), chunk_idx),
        pl.BlockSpec((SEQ_CHUNK, WINDOW, KV_COLS), chunk_idx),
        _const_spec((nseq, LRU_WIDTH)),
        _const_spec((CONV_WIDTH - 1, nseq, LRU_WIDTH)),
    )
    scratch = [pltpu.VMEM((nseq, D_IN), F32), pltpu.VMEM((nseq, Q_COLS), F32),
               pltpu.VMEM((SEQ_CHUNK, Q_COLS), F32)]
    return pl.pallas_call(
        _mixer_sample_kernel,
        grid_spec=pltpu.PrefetchScalarGridSpec(
            num_scalar_prefetch=1, grid=(nc,), in_specs=in_specs, out_specs=out_specs,
            scratch_shapes=scratch),
        out_shape=out_shape,
        compiler_params=pltpu.CompilerParams(
            dimension_semantics=("arbitrary",), vmem_limit_bytes=VMEM_LIMIT),
        name="mixer_sample",
    )(sinks, x, ck, cv, h0, cprev, *wts)


SEG_BITS = 6
TILE_ROWS = 1280
TOTAL_BITS = 8
assert TILE_ROWS >= TOP_K * TILE + N_EXPERTS * (SEG_ALIGN - 1) and TILE_ROWS % LANES == 0
assert TILE_ROWS < SEG_ALIGN << TOTAL_BITS
assert max(TILE, ROW_BLOCK - SEG_ALIGN) // SEG_ALIGN < 1 << SEG_BITS


def _grouped(rows):
    assert rows % SEG_ALIGN == 0
    return (rows // SEG_ALIGN, SEG_ALIGN, D_MODEL)


def _wait_groups(total, make_copy):
    for bit in range(TOTAL_BITS):
        size = 1 << bit

        @pl.when((total & size) != 0)
        def _():
            make_copy(0, 0, size).wait()


def _segment_copies(tile, e, seg_len_ref, seg_start_ref, seg_off_ref, make_copy, act):
    idx = tile * N_EXPERTS + e
    groups = seg_len_ref[idx]
    start = seg_start_ref[idx]
    off = seg_off_ref[idx]
    for bit in reversed(range(SEG_BITS)):
        size = 1 << bit

        @pl.when((groups & size) != 0)
        def _():
            done = groups & ~(2 * size - 1)
            act(make_copy(start + done, off + done, size))


def _tile_tokens(i, n_prompt_tiles, hp_ref, hs_ref):
    hs = hs_ref[...]
    hs_tile = jnp.concatenate([hs, jnp.zeros((TILE - hs.shape[0], D_MODEL), F32)], axis=0)
    return jnp.where(i < n_prompt_tiles, hp_ref[...], hs_tile)


def _dispatch_kernel(seg_rows_ref, seg_start_ref, seg_off_ref, tile_rows_ref, nb_ref,
                     hp_ref, hs_ref, te_ref, segv_ref,
                     dest_ref, xb_ref, ybuf, zbuf, sem, *, n_prompt_tiles):
    i = pl.program_id(0)
    n = pl.num_programs(0)
    slot = i % 2
    block_groups = ROW_BLOCK // SEG_ALIGN
    n_blocks = xb_ref.shape[0] // block_groups

    def copy_for(s):
        def make(src, dst, size):
            return pltpu.make_async_copy(ybuf.at[s, pl.ds(src, size)], xb_ref.at[pl.ds(dst, size)], sem.at[s])
        return make

    def zero_copy(src, dst, size):
        return pltpu.make_async_copy(zbuf.at[pl.ds(src, size)], xb_ref.at[pl.ds(dst, size)], sem.at[2])

    def zero_block(b):
        return zero_copy(0, b * block_groups, block_groups)

    def zero_fill(act):
        def tails(e, carry):
            _segment_copies(n, e, seg_rows_ref, seg_start_ref, seg_off_ref, zero_copy, act)
            return carry
        lax.fori_loop(0, N_EXPERTS, tails, 0)

        def blocks(b, carry):
            act(zero_block(b))
            return carry
        lax.fori_loop(nb_ref[0], n_blocks, blocks, 0)

    @pl.when(i == 0)
    def _():
        zbuf[...] = jnp.zeros(zbuf.shape, F32)
        zero_fill(lambda cp: cp.start())

    x = _tile_tokens(i, n_prompt_tiles, hp_ref, hs_ref).astype(BF16)
    te = te_ref[0]
    eidx = lax.broadcasted_iota(jnp.int32, (N_EXPERTS, TILE), 0)
    mem = jnp.zeros((N_EXPERTS, TILE), F32)
    for k in range(TOP_K):
        mem = mem + jnp.where(eidx == te[k:k + 1, :], 1.0, 0.0)
    tr = lax.broadcasted_iota(jnp.int32, (TILE, TILE), 0)
    tc = lax.broadcasted_iota(jnp.int32, (TILE, TILE), 1)
    before = jnp.where(tr < tc, 1.0, 0.0).astype(BF16)
    pos = jnp.dot(mem.astype(BF16), before, preferred_element_type=F32)
    base = pos.astype(jnp.int32) + segv_ref[0]
    dests = []
    for k in range(TOP_K):
        d = jnp.sum(jnp.where(eidx == te[k:k + 1, :], base, 0), axis=0, keepdims=True)
        dests.append(jnp.where(te[k:k + 1, :] >= 0, d, -1))
    dest_ref[0] = jnp.concatenate(dests, axis=0)

    rid = lax.broadcasted_iota(jnp.int32, (TILE_ROWS, TILE), 0)
    onehot = jnp.zeros((TILE_ROWS, TILE), F32)
    for k in range(TOP_K):
        onehot = jnp.where(rid == dests[k], 1.0, onehot)
    ybuf[slot] = jnp.dot(onehot.astype(BF16), x, preferred_element_type=F32).reshape(_grouped(TILE_ROWS))

    def start_all(e, carry):
        _segment_copies(i, e, seg_rows_ref, seg_start_ref, seg_off_ref, copy_for(slot),
                        lambda cp: cp.start())
        return carry

    lax.fori_loop(0, N_EXPERTS, start_all, 0)

    @pl.when(i > 0)
    def _():
        _wait_groups(tile_rows_ref[i - 1], copy_for(1 - slot))

    @pl.when(i == n - 1)
    def _():
        _wait_groups(tile_rows_ref[i], copy_for(slot))
        zero_fill(lambda cp: cp.wait())


def _dispatch(layout, h1p, h1s, te, segv, n_rows):
    n_tiles = te.shape[0]
    n_prompt_tiles = h1p.shape[0] // TILE
    tile_idx = lambda i, *_: (i, 0, 0)
    in_specs = [
        pl.BlockSpec((TILE, D_MODEL), lambda i, *_: (jnp.minimum(i, n_prompt_tiles - 1), 0)),
        _const_spec(h1s.shape),
        pl.BlockSpec((1, TOP_K, TILE), tile_idx),
        pl.BlockSpec((1, N_EXPERTS, 1), tile_idx),
    ]
    out_shape = (jax.ShapeDtypeStruct((n_tiles, TOP_K, TILE), jnp.int32),
                 jax.ShapeDtypeStruct(_grouped(n_rows), F32))
    out_specs = (pl.BlockSpec((1, TOP_K, TILE), tile_idx), pl.BlockSpec(memory_space=pl.ANY))
    return pl.pallas_call(
        functools.partial(_dispatch_kernel, n_prompt_tiles=n_prompt_tiles),
        grid_spec=pltpu.PrefetchScalarGridSpec(
            num_scalar_prefetch=len(layout), grid=(n_tiles,), in_specs=in_specs, out_specs=out_specs,
            scratch_shapes=[pltpu.VMEM((2,) + _grouped(TILE_ROWS), F32), pltpu.VMEM(_grouped(ROW_BLOCK), F32),
                            pltpu.SemaphoreType.DMA((3,))]),
        out_shape=out_shape,
        compiler_params=pltpu.CompilerParams(
            dimension_semantics=("arbitrary",), vmem_limit_bytes=VMEM_LIMIT),
        name="moe_dispatch",
    )(*layout, h1p, h1s, te, segv)


def _experts_kernel(be_ref, nxt_ref, nb_ref, xb_ref, wgu_ref, bgu_ref, wdn_ref, bdn_ref, yb_ref,
                    wgu_f, wdn_f, wgu_s, wdn_s, sem):
    b = pl.program_id(0)

    def weight_copies(e):
        return (pltpu.make_async_copy(wgu_ref.at[e], wgu_f, sem.at[0]),
                pltpu.make_async_copy(wdn_ref.at[e], wdn_f, sem.at[1]))

    @pl.when(b == 0)
    def _():
        for cp in weight_copies(be_ref[0]):
            cp.start()

    @pl.when(b < nb_ref[0])
    def _():
        @pl.when((b == 0) | (be_ref[b] != be_ref[jnp.maximum(b - 1, 0)]))
        def _():
            for cp in weight_copies(be_ref[b]):
                cp.wait()
            wgu_s[...] = wgu_f[...].astype(BF16)
            wdn_s[...] = wdn_f[...].astype(BF16)

            @pl.when(nxt_ref[b] >= 0)
            def _():
                for cp in weight_copies(nxt_ref[b]):
                    cp.start()

        hgu = jnp.dot(xb_ref[...].astype(BF16), wgu_s[...], preferred_element_type=F32) + bgu_ref[0]
        glu = jnp.minimum(hgu[:, :D_FF], SWIGLU_LIMIT)
        lin = jnp.clip(hgu[:, D_FF:], -SWIGLU_LIMIT, SWIGLU_LIMIT)
        act = glu * jax.nn.sigmoid(SWIGLU_ALPHA * glu) * (lin + 1.0)
        yb_ref[...] = jnp.dot(act.astype(BF16), wdn_s[...], preferred_element_type=F32) + bdn_ref[0]

    @pl.when(b >= nb_ref[0])
    def _():
        yb_ref[...] = jnp.zeros(yb_ref.shape, F32)


def _experts(block_expert, next_expert, n_used, xb, w_gu, b_gu, w_dn, b_dn):
    n_blocks = xb.shape[0] // ROW_BLOCK
    row_idx = lambda b, be, nxt, nb: (jnp.minimum(b, nb[0] - 1), 0)
    exp_idx = lambda b, be, nxt, nb: (be[b], 0, 0)
    in_specs = [
        pl.BlockSpec((ROW_BLOCK, D_MODEL), row_idx),
        pl.BlockSpec(memory_space=pl.ANY),
        pl.BlockSpec((1, 1, 2 * D_FF), exp_idx),
        pl.BlockSpec(memory_space=pl.ANY),
        pl.BlockSpec((1, 1, D_MODEL), exp_idx),
    ]
    scratch = [pltpu.VMEM((D_MODEL, 2 * D_FF), F32), pltpu.VMEM((D_FF, D_MODEL), F32),
               pltpu.VMEM((D_MODEL, 2 * D_FF), BF16), pltpu.VMEM((D_FF, D_MODEL), BF16),
               pltpu.SemaphoreType.DMA((2,))]
    return pl.pallas_call(
        _experts_kernel,
        grid_spec=pltpu.PrefetchScalarGridSpec(
            num_scalar_prefetch=3, grid=(n_blocks,), in_specs=in_specs,
            out_specs=pl.BlockSpec((ROW_BLOCK, D_MODEL), lambda b, be, nxt, nb: (b, 0)),
            scratch_shapes=scratch),
        out_shape=jax.ShapeDtypeStruct((xb.shape[0], D_MODEL), F32),
        compiler_params=pltpu.CompilerParams(
            dimension_semantics=("arbitrary",), vmem_limit_bytes=VMEM_LIMIT),
        name="moe_experts",
    )(block_expert, next_expert, n_used, xb, w_gu, b_gu[:, None, :], w_dn, b_dn[:, None, :])


def _combine_kernel(seg_rows_ref, seg_start_ref, seg_off_ref, tile_rows_ref, nb_ref,
                    hp_ref, hs_ref, dest_ref, gate_ref, g_ref, b_ref, yb_ref,
                    yp_ref, ys_ref, ybuf, sem, *, n_prompt_tiles):
    i = pl.program_id(0)
    n = pl.num_programs(0)
    slot = i % 2

    def copy_for(s):
        def make(buf, hbm, size):
            return pltpu.make_async_copy(yb_ref.at[pl.ds(hbm, size)], ybuf.at[s, pl.ds(buf, size)], sem.at[s])
        return make

    def gather_tile(tile, s, act):
        def body(e, carry):
            _segment_copies(tile, e, seg_rows_ref, seg_start_ref, seg_off_ref, copy_for(s), act)
            return carry
        lax.fori_loop(0, N_EXPERTS, body, 0)

    @pl.when(i == 0)
    def _():
        ybuf[...] = jnp.zeros(ybuf.shape, F32)
        gather_tile(i, slot, lambda cp: cp.start())

    @pl.when(i + 1 < n)
    def _():
        gather_tile(i + 1, 1 - slot, lambda cp: cp.start())

    _wait_groups(tile_rows_ref[i], copy_for(slot))

    dest = dest_ref[0]
    gate = gate_ref[0]
    rid = lax.broadcasted_iota(jnp.int32, (TILE, TILE_ROWS), 1)
    weights = jnp.zeros((TILE, TILE_ROWS), F32)
    for k in range(TOP_K):
        weights = jnp.where(rid == dest[:, k:k + 1], gate[:, k:k + 1], weights)
    rows = ybuf[slot].reshape(TILE_ROWS, D_MODEL).astype(BF16)
    ff = jnp.dot(weights.astype(BF16), rows, preferred_element_type=F32)
    h1 = _tile_tokens(i, n_prompt_tiles, hp_ref, hs_ref)
    y = _layer_norm(DN_ALPHA * h1 + ff, g_ref[...], b_ref[...])

    @pl.when(i < n_prompt_tiles)
    def _():
        yp_ref[...] = y

    @pl.when(i >= n_prompt_tiles)
    def _():
        ys_ref[...] = y[:ys_ref.shape[0]]


def _combine(layout, h1p, h1s, dest_t, gate_t, ln_g, ln_b, yb):
    n_tiles = dest_t.shape[0]
    n_prompt_tiles = h1p.shape[0] // TILE
    prompt_idx = lambda i, *_: (jnp.minimum(i, n_prompt_tiles - 1), 0)
    in_specs = [
        pl.BlockSpec((TILE, D_MODEL), prompt_idx),
        _const_spec(h1s.shape),
        pl.BlockSpec((1, TILE, TOP_K), lambda i, *_: (i, 0, 0)),
        pl.BlockSpec((1, TILE, TOP_K), lambda i, *_: (i, 0, 0)),
        _const_spec(ln_g.shape),
        _const_spec(ln_b.shape),
        pl.BlockSpec(memory_space=pl.ANY),
    ]
    out_shape = (jax.ShapeDtypeStruct(h1p.shape, F32), jax.ShapeDtypeStruct(h1s.shape, F32))
    out_specs = (pl.BlockSpec((TILE, D_MODEL), prompt_idx), _const_spec(h1s.shape))
    return pl.pallas_call(
        functools.partial(_combine_kernel, n_prompt_tiles=n_prompt_tiles),
        grid_spec=pltpu.PrefetchScalarGridSpec(
            num_scalar_prefetch=len(layout), grid=(n_tiles,), in_specs=in_specs, out_specs=out_specs,
            scratch_shapes=[pltpu.VMEM((2,) + _grouped(TILE_ROWS), F32), pltpu.SemaphoreType.DMA((2,))]),
        out_shape=out_shape,
        compiler_params=pltpu.CompilerParams(
            dimension_semantics=("arbitrary",), vmem_limit_bytes=VMEM_LIMIT),
        name="moe_combine",
    )(*layout, h1p, h1s, dest_t, gate_t, ln_g, ln_b, yb.reshape(_grouped(yb.shape[0])))


def _moe_layout(cnt):
    n_tiles = cnt.shape[0]
    seg_rows = (cnt + SEG_ALIGN - 1) // SEG_ALIGN * SEG_ALIGN
    seg_start = jnp.cumsum(seg_rows, axis=1) - seg_rows
    exp_rows = jnp.sum(seg_rows, axis=0)
    exp_blocks = (exp_rows + ROW_BLOCK - 1) // ROW_BLOCK
    blocks_end = jnp.cumsum(exp_blocks)
    exp_base = (blocks_end - exp_blocks) * ROW_BLOCK
    seg_off = exp_base[None, :] + jnp.cumsum(seg_rows, axis=0) - seg_rows
    max_rows = n_tiles * (TOP_K * TILE + N_EXPERTS * (SEG_ALIGN - 1)) + N_EXPERTS * (ROW_BLOCK - SEG_ALIGN)
    n_blocks = -(-max_rows // ROW_BLOCK)
    n_used = blocks_end[-1]
    blk = jnp.minimum(jnp.arange(n_blocks, dtype=jnp.int32), n_used - 1)
    block_expert = jnp.minimum(jnp.sum(blocks_end[None, :] <= blk[:, None], axis=1), N_EXPERTS - 1)
    eid = jnp.arange(N_EXPERTS, dtype=jnp.int32)
    later_used = (eid[None, :] > eid[:, None]) & (exp_blocks[None, :] > 0)
    next_of = jnp.min(jnp.where(later_used, eid[None, :], N_EXPERTS), axis=1)
    next_of = jnp.where(next_of < N_EXPERTS, next_of, -1)
    next_expert = jnp.sum(jnp.where(block_expert[:, None] == eid[None, :], next_of[None, :], 0), axis=1)
    zero = jnp.zeros((1, N_EXPERTS), seg_rows.dtype)
    tail_rows = (exp_blocks * ROW_BLOCK - exp_rows)[None, :]
    tail_off = (exp_base + exp_rows)[None, :]
    flat = lambda *a: (jnp.concatenate(a, axis=0).reshape(-1) // SEG_ALIGN).astype(jnp.int32)
    layout = (flat(seg_rows, tail_rows), flat(seg_start, zero), flat(seg_off, tail_off),
              (jnp.sum(seg_rows, axis=1) // SEG_ALIGN).astype(jnp.int32), n_used.reshape(1).astype(jnp.int32))
    return (layout, seg_start[:, :, None].astype(jnp.int32), block_expert.astype(jnp.int32),
            next_expert.astype(jnp.int32), n_blocks * ROW_BLOCK)


def _prep_weights(w_in, b_in, conv_w, conv_b, lru_w_a, lru_b_a, lru_w_i, lru_b_i, lru_lambda,
                  w_out, b_out, ln1_g, ln1_b, router_w, router_b):
    def regroup_heads(a):
        rest = a.shape[1:]
        q = a[:Q_COLS].reshape(N_KV_HEADS, HEAD_GROUPS, HEAD_DIM, *rest)
        q = jnp.swapaxes(q, 0, 1).reshape(Q_COLS, *rest)
        return jnp.concatenate([q, a[Q_COLS:]], axis=0)

    def diag_tiles(w):
        per = MXU_DIM // (LRU_WIDTH // LRU_BLOCKS)
        w4 = w.reshape(LRU_BLOCKS // per, per, LRU_WIDTH // LRU_BLOCKS, LRU_WIDTH // LRU_BLOCKS)
        t = jnp.einsum("taij,ab->taibj", w4, jnp.eye(per, dtype=w.dtype))
        return t.reshape(LRU_BLOCKS // per, MXU_DIM, MXU_DIM).astype(BF16)

    return (
        regroup_heads(w_in[0].T).T.astype(BF16), regroup_heads(b_in[0])[None],
        conv_w[0], conv_b[0][None],
        diag_tiles(lru_w_a[0]), lru_b_a[0].reshape(1, LRU_WIDTH),
        diag_tiles(lru_w_i[0]), lru_b_i[0].reshape(1, LRU_WIDTH),
        lru_lambda[0][None],
        regroup_heads(w_out[0]).astype(BF16), b_out[0][None],
        ln1_g[0][None], ln1_b[0][None],
        router_w[0].T.astype(BF16), router_b[0][:, None],
    )


def kernel(x_prompt, x_sample, cache_win_k, cache_win_v, state_lru_h, state_conv, w_in, b_in, attn_sinks, conv_w, conv_b, lru_w_a, lru_b_a, lru_w_i, lru_b_i, lru_lambda, w_out, b_out, ln1_g, ln1_b, router_w, router_b, w_gate_up, b_gate_up, w_down, b_down, ln2_g, ln2_b):
    assert w_in.shape[0] == 1, "single-layer step"
    B, S, _ = x_prompt.shape
    nseq = x_sample.shape[0]
    assert x_sample.shape[1] == 1 and S % TILE == 0
    wts = _prep_weights(w_in, b_in, conv_w, conv_b, lru_w_a, lru_b_a, lru_w_i, lru_b_i, lru_lambda,
                        w_out, b_out, ln1_g, ln1_b, router_w, router_b)
    sinks = attn_sinks[0]

    h1p, te_p, tg_p, cnt_p, pk, pv, ph, pc = _mixer_prompt(x_prompt, sinks, wts)
    h1s, te_s, tg_s, cnt_s, sk, sv, sh, sc = _mixer_sample(
        x_sample.reshape(nseq, D_MODEL),
        cache_win_k[0].reshape(nseq, WINDOW, KV_COLS), cache_win_v[0].reshape(nseq, WINDOW, KV_COLS),
        state_lru_h[0], jnp.transpose(state_conv[0], (1, 0, 2)), sinks, wts)

    per_tile = lambda a: a.reshape(-1, *a.shape[2:])
    te = jnp.concatenate([per_tile(te_p), te_s], axis=0)
    tg = jnp.concatenate([per_tile(tg_p), tg_s], axis=0)
    cnt = jnp.concatenate([per_tile(cnt_p), cnt_s], axis=0)[:, :, 0]
    layout, segv, block_expert, next_expert, n_rows = _moe_layout(cnt)

    h1p2 = h1p.reshape(B * S, D_MODEL)
    dest, xb = _dispatch(layout, h1p2, h1s, te, segv, n_rows)
    yb = _experts(block_expert, next_expert, layout[-1], xb.reshape(n_rows, D_MODEL),
                  w_gate_up[0], b_gate_up[0], w_down[0], b_down[0])
    yp, ys = _combine(layout, h1p2, h1s, jnp.transpose(dest, (0, 2, 1)), jnp.transpose(tg, (0, 2, 1)),
                      ln2_g[0][None], ln2_b[0][None], yb)

    kv_shape = (N_KV_HEADS, HEAD_DIM)
    return (
        yp.reshape(B, S, D_MODEL), ys.reshape(nseq, 1, D_MODEL),
        pk.reshape(1, B, WINDOW, *kv_shape), pv.reshape(1, B, WINDOW, *kv_shape),
        ph.reshape(1, B, LRU_WIDTH), pc[None],
        sk.reshape(1, nseq, WINDOW, *kv_shape), sv.reshape(1, nseq, WINDOW, *kv_shape),
        sh[None], jnp.transpose(sc, (1, 0, 2))[None],
    )
```

```python
import functools

import jax
import jax.numpy as jnp
import numpy as np
from jax import lax
from jax.experimental import pallas as pl
from jax.experimental.pallas import tpu as pltpu

F32 = jnp.float32
BF16 = jnp.bfloat16

D_MODEL = 1024
N_Q_HEADS = 8
N_KV_HEADS = 2
HEAD_DIM = 64
WINDOW = 128
Q_COLS = N_Q_HEADS * HEAD_DIM
KV_COLS = N_KV_HEADS * HEAD_DIM
LRU_WIDTH = 512
LRU_BLOCKS = 8
CONV_WIDTH = 4
RG_C = 8.0
N_EXPERTS = 32
TOP_K = 4
D_FF = 1024
SWIGLU_LIMIT = 7.0
SWIGLU_ALPHA = 1.702
LN_EPS = 1e-5
DN_ALPHA = 2.0 ** 0.25
NEG_INF = -1e30
LOG2E = 1.4426950408889634
D_IN = Q_COLS + 2 * KV_COLS + 2 * LRU_WIDTH

LANES = 128
SUBLANES = 8
MXU_DIM = 256

TILE = 256
SEG_ALIGN = SUBLANES
ROW_BLOCK = 512
HEAD_GROUPS = N_Q_HEADS // N_KV_HEADS
VMEM_LIMIT = 56 * 1024 * 1024


def _layer_norm(z, g, b):
    mu = jnp.mean(z, axis=-1, keepdims=True)
    zc = z - mu
    var = jnp.mean(zc * zc, axis=-1, keepdims=True)
    return zc * lax.rsqrt(var + LN_EPS) * g + b


def _sigmoid(x):
    return 0.5 + 0.5 * jnp.tanh(0.5 * x)


def _softplus(x):
    return jnp.maximum(x, 0.0) + jnp.log1p(jnp.exp(-jnp.abs(x)))


def _gelu_tanh(x):
    c = np.float32(np.sqrt(2.0 / np.pi))
    return 0.5 * x * (1.0 + jnp.tanh(c * (x + 0.044715 * (x * x * x))))


def _rglru_gates(xc, wa_ref, ba_ref, wi_ref, bi_ref, lam_ref):
    xcb = xc.astype(BF16)
    half = LRU_WIDTH // 2
    pre_a = jnp.concatenate(
        [jnp.dot(xcb[:, :half], wa_ref[0], preferred_element_type=F32),
         jnp.dot(xcb[:, half:], wa_ref[1], preferred_element_type=F32)], axis=1)
    pre_i = jnp.concatenate(
        [jnp.dot(xcb[:, :half], wi_ref[0], preferred_element_type=F32),
         jnp.dot(xcb[:, half:], wi_ref[1], preferred_element_type=F32)], axis=1)
    r = _sigmoid(pre_a + ba_ref[...])
    gi = _sigmoid(pre_i + bi_ref[...])
    log_a = (-RG_C) * r * _softplus(-lam_ref[...])
    a = jnp.exp(log_a)
    t = jnp.tanh(log_a)
    mult = jnp.sqrt(jnp.maximum(-2.0 * t / (1.0 - t), 0.0))
    return a, mult * (gi * xc)


def _route(h1, rw_ref, rb_ref):
    n = h1.shape[0]
    logits = lax.dot_general(rw_ref[...], h1.astype(BF16), (((1,), (1,)), ((), ())),
                             preferred_element_type=F32) + rb_ref[...]
    eidx = lax.broadcasted_iota(jnp.int32, (N_EXPERTS, n), 0)
    vals = logits
    top_v, top_e = [], []
    for _ in range(TOP_K):
        m = jnp.max(vals, axis=0, keepdims=True)
        idx = jnp.min(jnp.where(vals == m, eidx, N_EXPERTS), axis=0, keepdims=True)
        top_v.append(m)
        top_e.append(idx)
        vals = jnp.where(eidx == idx, -jnp.inf, vals)
    ex = [jnp.exp(v - top_v[0]) for v in top_v]
    den = ex[0] + ex[1] + ex[2] + ex[3]
    gates = jnp.concatenate([e / den for e in ex], axis=0)
    te = jnp.concatenate(top_e, axis=0)
    mem = jnp.zeros((N_EXPERTS, n), jnp.int32)
    for idx in top_e:
        mem = mem + jnp.where(eidx == idx, 1, 0)
    return te, gates, jnp.sum(mem, axis=1, keepdims=True)


def _post_mix(x, mix_in, w_out_ref, b_out_ref, g_ref, b_ref):
    mix = jnp.dot(mix_in.astype(BF16), w_out_ref[...], preferred_element_type=F32) + b_out_ref[...]
    return _layer_norm(DN_ALPHA * x + mix, g_ref[...], b_ref[...])


SEQ_PAIR = 2
PHASE_LAG = 2


def _interleave(chains, lag):
    results = [None] * len(chains)
    live = list(range(len(chains)))
    rnd = 0
    while live:
        for k in list(live):
            if rnd < lag * k:
                continue
            try:
                next(chains[k])
            except StopIteration as stop:
                results[k] = stop.value
                live.remove(k)
        rnd += 1
    return results


def _bias_tables(bias_s):
    row2 = lax.broadcasted_iota(jnp.int32, (2 * WINDOW, 2 * WINDOW), 0)
    col2 = lax.broadcasted_iota(jnp.int32, (2 * WINDOW, 2 * WINDOW), 1)
    dist = (row2 & (WINDOW - 1)) - col2 + WINDOW
    valid = (dist >= 0) & (dist <= WINDOW)
    distf = dist.astype(F32)
    for g in range(HEAD_GROUPS):
        slope = jnp.where(row2 >= WINDOW, 2.0 ** -(g + HEAD_GROUPS + 1), 2.0 ** -(g + 1))
        bias = jnp.where(valid, (-LOG2E) * (slope * distf), NEG_INF)
        bias_s[g, 0] = bias
        bias_s[g, 1] = jnp.where(col2 >= WINDOW, bias, NEG_INF)


def _mixer_prompt_kernel(sinks_ref, x_ref, *rest):
    wts = rest[:15]
    h1_ref, te_ref, tg_ref, cnt_ref, wk_ref, wv_ref, lh_ref, cv_ref = rest[15:23]
    kext, vext, uext, hcar, attn_buf, bias_s = rest[23:]
    j = pl.program_id(1)
    nj = pl.num_programs(1)
    R = TILE
    one = lambda ref, bb: ref.at[pl.ds(bb, 1)]

    @pl.when(j == 0)
    def _():
        for bb in range(SEQ_PAIR):
            kext[bb, 0:WINDOW, :] = jnp.zeros((WINDOW, KV_COLS), F32)
            vext[bb, 0:WINDOW, :] = jnp.zeros((WINDOW, KV_COLS), F32)
            uext[bb, 0:SUBLANES, :] = jnp.zeros((SUBLANES, LRU_WIDTH), F32)
            hcar[bb] = jnp.zeros((1, LRU_WIDTH), F32)

    @pl.when((pl.program_id(0) == 0) & (j == 0))
    def _():
        _bias_tables(bias_s)

    h_last = _interleave([
        _mixer_tile(j, sinks_ref, one(x_ref, bb), wts,
                    one(h1_ref, bb), te_ref.at[bb], tg_ref.at[bb], cnt_ref.at[bb],
                    kext.at[bb], vext.at[bb], uext.at[bb], hcar.at[bb], attn_buf.at[bb], bias_s)
        for bb in range(SEQ_PAIR)], PHASE_LAG)

    @pl.when(j == nj - 1)
    def _():
        for bb in range(SEQ_PAIR):
            wk_ref[bb] = kext[bb, R:R + WINDOW, :]
            wv_ref[bb] = vext[bb, R:R + WINDOW, :]
            lh_ref[bb] = h_last[bb]
            cv_ref[bb] = uext[bb, SUBLANES + R - (CONV_WIDTH - 1):SUBLANES + R, :]

    for bb in range(SEQ_PAIR):
        kext[bb, 0:WINDOW, :] = kext[bb, R:R + WINDOW, :]
        vext[bb, 0:WINDOW, :] = vext[bb, R:R + WINDOW, :]
        uext[bb, 0:SUBLANES, :] = uext[bb, R:R + SUBLANES, :]


def _mixer_tile(j, sinks_ref, x_ref, wts, h1_ref, te_ref, tg_ref, cnt_ref,
                kext, vext, uext, hcar, attn_buf, bias_s):
    (w_in_ref, b_in_ref, conv_w_ref, conv_b_ref, wa_ref, ba_ref, wi_ref, bi_ref, lam_ref,
     w_out_ref, b_out_ref, ln_g_ref, ln_b_ref, rw_ref, rb_ref) = wts
    R = TILE
    x = x_ref[0]
    proj = jnp.dot(x.astype(BF16), w_in_ref[...], preferred_element_type=F32) + b_in_ref[...]
    q = proj[:, :Q_COLS]
    kext[WINDOW:WINDOW + R, :] = proj[:, Q_COLS:Q_COLS + KV_COLS]
    vext[WINDOW:WINDOW + R, :] = proj[:, Q_COLS + KV_COLS:Q_COLS + 2 * KV_COLS]
    uext[SUBLANES:SUBLANES + R, :] = proj[:, Q_COLS + 2 * KV_COLS:Q_COLS + 2 * KV_COLS + LRU_WIDTH]
    ug = proj[:, Q_COLS + 2 * KV_COLS + LRU_WIDTH:]
    yield

    upper = lax.broadcasted_iota(jnp.int32, (2 * WINDOW, 1), 0) >= WINDOW
    lo_lane = lax.broadcasted_iota(jnp.int32, (WINDOW, LANES), 1) < HEAD_DIM
    qs = q * (HEAD_DIM ** -0.5 * LOG2E)
    for s in range(R // WINDOW):
        kk = kext[s * WINDOW:s * WINDOW + 2 * WINDOW, :].astype(BF16)
        vv = vext[s * WINDOW:s * WINDOW + 2 * WINDOW, :].astype(BF16)
        table = jnp.where(j == 0, 1, 0) if s == 0 else 0
        for g in range(HEAD_GROUPS):
            qg = qs[s * WINDOW:(s + 1) * WINDOW, g * LANES:(g + 1) * LANES]
            q2 = jnp.concatenate([jnp.where(lo_lane, qg, 0.0), jnp.where(lo_lane, 0.0, qg)], axis=0)
            sc = lax.dot_general(q2.astype(BF16), kk, (((1,), (1,)), ((), ())),
                                 preferred_element_type=F32) + bias_s[g, table]
            sink = jnp.where(upper, sinks_ref[g + HEAD_GROUPS], sinks_ref[g]) * LOG2E
            m = jnp.maximum(jnp.max(sc, axis=-1, keepdims=True), sink)
            p = jnp.exp2(sc - m)
            den = jnp.sum(p, axis=-1, keepdims=True) + jnp.exp2(sink - m)
            o = jnp.dot(p.astype(BF16), vv, preferred_element_type=F32) / den
            attn_buf[s * WINDOW:(s + 1) * WINDOW, g * LANES:(g + 1) * LANES] = jnp.where(
                lo_lane, o[:WINDOW], o[WINDOW:])
            yield

    xc = conv_b_ref[...]
    for tap in range(CONV_WIDTH):
        off = SUBLANES - (CONV_WIDTH - 1) + tap
        xc = xc + uext[off:off + R, :] * conv_w_ref[tap:tap + 1, :]
    a, bv = _rglru_gates(xc, wa_ref, ba_ref, wi_ref, bi_ref, lam_ref)
    yield
    groups = R // SUBLANES
    a3 = a.reshape(groups, SUBLANES, LRU_WIDTH)
    b3 = bv.reshape(groups, SUBLANES, LRU_WIDTH)
    sub = lax.broadcasted_iota(jnp.int32, (groups, SUBLANES, LRU_WIDTH), 1)
    d = 1
    while d < SUBLANES:
        keep = sub >= d
        a_prev = jnp.where(keep, pltpu.roll(a3, d, 1), 1.0)
        b_prev = jnp.where(keep, pltpu.roll(b3, d, 1), 0.0)
        b3 = a3 * b_prev + b3
        a3 = a3 * a_prev
        d *= 2
    h_prev = hcar[...]
    h_groups = []
    for c in range(groups):
        hc = a3[c] * h_prev + b3[c]
        h_groups.append(hc)
        h_prev = hc[SUBLANES - 1:SUBLANES, :]
    h = jnp.concatenate(h_groups, axis=0)
    hcar[...] = h_prev
    rnn = h * _gelu_tanh(ug)
    yield

    mix_in = jnp.concatenate([attn_buf[...], rnn], axis=1)
    h1 = _post_mix(x, mix_in, w_out_ref, b_out_ref, ln_g_ref, ln_b_ref)
    h1_ref[0] = h1
    yield
    te, tg, cnt = _route(h1, rw_ref, rb_ref)
    te_ref[0] = te
    tg_ref[0] = tg
    cnt_ref[0] = cnt
    return h_prev


def _const_spec(shape):
    return pl.BlockSpec(shape, lambda *_: (0,) * len(shape))


def _mixer_prompt(x, sinks, wts):
    B, S, _ = x.shape
    assert B % SEQ_PAIR == 0 and S % TILE == 0
    nj = S // TILE
    P = SEQ_PAIR
    tile_idx = lambda b, j, *_: (b, j, 0, 0)
    batch_idx = lambda b, j, *_: (b, 0, 0)
    in_specs = [pl.BlockSpec((P, TILE, D_MODEL), lambda b, j, *_: (b, j, 0))]
    in_specs += [_const_spec(w.shape) for w in wts]
    out_shape = (
        jax.ShapeDtypeStruct((B, S, D_MODEL), F32),
        jax.ShapeDtypeStruct((B, nj, TOP_K, TILE), jnp.int32),
        jax.ShapeDtypeStruct((B, nj, TOP_K, TILE), F32),
        jax.ShapeDtypeStruct((B, nj, N_EXPERTS, 1), jnp.int32),
        jax.ShapeDtypeStruct((B, WINDOW, KV_COLS), F32),
        jax.ShapeDtypeStruct((B, WINDOW, KV_COLS), F32),
        jax.ShapeDtypeStruct((B, 1, LRU_WIDTH), F32),
        jax.ShapeDtypeStruct((B, CONV_WIDTH - 1, LRU_WIDTH), F32),
    )
    out_specs = (
        pl.BlockSpec((P, TILE, D_MODEL), lambda b, j, *_: (b, j, 0)),
        pl.BlockSpec((P, 1, TOP_K, TILE), tile_idx),
        pl.BlockSpec((P, 1, TOP_K, TILE), tile_idx),
        pl.BlockSpec((P, 1, N_EXPERTS, 1), tile_idx),
        pl.BlockSpec((P, WINDOW, KV_COLS), batch_idx),
        pl.BlockSpec((P, WINDOW, KV_COLS), batch_idx),
        pl.BlockSpec((P, 1, LRU_WIDTH), batch_idx),
        pl.BlockSpec((P, CONV_WIDTH - 1, LRU_WIDTH), batch_idx),
    )
    scratch = [
        pltpu.VMEM((P, TILE + WINDOW, KV_COLS), F32),
        pltpu.VMEM((P, TILE + WINDOW, KV_COLS), F32),
        pltpu.VMEM((P, TILE + SUBLANES, LRU_WIDTH), F32),
        pltpu.VMEM((P, 1, LRU_WIDTH), F32),
        pltpu.VMEM((P, TILE, Q_COLS), F32),
        pltpu.VMEM((HEAD_GROUPS, 2, 2 * WINDOW, 2 * WINDOW), F32),
    ]
    return pl.pallas_call(
        _mixer_prompt_kernel,
        grid_spec=pltpu.PrefetchScalarGridSpec(
            num_scalar_prefetch=1, grid=(B // P, nj), in_specs=in_specs, out_specs=out_specs,
            scratch_shapes=scratch),
        out_shape=out_shape,
        compiler_params=pltpu.CompilerParams(
            dimension_semantics=("arbitrary", "arbitrary"), vmem_limit_bytes=VMEM_LIMIT),
        name="mixer_prompt",
    )(sinks, x, *wts)


SEQ_CHUNK = 16


def _mixer_sample_kernel(sinks_ref, x_ref, ck_ref, cv_ref, h0_ref, cprev_ref,
                         w_in_ref, b_in_ref, conv_w_ref, conv_b_ref,
                         wa_ref, ba_ref, wi_ref, bi_ref, lam_ref, w_out_ref, b_out_ref,
                         ln_g_ref, ln_b_ref, rw_ref, rb_ref,
                         h1_ref, te_ref, tg_ref, cnt_ref, wk_ref, wv_ref, lh_ref, cnew_ref,
                         proj_s, attn_s, attn_c):
    c = pl.program_id(0)
    nc = pl.num_programs(0)
    nseq = x_ref.shape[0]

    @pl.when(c == 0)
    def _():
        proj_s[...] = jnp.dot(x_ref[...].astype(BF16), w_in_ref[...],
                              preferred_element_type=F32) + b_in_ref[...]

    sub = lax.broadcasted_iota(jnp.int32, (N_Q_HEADS, LANES), 0)
    lane = lax.broadcasted_iota(jnp.int32, (N_Q_HEADS, LANES), 1)
    own_half = (lane < HEAD_DIM) == (sub < HEAD_GROUPS)
    sub1 = sub[:, 0:1]
    slope = jnp.zeros((N_Q_HEADS, 1), F32)
    sink = jnp.zeros((N_Q_HEADS, 1), F32)
    for hd in range(N_Q_HEADS):
        slope = jnp.where(sub1 == hd, 2.0 ** -(hd + 1), slope)
        sink = jnp.where(sub1 == hd, sinks_ref[hd], sink)
    dist = (WINDOW - lax.broadcasted_iota(jnp.int32, (1, WINDOW), 1)).astype(F32)
    lo_row = lax.broadcasted_iota(jnp.int32, (1, LANES), 1) < HEAD_DIM
    scale = HEAD_DIM ** -0.5

    for i in range(SEQ_CHUNK):
        b = c * SEQ_CHUNK + i
        prow = proj_s[pl.ds(b, 1), :]
        q8 = jnp.zeros((N_Q_HEADS, LANES), F32)
        for g in range(HEAD_GROUPS):
            qg = jnp.broadcast_to(prow[:, g * LANES:(g + 1) * LANES], (N_Q_HEADS, LANES))
            q8 = jnp.where(((sub & (HEAD_GROUPS - 1)) == g) & own_half, qg, q8)
        k_new = prow[:, Q_COLS:Q_COLS + KV_COLS]
        v_new = prow[:, Q_COLS + KV_COLS:Q_COLS + 2 * KV_COLS]
        kb = ck_ref[i]
        vb = cv_ref[i]
        sc = lax.dot_general(q8.astype(BF16), kb.astype(BF16), (((1,), (1,)), ((), ())),
                             preferred_element_type=F32) * scale - slope * dist
        sc_new = jnp.sum(q8 * k_new, axis=-1, keepdims=True) * scale
        m = jnp.maximum(jnp.maximum(jnp.max(sc, axis=-1, keepdims=True), sc_new), sink)
        p = jnp.exp(sc - m)
        p_new = jnp.exp(sc_new - m)
        den = jnp.sum(p, axis=-1, keepdims=True) + p_new + jnp.exp(sink - m)
        o = (jnp.dot(p.astype(BF16), vb.astype(BF16), preferred_element_type=F32)
             + p_new * v_new) / den
        for g in range(HEAD_GROUPS):
            attn_c[i:i + 1, g * LANES:(g + 1) * LANES] = jnp.where(
                lo_row, o[g:g + 1, :], o[g + HEAD_GROUPS:g + HEAD_GROUPS + 1, :])
        wk_ref[i, 0:WINDOW - 1, :] = ck_ref[i, 1:WINDOW, :]
        wk_ref[i, WINDOW - 1:WINDOW, :] = k_new
        wv_ref[i, 0:WINDOW - 1, :] = cv_ref[i, 1:WINDOW, :]
        wv_ref[i, WINDOW - 1:WINDOW, :] = v_new
    attn_s[pl.ds(pl.multiple_of(c * SEQ_CHUNK, SEQ_CHUNK), SEQ_CHUNK), :] = attn_c[...]

    @pl.when(c == nc - 1)
    def _():
        x = x_ref[...]
        ux = proj_s[:, Q_COLS + 2 * KV_COLS:Q_COLS + 2 * KV_COLS + LRU_WIDTH]
        ug = proj_s[:, Q_COLS + 2 * KV_COLS + LRU_WIDTH:]
        xc = conv_b_ref[...]
        for tap in range(CONV_WIDTH - 1):
            xc = xc + cprev_ref[tap] * conv_w_ref[tap:tap + 1, :]
        xc = xc + ux * conv_w_ref[CONV_WIDTH - 1:CONV_WIDTH, :]
        a, bv = _rglru_gates(xc, wa_ref, ba_ref, wi_ref, bi_ref, lam_ref)
        h = a * h0_ref[...] + bv
        rnn = h * _gelu_tanh(ug)
        mix_in = jnp.concatenate([attn_s[...], rnn], axis=1)
        h1 = _post_mix(x, mix_in, w_out_ref, b_out_ref, ln_g_ref, ln_b_ref)
        h1_ref[...] = h1
        te, tg, cnt = _route(h1, rw_ref, rb_ref)
        cnt_ref[0] = cnt
        te_ref[0] = jnp.concatenate([te, jnp.full((TOP_K, TILE - nseq), -1, jnp.int32)], axis=1)
        tg_ref[0] = jnp.concatenate([tg, jnp.zeros((TOP_K, TILE - nseq), F32)], axis=1)
        lh_ref[...] = h
        for tap in range(1, CONV_WIDTH - 1):
            cnew_ref[tap - 1] = cprev_ref[tap]
        cnew_ref[CONV_WIDTH - 2] = ux


def _mixer_sample(x, ck, cv, h0, cprev, sinks, wts):
    nseq = x.shape[0]
    assert nseq % SEQ_CHUNK == 0 and nseq <= TILE and nseq % LANES == 0
    nc = nseq // SEQ_CHUNK
    chunk_idx = lambda c, *_: (c, 0, 0)
    in_specs = [
        _const_spec((nseq, D_MODEL)),
        pl.BlockSpec((SEQ_CHUNK, WINDOW, KV_COLS), chunk_idx),
        pl.BlockSpec((SEQ_CHUNK, WINDOW, KV_COLS), chunk_idx),
        _const_spec((nseq, LRU_WIDTH)),
        _const_spec((CONV_WIDTH - 1, nseq, LRU_WIDTH)),
    ] + [_const_spec(w.shape) for w in wts]
    out_shape = (
        jax.ShapeDtypeStruct((nseq, D_MODEL), F32),
        jax.ShapeDtypeStruct((1, TOP_K, TILE), jnp.int32),
        jax.ShapeDtypeStruct((1, TOP_K, TILE), F32),
        jax.ShapeDtypeStruct((1, N_EXPERTS, 1), jnp.int32),
        jax.ShapeDtypeStruct((nseq, WINDOW, KV_COLS), F32),
        jax.ShapeDtypeStruct((nseq, WINDOW, KV_COLS), F32),
        jax.ShapeDtypeStruct((nseq, LRU_WIDTH), F32),
        jax.ShapeDtypeStruct((CONV_WIDTH - 1, nseq, LRU_WIDTH), F32),
    )
    out_specs = (
        _const_spec((nseq, D_MODEL)),
        _const_spec((1, TOP_K, TILE)),
        _const_spec((1, TOP_K, TILE)),
        _const_spec((1, N_EXPERTS, 1)),
        pl.BlockSpec((SEQ_CHUNK, WINDOW, KV_COLS), chunk_idx),
        pl.BlockSpec((SEQ_CHUNK, WINDOW, KV_COLS), chunk_idx),
        _const_spec((nseq, LRU_WIDTH)),
        _const_spec((CONV_WIDTH - 1, nseq, LRU_WIDTH)),
    )
    scratch = [pltpu.VMEM((nseq, D_IN), F32), pltpu.VMEM((nseq, Q_COLS), F32),
               pltpu.VMEM((SEQ_CHUNK, Q_COLS), F32)]
    return pl.pallas_call(
        _mixer_sample_kernel,
        grid_spec=pltpu.PrefetchScalarGridSpec(
            num_scalar_prefetch=1, grid=(nc,), in_specs=in_specs, out_specs=out_specs,
            scratch_shapes=scratch),
        out_shape=out_shape,
        compiler_params=pltpu.CompilerParams(
            dimension_semantics=("arbitrary",), vmem_limit_bytes=VMEM_LIMIT),
        name="mixer_sample",
    )(sinks, x, ck, cv, h0, cprev, *wts)


TILE_ROWS = 1280
assert TILE_ROWS >= TOP_K * TILE + N_EXPERTS * (SEG_ALIGN - 1) and TILE_ROWS % LANES == 0


def _grouped(rows):
    assert rows % SEG_ALIGN == 0
    return (rows // SEG_ALIGN, SEG_ALIGN, D_MODEL)


def _wait_groups(total, make_copy):
    @pl.when(total > 0)
    def _():
        make_copy(0, 0, total).wait()


def _segment_copies(tile, e, seg_len_ref, seg_start_ref, seg_off_ref, make_copy, act):
    idx = tile * N_EXPERTS + e
    groups = seg_len_ref[idx]

    @pl.when(groups > 0)
    def _():
        act(make_copy(seg_start_ref[idx], seg_off_ref[idx], groups))


def _tile_tokens(i, n_prompt_tiles, hp_ref, hs_ref):
    hs = hs_ref[...]
    hs_tile = jnp.concatenate([hs, jnp.zeros((TILE - hs.shape[0], D_MODEL), F32)], axis=0)
    return jnp.where(i < n_prompt_tiles, hp_ref[...], hs_tile)


def _dispatch_kernel(seg_rows_ref, seg_start_ref, seg_off_ref, tile_rows_ref, nb_ref,
                     hp_ref, hs_ref, te_ref, segv_ref,
                     dest_ref, xb_ref, ybuf, zbuf, sem, *, n_prompt_tiles):
    i = pl.program_id(0)
    n = pl.num_programs(0)
    slot = i % 2
    block_groups = ROW_BLOCK // SEG_ALIGN
    n_blocks = xb_ref.shape[0] // block_groups

    def copy_for(s):
        def make(src, dst, size):
            return pltpu.make_async_copy(ybuf.at[s, pl.ds(src, size)], xb_ref.at[pl.ds(dst, size)], sem.at[s])
        return make

    def zero_copy(src, dst, size):
        return pltpu.make_async_copy(zbuf.at[pl.ds(src, size)], xb_ref.at[pl.ds(dst, size)], sem.at[2])

    def zero_block(b):
        return zero_copy(0, b * block_groups, block_groups)

    def zero_fill(act):
        def tails(e, carry):
            _segment_copies(n, e, seg_rows_ref, seg_start_ref, seg_off_ref, zero_copy, act)
            return carry
        lax.fori_loop(0, N_EXPERTS, tails, 0)

        def blocks(b, carry):
            act(zero_block(b))
            return carry
        lax.fori_loop(nb_ref[0], n_blocks, blocks, 0)

    @pl.when(i == 0)
    def _():
        zbuf[...] = jnp.zeros(zbuf.shape, F32)
        zero_fill(lambda cp: cp.start())

    x = _tile_tokens(i, n_prompt_tiles, hp_ref, hs_ref).astype(BF16)
    te = te_ref[0]
    eidx = lax.broadcasted_iota(jnp.int32, (N_EXPERTS, TILE), 0)
    mem = jnp.zeros((N_EXPERTS, TILE), F32)
    for k in range(TOP_K):
        mem = mem + jnp.where(eidx == te[k:k + 1, :], 1.0, 0.0)
    tr = lax.broadcasted_iota(jnp.int32, (TILE, TILE), 0)
    tc = lax.broadcasted_iota(jnp.int32, (TILE, TILE), 1)
    before = jnp.where(tr < tc, 1.0, 0.0).astype(BF16)
    pos = jnp.dot(mem.astype(BF16), before, preferred_element_type=F32)
    base = pos.astype(jnp.int32) + segv_ref[0]
    dests = []
    for k in range(TOP_K):
        d = jnp.sum(jnp.where(eidx == te[k:k + 1, :], base, 0), axis=0, keepdims=True)
        dests.append(jnp.where(te[k:k + 1, :] >= 0, d, -1))
    dest_ref[0] = jnp.concatenate(dests, axis=0)

    rid = lax.broadcasted_iota(jnp.int32, (TILE_ROWS, TILE), 0)
    onehot = jnp.zeros((TILE_ROWS, TILE), F32)
    for k in range(TOP_K):
        onehot = jnp.where(rid == dests[k], 1.0, onehot)
    ybuf[slot] = jnp.dot(onehot.astype(BF16), x, preferred_element_type=F32).reshape(_grouped(TILE_ROWS))

    def start_all(e, carry):
        _segment_copies(i, e, seg_rows_ref, seg_start_ref, seg_off_ref, copy_for(slot),
                        lambda cp: cp.start())
        return carry

    lax.fori_loop(0, N_EXPERTS, start_all, 0)

    @pl.when(i > 0)
    def _():
        _wait_groups(tile_rows_ref[i - 1], copy_for(1 - slot))

    @pl.when(i == n - 1)
    def _():
        _wait_groups(tile_rows_ref[i], copy_for(slot))
        zero_fill(lambda cp: cp.wait())


def _dispatch(layout, h1p, h1s, te, segv, n_rows):
    n_tiles = te.shape[0]
    n_prompt_tiles = h1p.shape[0] // TILE
    tile_idx = lambda i, *_: (i, 0, 0)
    in_specs = [
        pl.BlockSpec((TILE, D_MODEL), lambda i, *_: (jnp.minimum(i, n_prompt_tiles - 1), 0)),
        _const_spec(h1s.shape),
        pl.BlockSpec((1, TOP_K, TILE), tile_idx),
        pl.BlockSpec((1, N_EXPERTS, 1), tile_idx),
    ]
    out_shape = (jax.ShapeDtypeStruct((n_tiles, TOP_K, TILE), jnp.int32),
                 jax.ShapeDtypeStruct(_grouped(n_rows), F32))
    out_specs = (pl.BlockSpec((1, TOP_K, TILE), tile_idx), pl.BlockSpec(memory_space=pl.ANY))
    return pl.pallas_call(
        functools.partial(_dispatch_kernel, n_prompt_tiles=n_prompt_tiles),
        grid_spec=pltpu.PrefetchScalarGridSpec(
            num_scalar_prefetch=len(layout), grid=(n_tiles,), in_specs=in_specs, out_specs=out_specs,
            scratch_shapes=[pltpu.VMEM((2,) + _grouped(TILE_ROWS), F32), pltpu.VMEM(_grouped(ROW_BLOCK), F32),
                            pltpu.SemaphoreType.DMA((3,))]),
        out_shape=out_shape,
        compiler_params=pltpu.CompilerParams(
            dimension_semantics=("arbitrary",), vmem_limit_bytes=VMEM_LIMIT),
        name="moe_dispatch",
    )(*layout, h1p, h1s, te, segv)


def _experts_kernel(be_ref, nxt_ref, nb_ref, xb_ref, wgu_ref, bgu_ref, wdn_ref, bdn_ref, yb_ref,
                    wgu_f, wdn_f, wgu_s, wdn_s, sem):
    b = pl.program_id(0)

    def weight_copies(e):
        return (pltpu.make_async_copy(wgu_ref.at[e], wgu_f, sem.at[0]),
                pltpu.make_async_copy(wdn_ref.at[e], wdn_f, sem.at[1]))

    @pl.when(b == 0)
    def _():
        for cp in weight_copies(be_ref[0]):
            cp.start()

    @pl.when(b < nb_ref[0])
    def _():
        @pl.when((b == 0) | (be_ref[b] != be_ref[jnp.maximum(b - 1, 0)]))
        def _():
            for cp in weight_copies(be_ref[b]):
                cp.wait()
            wgu_s[...] = wgu_f[...].astype(BF16)
            wdn_s[...] = wdn_f[...].astype(BF16)

            @pl.when(nxt_ref[b] >= 0)
            def _():
                for cp in weight_copies(nxt_ref[b]):
                    cp.start()

        hgu = jnp.dot(xb_ref[...].astype(BF16), wgu_s[...], preferred_element_type=F32) + bgu_ref[0]
        glu = jnp.minimum(hgu[:, :D_FF], SWIGLU_LIMIT)
        lin = jnp.clip(hgu[:, D_FF:], -SWIGLU_LIMIT, SWIGLU_LIMIT)
        act = glu * jax.nn.sigmoid(SWIGLU_ALPHA * glu) * (lin + 1.0)
        yb_ref[...] = jnp.dot(act.astype(BF16), wdn_s[...], preferred_element_type=F32) + bdn_ref[0]

    @pl.when(b >= nb_ref[0])
    def _():
        yb_ref[...] = jnp.zeros(yb_ref.shape, F32)


def _experts(block_expert, next_expert, n_used, xb, w_gu, b_gu, w_dn, b_dn):
    n_blocks = xb.shape[0] // ROW_BLOCK
    row_idx = lambda b, be, nxt, nb: (jnp.minimum(b, nb[0] - 1), 0)
    exp_idx = lambda b, be, nxt, nb: (be[b], 0, 0)
    in_specs = [
        pl.BlockSpec((ROW_BLOCK, D_MODEL), row_idx),
        pl.BlockSpec(memory_space=pl.ANY),
        pl.BlockSpec((1, 1, 2 * D_FF), exp_idx),
        pl.BlockSpec(memory_space=pl.ANY),
        pl.BlockSpec((1, 1, D_MODEL), exp_idx),
    ]
    scratch = [pltpu.VMEM((D_MODEL, 2 * D_FF), F32), pltpu.VMEM((D_FF, D_MODEL), F32),
               pltpu.VMEM((D_MODEL, 2 * D_FF), BF16), pltpu.VMEM((D_FF, D_MODEL), BF16),
               pltpu.SemaphoreType.DMA((2,))]
    return pl.pallas_call(
        _experts_kernel,
        grid_spec=pltpu.PrefetchScalarGridSpec(
            num_scalar_prefetch=3, grid=(n_blocks,), in_specs=in_specs,
            out_specs=pl.BlockSpec((ROW_BLOCK, D_MODEL), lambda b, be, nxt, nb: (b, 0)),
            scratch_shapes=scratch),
        out_shape=jax.ShapeDtypeStruct((xb.shape[0], D_MODEL), F32),
        compiler_params=pltpu.CompilerParams(
            dimension_semantics=("arbitrary",), vmem_limit_bytes=VMEM_LIMIT),
        name="moe_experts",
    )(block_expert, next_expert, n_used, xb, w_gu, b_gu[:, None, :], w_dn, b_dn[:, None, :])


def _combine_kernel(seg_rows_ref, seg_start_ref, seg_off_ref, tile_rows_ref, nb_ref,
                    hp_ref, hs_ref, dest_ref, gate_ref, g_ref, b_ref, yb_ref,
                    yp_ref, ys_ref, ybuf, sem, *, n_prompt_tiles):
    i = pl.program_id(0)
    n = pl.num_programs(0)
    slot = i % 2

    def copy_for(s):
        def make(buf, hbm, size):
            return pltpu.make_async_copy(yb_ref.at[pl.ds(hbm, size)], ybuf.at[s, pl.ds(buf, size)], sem.at[s])
        return make

    def gather_tile(tile, s, act):
        def body(e, carry):
            _segment_copies(tile, e, seg_rows_ref, seg_start_ref, seg_off_ref, copy_for(s), act)
            return carry
        lax.fori_loop(0, N_EXPERTS, body, 0)

    @pl.when(i == 0)
    def _():
        ybuf[...] = jnp.zeros(ybuf.shape, F32)
        gather_tile(i, slot, lambda cp: cp.start())

    @pl.when(i + 1 < n)
    def _():
        gather_tile(i + 1, 1 - slot, lambda cp: cp.start())

    _wait_groups(tile_rows_ref[i], copy_for(slot))

    dest = dest_ref[0]
    gate = gate_ref[0]
    rid = lax.broadcasted_iota(jnp.int32, (TILE_ROWS, TILE), 0)
    weights = jnp.zeros((TILE_ROWS, TILE), F32)
    for k in range(TOP_K):
        weights = jnp.where(rid == dest[k:k + 1, :], gate[k:k + 1, :], weights)
    rows = ybuf[slot].reshape(TILE_ROWS, D_MODEL).astype(BF16)
    ff = lax.dot_general(weights.astype(BF16), rows, (((0,), (0,)), ((), ())),
                         preferred_element_type=F32)
    h1 = _tile_tokens(i, n_prompt_tiles, hp_ref, hs_ref)
    y = _layer_norm(DN_ALPHA * h1 + ff, g_ref[...], b_ref[...])

    @pl.when(i < n_prompt_tiles)
    def _():
        yp_ref[...] = y

    @pl.when(i >= n_prompt_tiles)
    def _():
        ys_ref[...] = y[:ys_ref.shape[0]]


def _combine(layout, h1p, h1s, dest, gate, ln_g, ln_b, yb):
    n_tiles = dest.shape[0]
    n_prompt_tiles = h1p.shape[0] // TILE
    prompt_idx = lambda i, *_: (jnp.minimum(i, n_prompt_tiles - 1), 0)
    in_specs = [
        pl.BlockSpec((TILE, D_MODEL), prompt_idx),
        _const_spec(h1s.shape),
        pl.BlockSpec((1, TOP_K, TILE), lambda i, *_: (i, 0, 0)),
        pl.BlockSpec((1, TOP_K, TILE), lambda i, *_: (i, 0, 0)),
        _const_spec(ln_g.shape),
        _const_spec(ln_b.shape),
        pl.BlockSpec(memory_space=pl.ANY),
    ]
    out_shape = (jax.ShapeDtypeStruct(h1p.shape, F32), jax.ShapeDtypeStruct(h1s.shape, F32))
    out_specs = (pl.BlockSpec((TILE, D_MODEL), prompt_idx), _const_spec(h1s.shape))
    return pl.pallas_call(
        functools.partial(_combine_kernel, n_prompt_tiles=n_prompt_tiles),
        grid_spec=pltpu.PrefetchScalarGridSpec(
            num_scalar_prefetch=len(layout), grid=(n_tiles,), in_specs=in_specs, out_specs=out_specs,
            scratch_shapes=[pltpu.VMEM((2,) + _grouped(TILE_ROWS), F32), pltpu.SemaphoreType.DMA((2,))]),
        out_shape=out_shape,
        compiler_params=pltpu.CompilerParams(
            dimension_semantics=("arbitrary",), vmem_limit_bytes=VMEM_LIMIT),
        name="moe_combine",
    )(*layout, h1p, h1s, dest, gate, ln_g, ln_b, yb.reshape(_grouped(yb.shape[0])))


def _moe_layout(cnt):
    n_tiles = cnt.shape[0]
    seg_rows = (cnt + SEG_ALIGN - 1) // SEG_ALIGN * SEG_ALIGN
    seg_start = jnp.cumsum(seg_rows, axis=1) - seg_rows
    exp_rows = jnp.sum(seg_rows, axis=0)
    exp_blocks = (exp_rows + ROW_BLOCK - 1) // ROW_BLOCK
    blocks_end = jnp.cumsum(exp_blocks)
    exp_base = (blocks_end - exp_blocks) * ROW_BLOCK
    seg_off = exp_base[None, :] + jnp.cumsum(seg_rows, axis=0) - seg_rows
    max_rows = n_tiles * (TOP_K * TILE + N_EXPERTS * (SEG_ALIGN - 1)) + N_EXPERTS * (ROW_BLOCK - SEG_ALIGN)
    n_blocks = -(-max_rows // ROW_BLOCK)
    n_used = blocks_end[-1]
    blk = jnp.minimum(jnp.arange(n_blocks, dtype=jnp.int32), n_used - 1)
    block_expert = jnp.minimum(jnp.sum(blocks_end[None, :] <= blk[:, None], axis=1), N_EXPERTS - 1)
    eid = jnp.arange(N_EXPERTS, dtype=jnp.int32)
    later_used = (eid[None, :] > eid[:, None]) & (exp_blocks[None, :] > 0)
    next_of = jnp.min(jnp.where(later_used, eid[None, :], N_EXPERTS), axis=1)
    next_of = jnp.where(next_of < N_EXPERTS, next_of, -1)
    next_expert = jnp.sum(jnp.where(block_expert[:, None] == eid[None, :], next_of[None, :], 0), axis=1)
    zero = jnp.zeros((1, N_EXPERTS), seg_rows.dtype)
    tail_rows = (exp_blocks * ROW_BLOCK - exp_rows)[None, :]
    tail_off = (exp_base + exp_rows)[None, :]
    flat = lambda *a: (jnp.concatenate(a, axis=0).reshape(-1) // SEG_ALIGN).astype(jnp.int32)
    layout = (flat(seg_rows, tail_rows), flat(seg_start, zero), flat(seg_off, tail_off),
              (jnp.sum(seg_rows, axis=1) // SEG_ALIGN).astype(jnp.int32), n_used.reshape(1).astype(jnp.int32))
    return (layout, seg_start[:, :, None].astype(jnp.int32), block_expert.astype(jnp.int32),
            next_expert.astype(jnp.int32), n_blocks * ROW_BLOCK)


def _prep_weights(w_in, b_in, conv_w, conv_b, lru_w_a, lru_b_a, lru_w_i, lru_b_i, lru_lambda,
                  w_out, b_out, ln1_g, ln1_b, router_w, router_b):
    def regroup_heads(a):
        rest = a.shape[1:]
        q = a[:Q_COLS].reshape(N_KV_HEADS, HEAD_GROUPS, HEAD_DIM, *rest)
        q = jnp.swapaxes(q, 0, 1).reshape(Q_COLS, *rest)
        return jnp.concatenate([q, a[Q_COLS:]], axis=0)

    def diag_tiles(w):
        per = MXU_DIM // (LRU_WIDTH // LRU_BLOCKS)
        w4 = w.reshape(LRU_BLOCKS // per, per, LRU_WIDTH // LRU_BLOCKS, LRU_WIDTH // LRU_BLOCKS)
        t = jnp.einsum("taij,ab->taibj", w4, jnp.eye(per, dtype=w.dtype))
        return t.reshape(LRU_BLOCKS // per, MXU_DIM, MXU_DIM).astype(BF16)

    return (
        regroup_heads(w_in[0].T).T.astype(BF16), regroup_heads(b_in[0])[None],
        conv_w[0], conv_b[0][None],
        diag_tiles(lru_w_a[0]), lru_b_a[0].reshape(1, LRU_WIDTH),
        diag_tiles(lru_w_i[0]), lru_b_i[0].reshape(1, LRU_WIDTH),
        lru_lambda[0][None],
        regroup_heads(w_out[0]).astype(BF16), b_out[0][None],
        ln1_g[0][None], ln1_b[0][None],
        router_w[0].T.astype(BF16), router_b[0][:, None],
    )


def kernel(x_prompt, x_sample, cache_win_k, cache_win_v, state_lru_h, state_conv, w_in, b_in, attn_sinks, conv_w, conv_b, lru_w_a, lru_b_a, lru_w_i, lru_b_i, lru_lambda, w_out, b_out, ln1_g, ln1_b, router_w, router_b, w_gate_up, b_gate_up, w_down, b_down, ln2_g, ln2_b):
    assert w_in.shape[0] == 1, "single-layer step"
    B, S, _ = x_prompt.shape
    nseq = x_sample.shape[0]
    assert x_sample.shape[1] == 1 and S % TILE == 0
    wts = _prep_weights(w_in, b_in, conv_w, conv_b, lru_w_a, lru_b_a, lru_w_i, lru_b_i, lru_lambda,
                        w_out, b_out, ln1_g, ln1_b, router_w, router_b)
    sinks = attn_sinks[0]

    h1p, te_p, tg_p, cnt_p, pk, pv, ph, pc = _mixer_prompt(x_prompt, sinks, wts)
    h1s, te_s, tg_s, cnt_s, sk, sv, sh, sc = _mixer_sample(
        x_sample.reshape(nseq, D_MODEL),
        cache_win_k[0].reshape(nseq, WINDOW, KV_COLS), cache_win_v[0].reshape(nseq, WINDOW, KV_COLS),
        state_lru_h[0], jnp.transpose(state_conv[0], (1, 0, 2)), sinks, wts)

    per_tile = lambda a: a.reshape(-1, *a.shape[2:])
    te = jnp.concatenate([per_tile(te_p), te_s], axis=0)
    tg = jnp.concatenate([per_tile(tg_p), tg_s], axis=0)
    cnt = jnp.concatenate([per_tile(cnt_p), cnt_s], axis=0)[:, :, 0]
    layout, segv, block_expert, next_expert, n_rows = _moe_layout(cnt)

    h1p2 = h1p.reshape(B * S, D_MODEL)
    dest, xb = _dispatch(layout, h1p2, h1s, te, segv, n_rows)
    yb = _experts(block_expert, next_expert, layout[-1], xb.reshape(n_rows, D_MODEL),
                  w_gate_up[0], b_gate_up[0], w_down[0], b_down[0])
    yp, ys = _combine(layout, h1p2, h1s, dest, tg, ln2_g[0][None], ln2_b[0][None], yb)

    kv_shape = (N_KV_HEADS, HEAD_DIM)
    return (
        yp.reshape(B, S, D_MODEL), ys.reshape(nseq, 1, D_MODEL),
        pk.reshape(1, B, WINDOW, *kv_shape), pv.reshape(1, B, WINDOW, *kv_shape),
        ph.reshape(1, B, LRU_WIDTH), pc[None],
        sk.reshape(1, nseq, WINDOW, *kv_shape), sv.reshape(1, nseq, WINDOW, *kv_shape),
        sh[None], jnp.transpose(sc, (1, 0, 2))[None],
    )
```

```python
import functools

import jax
import jax.numpy as jnp
import numpy as np
from jax import lax
from jax.experimental import pallas as pl
from jax.experimental.pallas import tpu as pltpu

F32 = jnp.float32
BF16 = jnp.bfloat16

D_MODEL = 1024
N_Q_HEADS = 8
N_KV_HEADS = 2
HEAD_DIM = 64
WINDOW = 128
Q_COLS = N_Q_HEADS * HEAD_DIM
KV_COLS = N_KV_HEADS * HEAD_DIM
LRU_WIDTH = 512
LRU_BLOCKS = 8
CONV_WIDTH = 4
RG_C = 8.0
N_EXPERTS = 32
TOP_K = 4
D_FF = 1024
SWIGLU_LIMIT = 7.0
SWIGLU_ALPHA = 1.702
LN_EPS = 1e-5
DN_ALPHA = 2.0 ** 0.25
NEG_INF = -1e30
LOG2E = 1.4426950408889634
D_IN = Q_COLS + 2 * KV_COLS + 2 * LRU_WIDTH

LANES = 128
SUBLANES = 8
MXU_DIM = 256

TILE = 256
SEG_ALIGN = SUBLANES
ROW_BLOCK = 512
HEAD_GROUPS = N_Q_HEADS // N_KV_HEADS
VMEM_LIMIT = 56 * 1024 * 1024

TILE_ROWS = 1280
assert TILE_ROWS >= TOP_K * TILE + N_EXPERTS * (SEG_ALIGN - 1) and TILE_ROWS % LANES == 0


def _layer_norm(z, g, b):
    mu = jnp.mean(z, axis=-1, keepdims=True)
    zc = z - mu
    var = jnp.mean(zc * zc, axis=-1, keepdims=True)
    return zc * lax.rsqrt(var + LN_EPS) * g + b


def _sigmoid(x):
    return 0.5 + 0.5 * jnp.tanh(0.5 * x)


def _softplus(x):
    return jnp.maximum(x, 0.0) + jnp.log1p(jnp.exp(-jnp.abs(x)))


def _gelu_tanh(x):
    c = np.float32(np.sqrt(2.0 / np.pi))
    return 0.5 * x * (1.0 + jnp.tanh(c * (x + 0.044715 * (x * x * x))))


def _rglru_gates(xc, wa_ref, ba_ref, wi_ref, bi_ref, lam_ref):
    xcb = xc.astype(BF16)
    half = LRU_WIDTH // 2
    pre_a = jnp.concatenate(
        [jnp.dot(xcb[:, :half], wa_ref[0], preferred_element_type=F32),
         jnp.dot(xcb[:, half:], wa_ref[1], preferred_element_type=F32)], axis=1)
    pre_i = jnp.concatenate(
        [jnp.dot(xcb[:, :half], wi_ref[0], preferred_element_type=F32),
         jnp.dot(xcb[:, half:], wi_ref[1], preferred_element_type=F32)], axis=1)
    r = _sigmoid(pre_a + ba_ref[...])
    gi = _sigmoid(pre_i + bi_ref[...])
    log_a = (-RG_C) * r * _softplus(-lam_ref[...])
    a = jnp.exp(log_a)
    t = jnp.tanh(log_a)
    mult = jnp.sqrt(jnp.maximum(-2.0 * t / (1.0 - t), 0.0))
    return a, mult * (gi * xc)


def _route(h1, rw_ref, rb_ref):
    n = h1.shape[0]
    logits = lax.dot_general(rw_ref[...], h1.astype(BF16), (((1,), (1,)), ((), ())),
                             preferred_element_type=F32) + rb_ref[...]
    eidx = lax.broadcasted_iota(jnp.int32, (N_EXPERTS, n), 0)
    vals = logits
    top_v, top_e = [], []
    for _ in range(TOP_K):
        m = jnp.max(vals, axis=0, keepdims=True)
        idx = jnp.min(jnp.where(vals == m, eidx, N_EXPERTS), axis=0, keepdims=True)
        top_v.append(m)
        top_e.append(idx)
        vals = jnp.where(eidx == idx, -jnp.inf, vals)
    ex = [jnp.exp(v - top_v[0]) for v in top_v]
    den = ex[0] + ex[1] + ex[2] + ex[3]
    gates = jnp.concatenate([e / den for e in ex], axis=0)
    te = jnp.concatenate(top_e, axis=0)
    mem = jnp.zeros((N_EXPERTS, n), jnp.int32)
    for idx in top_e:
        mem = mem + jnp.where(eidx == idx, 1, 0)
    return te, gates, jnp.sum(mem, axis=1, keepdims=True)


def _compact_tile(h1, te, cnt):
    eidx = lax.broadcasted_iota(jnp.int32, (N_EXPERTS, TILE), 0)
    mem = jnp.zeros((N_EXPERTS, TILE), F32)
    for k in range(TOP_K):
        mem = mem + jnp.where(eidx == te[k:k + 1, :], 1.0, 0.0)
    tr = lax.broadcasted_iota(jnp.int32, (TILE, TILE), 0)
    tc = lax.broadcasted_iota(jnp.int32, (TILE, TILE), 1)
    before = jnp.where(tr < tc, 1.0, 0.0).astype(BF16)
    pos = jnp.dot(mem.astype(BF16), before, preferred_element_type=F32)
    seg_rows = ((cnt + (SEG_ALIGN - 1)) // SEG_ALIGN * SEG_ALIGN).astype(F32)
    er = lax.broadcasted_iota(jnp.int32, (N_EXPERTS, N_EXPERTS), 0)
    ec = lax.broadcasted_iota(jnp.int32, (N_EXPERTS, N_EXPERTS), 1)
    earlier = jnp.where(ec < er, 1.0, 0.0).astype(BF16)
    seg_start = jnp.dot(earlier, jnp.broadcast_to(seg_rows, (N_EXPERTS, LANES)).astype(BF16),
                        preferred_element_type=F32)[:, 0:1]
    base = (pos + seg_start).astype(jnp.int32)
    dests = []
    for k in range(TOP_K):
        d = jnp.sum(jnp.where(eidx == te[k:k + 1, :], base, 0), axis=0, keepdims=True)
        dests.append(jnp.where(te[k:k + 1, :] >= 0, d, -1))
    rid = lax.broadcasted_iota(jnp.int32, (TILE_ROWS, TILE), 0)
    onehot = jnp.zeros((TILE_ROWS, TILE), F32)
    for k in range(TOP_K):
        onehot = jnp.where(rid == dests[k], 1.0, onehot)
    picked = jnp.dot(onehot.astype(BF16), h1.astype(BF16), preferred_element_type=F32)
    return picked, jnp.concatenate(dests, axis=0)


def _post_mix(x, mix_in, w_out_ref, b_out_ref, g_ref, b_ref):
    mix = jnp.dot(mix_in.astype(BF16), w_out_ref[...], preferred_element_type=F32) + b_out_ref[...]
    return _layer_norm(DN_ALPHA * x + mix, g_ref[...], b_ref[...])


SEQ_PAIR = 2
PHASE_LAG = 2


def _interleave(chains, lag):
    results = [None] * len(chains)
    live = list(range(len(chains)))
    rnd = 0
    while live:
        for k in list(live):
            if rnd < lag * k:
                continue
            try:
                next(chains[k])
            except StopIteration as stop:
                results[k] = stop.value
                live.remove(k)
        rnd += 1
    return results


def _bias_tables(bias_s):
    row2 = lax.broadcasted_iota(jnp.int32, (2 * WINDOW, 2 * WINDOW), 0)
    col2 = lax.broadcasted_iota(jnp.int32, (2 * WINDOW, 2 * WINDOW), 1)
    dist = (row2 & (WINDOW - 1)) - col2 + WINDOW
    valid = (dist >= 0) & (dist <= WINDOW)
    distf = dist.astype(F32)
    for g in range(HEAD_GROUPS):
        slope = jnp.where(row2 >= WINDOW, 2.0 ** -(g + HEAD_GROUPS + 1), 2.0 ** -(g + 1))
        bias = jnp.where(valid, (-LOG2E) * (slope * distf), NEG_INF)
        bias_s[g, 0] = bias
        bias_s[g, 1] = jnp.where(col2 >= WINDOW, bias, NEG_INF)


def _mixer_prompt_kernel(sinks_ref, x_ref, *rest):
    wts = rest[:15]
    h1_ref, xt_ref, dest_ref, tg_ref, cnt_ref, wk_ref, wv_ref, lh_ref, cv_ref = rest[15:24]
    kext, vext, uext, hcar, attn_buf, bias_s = rest[24:]
    j = pl.program_id(1)
    nj = pl.num_programs(1)
    R = TILE
    one = lambda ref, bb: ref.at[pl.ds(bb, 1)]

    @pl.when(j == 0)
    def _():
        for bb in range(SEQ_PAIR):
            kext[bb, 0:WINDOW, :] = jnp.zeros((WINDOW, KV_COLS), F32)
            vext[bb, 0:WINDOW, :] = jnp.zeros((WINDOW, KV_COLS), F32)
            uext[bb, 0:SUBLANES, :] = jnp.zeros((SUBLANES, LRU_WIDTH), F32)
            hcar[bb] = jnp.zeros((1, LRU_WIDTH), F32)

    @pl.when((pl.program_id(0) == 0) & (j == 0))
    def _():
        _bias_tables(bias_s)

    per_tile = lambda ref, bb: ref.at[0, pl.ds(bb, 1)]
    h_last = _interleave([
        _mixer_tile(j, sinks_ref, one(x_ref, bb), wts, one(h1_ref, bb), one(xt_ref, bb),
                    per_tile(dest_ref, bb), per_tile(tg_ref, bb), per_tile(cnt_ref, bb),
                    kext.at[bb], vext.at[bb], uext.at[bb], hcar.at[bb], attn_buf.at[bb], bias_s)
        for bb in range(SEQ_PAIR)], PHASE_LAG)

    @pl.when(j == nj - 1)
    def _():
        for bb in range(SEQ_PAIR):
            wk_ref[bb] = kext[bb, R:R + WINDOW, :]
            wv_ref[bb] = vext[bb, R:R + WINDOW, :]
            lh_ref[bb] = h_last[bb]
            cv_ref[bb] = uext[bb, SUBLANES + R - (CONV_WIDTH - 1):SUBLANES + R, :]

    for bb in range(SEQ_PAIR):
        kext[bb, 0:WINDOW, :] = kext[bb, R:R + WINDOW, :]
        vext[bb, 0:WINDOW, :] = vext[bb, R:R + WINDOW, :]
        uext[bb, 0:SUBLANES, :] = uext[bb, R:R + SUBLANES, :]


def _mixer_tile(j, sinks_ref, x_ref, wts, h1_ref, xt_ref, dest_ref, tg_ref, cnt_ref,
                kext, vext, uext, hcar, attn_buf, bias_s):
    (w_in_ref, b_in_ref, conv_w_ref, conv_b_ref, wa_ref, ba_ref, wi_ref, bi_ref, lam_ref,
     w_out_ref, b_out_ref, ln_g_ref, ln_b_ref, rw_ref, rb_ref) = wts
    R = TILE
    x = x_ref[0]
    proj = jnp.dot(x.astype(BF16), w_in_ref[...], preferred_element_type=F32) + b_in_ref[...]
    q = proj[:, :Q_COLS]
    kext[WINDOW:WINDOW + R, :] = proj[:, Q_COLS:Q_COLS + KV_COLS]
    vext[WINDOW:WINDOW + R, :] = proj[:, Q_COLS + KV_COLS:Q_COLS + 2 * KV_COLS]
    uext[SUBLANES:SUBLANES + R, :] = proj[:, Q_COLS + 2 * KV_COLS:Q_COLS + 2 * KV_COLS + LRU_WIDTH]
    ug = proj[:, Q_COLS + 2 * KV_COLS + LRU_WIDTH:]
    yield

    upper = lax.broadcasted_iota(jnp.int32, (2 * WINDOW, 1), 0) >= WINDOW
    lo_lane = lax.broadcasted_iota(jnp.int32, (WINDOW, LANES), 1) < HEAD_DIM
    qs = q * (HEAD_DIM ** -0.5 * LOG2E)
    for s in range(R // WINDOW):
        kk = kext[s * WINDOW:s * WINDOW + 2 * WINDOW, :].astype(BF16)
        vv = vext[s * WINDOW:s * WINDOW + 2 * WINDOW, :].astype(BF16)
        table = jnp.where(j == 0, 1, 0) if s == 0 else 0
        for g in range(HEAD_GROUPS):
            qg = qs[s * WINDOW:(s + 1) * WINDOW, g * LANES:(g + 1) * LANES]
            q2 = jnp.concatenate([jnp.where(lo_lane, qg, 0.0), jnp.where(lo_lane, 0.0, qg)], axis=0)
            sc = lax.dot_general(q2.astype(BF16), kk, (((1,), (1,)), ((), ())),
                                 preferred_element_type=F32) + bias_s[g, table]
            sink = jnp.where(upper, sinks_ref[g + HEAD_GROUPS], sinks_ref[g]) * LOG2E
            m = jnp.maximum(jnp.max(sc, axis=-1, keepdims=True), sink)
            p = jnp.exp2(sc - m)
            den = jnp.sum(p, axis=-1, keepdims=True) + jnp.exp2(sink - m)
            o = jnp.dot(p.astype(BF16), vv, preferred_element_type=F32) / den
            attn_buf[s * WINDOW:(s + 1) * WINDOW, g * LANES:(g + 1) * LANES] = jnp.where(
                lo_lane, o[:WINDOW], o[WINDOW:])
            yield

    xc = conv_b_ref[...]
    for tap in range(CONV_WIDTH):
        off = SUBLANES - (CONV_WIDTH - 1) + tap
        xc = xc + uext[off:off + R, :] * conv_w_ref[tap:tap + 1, :]
    a, bv = _rglru_gates(xc, wa_ref, ba_ref, wi_ref, bi_ref, lam_ref)
    yield
    groups = R // SUBLANES
    a3 = a.reshape(groups, SUBLANES, LRU_WIDTH)
    b3 = bv.reshape(groups, SUBLANES, LRU_WIDTH)
    sub = lax.broadcasted_iota(jnp.int32, (groups, SUBLANES, LRU_WIDTH), 1)
    d = 1
    while d < SUBLANES:
        keep = sub >= d
        a_prev = jnp.where(keep, pltpu.roll(a3, d, 1), 1.0)
        b_prev = jnp.where(keep, pltpu.roll(b3, d, 1), 0.0)
        b3 = a3 * b_prev + b3
        a3 = a3 * a_prev
        d *= 2
    h_prev = hcar[...]
    h_groups = []
    for c in range(groups):
        hc = a3[c] * h_prev + b3[c]
        h_groups.append(hc)
        h_prev = hc[SUBLANES - 1:SUBLANES, :]
    h = jnp.concatenate(h_groups, axis=0)
    hcar[...] = h_prev
    rnn = h * _gelu_tanh(ug)
    yield

    mix_in = jnp.concatenate([attn_buf[...], rnn], axis=1)
    h1 = _post_mix(x, mix_in, w_out_ref, b_out_ref, ln_g_ref, ln_b_ref)
    h1_ref[0] = h1
    yield
    te, tg, cnt = _route(h1, rw_ref, rb_ref)
    tg_ref[0] = tg
    cnt_ref[0] = cnt
    yield
    picked, dest = _compact_tile(h1, te, cnt)
    xt_ref[0] = picked
    dest_ref[0] = dest
    return h_prev


def _const_spec(shape):
    return pl.BlockSpec(shape, lambda *_: (0,) * len(shape))


def _mixer_prompt(x, sinks, wts):
    B, S, _ = x.shape
    assert B % SEQ_PAIR == 0 and S % TILE == 0
    nj = S // TILE
    P = SEQ_PAIR
    tile_idx = lambda b, j, *_: (j, b, 0, 0)
    batch_idx = lambda b, j, *_: (b, 0, 0)
    in_specs = [pl.BlockSpec((P, TILE, D_MODEL), lambda b, j, *_: (b, j, 0))]
    in_specs += [_const_spec(w.shape) for w in wts]
    out_shape = (
        jax.ShapeDtypeStruct((B, S, D_MODEL), F32),
        jax.ShapeDtypeStruct((B * nj, TILE_ROWS, D_MODEL), F32),
        jax.ShapeDtypeStruct((nj, B, TOP_K, TILE), jnp.int32),
        jax.ShapeDtypeStruct((nj, B, TOP_K, TILE), F32),
        jax.ShapeDtypeStruct((nj, B, N_EXPERTS, 1), jnp.int32),
        jax.ShapeDtypeStruct((B, WINDOW, KV_COLS), F32),
        jax.ShapeDtypeStruct((B, WINDOW, KV_COLS), F32),
        jax.ShapeDtypeStruct((B, 1, LRU_WIDTH), F32),
        jax.ShapeDtypeStruct((B, CONV_WIDTH - 1, LRU_WIDTH), F32),
    )
    out_specs = (
        pl.BlockSpec((P, TILE, D_MODEL), lambda b, j, *_: (b, j, 0)),
        pl.BlockSpec((P, TILE_ROWS, D_MODEL), lambda b, j, *_: (j * (B // P) + b, 0, 0)),
        pl.BlockSpec((1, P, TOP_K, TILE), tile_idx),
        pl.BlockSpec((1, P, TOP_K, TILE), tile_idx),
        pl.BlockSpec((1, P, N_EXPERTS, 1), tile_idx),
        pl.BlockSpec((P, WINDOW, KV_COLS), batch_idx),
        pl.BlockSpec((P, WINDOW, KV_COLS), batch_idx),
        pl.BlockSpec((P, 1, LRU_WIDTH), batch_idx),
        pl.BlockSpec((P, CONV_WIDTH - 1, LRU_WIDTH), batch_idx),
    )
    scratch = [
        pltpu.VMEM((P, TILE + WINDOW, KV_COLS), F32),
        pltpu.VMEM((P, TILE + WINDOW, KV_COLS), F32),
        pltpu.VMEM((P, TILE + SUBLANES, LRU_WIDTH), F32),
        pltpu.VMEM((P, 1, LRU_WIDTH), F32),
        pltpu.VMEM((P, TILE, Q_COLS), F32),
        pltpu.VMEM((HEAD_GROUPS, 2, 2 * WINDOW, 2 * WINDOW), F32),
    ]
    return pl.pallas_call(
        _mixer_prompt_kernel,
        grid_spec=pltpu.PrefetchScalarGridSpec(
            num_scalar_prefetch=1, grid=(B // P, nj), in_specs=in_specs, out_specs=out_specs,
            scratch_shapes=scratch),
        out_shape=out_shape,
        compiler_params=pltpu.CompilerParams(
            dimension_semantics=("arbitrary", "arbitrary"), vmem_limit_bytes=VMEM_LIMIT),
        name="mixer_prompt",
    )(sinks, x, *wts)


SEQ_CHUNK = 16


def _mixer_sample_kernel(sinks_ref, x_ref, ck_ref, cv_ref, h0_ref, cprev_ref,
                         w_in_ref, b_in_ref, conv_w_ref, conv_b_ref,
                         wa_ref, ba_ref, wi_ref, bi_ref, lam_ref, w_out_ref, b_out_ref,
                         ln_g_ref, ln_b_ref, rw_ref, rb_ref,
                         h1_ref, xt_ref, dest_ref, tg_ref, cnt_ref, wk_ref, wv_ref, lh_ref, cnew_ref,
                         proj_s, attn_s, attn_c):
    c = pl.program_id(0)
    nc = pl.num_programs(0)
    nseq = x_ref.shape[0]

    @pl.when(c == 0)
    def _():
        proj_s[...] = jnp.dot(x_ref[...].astype(BF16), w_in_ref[...],
                              preferred_element_type=F32) + b_in_ref[...]

    sub = lax.broadcasted_iota(jnp.int32, (N_Q_HEADS, LANES), 0)
    lane = lax.broadcasted_iota(jnp.int32, (N_Q_HEADS, LANES), 1)
    own_half = (lane < HEAD_DIM) == (sub < HEAD_GROUPS)
    sub1 = sub[:, 0:1]
    slope = jnp.zeros((N_Q_HEADS, 1), F32)
    sink = jnp.zeros((N_Q_HEADS, 1), F32)
    for hd in range(N_Q_HEADS):
        slope = jnp.where(sub1 == hd, 2.0 ** -(hd + 1), slope)
        sink = jnp.where(sub1 == hd, sinks_ref[hd], sink)
    dist = (WINDOW - lax.broadcasted_iota(jnp.int32, (1, WINDOW), 1)).astype(F32)
    lo_row = lax.broadcasted_iota(jnp.int32, (1, LANES), 1) < HEAD_DIM
    scale = HEAD_DIM ** -0.5

    for i in range(SEQ_CHUNK):
        b = c * SEQ_CHUNK + i
        prow = proj_s[pl.ds(b, 1), :]
        q8 = jnp.zeros((N_Q_HEADS, LANES), F32)
        for g in range(HEAD_GROUPS):
            qg = jnp.broadcast_to(prow[:, g * LANES:(g + 1) * LANES], (N_Q_HEADS, LANES))
            q8 = jnp.where(((sub & (HEAD_GROUPS - 1)) == g) & own_half, qg, q8)
        k_new = prow[:, Q_COLS:Q_COLS + KV_COLS]
        v_new = prow[:, Q_COLS + KV_COLS:Q_COLS + 2 * KV_COLS]
        kb = ck_ref[i]
        vb = cv_ref[i]
        sc = lax.dot_general(q8.astype(BF16), kb.astype(BF16), (((1,), (1,)), ((), ())),
                             preferred_element_type=F32) * scale - slope * dist
        sc_new = jnp.sum(q8 * k_new, axis=-1, keepdims=True) * scale
        m = jnp.maximum(jnp.maximum(jnp.max(sc, axis=-1, keepdims=True), sc_new), sink)
        p = jnp.exp(sc - m)
        p_new = jnp.exp(sc_new - m)
        den = jnp.sum(p, axis=-1, keepdims=True) + p_new + jnp.exp(sink - m)
        o = (jnp.dot(p.astype(BF16), vb.astype(BF16), preferred_element_type=F32)
             + p_new * v_new) / den
        for g in range(HEAD_GROUPS):
            attn_c[i:i + 1, g * LANES:(g + 1) * LANES] = jnp.where(
                lo_row, o[g:g + 1, :], o[g + HEAD_GROUPS:g + HEAD_GROUPS + 1, :])
        wk_ref[i, 0:WINDOW - 1, :] = ck_ref[i, 1:WINDOW, :]
        wk_ref[i, WINDOW - 1:WINDOW, :] = k_new
        wv_ref[i, 0:WINDOW - 1, :] = cv_ref[i, 1:WINDOW, :]
        wv_ref[i, WINDOW - 1:WINDOW, :] = v_new
    attn_s[pl.ds(pl.multiple_of(c * SEQ_CHUNK, SEQ_CHUNK), SEQ_CHUNK), :] = attn_c[...]

    @pl.when(c == nc - 1)
    def _():
        x = x_ref[...]
        ux = proj_s[:, Q_COLS + 2 * KV_COLS:Q_COLS + 2 * KV_COLS + LRU_WIDTH]
        ug = proj_s[:, Q_COLS + 2 * KV_COLS + LRU_WIDTH:]
        xc = conv_b_ref[...]
        for tap in range(CONV_WIDTH - 1):
            xc = xc + cprev_ref[tap] * conv_w_ref[tap:tap + 1, :]
        xc = xc + ux * conv_w_ref[CONV_WIDTH - 1:CONV_WIDTH, :]
        a, bv = _rglru_gates(xc, wa_ref, ba_ref, wi_ref, bi_ref, lam_ref)
        h = a * h0_ref[...] + bv
        rnn = h * _gelu_tanh(ug)
        mix_in = jnp.concatenate([attn_s[...], rnn], axis=1)
        h1 = _post_mix(x, mix_in, w_out_ref, b_out_ref, ln_g_ref, ln_b_ref)
        h1_ref[...] = h1
        te, tg, cnt = _route(h1, rw_ref, rb_ref)
        cnt_ref[0] = cnt
        tg_ref[0] = jnp.concatenate([tg, jnp.zeros((TOP_K, TILE - nseq), F32)], axis=1)
        te_tile = jnp.concatenate([te, jnp.full((TOP_K, TILE - nseq), -1, jnp.int32)], axis=1)
        h1_tile = jnp.concatenate([h1, jnp.zeros((TILE - nseq, D_MODEL), F32)], axis=0)
        picked, dest = _compact_tile(h1_tile, te_tile, cnt)
        xt_ref[0] = picked
        dest_ref[0] = dest
        lh_ref[...] = h
        for tap in range(1, CONV_WIDTH - 1):
            cnew_ref[tap - 1] = cprev_ref[tap]
        cnew_ref[CONV_WIDTH - 2] = ux


def _mixer_sample(x, ck, cv, h0, cprev, sinks, wts):
    nseq = x.shape[0]
    assert nseq % SEQ_CHUNK == 0 and nseq <= TILE and nseq % LANES == 0
    nc = nseq // SEQ_CHUNK
    chunk_idx = lambda c, *_: (c, 0, 0)
    in_specs = [
        _const_spec((nseq, D_MODEL)),
        pl.BlockSpec((SEQ_CHUNK, WINDOW, KV_COLS), chunk_idx),
        pl.BlockSpec((SEQ_CHUNK, WINDOW, KV_COLS), chunk_idx),
        _const_spec((nseq, LRU_WIDTH)),
        _const_spec((CONV_WIDTH - 1, nseq, LRU_WIDTH)),
    ] + [_const_spec(w.shape) for w in wts]
    out_shape = (
        jax.ShapeDtypeStruct((nseq, D_MODEL), F32),
        jax.ShapeDtypeStruct((1, TILE_ROWS, D_MODEL), F32),
        jax.ShapeDtypeStruct((1, TOP_K, TILE), jnp.int32),
        jax.ShapeDtypeStruct((1, TOP_K, TILE), F32),
        jax.ShapeDtypeStruct((1, N_EXPERTS, 1), jnp.int32),
        jax.ShapeDtypeStruct((nseq, WINDOW, KV_COLS), F32),
        jax.ShapeDtypeStruct((nseq, WINDOW, KV_COLS), F32),
        jax.ShapeDtypeStruct((nseq, LRU_WIDTH), F32),
        jax.ShapeDtypeStruct((CONV_WIDTH - 1, nseq, LRU_WIDTH), F32),
    )
    out_specs = (
        _const_spec((nseq, D_MODEL)),
        _const_spec((1, TILE_ROWS, D_MODEL)),
        _const_spec((1, TOP_K, TILE)),
        _const_spec((1, TOP_K, TILE)),
        _const_spec((1, N_EXPERTS, 1)),
        pl.BlockSpec((SEQ_CHUNK, WINDOW, KV_COLS), chunk_idx),
        pl.BlockSpec((SEQ_CHUNK, WINDOW, KV_COLS), chunk_idx),
        _const_spec((nseq, LRU_WIDTH)),
        _const_spec((CONV_WIDTH - 1, nseq, LRU_WIDTH)),
    )
    scratch = [pltpu.VMEM((nseq, D_IN), F32), pltpu.VMEM((nseq, Q_COLS), F32),
               pltpu.VMEM((SEQ_CHUNK, Q_COLS), F32)]
    return pl.pallas_call(
        _mixer_sample_kernel,
        grid_spec=pltpu.PrefetchScalarGridSpec(
            num_scalar_prefetch=1, grid=(nc,), in_specs=in_specs, out_specs=out_specs,
            scratch_shapes=scratch),
        out_shape=out_shape,
        compiler_params=pltpu.CompilerParams(
            dimension_semantics=("arbitrary",), vmem_limit_bytes=VMEM_LIMIT),
        name="mixer_sample",
    )(sinks, x, ck, cv, h0, cprev, *wts)


def _grouped(rows):
    assert rows % SEG_ALIGN == 0
    return (rows // SEG_ALIGN, SEG_ALIGN, D_MODEL)


BLOCK_GROUPS = ROW_BLOCK // SEG_ALIGN
TILE_GROUPS = TILE_ROWS // SEG_ALIGN


def _experts_kernel(be_ref, nxt_ref, nb_ref, tlo_ref, thi_ref, valid_ref,
                    seg_len_ref, seg_src_ref, seg_off_ref,
                    xtp_ref, xts_ref, wgu_ref, bgu_ref, wdn_ref, bdn_ref, yb_ref,
                    xbuf, wgu_f, wdn_f, wgu_s, wdn_s, wsem, xsem, *, n_tiles, n_prompt_tiles):
    b = pl.program_id(0)
    slot = b % 2

    def weight_copies(e):
        return (pltpu.make_async_copy(wgu_ref.at[e], wgu_f, wsem.at[0]),
                pltpu.make_async_copy(wdn_ref.at[e], wdn_f, wsem.at[1]))

    def start_gather(blk, s):
        lo_b = blk * BLOCK_GROUPS
        first = be_ref[blk] * n_tiles

        t_lo = tlo_ref[blk]
        t_hi = thi_ref[blk]

        def piece(t):
            off = seg_off_ref[first + t]
            lo = jnp.maximum(off, lo_b)
            n = jnp.minimum(off + seg_len_ref[first + t], lo_b + BLOCK_GROUPS) - lo

            src = pl.ds(seg_src_ref[first + t] + lo, n)
            dst = xbuf.at[s, pl.ds(lo - lo_b, n)]
            wanted = (n > 0) & (t <= t_hi)

            @pl.when(wanted & (t < n_prompt_tiles))
            def _():
                pltpu.make_async_copy(xtp_ref.at[src], dst, xsem.at[s]).start()

            @pl.when(wanted & (t >= n_prompt_tiles))
            def _():
                pltpu.make_async_copy(xts_ref.at[src], dst, xsem.at[s]).start()

        def two_pieces(i, carry):
            piece(t_lo + 2 * i)
            piece(t_lo + 2 * i + 1)
            return carry

        lax.fori_loop(0, (t_hi - t_lo + 2) // 2, two_pieces, 0)

    @pl.when(b == 0)
    def _():
        xbuf[...] = jnp.zeros(xbuf.shape, F32)
        for cp in weight_copies(be_ref[0]):
            cp.start()
        start_gather(b, slot)

    @pl.when(b + 1 < nb_ref[0])
    def _():
        start_gather(b + 1, 1 - slot)

    @pl.when(b < nb_ref[0])
    def _():
        n = valid_ref[b]
        pltpu.make_async_copy(xtp_ref.at[pl.ds(0, n)], xbuf.at[slot, pl.ds(0, n)], xsem.at[slot]).wait()

        @pl.when((b == 0) | (be_ref[b] != be_ref[jnp.maximum(b - 1, 0)]))
        def _():
            for cp in weight_copies(be_ref[b]):
                cp.wait()
            wgu_s[...] = wgu_f[...].astype(BF16)
            wdn_s[...] = wdn_f[...].astype(BF16)

            @pl.when(nxt_ref[b] >= 0)
            def _():
                for cp in weight_copies(nxt_ref[b]):
                    cp.start()

        x = xbuf[slot].reshape(ROW_BLOCK, D_MODEL).astype(BF16)
        hgu = jnp.dot(x, wgu_s[...], preferred_element_type=F32) + bgu_ref[0]
        glu = jnp.minimum(hgu[:, :D_FF], SWIGLU_LIMIT)
        lin = jnp.clip(hgu[:, D_FF:], -SWIGLU_LIMIT, SWIGLU_LIMIT)
        act = glu * jax.nn.sigmoid(SWIGLU_ALPHA * glu) * (lin + 1.0)
        yb_ref[...] = jnp.dot(act.astype(BF16), wdn_s[...], preferred_element_type=F32) + bdn_ref[0]

    @pl.when(b >= nb_ref[0])
    def _():
        yb_ref[...] = jnp.zeros(yb_ref.shape, F32)


def _experts(sched, segs, xt_prompt, xt_sample, w_gu, b_gu, w_dn, b_dn, n_rows):
    n_blocks = n_rows // ROW_BLOCK
    n_prompt_tiles = xt_prompt.shape[0] // TILE_GROUPS
    n_tiles = n_prompt_tiles + xt_sample.shape[0] // TILE_GROUPS
    n_prefetch = len(sched) + len(segs)
    exp_idx = lambda b, be, *_: (be[b], 0, 0)
    in_specs = [
        pl.BlockSpec(memory_space=pl.ANY),
        pl.BlockSpec(memory_space=pl.ANY),
        pl.BlockSpec(memory_space=pl.ANY),
        pl.BlockSpec((1, 1, 2 * D_FF), exp_idx),
        pl.BlockSpec(memory_space=pl.ANY),
        pl.BlockSpec((1, 1, D_MODEL), exp_idx),
    ]
    scratch = [pltpu.VMEM((2,) + _grouped(ROW_BLOCK), F32),
               pltpu.VMEM((D_MODEL, 2 * D_FF), F32), pltpu.VMEM((D_FF, D_MODEL), F32),
               pltpu.VMEM((D_MODEL, 2 * D_FF), BF16), pltpu.VMEM((D_FF, D_MODEL), BF16),
               pltpu.SemaphoreType.DMA((2,)), pltpu.SemaphoreType.DMA((2,))]
    return pl.pallas_call(
        functools.partial(_experts_kernel, n_tiles=n_tiles, n_prompt_tiles=n_prompt_tiles),
        grid_spec=pltpu.PrefetchScalarGridSpec(
            num_scalar_prefetch=n_prefetch, grid=(n_blocks,), in_specs=in_specs,
            out_specs=pl.BlockSpec((ROW_BLOCK, D_MODEL), lambda b, *_: (b, 0)),
            scratch_shapes=scratch),
        out_shape=jax.ShapeDtypeStruct((n_rows, D_MODEL), F32),
        compiler_params=pltpu.CompilerParams(
            dimension_semantics=("arbitrary",), vmem_limit_bytes=VMEM_LIMIT),
        name="moe_experts",
    )(*sched, *segs, xt_prompt, xt_sample, w_gu, b_gu[:, None, :], w_dn, b_dn[:, None, :])


def _tile_tokens(i, n_prompt_tiles, hp_ref, hs_ref):
    hs = hs_ref[...]
    hs_tile = jnp.concatenate([hs, jnp.zeros((TILE - hs.shape[0], D_MODEL), F32)], axis=0)
    return jnp.where(i < n_prompt_tiles, hp_ref[...], hs_tile)


def _combine_kernel(seg_len_ref, seg_start_ref, seg_off_ref, tile_groups_ref,
                    hp_ref, hs_ref, dest_ref, gate_ref, g_ref, b_ref, yb_ref,
                    yp_ref, ys_ref, ybuf, sem, *, n_prompt_tiles):
    i = pl.program_id(0)
    n = pl.num_programs(0)
    slot = i % 2

    def start_gather(tile, s):
        def segment(e, carry):
            idx = tile * N_EXPERTS + e
            groups = seg_len_ref[idx]

            @pl.when(groups > 0)
            def _():
                pltpu.make_async_copy(yb_ref.at[pl.ds(seg_off_ref[idx], groups)],
                                      ybuf.at[s, pl.ds(seg_start_ref[idx], groups)], sem.at[s]).start()
            return carry
        lax.fori_loop(0, N_EXPERTS, segment, 0)

    @pl.when(i == 0)
    def _():
        ybuf[...] = jnp.zeros(ybuf.shape, F32)
        start_gather(i, slot)

    @pl.when(i + 1 < n)
    def _():
        start_gather(i + 1, 1 - slot)

    total = tile_groups_ref[i]

    @pl.when(total > 0)
    def _():
        pltpu.make_async_copy(yb_ref.at[pl.ds(0, total)], ybuf.at[slot, pl.ds(0, total)], sem.at[slot]).wait()

    dest = dest_ref[0]
    gate = gate_ref[0]
    rid = lax.broadcasted_iota(jnp.int32, (TILE_ROWS, TILE), 0)
    weights = jnp.zeros((TILE_ROWS, TILE), F32)
    for k in range(TOP_K):
        weights = jnp.where(rid == dest[k:k + 1, :], gate[k:k + 1, :], weights)
    rows = ybuf[slot].reshape(TILE_ROWS, D_MODEL).astype(BF16)
    ff = lax.dot_general(weights.astype(BF16), rows, (((0,), (0,)), ((), ())),
                         preferred_element_type=F32)
    h1 = _tile_tokens(i, n_prompt_tiles, hp_ref, hs_ref)
    y = _layer_norm(DN_ALPHA * h1 + ff, g_ref[...], b_ref[...])

    @pl.when(i < n_prompt_tiles)
    def _():
        yp_ref[...] = y

    @pl.when(i >= n_prompt_tiles)
    def _():
        ys_ref[...] = y[:ys_ref.shape[0]]


def _combine(segs, tile_groups, n_seq, h1p, h1s, dest, gate, ln_g, ln_b, yb):
    n_tiles = dest.shape[0]
    n_prompt_tiles = h1p.shape[0] // TILE
    blocks_per_seq = n_prompt_tiles // n_seq

    def prompt_idx(i, *_):
        t = jnp.minimum(i, n_prompt_tiles - 1)
        return ((t % n_seq) * blocks_per_seq + t // n_seq, 0)
    in_specs = [
        pl.BlockSpec((TILE, D_MODEL), prompt_idx),
        _const_spec(h1s.shape),
        pl.BlockSpec((1, TOP_K, TILE), lambda i, *_: (i, 0, 0)),
        pl.BlockSpec((1, TOP_K, TILE), lambda i, *_: (i, 0, 0)),
        _const_spec(ln_g.shape),
        _const_spec(ln_b.shape),
        pl.BlockSpec(memory_space=pl.ANY),
    ]
    out_shape = (jax.ShapeDtypeStruct(h1p.shape, F32), jax.ShapeDtypeStruct(h1s.shape, F32))
    out_specs = (pl.BlockSpec((TILE, D_MODEL), prompt_idx), _const_spec(h1s.shape))
    return pl.pallas_call(
        functools.partial(_combine_kernel, n_prompt_tiles=n_prompt_tiles),
        grid_spec=pltpu.PrefetchScalarGridSpec(
            num_scalar_prefetch=len(segs) + 1, grid=(n_tiles,), in_specs=in_specs, out_specs=out_specs,
            scratch_shapes=[pltpu.VMEM((2,) + _grouped(TILE_ROWS), F32), pltpu.SemaphoreType.DMA((2,))]),
        out_shape=out_shape,
        compiler_params=pltpu.CompilerParams(
            dimension_semantics=("arbitrary",), vmem_limit_bytes=VMEM_LIMIT),
        name="moe_combine",
    )(*segs, tile_groups, h1p, h1s, dest, gate, ln_g, ln_b, yb.reshape(_grouped(yb.shape[0])))


def _moe_layout(cnt, n_prompt_tiles):
    n_tiles = cnt.shape[0]
    i32 = lambda a: a.astype(jnp.int32)
    seg_len = (cnt + SEG_ALIGN - 1) // SEG_ALIGN
    seg_start = jnp.cumsum(seg_len, axis=1) - seg_len
    exp_len = jnp.sum(seg_len, axis=0)
    exp_blocks = (exp_len + BLOCK_GROUPS - 1) // BLOCK_GROUPS
    blocks_end = jnp.cumsum(exp_blocks)
    first_block = blocks_end - exp_blocks
    seg_off = first_block[None, :] * BLOCK_GROUPS + jnp.cumsum(seg_len, axis=0) - seg_len
    max_rows = n_tiles * (TOP_K * TILE + N_EXPERTS * (SEG_ALIGN - 1)) + N_EXPERTS * (ROW_BLOCK - SEG_ALIGN)
    n_blocks = -(-max_rows // ROW_BLOCK)
    n_used = blocks_end[-1]
    blk = jnp.minimum(jnp.arange(n_blocks, dtype=jnp.int32), n_used - 1)
    eid = jnp.arange(N_EXPERTS, dtype=jnp.int32)
    block_expert = jnp.minimum(jnp.sum(blocks_end[None, :] <= blk[:, None], axis=1), N_EXPERTS - 1)
    own = block_expert[:, None] == eid[None, :]
    pick = lambda per_expert: jnp.sum(jnp.where(own, per_expert[None, :], 0), axis=1)
    later_used = (eid[None, :] > eid[:, None]) & (exp_blocks[None, :] > 0)
    next_of = jnp.min(jnp.where(later_used, eid[None, :], N_EXPERTS), axis=1)
    next_expert = pick(jnp.where(next_of < N_EXPERTS, next_of, -1))
    lo = blk * BLOCK_GROUPS
    off_b = jnp.sum(jnp.where(own[:, None, :], seg_off[None, :, :], 0), axis=2)
    len_b = jnp.sum(jnp.where(own[:, None, :], seg_len[None, :, :], 0), axis=2)
    tile_lo = jnp.sum(off_b + len_b <= lo[:, None], axis=1)
    tile_hi = jnp.sum(off_b < lo[:, None] + BLOCK_GROUPS, axis=1) - 1
    valid = jnp.clip(pick(first_block * BLOCK_GROUPS + exp_len) - lo, 0, BLOCK_GROUPS)
    flat = lambda a: i32(a.reshape(-1))
    sched = (i32(block_expert), i32(next_expert), i32(n_used.reshape(1)), i32(tile_lo), i32(tile_hi),
             i32(valid))
    segs = (flat(seg_len), flat(seg_start), flat(seg_off))
    tile = jnp.arange(n_tiles, dtype=seg_off.dtype)
    tile_in_array = jnp.where(tile < n_prompt_tiles, tile, tile - n_prompt_tiles)
    seg_src = tile_in_array[:, None] * TILE_GROUPS + seg_start - seg_off
    pad1 = lambda a: jnp.concatenate([flat(a.T), jnp.zeros((1,), jnp.int32)])
    segs_by_expert = (pad1(seg_len), pad1(seg_src), pad1(seg_off))
    return sched, segs, segs_by_expert, i32(jnp.sum(seg_len, axis=1)), n_blocks * ROW_BLOCK


def _prep_weights(w_in, b_in, conv_w, conv_b, lru_w_a, lru_b_a, lru_w_i, lru_b_i, lru_lambda,
                  w_out, b_out, ln1_g, ln1_b, router_w, router_b):
    def regroup_heads(a):
        rest = a.shape[1:]
        q = a[:Q_COLS].reshape(N_KV_HEADS, HEAD_GROUPS, HEAD_DIM, *rest)
        q = jnp.swapaxes(q, 0, 1).reshape(Q_COLS, *rest)
        return jnp.concatenate([q, a[Q_COLS:]], axis=0)

    def diag_tiles(w):
        per = MXU_DIM // (LRU_WIDTH // LRU_BLOCKS)
        w4 = w.reshape(LRU_BLOCKS // per, per, LRU_WIDTH // LRU_BLOCKS, LRU_WIDTH // LRU_BLOCKS)
        t = jnp.einsum("taij,ab->taibj", w4, jnp.eye(per, dtype=w.dtype))
        return t.reshape(LRU_BLOCKS // per, MXU_DIM, MXU_DIM).astype(BF16)

    return (
        regroup_heads(w_in[0].T).T.astype(BF16), regroup_heads(b_in[0])[None],
        conv_w[0], conv_b[0][None],
        diag_tiles(lru_w_a[0]), lru_b_a[0].reshape(1, LRU_WIDTH),
        diag_tiles(lru_w_i[0]), lru_b_i[0].reshape(1, LRU_WIDTH),
        lru_lambda[0][None],
        regroup_heads(w_out[0]).astype(BF16), b_out[0][None],
        ln1_g[0][None], ln1_b[0][None],
        router_w[0].T.astype(BF16), router_b[0][:, None],
    )


def kernel(x_prompt, x_sample, cache_win_k, cache_win_v, state_lru_h, state_conv, w_in, b_in, attn_sinks, conv_w, conv_b, lru_w_a, lru_b_a, lru_w_i, lru_b_i, lru_lambda, w_out, b_out, ln1_g, ln1_b, router_w, router_b, w_gate_up, b_gate_up, w_down, b_down, ln2_g, ln2_b):
    assert w_in.shape[0] == 1, "single-layer step"
    B, S, _ = x_prompt.shape
    nseq = x_sample.shape[0]
    assert x_sample.shape[1] == 1 and S % TILE == 0
    wts = _prep_weights(w_in, b_in, conv_w, conv_b, lru_w_a, lru_b_a, lru_w_i, lru_b_i, lru_lambda,
                        w_out, b_out, ln1_g, ln1_b, router_w, router_b)
    sinks = attn_sinks[0]

    h1p, xt_p, dest_p, tg_p, cnt_p, pk, pv, ph, pc = _mixer_prompt(x_prompt, sinks, wts)
    h1s, xt_s, dest_s, tg_s, cnt_s, sk, sv, sh, sc = _mixer_sample(
        x_sample.reshape(nseq, D_MODEL),
        cache_win_k[0].reshape(nseq, WINDOW, KV_COLS), cache_win_v[0].reshape(nseq, WINDOW, KV_COLS),
        state_lru_h[0], jnp.transpose(state_conv[0], (1, 0, 2)), sinks, wts)

    per_tile = lambda a: a.reshape(-1, *a.shape[2:])
    dest = jnp.concatenate([per_tile(dest_p), dest_s], axis=0)
    tg = jnp.concatenate([per_tile(tg_p), tg_s], axis=0)
    cnt = jnp.concatenate([per_tile(cnt_p), cnt_s], axis=0)[:, :, 0]
    sched, segs, segs_by_expert, tile_groups, n_rows = _moe_layout(cnt, xt_p.shape[0])

    end_to_end = lambda a: a.reshape(-1, SEG_ALIGN, D_MODEL)
    yb = _experts(sched, segs_by_expert, end_to_end(xt_p), end_to_end(xt_s),
                  w_gate_up[0], b_gate_up[0], w_down[0], b_down[0], n_rows)
    yp, ys = _combine(segs, tile_groups, B, h1p.reshape(B * S, D_MODEL), h1s, dest, tg,
                      ln2_g[0][None], ln2_b[0][None], yb)

    kv_shape = (N_KV_HEADS, HEAD_DIM)
    return (
        yp.reshape(B, S, D_MODEL), ys.reshape(nseq, 1, D_MODEL),
        pk.reshape(1, B, WINDOW, *kv_shape), pv.reshape(1, B, WINDOW, *kv_shape),
        ph.reshape(1, B, LRU_WIDTH), pc[None],
        sk.reshape(1, nseq, WINDOW, *kv_shape), sv.reshape(1, nseq, WINDOW, *kv_shape),
        sh[None], jnp.transpose(sc, (1, 0, 2))[None],
    )
```

```python
import functools

import jax
import jax.numpy as jnp
import numpy as np
from jax import lax
from jax.experimental import pallas as pl
from jax.experimental.pallas import tpu as pltpu

F32 = jnp.float32
BF16 = jnp.bfloat16

D_MODEL = 1024
N_Q_HEADS = 8
N_KV_HEADS = 2
HEAD_DIM = 64
WINDOW = 128
Q_COLS = N_Q_HEADS * HEAD_DIM
KV_COLS = N_KV_HEADS * HEAD_DIM
LRU_WIDTH = 512
LRU_BLOCKS = 8
CONV_WIDTH = 4
RG_C = 8.0
N_EXPERTS = 32
TOP_K = 4
D_FF = 1024
SWIGLU_LIMIT = 7.0
SWIGLU_ALPHA = 1.702
LN_EPS = 1e-5
DN_ALPHA = 2.0 ** 0.25
NEG_INF = -1e30
LOG2E = 1.4426950408889634
D_IN = Q_COLS + 2 * KV_COLS + 2 * LRU_WIDTH

LANES = 128
SUBLANES = 8
MXU_DIM = 256

TILE = 256
SEG_ALIGN = SUBLANES
ROW_BLOCK = 512
HEAD_GROUPS = N_Q_HEADS // N_KV_HEADS
VMEM_LIMIT = 56 * 1024 * 1024

TILE_ROWS = 1280
assert TILE_ROWS >= TOP_K * TILE + N_EXPERTS * (SEG_ALIGN - 1) and TILE_ROWS % LANES == 0
DEST_SHIFT = 16
assert TILE_ROWS < 1 << DEST_SHIFT


def _layer_norm(z, g, b):
    mu = jnp.mean(z, axis=-1, keepdims=True)
    zc = z - mu
    var = jnp.mean(zc * zc, axis=-1, keepdims=True)
    return zc * lax.rsqrt(var + LN_EPS) * g + b


def _sigmoid(x):
    return 0.5 + 0.5 * jnp.tanh(0.5 * x)


def _softplus(x):
    return jnp.maximum(x, 0.0) + jnp.log1p(jnp.exp(-jnp.abs(x)))


def _gelu_tanh(x):
    c = np.float32(np.sqrt(2.0 / np.pi))
    return 0.5 * x * (1.0 + jnp.tanh(c * (x + 0.044715 * (x * x * x))))


def _rglru_gates(xc, wa_ref, ba_ref, wi_ref, bi_ref, lam_ref):
    xcb = xc.astype(BF16)
    half = LRU_WIDTH // 2
    pre_a = jnp.concatenate(
        [jnp.dot(xcb[:, :half], wa_ref[0], preferred_element_type=F32),
         jnp.dot(xcb[:, half:], wa_ref[1], preferred_element_type=F32)], axis=1)
    pre_i = jnp.concatenate(
        [jnp.dot(xcb[:, :half], wi_ref[0], preferred_element_type=F32),
         jnp.dot(xcb[:, half:], wi_ref[1], preferred_element_type=F32)], axis=1)
    r = _sigmoid(pre_a + ba_ref[...])
    gi = _sigmoid(pre_i + bi_ref[...])
    log_a = (-RG_C) * r * _softplus(-lam_ref[...])
    a = jnp.exp(log_a)
    t = jnp.tanh(log_a)
    mult = jnp.sqrt(jnp.maximum(-2.0 * t / (1.0 - t), 0.0))
    return a, mult * (gi * xc)


def _route(h1, rw_ref, rb_ref):
    n = h1.shape[0]
    logits = lax.dot_general(rw_ref[...], h1.astype(BF16), (((1,), (1,)), ((), ())),
                             preferred_element_type=F32) + rb_ref[...]
    eidx = lax.broadcasted_iota(jnp.int32, (N_EXPERTS, n), 0)
    vals = logits
    top_v, top_e = [], []
    for _ in range(TOP_K):
        m = jnp.max(vals, axis=0, keepdims=True)
        idx = jnp.min(jnp.where(vals == m, eidx, N_EXPERTS), axis=0, keepdims=True)
        top_v.append(m)
        top_e.append(idx)
        vals = jnp.where(eidx == idx, -jnp.inf, vals)
    ex = [jnp.exp(v - top_v[0]) for v in top_v]
    den = ex[0] + ex[1] + ex[2] + ex[3]
    gates = jnp.concatenate([e / den for e in ex], axis=0)
    te = jnp.concatenate(top_e, axis=0)
    mem = jnp.zeros((N_EXPERTS, n), jnp.int32)
    for idx in top_e:
        mem = mem + jnp.where(eidx == idx, 1, 0)
    return te, gates, jnp.sum(mem, axis=1, keepdims=True)


def _compact_tile(h1, te, cnt):
    eidx = lax.broadcasted_iota(jnp.int32, (N_EXPERTS, TILE), 0)
    mem = jnp.zeros((N_EXPERTS, TILE), F32)
    for k in range(TOP_K):
        mem = mem + jnp.where(eidx == te[k:k + 1, :], 1.0, 0.0)
    tr = lax.broadcasted_iota(jnp.int32, (TILE, TILE), 0)
    tc = lax.broadcasted_iota(jnp.int32, (TILE, TILE), 1)
    before = jnp.where(tr < tc, 1.0, 0.0).astype(BF16)
    pos = jnp.dot(mem.astype(BF16), before, preferred_element_type=F32)
    seg_rows = ((cnt + (SEG_ALIGN - 1)) // SEG_ALIGN * SEG_ALIGN).astype(F32)
    er = lax.broadcasted_iota(jnp.int32, (N_EXPERTS, N_EXPERTS), 0)
    ec = lax.broadcasted_iota(jnp.int32, (N_EXPERTS, N_EXPERTS), 1)
    earlier = jnp.where(ec < er, 1.0, 0.0).astype(BF16)
    seg_start = jnp.dot(earlier, jnp.broadcast_to(seg_rows, (N_EXPERTS, LANES)).astype(BF16),
                        preferred_element_type=F32)[:, 0:1]
    base = (pos + seg_start).astype(jnp.int32)
    dests = []
    for k in range(TOP_K):
        d = jnp.sum(jnp.where(eidx == te[k:k + 1, :], base, 0), axis=0, keepdims=True)
        dests.append(jnp.where(te[k:k + 1, :] >= 0, d, -1))
    rid = lax.broadcasted_iota(jnp.int32, (TILE_ROWS, TILE), 0)
    onehot = jnp.zeros((TILE_ROWS, TILE), F32)
    for k in range(TOP_K):
        onehot = jnp.where(rid == dests[k], 1.0, onehot)
    picked = jnp.dot(onehot.astype(BF16), h1.astype(BF16), preferred_element_type=F32)
    words = [jnp.where(te[k:k + 1, :] >= 0, (te[k:k + 1, :] << DEST_SHIFT) | dests[k], -1)
             for k in range(TOP_K)]
    return picked, jnp.concatenate(words, axis=0)


def _post_mix(x, mix_in, w_out_ref, b_out_ref, g_ref, b_ref):
    mix = jnp.dot(mix_in.astype(BF16), w_out_ref[...], preferred_element_type=F32) + b_out_ref[...]
    return _layer_norm(DN_ALPHA * x + mix, g_ref[...], b_ref[...])


SEQ_PAIR = 2
PHASE_LAG = 2


def _interleave(chains, lag):
    results = [None] * len(chains)
    live = list(range(len(chains)))
    rnd = 0
    while live:
        for k in list(live):
            if rnd < lag * k:
                continue
            try:
                next(chains[k])
            except StopIteration as stop:
                results[k] = stop.value
                live.remove(k)
        rnd += 1
    return results


def _bias_tables(bias_s):
    row2 = lax.broadcasted_iota(jnp.int32, (2 * WINDOW, 2 * WINDOW), 0)
    col2 = lax.broadcasted_iota(jnp.int32, (2 * WINDOW, 2 * WINDOW), 1)
    dist = (row2 & (WINDOW - 1)) - col2 + WINDOW
    valid = (dist >= 0) & (dist <= WINDOW)
    distf = dist.astype(F32)
    for g in range(HEAD_GROUPS):
        slope = jnp.where(row2 >= WINDOW, 2.0 ** -(g + HEAD_GROUPS + 1), 2.0 ** -(g + 1))
        bias = jnp.where(valid, (-LOG2E) * (slope * distf), NEG_INF)
        bias_s[g, 0] = bias
        bias_s[g, 1] = jnp.where(col2 >= WINDOW, bias, NEG_INF)


def _mixer_prompt_kernel(sinks_ref, x_ref, *rest):
    wts = rest[:15]
    h1_ref, xt_ref, dest_ref, tg_ref, cnt_ref, wk_ref, wv_ref, lh_ref, cv_ref = rest[15:24]
    kext, vext, uext, hcar, attn_buf, bias_s = rest[24:]
    j = pl.program_id(1)
    nj = pl.num_programs(1)
    R = TILE
    one = lambda ref, bb: ref.at[pl.ds(bb, 1)]

    @pl.when(j == 0)
    def _():
        for bb in range(SEQ_PAIR):
            kext[bb, 0:WINDOW, :] = jnp.zeros((WINDOW, KV_COLS), F32)
            vext[bb, 0:WINDOW, :] = jnp.zeros((WINDOW, KV_COLS), F32)
            uext[bb, 0:SUBLANES, :] = jnp.zeros((SUBLANES, LRU_WIDTH), F32)
            hcar[bb] = jnp.zeros((1, LRU_WIDTH), F32)

    @pl.when((pl.program_id(0) == 0) & (j == 0))
    def _():
        _bias_tables(bias_s)

    per_tile = lambda ref, bb: ref.at[0, pl.ds(bb, 1)]
    h_last = _interleave([
        _mixer_tile(j, sinks_ref, one(x_ref, bb), wts, one(h1_ref, bb), one(xt_ref, bb),
                    per_tile(dest_ref, bb), per_tile(tg_ref, bb), per_tile(cnt_ref, bb),
                    kext.at[bb], vext.at[bb], uext.at[bb], hcar.at[bb], attn_buf.at[bb], bias_s)
        for bb in range(SEQ_PAIR)], PHASE_LAG)

    @pl.when(j == nj - 1)
    def _():
        for bb in range(SEQ_PAIR):
            wk_ref[bb] = kext[bb, R:R + WINDOW, :]
            wv_ref[bb] = vext[bb, R:R + WINDOW, :]
            lh_ref[bb] = h_last[bb]
            cv_ref[bb] = uext[bb, SUBLANES + R - (CONV_WIDTH - 1):SUBLANES + R, :]

    for bb in range(SEQ_PAIR):
        kext[bb, 0:WINDOW, :] = kext[bb, R:R + WINDOW, :]
        vext[bb, 0:WINDOW, :] = vext[bb, R:R + WINDOW, :]
        uext[bb, 0:SUBLANES, :] = uext[bb, R:R + SUBLANES, :]


def _mixer_tile(j, sinks_ref, x_ref, wts, h1_ref, xt_ref, dest_ref, tg_ref, cnt_ref,
                kext, vext, uext, hcar, attn_buf, bias_s):
    (w_in_ref, b_in_ref, conv_w_ref, conv_b_ref, wa_ref, ba_ref, wi_ref, bi_ref, lam_ref,
     w_out_ref, b_out_ref, ln_g_ref, ln_b_ref, rw_ref, rb_ref) = wts
    R = TILE
    x = x_ref[0]
    proj = jnp.dot(x.astype(BF16), w_in_ref[...], preferred_element_type=F32) + b_in_ref[...]
    q = proj[:, :Q_COLS]
    kext[WINDOW:WINDOW + R, :] = proj[:, Q_COLS:Q_COLS + KV_COLS]
    vext[WINDOW:WINDOW + R, :] = proj[:, Q_COLS + KV_COLS:Q_COLS + 2 * KV_COLS]
    uext[SUBLANES:SUBLANES + R, :] = proj[:, Q_COLS + 2 * KV_COLS:Q_COLS + 2 * KV_COLS + LRU_WIDTH]
    ug = proj[:, Q_COLS + 2 * KV_COLS + LRU_WIDTH:]
    yield

    upper = lax.broadcasted_iota(jnp.int32, (2 * WINDOW, 1), 0) >= WINDOW
    lo_lane = lax.broadcasted_iota(jnp.int32, (WINDOW, LANES), 1) < HEAD_DIM
    qs = q * (HEAD_DIM ** -0.5 * LOG2E)
    for s in range(R // WINDOW):
        kk = kext[s * WINDOW:s * WINDOW + 2 * WINDOW, :].astype(BF16)
        vv = vext[s * WINDOW:s * WINDOW + 2 * WINDOW, :].astype(BF16)
        table = jnp.where(j == 0, 1, 0) if s == 0 else 0
        for g in range(HEAD_GROUPS):
            qg = qs[s * WINDOW:(s + 1) * WINDOW, g * LANES:(g + 1) * LANES]
            q2 = jnp.concatenate([jnp.where(lo_lane, qg, 0.0), jnp.where(lo_lane, 0.0, qg)], axis=0)
            sc = lax.dot_general(q2.astype(BF16), kk, (((1,), (1,)), ((), ())),
                                 preferred_element_type=F32) + bias_s[g, table]
            sink = jnp.where(upper, sinks_ref[g + HEAD_GROUPS], sinks_ref[g]) * LOG2E
            m = jnp.maximum(jnp.max(sc, axis=-1, keepdims=True), sink)
            p = jnp.exp2(sc - m)
            den = jnp.sum(p, axis=-1, keepdims=True) + jnp.exp2(sink - m)
            o = jnp.dot(p.astype(BF16), vv, preferred_element_type=F32) / den
            attn_buf[s * WINDOW:(s + 1) * WINDOW, g * LANES:(g + 1) * LANES] = jnp.where(
                lo_lane, o[:WINDOW], o[WINDOW:])
            yield

    xc = conv_b_ref[...]
    for tap in range(CONV_WIDTH):
        off = SUBLANES - (CONV_WIDTH - 1) + tap
        xc = xc + uext[off:off + R, :] * conv_w_ref[tap:tap + 1, :]
    a, bv = _rglru_gates(xc, wa_ref, ba_ref, wi_ref, bi_ref, lam_ref)
    yield
    groups = R // SUBLANES
    a3 = a.reshape(groups, SUBLANES, LRU_WIDTH)
    b3 = bv.reshape(groups, SUBLANES, LRU_WIDTH)
    sub = lax.broadcasted_iota(jnp.int32, (groups, SUBLANES, LRU_WIDTH), 1)
    d = 1
    while d < SUBLANES:
        keep = sub >= d
        a_prev = jnp.where(keep, pltpu.roll(a3, d, 1), 1.0)
        b_prev = jnp.where(keep, pltpu.roll(b3, d, 1), 0.0)
        b3 = a3 * b_prev + b3
        a3 = a3 * a_prev
        d *= 2
    h_prev = hcar[...]
    h_groups = []
    for c in range(groups):
        hc = a3[c] * h_prev + b3[c]
        h_groups.append(hc)
        h_prev = hc[SUBLANES - 1:SUBLANES, :]
    h = jnp.concatenate(h_groups, axis=0)
    hcar[...] = h_prev
    rnn = h * _gelu_tanh(ug)
    yield

    mix_in = jnp.concatenate([attn_buf[...], rnn], axis=1)
    h1 = _post_mix(x, mix_in, w_out_ref, b_out_ref, ln_g_ref, ln_b_ref)
    h1_ref[0] = h1
    yield
    te, tg, cnt = _route(h1, rw_ref, rb_ref)
    tg_ref[0] = tg
    cnt_ref[0] = cnt
    yield
    picked, dest = _compact_tile(h1, te, cnt)
    xt_ref[0] = picked
    dest_ref[0] = dest
    return h_prev


def _const_spec(shape):
    return pl.BlockSpec(shape, lambda *_: (0,) * len(shape))


def _mixer_prompt(x, sinks, wts):
    B, S, _ = x.shape
    assert B % SEQ_PAIR == 0 and S % TILE == 0
    nj = S // TILE
    P = SEQ_PAIR
    tile_idx = lambda b, j, *_: (j, b, 0, 0)
    batch_idx = lambda b, j, *_: (b, 0, 0)
    in_specs = [pl.BlockSpec((P, TILE, D_MODEL), lambda b, j, *_: (b, j, 0))]
    in_specs += [_const_spec(w.shape) for w in wts]
    out_shape = (
        jax.ShapeDtypeStruct((B, S, D_MODEL), F32),
        jax.ShapeDtypeStruct((B * nj, TILE_ROWS, D_MODEL), F32),
        jax.ShapeDtypeStruct((nj, B, TOP_K, TILE), jnp.int32),
        jax.ShapeDtypeStruct((nj, B, TOP_K, TILE), F32),
        jax.ShapeDtypeStruct((nj, B, N_EXPERTS, 1), jnp.int32),
        jax.ShapeDtypeStruct((B, WINDOW, KV_COLS), F32),
        jax.ShapeDtypeStruct((B, WINDOW, KV_COLS), F32),
        jax.ShapeDtypeStruct((B, 1, LRU_WIDTH), F32),
        jax.ShapeDtypeStruct((B, CONV_WIDTH - 1, LRU_WIDTH), F32),
    )
    out_specs = (
        pl.BlockSpec((P, TILE, D_MODEL), lambda b, j, *_: (b, j, 0)),
        pl.BlockSpec((P, TILE_ROWS, D_MODEL), lambda b, j, *_: (j * (B // P) + b, 0, 0)),
        pl.BlockSpec((1, P, TOP_K, TILE), tile_idx),
        pl.BlockSpec((1, P, TOP_K, TILE), tile_idx),
        pl.BlockSpec((1, P, N_EXPERTS, 1), tile_idx),
        pl.BlockSpec((P, WINDOW, KV_COLS), batch_idx),
        pl.BlockSpec((P, WINDOW, KV_COLS), batch_idx),
        pl.BlockSpec((P, 1, LRU_WIDTH), batch_idx),
        pl.BlockSpec((P, CONV_WIDTH - 1, LRU_WIDTH), batch_idx),
    )
    scratch = [
        pltpu.VMEM((P, TILE + WINDOW, KV_COLS), F32),
        pltpu.VMEM((P, TILE + WINDOW, KV_COLS), F32),
        pltpu.VMEM((P, TILE + SUBLANES, LRU_WIDTH), F32),
        pltpu.VMEM((P, 1, LRU_WIDTH), F32),
        pltpu.VMEM((P, TILE, Q_COLS), F32),
        pltpu.VMEM((HEAD_GROUPS, 2, 2 * WINDOW, 2 * WINDOW), F32),
    ]
    return pl.pallas_call(
        _mixer_prompt_kernel,
        grid_spec=pltpu.PrefetchScalarGridSpec(
            num_scalar_prefetch=1, grid=(B // P, nj), in_specs=in_specs, out_specs=out_specs,
            scratch_shapes=scratch),
        out_shape=out_shape,
        compiler_params=pltpu.CompilerParams(
            dimension_semantics=("arbitrary", "arbitrary"), vmem_limit_bytes=VMEM_LIMIT),
        name="mixer_prompt",
    )(sinks, x, *wts)


SEQ_CHUNK = 16


def _mixer_sample_kernel(sinks_ref, x_ref, ck_ref, cv_ref, h0_ref, cprev_ref,
                         w_in_ref, b_in_ref, conv_w_ref, conv_b_ref,
                         wa_ref, ba_ref, wi_ref, bi_ref, lam_ref, w_out_ref, b_out_ref,
                         ln_g_ref, ln_b_ref, rw_ref, rb_ref,
                         h1_ref, xt_ref, dest_ref, tg_ref, cnt_ref, wk_ref, wv_ref, lh_ref, cnew_ref,
                         proj_s, attn_s, attn_c):
    c = pl.program_id(0)
    nc = pl.num_programs(0)
    nseq = x_ref.shape[0]

    @pl.when(c == 0)
    def _():
        proj_s[...] = jnp.dot(x_ref[...].astype(BF16), w_in_ref[...],
                              preferred_element_type=F32) + b_in_ref[...]

    sub = lax.broadcasted_iota(jnp.int32, (N_Q_HEADS, LANES), 0)
    lane = lax.broadcasted_iota(jnp.int32, (N_Q_HEADS, LANES), 1)
    own_half = (lane < HEAD_DIM) == (sub < HEAD_GROUPS)
    sub1 = sub[:, 0:1]
    slope = jnp.zeros((N_Q_HEADS, 1), F32)
    sink = jnp.zeros((N_Q_HEADS, 1), F32)
    for hd in range(N_Q_HEADS):
        slope = jnp.where(sub1 == hd, 2.0 ** -(hd + 1), slope)
        sink = jnp.where(sub1 == hd, sinks_ref[hd], sink)
    dist = (WINDOW - lax.broadcasted_iota(jnp.int32, (1, WINDOW), 1)).astype(F32)
    lo_row = lax.broadcasted_iota(jnp.int32, (1, LANES), 1) < HEAD_DIM
    scale = HEAD_DIM ** -0.5

    for i in range(SEQ_CHUNK):
        b = c * SEQ_CHUNK + i
        prow = proj_s[pl.ds(b, 1), :]
        q8 = jnp.zeros((N_Q_HEADS, LANES), F32)
        for g in range(HEAD_GROUPS):
            qg = jnp.broadcast_to(prow[:, g * LANES:(g + 1) * LANES], (N_Q_HEADS, LANES))
            q8 = jnp.where(((sub & (HEAD_GROUPS - 1)) == g) & own_half, qg, q8)
        k_new = prow[:, Q_COLS:Q_COLS + KV_COLS]
        v_new = prow[:, Q_COLS + KV_COLS:Q_COLS + 2 * KV_COLS]
        kb = ck_ref[i]
        vb = cv_ref[i]
        sc = lax.dot_general(q8.astype(BF16), kb.astype(BF16), (((1,), (1,)), ((), ())),
                             preferred_element_type=F32) * scale - slope * dist
        sc_new = jnp.sum(q8 * k_new, axis=-1, keepdims=True) * scale
        m = jnp.maximum(jnp.maximum(jnp.max(sc, axis=-1, keepdims=True), sc_new), sink)
        p = jnp.exp(sc - m)
        p_new = jnp.exp(sc_new - m)
        den = jnp.sum(p, axis=-1, keepdims=True) + p_new + jnp.exp(sink - m)
        o = (jnp.dot(p.astype(BF16), vb.astype(BF16), preferred_element_type=F32)
             + p_new * v_new) / den
        for g in range(HEAD_GROUPS):
            attn_c[i:i + 1, g * LANES:(g + 1) * LANES] = jnp.where(
                lo_row, o[g:g + 1, :], o[g + HEAD_GROUPS:g + HEAD_GROUPS + 1, :])
        wk_ref[i, 0:WINDOW - 1, :] = ck_ref[i, 1:WINDOW, :]
        wk_ref[i, WINDOW - 1:WINDOW, :] = k_new
        wv_ref[i, 0:WINDOW - 1, :] = cv_ref[i, 1:WINDOW, :]
        wv_ref[i, WINDOW - 1:WINDOW, :] = v_new
    attn_s[pl.ds(pl.multiple_of(c * SEQ_CHUNK, SEQ_CHUNK), SEQ_CHUNK), :] = attn_c[...]

    @pl.when(c == nc - 1)
    def _():
        x = x_ref[...]
        ux = proj_s[:, Q_COLS + 2 * KV_COLS:Q_COLS + 2 * KV_COLS + LRU_WIDTH]
        ug = proj_s[:, Q_COLS + 2 * KV_COLS + LRU_WIDTH:]
        xc = conv_b_ref[...]
        for tap in range(CONV_WIDTH - 1):
            xc = xc + cprev_ref[tap] * conv_w_ref[tap:tap + 1, :]
        xc = xc + ux * conv_w_ref[CONV_WIDTH - 1:CONV_WIDTH, :]
        a, bv = _rglru_gates(xc, wa_ref, ba_ref, wi_ref, bi_ref, lam_ref)
        h = a * h0_ref[...] + bv
        rnn = h * _gelu_tanh(ug)
        mix_in = jnp.concatenate([attn_s[...], rnn], axis=1)
        h1 = _post_mix(x, mix_in, w_out_ref, b_out_ref, ln_g_ref, ln_b_ref)
        h1_ref[...] = h1
        te, tg, cnt = _route(h1, rw_ref, rb_ref)
        cnt_ref[0] = cnt
        tg_ref[0] = jnp.concatenate([tg, jnp.zeros((TOP_K, TILE - nseq), F32)], axis=1)
        te_tile = jnp.concatenate([te, jnp.full((TOP_K, TILE - nseq), -1, jnp.int32)], axis=1)
        h1_tile = jnp.concatenate([h1, jnp.zeros((TILE - nseq, D_MODEL), F32)], axis=0)
        picked, dest = _compact_tile(h1_tile, te_tile, cnt)
        xt_ref[0] = picked
        dest_ref[0] = dest
        lh_ref[...] = h
        for tap in range(1, CONV_WIDTH - 1):
            cnew_ref[tap - 1] = cprev_ref[tap]
        cnew_ref[CONV_WIDTH - 2] = ux


def _mixer_sample(x, ck, cv, h0, cprev, sinks, wts):
    nseq = x.shape[0]
    assert nseq % SEQ_CHUNK == 0 and nseq <= TILE and nseq % LANES == 0
    nc = nseq // SEQ_CHUNK
    chunk_idx = lambda c, *_: (c, 0, 0)
    in_specs = [
        _const_spec((nseq, D_MODEL)),
        pl.BlockSpec((SEQ_CHUNK, WINDOW, KV_COLS), chunk_idx),
        pl.BlockSpec((SEQ_CHUNK, WINDOW, KV_COLS), chunk_idx),
        _const_spec((nseq, LRU_WIDTH)),
        _const_spec((CONV_WIDTH - 1, nseq, LRU_WIDTH)),
    ] + [_const_spec(w.shape) for w in wts]
    out_shape = (
        jax.ShapeDtypeStruct((nseq, D_MODEL), F32),
        jax.ShapeDtypeStruct((1, TILE_ROWS, D_MODEL), F32),
        jax.ShapeDtypeStruct((1, TOP_K, TILE), jnp.int32),
        jax.ShapeDtypeStruct((1, TOP_K, TILE), F32),
        jax.ShapeDtypeStruct((1, N_EXPERTS, 1), jnp.int32),
        jax.ShapeDtypeStruct((nseq, WINDOW, KV_COLS), F32),
        jax.ShapeDtypeStruct((nseq, WINDOW, KV_COLS), F32),
        jax.ShapeDtypeStruct((nseq, LRU_WIDTH), F32),
        jax.ShapeDtypeStruct((CONV_WIDTH - 1, nseq, LRU_WIDTH), F32),
    )
    out_specs = (
        _const_spec((nseq, D_MODEL)),
        _const_spec((1, TILE_ROWS, D_MODEL)),
        _const_spec((1, TOP_K, TILE)),
        _const_spec((1, TOP_K, TILE)),
        _const_spec((1, N_EXPERTS, 1)),
        pl.BlockSpec((SEQ_CHUNK, WINDOW, KV_COLS), chunk_idx),
        pl.BlockSpec((SEQ_CHUNK, WINDOW, KV_COLS), chunk_idx),
        _const_spec((nseq, LRU_WIDTH)),
        _const_spec((CONV_WIDTH - 1, nseq, LRU_WIDTH)),
    )
    scratch = [pltpu.VMEM((nseq, D_IN), F32), pltpu.VMEM((nseq, Q_COLS), F32),
               pltpu.VMEM((SEQ_CHUNK, Q_COLS), F32)]
    return pl.pallas_call(
        _mixer_sample_kernel,
        grid_spec=pltpu.PrefetchScalarGridSpec(
            num_scalar_prefetch=1, grid=(nc,), in_specs=in_specs, out_specs=out_specs,
            scratch_shapes=scratch),
        out_shape=out_shape,
        compiler_params=pltpu.CompilerParams(
            dimension_semantics=("arbitrary",), vmem_limit_bytes=VMEM_LIMIT),
        name="mixer_sample",
    )(sinks, x, ck, cv, h0, cprev, *wts)


def _grouped(rows):
    assert rows % SEG_ALIGN == 0
    return (rows // SEG_ALIGN, SEG_ALIGN, D_MODEL)


BLOCK_GROUPS = ROW_BLOCK // SEG_ALIGN
TILE_GROUPS = TILE_ROWS // SEG_ALIGN


def _experts_kernel(be_ref, nxt_ref, nb_ref, tlo_ref, thi_ref, valid_ref,
                    seg_len_ref, seg_src_ref, seg_off_ref,
                    xtp_ref, xts_ref, wgu_ref, bgu_ref, wdn_ref, bdn_ref, yb_ref,
                    xbuf, wgu_f, wdn_f, wgu_s, wdn_s, wsem, xsem, *, n_tiles, n_prompt_tiles):
    b = pl.program_id(0)
    slot = b % 2

    def weight_copies(e):
        return (pltpu.make_async_copy(wgu_ref.at[e], wgu_f, wsem.at[0]),
                pltpu.make_async_copy(wdn_ref.at[e], wdn_f, wsem.at[1]))

    def start_gather(blk, s):
        lo_b = blk * BLOCK_GROUPS
        first = be_ref[blk] * n_tiles

        t_lo = tlo_ref[blk]
        t_hi = thi_ref[blk]

        def piece(t):
            off = seg_off_ref[first + t]
            lo = jnp.maximum(off, lo_b)
            n = jnp.minimum(off + seg_len_ref[first + t], lo_b + BLOCK_GROUPS) - lo

            src = pl.ds(seg_src_ref[first + t] + lo, n)
            dst = xbuf.at[s, pl.ds(lo - lo_b, n)]
            wanted = (n > 0) & (t <= t_hi)

            @pl.when(wanted & (t < n_prompt_tiles))
            def _():
                pltpu.make_async_copy(xtp_ref.at[src], dst, xsem.at[s]).start()

            @pl.when(wanted & (t >= n_prompt_tiles))
            def _():
                pltpu.make_async_copy(xts_ref.at[src], dst, xsem.at[s]).start()

        def two_pieces(i, carry):
            piece(t_lo + 2 * i)
            piece(t_lo + 2 * i + 1)
            return carry

        lax.fori_loop(0, (t_hi - t_lo + 2) // 2, two_pieces, 0)

    @pl.when(b == 0)
    def _():
        xbuf[...] = jnp.zeros(xbuf.shape, F32)
        for cp in weight_copies(be_ref[0]):
            cp.start()
        start_gather(b, slot)

    @pl.when(b + 1 < nb_ref[0])
    def _():
        start_gather(b + 1, 1 - slot)

    @pl.when(b < nb_ref[0])
    def _():
        n = valid_ref[b]
        pltpu.make_async_copy(xtp_ref.at[pl.ds(0, n)], xbuf.at[slot, pl.ds(0, n)], xsem.at[slot]).wait()

        @pl.when((b == 0) | (be_ref[b] != be_ref[jnp.maximum(b - 1, 0)]))
        def _():
            for cp in weight_copies(be_ref[b]):
                cp.wait()
            wgu_s[...] = wgu_f[...].astype(BF16)
            wdn_s[...] = wdn_f[...].astype(BF16)

            @pl.when(nxt_ref[b] >= 0)
            def _():
                for cp in weight_copies(nxt_ref[b]):
                    cp.start()

        x = xbuf[slot].reshape(ROW_BLOCK, D_MODEL).astype(BF16)
        hgu = jnp.dot(x, wgu_s[...], preferred_element_type=F32) + bgu_ref[0]
        glu = jnp.minimum(hgu[:, :D_FF], SWIGLU_LIMIT)
        lin = jnp.clip(hgu[:, D_FF:], -SWIGLU_LIMIT, SWIGLU_LIMIT)
        act = glu * jax.nn.sigmoid(SWIGLU_ALPHA * glu) * (lin + 1.0)
        y = jnp.dot(act.astype(BF16), wdn_s[...], preferred_element_type=F32) + bdn_ref[0]
        yb_ref[...] = y.astype(BF16)

    @pl.when(b >= nb_ref[0])
    def _():
        yb_ref[...] = jnp.zeros(yb_ref.shape, BF16)


def _experts(sched, segs, xt_prompt, xt_sample, w_gu, b_gu, w_dn, b_dn, n_rows):
    n_blocks = n_rows // ROW_BLOCK
    n_prompt_tiles = xt_prompt.shape[0] // TILE_GROUPS
    n_tiles = n_prompt_tiles + xt_sample.shape[0] // TILE_GROUPS
    n_prefetch = len(sched) + len(segs)
    exp_idx = lambda b, be, *_: (be[b], 0, 0)
    in_specs = [
        pl.BlockSpec(memory_space=pl.ANY),
        pl.BlockSpec(memory_space=pl.ANY),
        pl.BlockSpec(memory_space=pl.ANY),
        pl.BlockSpec((1, 1, 2 * D_FF), exp_idx),
        pl.BlockSpec(memory_space=pl.ANY),
        pl.BlockSpec((1, 1, D_MODEL), exp_idx),
    ]
    scratch = [pltpu.VMEM((2,) + _grouped(ROW_BLOCK), F32),
               pltpu.VMEM((D_MODEL, 2 * D_FF), F32), pltpu.VMEM((D_FF, D_MODEL), F32),
               pltpu.VMEM((D_MODEL, 2 * D_FF), BF16), pltpu.VMEM((D_FF, D_MODEL), BF16),
               pltpu.SemaphoreType.DMA((2,)), pltpu.SemaphoreType.DMA((2,))]
    return pl.pallas_call(
        functools.partial(_experts_kernel, n_tiles=n_tiles, n_prompt_tiles=n_prompt_tiles),
        grid_spec=pltpu.PrefetchScalarGridSpec(
            num_scalar_prefetch=n_prefetch, grid=(n_blocks,), in_specs=in_specs,
            out_specs=pl.BlockSpec((ROW_BLOCK, D_MODEL), lambda b, *_: (b, 0)),
            scratch_shapes=scratch),
        out_shape=jax.ShapeDtypeStruct((n_rows, D_MODEL), BF16),
        compiler_params=pltpu.CompilerParams(
            dimension_semantics=("arbitrary",), vmem_limit_bytes=VMEM_LIMIT),
        name="moe_experts",
    )(*sched, *segs, xt_prompt, xt_sample, w_gu, b_gu[:, None, :], w_dn, b_dn[:, None, :])


def _tile_tokens(i, n_prompt_tiles, hp_ref, hs_ref):
    hs = hs_ref[...]
    hs_tile = jnp.concatenate([hs, jnp.zeros((TILE - hs.shape[0], D_MODEL), F32)], axis=0)
    return jnp.where(i < n_prompt_tiles, hp_ref[...], hs_tile)


SPAN = 2 * SEG_ALIGN
COMBINE_ROWS = 1792
assert COMBINE_ROWS >= TOP_K * TILE + N_EXPERTS * (SEG_ALIGN - 1 + SPAN) and COMBINE_ROWS % MXU_DIM == 0


def _combine_kernel(span_len_ref, span_start_ref, span_off_ref, tile_spans_ref,
                    hp_ref, hs_ref, dest_ref, gate_ref, shift_ref, g_ref, b_ref, yb_ref,
                    yp_ref, ys_ref, ybuf, sem, *, n_prompt_tiles):
    i = pl.program_id(0)
    n = pl.num_programs(0)
    slot = i % 2

    def start_gather(tile, s):
        def segment(e, carry):
            idx = tile * N_EXPERTS + e
            spans = span_len_ref[idx]

            @pl.when(spans > 0)
            def _():
                pltpu.make_async_copy(yb_ref.at[pl.ds(span_off_ref[idx], spans)],
                                      ybuf.at[s, pl.ds(span_start_ref[idx], spans)], sem.at[s]).start()
            return carry
        lax.fori_loop(0, N_EXPERTS, segment, 0)

    @pl.when(i == 0)
    def _():
        ybuf[...] = jnp.zeros(ybuf.shape, BF16)
        start_gather(i, slot)

    @pl.when(i + 1 < n)
    def _():
        start_gather(i + 1, 1 - slot)

    total = tile_spans_ref[i]

    @pl.when(total > 0)
    def _():
        pltpu.make_async_copy(yb_ref.at[pl.ds(0, total)], ybuf.at[slot, pl.ds(0, total)], sem.at[slot]).wait()

    word = dest_ref[0]
    gate = gate_ref[0]
    shift = shift_ref[0]
    eidx = lax.broadcasted_iota(jnp.int32, (N_EXPERTS, TILE), 0)
    rid = lax.broadcasted_iota(jnp.int32, (COMBINE_ROWS, TILE), 0)
    weights = jnp.zeros((COMBINE_ROWS, TILE), F32)
    for k in range(TOP_K):
        expert = word[k:k + 1, :] >> DEST_SHIFT
        row = (word[k:k + 1, :] & ((1 << DEST_SHIFT) - 1)) + jnp.sum(
            jnp.where(eidx == expert, shift, 0), axis=0, keepdims=True)
        weights = jnp.where(rid == row, gate[k:k + 1, :], weights)
    rows = ybuf[slot].reshape(COMBINE_ROWS, D_MODEL)
    ff = lax.dot_general(weights.astype(BF16), rows, (((0,), (0,)), ((), ())),
                         preferred_element_type=F32)
    h1 = _tile_tokens(i, n_prompt_tiles, hp_ref, hs_ref)
    y = _layer_norm(DN_ALPHA * h1 + ff, g_ref[...], b_ref[...])

    @pl.when(i < n_prompt_tiles)
    def _():
        yp_ref[...] = y

    @pl.when(i >= n_prompt_tiles)
    def _():
        ys_ref[...] = y[:ys_ref.shape[0]]


def _combine(spans, n_seq, h1p, h1s, dest, gate, shift, ln_g, ln_b, yb):
    n_tiles = dest.shape[0]
    n_prompt_tiles = h1p.shape[0] // TILE
    blocks_per_seq = n_prompt_tiles // n_seq

    def prompt_idx(i, *_):
        t = jnp.minimum(i, n_prompt_tiles - 1)
        return ((t % n_seq) * blocks_per_seq + t // n_seq, 0)
    in_specs = [
        pl.BlockSpec((TILE, D_MODEL), prompt_idx),
        _const_spec(h1s.shape),
        pl.BlockSpec((1, TOP_K, TILE), lambda i, *_: (i, 0, 0)),
        pl.BlockSpec((1, TOP_K, TILE), lambda i, *_: (i, 0, 0)),
        pl.BlockSpec((1, N_EXPERTS, 1), lambda i, *_: (i, 0, 0)),
        _const_spec(ln_g.shape),
        _const_spec(ln_b.shape),
        pl.BlockSpec(memory_space=pl.ANY),
    ]
    out_shape = (jax.ShapeDtypeStruct(h1p.shape, F32), jax.ShapeDtypeStruct(h1s.shape, F32))
    out_specs = (pl.BlockSpec((TILE, D_MODEL), prompt_idx), _const_spec(h1s.shape))
    spanned = lambda rows: (rows // SPAN, SPAN, D_MODEL)
    return pl.pallas_call(
        functools.partial(_combine_kernel, n_prompt_tiles=n_prompt_tiles),
        grid_spec=pltpu.PrefetchScalarGridSpec(
            num_scalar_prefetch=len(spans), grid=(n_tiles,), in_specs=in_specs, out_specs=out_specs,
            scratch_shapes=[pltpu.VMEM((2,) + spanned(COMBINE_ROWS), BF16), pltpu.SemaphoreType.DMA((2,))]),
        out_shape=out_shape,
        compiler_params=pltpu.CompilerParams(
            dimension_semantics=("arbitrary",), vmem_limit_bytes=VMEM_LIMIT),
        name="moe_combine",
    )(*spans, h1p, h1s, dest, gate, shift, ln_g, ln_b, yb.reshape(spanned(yb.shape[0])))


def _moe_layout(cnt, n_prompt_tiles):
    n_tiles = cnt.shape[0]
    i32 = lambda a: a.astype(jnp.int32)
    seg_len = (cnt + SEG_ALIGN - 1) // SEG_ALIGN
    seg_start = jnp.cumsum(seg_len, axis=1) - seg_len
    exp_len = jnp.sum(seg_len, axis=0)
    exp_blocks = (exp_len + BLOCK_GROUPS - 1) // BLOCK_GROUPS
    blocks_end = jnp.cumsum(exp_blocks)
    first_block = blocks_end - exp_blocks
    seg_off = first_block[None, :] * BLOCK_GROUPS + jnp.cumsum(seg_len, axis=0) - seg_len
    max_rows = n_tiles * (TOP_K * TILE + N_EXPERTS * (SEG_ALIGN - 1)) + N_EXPERTS * (ROW_BLOCK - SEG_ALIGN)
    n_blocks = -(-max_rows // ROW_BLOCK)
    n_used = blocks_end[-1]
    blk = jnp.minimum(jnp.arange(n_blocks, dtype=jnp.int32), n_used - 1)
    eid = jnp.arange(N_EXPERTS, dtype=jnp.int32)
    block_expert = jnp.minimum(jnp.sum(blocks_end[None, :] <= blk[:, None], axis=1), N_EXPERTS - 1)
    own = block_expert[:, None] == eid[None, :]
    pick = lambda per_expert: jnp.sum(jnp.where(own, per_expert[None, :], 0), axis=1)
    later_used = (eid[None, :] > eid[:, None]) & (exp_blocks[None, :] > 0)
    next_of = jnp.min(jnp.where(later_used, eid[None, :], N_EXPERTS), axis=1)
    next_expert = pick(jnp.where(next_of < N_EXPERTS, next_of, -1))
    lo = blk * BLOCK_GROUPS
    off_b = jnp.sum(jnp.where(own[:, None, :], seg_off[None, :, :], 0), axis=2)
    len_b = jnp.sum(jnp.where(own[:, None, :], seg_len[None, :, :], 0), axis=2)
    tile_lo = jnp.sum(off_b + len_b <= lo[:, None], axis=1)
    tile_hi = jnp.sum(off_b < lo[:, None] + BLOCK_GROUPS, axis=1) - 1
    valid = jnp.clip(pick(first_block * BLOCK_GROUPS + exp_len) - lo, 0, BLOCK_GROUPS)
    flat = lambda a: i32(a.reshape(-1))
    sched = (i32(block_expert), i32(next_expert), i32(n_used.reshape(1)), i32(tile_lo), i32(tile_hi),
             i32(valid))
    groups_per_span = SPAN // SEG_ALIGN
    span_off = seg_off // groups_per_span
    span_len = jnp.where(seg_len > 0, (seg_off + seg_len + groups_per_span - 1) // groups_per_span - span_off, 0)
    span_start = jnp.cumsum(span_len, axis=1) - span_len
    shift = span_start * SPAN + (seg_off % groups_per_span) * SEG_ALIGN - seg_start * SEG_ALIGN
    spans = (flat(span_len), flat(span_start), flat(span_off), i32(jnp.sum(span_len, axis=1)))
    tile = jnp.arange(n_tiles, dtype=seg_off.dtype)
    tile_in_array = jnp.where(tile < n_prompt_tiles, tile, tile - n_prompt_tiles)
    seg_src = tile_in_array[:, None] * TILE_GROUPS + seg_start - seg_off
    pad1 = lambda a: jnp.concatenate([flat(a.T), jnp.zeros((1,), jnp.int32)])
    segs_by_expert = (pad1(seg_len), pad1(seg_src), pad1(seg_off))
    return sched, segs_by_expert, spans, i32(shift[:, :, None]), n_blocks * ROW_BLOCK


def _prep_weights(w_in, b_in, conv_w, conv_b, lru_w_a, lru_b_a, lru_w_i, lru_b_i, lru_lambda,
                  w_out, b_out, ln1_g, ln1_b, router_w, router_b):
    def regroup_heads(a):
        rest = a.shape[1:]
        q = a[:Q_COLS].reshape(N_KV_HEADS, HEAD_GROUPS, HEAD_DIM, *rest)
        q = jnp.swapaxes(q, 0, 1).reshape(Q_COLS, *rest)
        return jnp.concatenate([q, a[Q_COLS:]], axis=0)

    def diag_tiles(w):
        per = MXU_DIM // (LRU_WIDTH // LRU_BLOCKS)
        w4 = w.reshape(LRU_BLOCKS // per, per, LRU_WIDTH // LRU_BLOCKS, LRU_WIDTH // LRU_BLOCKS)
        t = jnp.einsum("taij,ab->taibj", w4, jnp.eye(per, dtype=w.dtype))
        return t.reshape(LRU_BLOCKS // per, MXU_DIM, MXU_DIM).astype(BF16)

    return (
        regroup_heads(w_in[0].T).T.astype(BF16), regroup_heads(b_in[0])[None],
        conv_w[0], conv_b[0][None],
        diag_tiles(lru_w_a[0]), lru_b_a[0].reshape(1, LRU_WIDTH),
        diag_tiles(lru_w_i[0]), lru_b_i[0].reshape(1, LRU_WIDTH),
        lru_lambda[0][None],
        regroup_heads(w_out[0]).astype(BF16), b_out[0][None],
        ln1_g[0][None], ln1_b[0][None],
        router_w[0].T.astype(BF16), router_b[0][:, None],
    )


def kernel(x_prompt, x_sample, cache_win_k, cache_win_v, state_lru_h, state_conv, w_in, b_in, attn_sinks, conv_w, conv_b, lru_w_a, lru_b_a, lru_w_i, lru_b_i, lru_lambda, w_out, b_out, ln1_g, ln1_b, router_w, router_b, w_gate_up, b_gate_up, w_down, b_down, ln2_g, ln2_b):
    assert w_in.shape[0] == 1, "single-layer step"
    B, S, _ = x_prompt.shape
    nseq = x_sample.shape[0]
    assert x_sample.shape[1] == 1 and S % TILE == 0
    wts = _prep_weights(w_in, b_in, conv_w, conv_b, lru_w_a, lru_b_a, lru_w_i, lru_b_i, lru_lambda,
                        w_out, b_out, ln1_g, ln1_b, router_w, router_b)
    sinks = attn_sinks[0]

    h1p, xt_p, dest_p, tg_p, cnt_p, pk, pv, ph, pc = _mixer_prompt(x_prompt, sinks, wts)
    h1s, xt_s, dest_s, tg_s, cnt_s, sk, sv, sh, sc = _mixer_sample(
        x_sample.reshape(nseq, D_MODEL),
        cache_win_k[0].reshape(nseq, WINDOW, KV_COLS), cache_win_v[0].reshape(nseq, WINDOW, KV_COLS),
        state_lru_h[0], jnp.transpose(state_conv[0], (1, 0, 2)), sinks, wts)

    per_tile = lambda a: a.reshape(-1, *a.shape[2:])
    dest = jnp.concatenate([per_tile(dest_p), dest_s], axis=0)
    tg = jnp.concatenate([per_tile(tg_p), tg_s], axis=0)
    cnt = jnp.concatenate([per_tile(cnt_p), cnt_s], axis=0)[:, :, 0]
    sched, segs_by_expert, spans, shift, n_rows = _moe_layout(cnt, xt_p.shape[0])

    end_to_end = lambda a: a.reshape(-1, SEG_ALIGN, D_MODEL)
    yb = _experts(sched, segs_by_expert, end_to_end(xt_p), end_to_end(xt_s),
                  w_gate_up[0], b_gate_up[0], w_down[0], b_down[0], n_rows)
    yp, ys = _combine(spans, B, h1p.reshape(B * S, D_MODEL), h1s, dest, tg, shift,
                      ln2_g[0][None], ln2_b[0][None], yb)

    kv_shape = (N_KV_HEADS, HEAD_DIM)
    return (
        yp.reshape(B, S, D_MODEL), ys.reshape(nseq, 1, D_MODEL),
        pk.reshape(1, B, WINDOW, *kv_shape), pv.reshape(1, B, WINDOW, *kv_shape),
        ph.reshape(1, B, LRU_WIDTH), pc[None],
        sk.reshape(1, nseq, WINDOW, *kv_shape), sv.reshape(1, nseq, WINDOW, *kv_shape),
        sh[None], jnp.transpose(sc, (1, 0, 2))[None],
    )
```

```python
import functools

import jax
import jax.numpy as jnp
import numpy as np
from jax import lax
from jax.experimental import pallas as pl
from jax.experimental.pallas import tpu as pltpu

F32 = jnp.float32
BF16 = jnp.bfloat16

D_MODEL = 1024
N_Q_HEADS = 8
N_KV_HEADS = 2
HEAD_DIM = 64
WINDOW = 128
Q_COLS = N_Q_HEADS * HEAD_DIM
KV_COLS = N_KV_HEADS * HEAD_DIM
LRU_WIDTH = 512
LRU_BLOCKS = 8
CONV_WIDTH = 4
RG_C = 8.0
N_EXPERTS = 32
TOP_K = 4
D_FF = 1024
SWIGLU_LIMIT = 7.0
SWIGLU_ALPHA = 1.702
LN_EPS = 1e-5
DN_ALPHA = 2.0 ** 0.25
NEG_INF = -1e30
LOG2E = 1.4426950408889634
D_IN = Q_COLS + 2 * KV_COLS + 2 * LRU_WIDTH

LANES = 128
SUBLANES = 8
MXU_DIM = 256

TILE = 256
SEG_ALIGN = SUBLANES
ROW_BLOCK = 512
HEAD_GROUPS = N_Q_HEADS // N_KV_HEADS
VMEM_LIMIT = 56 * 1024 * 1024

TILE_ROWS = 1280
assert TILE_ROWS >= TOP_K * TILE + N_EXPERTS * (SEG_ALIGN - 1) and TILE_ROWS % LANES == 0


def _layer_norm(z, g, b):
    mu = jnp.mean(z, axis=-1, keepdims=True)
    zc = z - mu
    var = jnp.mean(zc * zc, axis=-1, keepdims=True)
    return zc * lax.rsqrt(var + LN_EPS) * g + b


def _sigmoid(x):
    return 0.5 + 0.5 * jnp.tanh(0.5 * x)


def _softplus(x):
    return jnp.maximum(x, 0.0) + jnp.log1p(jnp.exp(-jnp.abs(x)))


def _gelu_tanh(x):
    c = np.float32(np.sqrt(2.0 / np.pi))
    return 0.5 * x * (1.0 + jnp.tanh(c * (x + 0.044715 * (x * x * x))))


def _rglru_gates(xc, wa_ref, ba_ref, wi_ref, bi_ref, lam_ref):
    xcb = xc.astype(BF16)
    half = LRU_WIDTH // 2
    pre_a = jnp.concatenate(
        [jnp.dot(xcb[:, :half], wa_ref[0], preferred_element_type=F32),
         jnp.dot(xcb[:, half:], wa_ref[1], preferred_element_type=F32)], axis=1)
    pre_i = jnp.concatenate(
        [jnp.dot(xcb[:, :half], wi_ref[0], preferred_element_type=F32),
         jnp.dot(xcb[:, half:], wi_ref[1], preferred_element_type=F32)], axis=1)
    r = _sigmoid(pre_a + ba_ref[...])
    gi = _sigmoid(pre_i + bi_ref[...])
    log_a = (-RG_C) * r * _softplus(-lam_ref[...])
    a = jnp.exp(log_a)
    t = jnp.tanh(log_a)
    mult = jnp.sqrt(jnp.maximum(-2.0 * t / (1.0 - t), 0.0))
    return a, mult * (gi * xc)


def _route(h1, rw_ref, rb_ref):
    n = h1.shape[0]
    logits = lax.dot_general(rw_ref[...], h1.astype(BF16), (((1,), (1,)), ((), ())),
                             preferred_element_type=F32) + rb_ref[...]
    eidx = lax.broadcasted_iota(jnp.int32, (N_EXPERTS, n), 0)
    vals = logits
    top_v, top_e = [], []
    for _ in range(TOP_K):
        m = jnp.max(vals, axis=0, keepdims=True)
        idx = jnp.min(jnp.where(vals == m, eidx, N_EXPERTS), axis=0, keepdims=True)
        top_v.append(m)
        top_e.append(idx)
        vals = jnp.where(eidx == idx, -jnp.inf, vals)
    ex = [jnp.exp(v - top_v[0]) for v in top_v]
    den = ex[0] + ex[1] + ex[2] + ex[3]
    gates = jnp.concatenate([e / den for e in ex], axis=0)
    te = jnp.concatenate(top_e, axis=0)
    mem = jnp.zeros((N_EXPERTS, n), jnp.int32)
    for idx in top_e:
        mem = mem + jnp.where(eidx == idx, 1, 0)
    return te, gates, jnp.sum(mem, axis=1, keepdims=True)


def _compact_tile(h1, te, cnt):
    eidx = lax.broadcasted_iota(jnp.int32, (N_EXPERTS, TILE), 0)
    mem = jnp.zeros((N_EXPERTS, TILE), F32)
    for k in range(TOP_K):
        mem = mem + jnp.where(eidx == te[k:k + 1, :], 1.0, 0.0)
    tr = lax.broadcasted_iota(jnp.int32, (TILE, TILE), 0)
    tc = lax.broadcasted_iota(jnp.int32, (TILE, TILE), 1)
    before = jnp.where(tr < tc, 1.0, 0.0).astype(BF16)
    pos = jnp.dot(mem.astype(BF16), before, preferred_element_type=F32)
    seg_rows = ((cnt + (SEG_ALIGN - 1)) // SEG_ALIGN * SEG_ALIGN).astype(F32)
    er = lax.broadcasted_iota(jnp.int32, (N_EXPERTS, N_EXPERTS), 0)
    ec = lax.broadcasted_iota(jnp.int32, (N_EXPERTS, N_EXPERTS), 1)
    earlier = jnp.where(ec < er, 1.0, 0.0).astype(BF16)
    seg_start = jnp.dot(earlier, jnp.broadcast_to(seg_rows, (N_EXPERTS, LANES)).astype(BF16),
                        preferred_element_type=F32)[:, 0:1]
    base = (pos + seg_start).astype(jnp.int32)
    dests = []
    for k in range(TOP_K):
        d = jnp.sum(jnp.where(eidx == te[k:k + 1, :], base, 0), axis=0, keepdims=True)
        dests.append(jnp.where(te[k:k + 1, :] >= 0, d, -1))
    rid = lax.broadcasted_iota(jnp.int32, (TILE_ROWS, TILE), 0)
    onehot = jnp.zeros((TILE_ROWS, TILE), F32)
    for k in range(TOP_K):
        onehot = jnp.where(rid == dests[k], 1.0, onehot)
    picked = jnp.dot(onehot.astype(BF16), h1.astype(BF16), preferred_element_type=F32)
    return picked, jnp.concatenate(dests, axis=0)


def _post_mix(x, mix_in, w_out_ref, b_out_ref, g_ref, b_ref):
    mix = jnp.dot(mix_in.astype(BF16), w_out_ref[...], preferred_element_type=F32) + b_out_ref[...]
    return _layer_norm(DN_ALPHA * x + mix, g_ref[...], b_ref[...])


SEQ_PAIR = 2
PHASE_LAG = 0


def _interleave(chains, lag):
    results = [None] * len(chains)
    live = list(range(len(chains)))
    rnd = 0
    while live:
        for k in list(live):
            if rnd < lag * k:
                continue
            try:
                next(chains[k])
            except StopIteration as stop:
                results[k] = stop.value
                live.remove(k)
        rnd += 1
    return results


def _bias_tables(bias_s):
    row2 = lax.broadcasted_iota(jnp.int32, (2 * WINDOW, 2 * WINDOW), 0)
    col2 = lax.broadcasted_iota(jnp.int32, (2 * WINDOW, 2 * WINDOW), 1)
    dist = (row2 & (WINDOW - 1)) - col2 + WINDOW
    valid = (dist >= 0) & (dist <= WINDOW)
    distf = dist.astype(F32)
    for g in range(HEAD_GROUPS):
        slope = jnp.where(row2 >= WINDOW, 2.0 ** -(g + HEAD_GROUPS + 1), 2.0 ** -(g + 1))
        bias = jnp.where(valid, (-LOG2E) * (slope * distf), NEG_INF)
        bias_s[g, 0] = bias
        bias_s[g, 1] = jnp.where(col2 >= WINDOW, bias, NEG_INF)


def _mixer_prompt_kernel(sinks_ref, x_ref, *rest):
    wts = rest[:15]
    h1_ref, xt_ref, dest_ref, tg_ref, cnt_ref, wk_ref, wv_ref, lh_ref, cv_ref = rest[15:24]
    kext, vext, uext, hcar, attn_buf, bias_s = rest[24:]
    j = pl.program_id(1)
    nj = pl.num_programs(1)
    R = TILE
    one = lambda ref, bb: ref.at[pl.ds(bb, 1)]

    @pl.when(j == 0)
    def _():
        for bb in range(SEQ_PAIR):
            kext[bb, 0:WINDOW, :] = jnp.zeros((WINDOW, KV_COLS), F32)
            vext[bb, 0:WINDOW, :] = jnp.zeros((WINDOW, KV_COLS), F32)
            uext[bb, 0:SUBLANES, :] = jnp.zeros((SUBLANES, LRU_WIDTH), F32)
            hcar[bb] = jnp.zeros((1, LRU_WIDTH), F32)

    @pl.when((pl.program_id(0) == 0) & (j == 0))
    def _():
        _bias_tables(bias_s)

    per_tile = lambda ref, bb: ref.at[0, pl.ds(bb, 1)]
    h_last = _interleave([
        _mixer_tile(j, sinks_ref, one(x_ref, bb), wts, one(h1_ref, bb), one(xt_ref, bb),
                    per_tile(dest_ref, bb), per_tile(tg_ref, bb), per_tile(cnt_ref, bb),
                    kext.at[bb], vext.at[bb], uext.at[bb], hcar.at[bb], attn_buf.at[bb], bias_s)
        for bb in range(SEQ_PAIR)], PHASE_LAG)

    @pl.when(j == nj - 1)
    def _():
        for bb in range(SEQ_PAIR):
            wk_ref[bb] = kext[bb, R:R + WINDOW, :]
            wv_ref[bb] = vext[bb, R:R + WINDOW, :]
            lh_ref[bb] = h_last[bb]
            cv_ref[bb] = uext[bb, SUBLANES + R - (CONV_WIDTH - 1):SUBLANES + R, :]

    for bb in range(SEQ_PAIR):
        kext[bb, 0:WINDOW, :] = kext[bb, R:R + WINDOW, :]
        vext[bb, 0:WINDOW, :] = vext[bb, R:R + WINDOW, :]
        uext[bb, 0:SUBLANES, :] = uext[bb, R:R + SUBLANES, :]


def _mixer_tile(j, sinks_ref, x_ref, wts, h1_ref, xt_ref, dest_ref, tg_ref, cnt_ref,
                kext, vext, uext, hcar, attn_buf, bias_s):
    (w_in_ref, b_in_ref, conv_w_ref, conv_b_ref, wa_ref, ba_ref, wi_ref, bi_ref, lam_ref,
     w_out_ref, b_out_ref, ln_g_ref, ln_b_ref, rw_ref, rb_ref) = wts
    R = TILE
    x = x_ref[0]
    proj = jnp.dot(x.astype(BF16), w_in_ref[...], preferred_element_type=F32) + b_in_ref[...]
    q = proj[:, :Q_COLS]
    kext[WINDOW:WINDOW + R, :] = proj[:, Q_COLS:Q_COLS + KV_COLS]
    vext[WINDOW:WINDOW + R, :] = proj[:, Q_COLS + KV_COLS:Q_COLS + 2 * KV_COLS]
    uext[SUBLANES:SUBLANES + R, :] = proj[:, Q_COLS + 2 * KV_COLS:Q_COLS + 2 * KV_COLS + LRU_WIDTH]
    ug = proj[:, Q_COLS + 2 * KV_COLS + LRU_WIDTH:]
    yield

    upper = lax.broadcasted_iota(jnp.int32, (2 * WINDOW, 1), 0) >= WINDOW
    lo_lane = lax.broadcasted_iota(jnp.int32, (WINDOW, LANES), 1) < HEAD_DIM
    qs = q * (HEAD_DIM ** -0.5 * LOG2E)
    for s in range(R // WINDOW):
        kk = kext[s * WINDOW:s * WINDOW + 2 * WINDOW, :].astype(BF16)
        vv = vext[s * WINDOW:s * WINDOW + 2 * WINDOW, :].astype(BF16)
        table = jnp.where(j == 0, 1, 0) if s == 0 else 0
        for g in range(HEAD_GROUPS):
            qg = qs[s * WINDOW:(s + 1) * WINDOW, g * LANES:(g + 1) * LANES]
            q2 = jnp.concatenate([jnp.where(lo_lane, qg, 0.0), jnp.where(lo_lane, 0.0, qg)], axis=0)
            sc = lax.dot_general(q2.astype(BF16), kk, (((1,), (1,)), ((), ())),
                                 preferred_element_type=F32) + bias_s[g, table]
            sink = jnp.where(upper, sinks_ref[g + HEAD_GROUPS], sinks_ref[g]) * LOG2E
            m = jnp.maximum(jnp.max(sc, axis=-1, keepdims=True), sink)
            p = jnp.exp2(sc - m)
            den = jnp.sum(p, axis=-1, keepdims=True) + jnp.exp2(sink - m)
            o = jnp.dot(p.astype(BF16), vv, preferred_element_type=F32) / den
            attn_buf[s * WINDOW:(s + 1) * WINDOW, g * LANES:(g + 1) * LANES] = jnp.where(
                lo_lane, o[:WINDOW], o[WINDOW:])
            yield

    xc = conv_b_ref[...]
    for tap in range(CONV_WIDTH):
        off = SUBLANES - (CONV_WIDTH - 1) + tap
        xc = xc + uext[off:off + R, :] * conv_w_ref[tap:tap + 1, :]
    a, bv = _rglru_gates(xc, wa_ref, ba_ref, wi_ref, bi_ref, lam_ref)
    yield
    groups = R // SUBLANES
    a3 = a.reshape(groups, SUBLANES, LRU_WIDTH)
    b3 = bv.reshape(groups, SUBLANES, LRU_WIDTH)
    sub = lax.broadcasted_iota(jnp.int32, (groups, SUBLANES, LRU_WIDTH), 1)
    d = 1
    while d < SUBLANES:
        keep = sub >= d
        a_prev = jnp.where(keep, pltpu.roll(a3, d, 1), 1.0)
        b_prev = jnp.where(keep, pltpu.roll(b3, d, 1), 0.0)
        b3 = a3 * b_prev + b3
        a3 = a3 * a_prev
        d *= 2
    h_prev = hcar[...]
    h_groups = []
    for c in range(groups):
        hc = a3[c] * h_prev + b3[c]
        h_groups.append(hc)
        h_prev = hc[SUBLANES - 1:SUBLANES, :]
    h = jnp.concatenate(h_groups, axis=0)
    hcar[...] = h_prev
    rnn = h * _gelu_tanh(ug)
    yield

    mix_in = jnp.concatenate([attn_buf[...], rnn], axis=1)
    h1 = _post_mix(x, mix_in, w_out_ref, b_out_ref, ln_g_ref, ln_b_ref)
    h1_ref[0] = h1
    yield
    te, tg, cnt = _route(h1, rw_ref, rb_ref)
    tg_ref[0] = tg
    cnt_ref[0] = cnt
    yield
    picked, dest = _compact_tile(h1, te, cnt)
    xt_ref[0] = picked
    dest_ref[0] = dest
    return h_prev


def _const_spec(shape):
    return pl.BlockSpec(shape, lambda *_: (0,) * len(shape))


def _mixer_prompt(x, sinks, wts):
    B, S, _ = x.shape
    assert B % SEQ_PAIR == 0 and S % TILE == 0
    nj = S // TILE
    P = SEQ_PAIR
    tile_idx = lambda b, j, *_: (j, b, 0, 0)
    batch_idx = lambda b, j, *_: (b, 0, 0)
    in_specs = [pl.BlockSpec((P, TILE, D_MODEL), lambda b, j, *_: (b, j, 0))]
    in_specs += [_const_spec(w.shape) for w in wts]
    out_shape = (
        jax.ShapeDtypeStruct((B, S, D_MODEL), F32),
        jax.ShapeDtypeStruct((B * nj, TILE_ROWS, D_MODEL), F32),
        jax.ShapeDtypeStruct((nj, B, TOP_K, TILE), jnp.int32),
        jax.ShapeDtypeStruct((nj, B, TOP_K, TILE), F32),
        jax.ShapeDtypeStruct((nj, B, N_EXPERTS, 1), jnp.int32),
        jax.ShapeDtypeStruct((B, WINDOW, KV_COLS), F32),
        jax.ShapeDtypeStruct((B, WINDOW, KV_COLS), F32),
        jax.ShapeDtypeStruct((B, 1, LRU_WIDTH), F32),
        jax.ShapeDtypeStruct((B, CONV_WIDTH - 1, LRU_WIDTH), F32),
    )
    out_specs = (
        pl.BlockSpec((P, TILE, D_MODEL), lambda b, j, *_: (b, j, 0)),
        pl.BlockSpec((P, TILE_ROWS, D_MODEL), lambda b, j, *_: (j * (B // P) + b, 0, 0)),
        pl.BlockSpec((1, P, TOP_K, TILE), tile_idx),
        pl.BlockSpec((1, P, TOP_K, TILE), tile_idx),
        pl.BlockSpec((1, P, N_EXPERTS, 1), tile_idx),
        pl.BlockSpec((P, WINDOW, KV_COLS), batch_idx),
        pl.BlockSpec((P, WINDOW, KV_COLS), batch_idx),
        pl.BlockSpec((P, 1, LRU_WIDTH), batch_idx),
        pl.BlockSpec((P, CONV_WIDTH - 1, LRU_WIDTH), batch_idx),
    )
    scratch = [
        pltpu.VMEM((P, TILE + WINDOW, KV_COLS), F32),
        pltpu.VMEM((P, TILE + WINDOW, KV_COLS), F32),
        pltpu.VMEM((P, TILE + SUBLANES, LRU_WIDTH), F32),
        pltpu.VMEM((P, 1, LRU_WIDTH), F32),
        pltpu.VMEM((P, TILE, Q_COLS), F32),
        pltpu.VMEM((HEAD_GROUPS, 2, 2 * WINDOW, 2 * WINDOW), F32),
    ]
    return pl.pallas_call(
        _mixer_prompt_kernel,
        grid_spec=pltpu.PrefetchScalarGridSpec(
            num_scalar_prefetch=1, grid=(B // P, nj), in_specs=in_specs, out_specs=out_specs,
            scratch_shapes=scratch),
        out_shape=out_shape,
        compiler_params=pltpu.CompilerParams(
            dimension_semantics=("arbitrary", "arbitrary"), vmem_limit_bytes=VMEM_LIMIT),
        name="mixer_prompt",
    )(sinks, x, *wts)


SEQ_CHUNK = 16


def _mixer_sample_kernel(sinks_ref, x_ref, ck_ref, cv_ref, h0_ref, cprev_ref,
                         w_in_ref, b_in_ref, conv_w_ref, conv_b_ref,
                         wa_ref, ba_ref, wi_ref, bi_ref, lam_ref, w_out_ref, b_out_ref,
                         ln_g_ref, ln_b_ref, rw_ref, rb_ref,
                         h1_ref, xt_ref, dest_ref, tg_ref, cnt_ref, wk_ref, wv_ref, lh_ref, cnew_ref,
                         proj_s, attn_s, attn_c):
    c = pl.program_id(0)
    nc = pl.num_programs(0)
    nseq = x_ref.shape[0]

    @pl.when(c == 0)
    def _():
        proj_s[...] = jnp.dot(x_ref[...].astype(BF16), w_in_ref[...],
                              preferred_element_type=F32) + b_in_ref[...]

    sub = lax.broadcasted_iota(jnp.int32, (N_Q_HEADS, LANES), 0)
    lane = lax.broadcasted_iota(jnp.int32, (N_Q_HEADS, LANES), 1)
    own_half = (lane < HEAD_DIM) == (sub < HEAD_GROUPS)
    sub1 = sub[:, 0:1]
    slope = jnp.zeros((N_Q_HEADS, 1), F32)
    sink = jnp.zeros((N_Q_HEADS, 1), F32)
    for hd in range(N_Q_HEADS):
        slope = jnp.where(sub1 == hd, 2.0 ** -(hd + 1), slope)
        sink = jnp.where(sub1 == hd, sinks_ref[hd], sink)
    dist = (WINDOW - lax.broadcasted_iota(jnp.int32, (1, WINDOW), 1)).astype(F32)
    lo_row = lax.broadcasted_iota(jnp.int32, (1, LANES), 1) < HEAD_DIM
    scale = HEAD_DIM ** -0.5

    for i in range(SEQ_CHUNK):
        b = c * SEQ_CHUNK + i
        prow = proj_s[pl.ds(b, 1), :]
        q8 = jnp.zeros((N_Q_HEADS, LANES), F32)
        for g in range(HEAD_GROUPS):
            qg = jnp.broadcast_to(prow[:, g * LANES:(g + 1) * LANES], (N_Q_HEADS, LANES))
            q8 = jnp.where(((sub & (HEAD_GROUPS - 1)) == g) & own_half, qg, q8)
        k_new = prow[:, Q_COLS:Q_COLS + KV_COLS]
        v_new = prow[:, Q_COLS + KV_COLS:Q_COLS + 2 * KV_COLS]
        kb = ck_ref[i]
        vb = cv_ref[i]
        sc = lax.dot_general(q8.astype(BF16), kb.astype(BF16), (((1,), (1,)), ((), ())),
                             preferred_element_type=F32) * scale - slope * dist
        sc_new = jnp.sum(q8 * k_new, axis=-1, keepdims=True) * scale
        m = jnp.maximum(jnp.maximum(jnp.max(sc, axis=-1, keepdims=True), sc_new), sink)
        p = jnp.exp(sc - m)
        p_new = jnp.exp(sc_new - m)
        den = jnp.sum(p, axis=-1, keepdims=True) + p_new + jnp.exp(sink - m)
        o = (jnp.dot(p.astype(BF16), vb.astype(BF16), preferred_element_type=F32)
             + p_new * v_new) / den
        for g in range(HEAD_GROUPS):
            attn_c[i:i + 1, g * LANES:(g + 1) * LANES] = jnp.where(
                lo_row, o[g:g + 1, :], o[g + HEAD_GROUPS:g + HEAD_GROUPS + 1, :])
        wk_ref[i, 0:WINDOW - 1, :] = ck_ref[i, 1:WINDOW, :]
        wk_ref[i, WINDOW - 1:WINDOW, :] = k_new
        wv_ref[i, 0:WINDOW - 1, :] = cv_ref[i, 1:WINDOW, :]
        wv_ref[i, WINDOW - 1:WINDOW, :] = v_new
    attn_s[pl.ds(pl.multiple_of(c * SEQ_CHUNK, SEQ_CHUNK), SEQ_CHUNK), :] = attn_c[...]

    @pl.when(c == nc - 1)
    def _():
        x = x_ref[...]
        ux = proj_s[:, Q_COLS + 2 * KV_COLS:Q_COLS + 2 * KV_COLS + LRU_WIDTH]
        ug = proj_s[:, Q_COLS + 2 * KV_COLS + LRU_WIDTH:]
        xc = conv_b_ref[...]
        for tap in range(CONV_WIDTH - 1):
            xc = xc + cprev_ref[tap] * conv_w_ref[tap:tap + 1, :]
        xc = xc + ux * conv_w_ref[CONV_WIDTH - 1:CONV_WIDTH, :]
        a, bv = _rglru_gates(xc, wa_ref, ba_ref, wi_ref, bi_ref, lam_ref)
        h = a * h0_ref[...] + bv
        rnn = h * _gelu_tanh(ug)
        mix_in = jnp.concatenate([attn_s[...], rnn], axis=1)
        h1 = _post_mix(x, mix_in, w_out_ref, b_out_ref, ln_g_ref, ln_b_ref)
        h1_ref[...] = h1
        te, tg, cnt = _route(h1, rw_ref, rb_ref)
        cnt_ref[0] = cnt
        tg_ref[0] = jnp.concatenate([tg, jnp.zeros((TOP_K, TILE - nseq), F32)], axis=1)
        te_tile = jnp.concatenate([te, jnp.full((TOP_K, TILE - nseq), -1, jnp.int32)], axis=1)
        h1_tile = jnp.concatenate([h1, jnp.zeros((TILE - nseq, D_MODEL), F32)], axis=0)
        picked, dest = _compact_tile(h1_tile, te_tile, cnt)
        xt_ref[0] = picked
        dest_ref[0] = dest
        lh_ref[...] = h
        for tap in range(1, CONV_WIDTH - 1):
            cnew_ref[tap - 1] = cprev_ref[tap]
        cnew_ref[CONV_WIDTH - 2] = ux


def _mixer_sample(x, ck, cv, h0, cprev, sinks, wts):
    nseq = x.shape[0]
    assert nseq % SEQ_CHUNK == 0 and nseq <= TILE and nseq % LANES == 0
    nc = nseq // SEQ_CHUNK
    chunk_idx = lambda c, *_: (c, 0, 0)
    in_specs = [
        _const_spec((nseq, D_MODEL)),
        pl.BlockSpec((SEQ_CHUNK, WINDOW, KV_COLS), chunk_idx),
        pl.BlockSpec((SEQ_CHUNK, WINDOW, KV_COLS), chunk_idx),
        _const_spec((nseq, LRU_WIDTH)),
        _const_spec((CONV_WIDTH - 1, nseq, LRU_WIDTH)),
    ] + [_const_spec(w.shape) for w in wts]
    out_shape = (
        jax.ShapeDtypeStruct((nseq, D_MODEL), F32),
        jax.ShapeDtypeStruct((1, TILE_ROWS, D_MODEL), F32),
        jax.ShapeDtypeStruct((1, TOP_K, TILE), jnp.int32),
        jax.ShapeDtypeStruct((1, TOP_K, TILE), F32),
        jax.ShapeDtypeStruct((1, N_EXPERTS, 1), jnp.int32),
        jax.ShapeDtypeStruct((nseq, WINDOW, KV_COLS), F32),
        jax.ShapeDtypeStruct((nseq, WINDOW, KV_COLS), F32),
        jax.ShapeDtypeStruct((nseq, LRU_WIDTH), F32),
        jax.ShapeDtypeStruct((CONV_WIDTH - 1, nseq, LRU_WIDTH), F32),
    )
    out_specs = (
        _const_spec((nseq, D_MODEL)),
        _const_spec((1, TILE_ROWS, D_MODEL)),
        _const_spec((1, TOP_K, TILE)),
        _const_spec((1, TOP_K, TILE)),
        _const_spec((1, N_EXPERTS, 1)),
        pl.BlockSpec((SEQ_CHUNK, WINDOW, KV_COLS), chunk_idx),
        pl.BlockSpec((SEQ_CHUNK, WINDOW, KV_COLS), chunk_idx),
        _const_spec((nseq, LRU_WIDTH)),
        _const_spec((CONV_WIDTH - 1, nseq, LRU_WIDTH)),
    )
    scratch = [pltpu.VMEM((nseq, D_IN), F32), pltpu.VMEM((nseq, Q_COLS), F32),
               pltpu.VMEM((SEQ_CHUNK, Q_COLS), F32)]
    return pl.pallas_call(
        _mixer_sample_kernel,
        grid_spec=pltpu.PrefetchScalarGridSpec(
            num_scalar_prefetch=1, grid=(nc,), in_specs=in_specs, out_specs=out_specs,
            scratch_shapes=scratch),
        out_shape=out_shape,
        compiler_params=pltpu.CompilerParams(
            dimension_semantics=("arbitrary",), vmem_limit_bytes=VMEM_LIMIT),
        name="mixer_sample",
    )(sinks, x, ck, cv, h0, cprev, *wts)


def _grouped(rows):
    assert rows % SEG_ALIGN == 0
    return (rows // SEG_ALIGN, SEG_ALIGN, D_MODEL)


BLOCK_GROUPS = ROW_BLOCK // SEG_ALIGN
TILE_GROUPS = TILE_ROWS // SEG_ALIGN


def _experts_kernel(be_ref, nxt_ref, nb_ref, tlo_ref, thi_ref, valid_ref,
                    seg_len_ref, seg_src_ref, seg_off_ref,
                    xtp_ref, xts_ref, wgu_ref, bgu_ref, wdn_ref, bdn_ref, yb_ref,
                    xbuf, wgu_f, wdn_f, wgu_s, wdn_s, wsem, xsem, *, n_tiles, n_prompt_tiles):
    b = pl.program_id(0)
    slot = b % 2

    def weight_copies(e):
        return (pltpu.make_async_copy(wgu_ref.at[e], wgu_f, wsem.at[0]),
                pltpu.make_async_copy(wdn_ref.at[e], wdn_f, wsem.at[1]))

    def start_gather(blk, s):
        lo_b = blk * BLOCK_GROUPS
        first = be_ref[blk] * n_tiles

        t_lo = tlo_ref[blk]
        t_hi = thi_ref[blk]

        def piece(t):
            off = seg_off_ref[first + t]
            lo = jnp.maximum(off, lo_b)
            n = jnp.minimum(off + seg_len_ref[first + t], lo_b + BLOCK_GROUPS) - lo

            src = pl.ds(seg_src_ref[first + t] + lo, n)
            dst = xbuf.at[s, pl.ds(lo - lo_b, n)]
            wanted = (n > 0) & (t <= t_hi)

            @pl.when(wanted & (t < n_prompt_tiles))
            def _():
                pltpu.make_async_copy(xtp_ref.at[src], dst, xsem.at[s]).start()

            @pl.when(wanted & (t >= n_prompt_tiles))
            def _():
                pltpu.make_async_copy(xts_ref.at[src], dst, xsem.at[s]).start()

        def two_pieces(i, carry):
            piece(t_lo + 2 * i)
            piece(t_lo + 2 * i + 1)
            return carry

        lax.fori_loop(0, (t_hi - t_lo + 2) // 2, two_pieces, 0)

    @pl.when(b == 0)
    def _():
        xbuf[...] = jnp.zeros(xbuf.shape, F32)
        for cp in weight_copies(be_ref[0]):
            cp.start()
        start_gather(b, slot)

    @pl.when(b + 1 < nb_ref[0])
    def _():
        start_gather(b + 1, 1 - slot)

    @pl.when(b < nb_ref[0])
    def _():
        n = valid_ref[b]
        pltpu.make_async_copy(xtp_ref.at[pl.ds(0, n)], xbuf.at[slot, pl.ds(0, n)], xsem.at[slot]).wait()

        @pl.when((b == 0) | (be_ref[b] != be_ref[jnp.maximum(b - 1, 0)]))
        def _():
            for cp in weight_copies(be_ref[b]):
                cp.wait()
            wgu_s[...] = wgu_f[...].astype(BF16)
            wdn_s[...] = wdn_f[...].astype(BF16)

            @pl.when(nxt_ref[b] >= 0)
            def _():
                for cp in weight_copies(nxt_ref[b]):
                    cp.start()

        x = xbuf[slot].reshape(ROW_BLOCK, D_MODEL).astype(BF16)
        hgu = jnp.dot(x, wgu_s[...], preferred_element_type=F32) + bgu_ref[0]
        glu = jnp.minimum(hgu[:, :D_FF], SWIGLU_LIMIT)
        lin = jnp.clip(hgu[:, D_FF:], -SWIGLU_LIMIT, SWIGLU_LIMIT)
        act = glu * jax.nn.sigmoid(SWIGLU_ALPHA * glu) * (lin + 1.0)
        yb_ref[...] = jnp.dot(act.astype(BF16), wdn_s[...], preferred_element_type=F32) + bdn_ref[0]

    @pl.when(b >= nb_ref[0])
    def _():
        yb_ref[...] = jnp.zeros(yb_ref.shape, F32)


def _experts(sched, segs, xt_prompt, xt_sample, w_gu, b_gu, w_dn, b_dn, n_rows):
    n_blocks = n_rows // ROW_BLOCK
    n_prompt_tiles = xt_prompt.shape[0] // TILE_GROUPS
    n_tiles = n_prompt_tiles + xt_sample.shape[0] // TILE_GROUPS
    n_prefetch = len(sched) + len(segs)
    exp_idx = lambda b, be, *_: (be[b], 0, 0)
    in_specs = [
        pl.BlockSpec(memory_space=pl.ANY),
        pl.BlockSpec(memory_space=pl.ANY),
        pl.BlockSpec(memory_space=pl.ANY),
        pl.BlockSpec((1, 1, 2 * D_FF), exp_idx),
        pl.BlockSpec(memory_space=pl.ANY),
        pl.BlockSpec((1, 1, D_MODEL), exp_idx),
    ]
    scratch = [pltpu.VMEM((2,) + _grouped(ROW_BLOCK), F32),
               pltpu.VMEM((D_MODEL, 2 * D_FF), F32), pltpu.VMEM((D_FF, D_MODEL), F32),
               pltpu.VMEM((D_MODEL, 2 * D_FF), BF16), pltpu.VMEM((D_FF, D_MODEL), BF16),
               pltpu.SemaphoreType.DMA((2,)), pltpu.SemaphoreType.DMA((2,))]
    return pl.pallas_call(
        functools.partial(_experts_kernel, n_tiles=n_tiles, n_prompt_tiles=n_prompt_tiles),
        grid_spec=pltpu.PrefetchScalarGridSpec(
            num_scalar_prefetch=n_prefetch, grid=(n_blocks,), in_specs=in_specs,
            out_specs=pl.BlockSpec((ROW_BLOCK, D_MODEL), lambda b, *_: (b, 0)),
            scratch_shapes=scratch),
        out_shape=jax.ShapeDtypeStruct((n_rows, D_MODEL), F32),
        compiler_params=pltpu.CompilerParams(
            dimension_semantics=("arbitrary",), vmem_limit_bytes=VMEM_LIMIT),
        name="moe_experts",
    )(*sched, *segs, xt_prompt, xt_sample, w_gu, b_gu[:, None, :], w_dn, b_dn[:, None, :])


def _tile_tokens(i, n_prompt_tiles, hp_ref, hs_ref):
    hs = hs_ref[...]
    hs_tile = jnp.concatenate([hs, jnp.zeros((TILE - hs.shape[0], D_MODEL), F32)], axis=0)
    return jnp.where(i < n_prompt_tiles, hp_ref[...], hs_tile)


COMBINE_SLOTS = 3


def _combine_kernel(seg_len_ref, seg_start_ref, seg_off_ref, tile_groups_ref,
                    hp_ref, hs_ref, dest_ref, gate_ref, g_ref, b_ref, yb_ref,
                    yp_ref, ys_ref, ybuf, sem, *, n_prompt_tiles):
    i = pl.program_id(0)
    n = pl.num_programs(0)
    slot = i % COMBINE_SLOTS

    def start_gather(tile):
        s = tile % COMBINE_SLOTS

        def segment(e, carry):
            idx = tile * N_EXPERTS + e
            groups = seg_len_ref[idx]

            @pl.when(groups > 0)
            def _():
                pltpu.make_async_copy(yb_ref.at[pl.ds(seg_off_ref[idx], groups)],
                                      ybuf.at[s, pl.ds(seg_start_ref[idx], groups)], sem.at[s]).start()
            return carry
        lax.fori_loop(0, N_EXPERTS, segment, 0)

    @pl.when(i == 0)
    def _():
        ybuf[...] = jnp.zeros(ybuf.shape, F32)
        for ahead in range(COMBINE_SLOTS - 1):
            @pl.when(ahead < n)
            def _():
                start_gather(i + ahead)

    @pl.when(i + COMBINE_SLOTS - 1 < n)
    def _():
        start_gather(i + COMBINE_SLOTS - 1)

    total = tile_groups_ref[i]

    @pl.when(total > 0)
    def _():
        pltpu.make_async_copy(yb_ref.at[pl.ds(0, total)], ybuf.at[slot, pl.ds(0, total)], sem.at[slot]).wait()

    dest = dest_ref[0]
    gate = gate_ref[0]
    rid = lax.broadcasted_iota(jnp.int32, (TILE_ROWS, TILE), 0)
    weights = jnp.zeros((TILE_ROWS, TILE), F32)
    for k in range(TOP_K):
        weights = jnp.where(rid == dest[k:k + 1, :], gate[k:k + 1, :], weights)
    rows = ybuf[slot].reshape(TILE_ROWS, D_MODEL).astype(BF16)
    ff = lax.dot_general(weights.astype(BF16), rows, (((0,), (0,)), ((), ())),
                         preferred_element_type=F32)
    h1 = _tile_tokens(i, n_prompt_tiles, hp_ref, hs_ref)
    y = _layer_norm(DN_ALPHA * h1 + ff, g_ref[...], b_ref[...])

    @pl.when(i < n_prompt_tiles)
    def _():
        yp_ref[...] = y

    @pl.when(i >= n_prompt_tiles)
    def _():
        ys_ref[...] = y[:ys_ref.shape[0]]


def _combine(segs, tile_groups, n_seq, h1p, h1s, dest, gate, ln_g, ln_b, yb):
    n_tiles = dest.shape[0]
    n_prompt_tiles = h1p.shape[0] // TILE
    blocks_per_seq = n_prompt_tiles // n_seq

    def prompt_idx(i, *_):
        t = jnp.minimum(i, n_prompt_tiles - 1)
        return ((t % n_seq) * blocks_per_seq + t // n_seq, 0)
    in_specs = [
        pl.BlockSpec((TILE, D_MODEL), prompt_idx),
        _const_spec(h1s.shape),
        pl.BlockSpec((1, TOP_K, TILE), lambda i, *_: (i, 0, 0)),
        pl.BlockSpec((1, TOP_K, TILE), lambda i, *_: (i, 0, 0)),
        _const_spec(ln_g.shape),
        _const_spec(ln_b.shape),
        pl.BlockSpec(memory_space=pl.ANY),
    ]
    out_shape = (jax.ShapeDtypeStruct(h1p.shape, F32), jax.ShapeDtypeStruct(h1s.shape, F32))
    out_specs = (pl.BlockSpec((TILE, D_MODEL), prompt_idx), _const_spec(h1s.shape))
    return pl.pallas_call(
        functools.partial(_combine_kernel, n_prompt_tiles=n_prompt_tiles),
        grid_spec=pltpu.PrefetchScalarGridSpec(
            num_scalar_prefetch=len(segs) + 1, grid=(n_tiles,), in_specs=in_specs, out_specs=out_specs,
            scratch_shapes=[pltpu.VMEM((COMBINE_SLOTS,) + _grouped(TILE_ROWS), F32),
                            pltpu.SemaphoreType.DMA((COMBINE_SLOTS,))]),
        out_shape=out_shape,
        compiler_params=pltpu.CompilerParams(
            dimension_semantics=("arbitrary",), vmem_limit_bytes=VMEM_LIMIT),
        name="moe_combine",
    )(*segs, tile_groups, h1p, h1s, dest, gate, ln_g, ln_b, yb.reshape(_grouped(yb.shape[0])))


def _moe_layout(cnt, n_prompt_tiles):
    n_tiles = cnt.shape[0]
    i32 = lambda a: a.astype(jnp.int32)
    seg_len = (cnt + SEG_ALIGN - 1) // SEG_ALIGN
    seg_start = jnp.cumsum(seg_len, axis=1) - seg_len
    exp_len = jnp.sum(seg_len, axis=0)
    exp_blocks = (exp_len + BLOCK_GROUPS - 1) // BLOCK_GROUPS
    blocks_end = jnp.cumsum(exp_blocks)
    first_block = blocks_end - exp_blocks
    seg_off = first_block[None, :] * BLOCK_GROUPS + jnp.cumsum(seg_len, axis=0) - seg_len
    max_rows = n_tiles * (TOP_K * TILE + N_EXPERTS * (SEG_ALIGN - 1)) + N_EXPERTS * (ROW_BLOCK - SEG_ALIGN)
    n_blocks = -(-max_rows // ROW_BLOCK)
    n_used = blocks_end[-1]
    blk = jnp.minimum(jnp.arange(n_blocks, dtype=jnp.int32), n_used - 1)
    eid = jnp.arange(N_EXPERTS, dtype=jnp.int32)
    block_expert = jnp.minimum(jnp.sum(blocks_end[None, :] <= blk[:, None], axis=1), N_EXPERTS - 1)
    own = block_expert[:, None] == eid[None, :]
    pick = lambda per_expert: jnp.sum(jnp.where(own, per_expert[None, :], 0), axis=1)
    later_used = (eid[None, :] > eid[:, None]) & (exp_blocks[None, :] > 0)
    next_of = jnp.min(jnp.where(later_used, eid[None, :], N_EXPERTS), axis=1)
    next_expert = pick(jnp.where(next_of < N_EXPERTS, next_of, -1))
    lo = blk * BLOCK_GROUPS
    off_b = jnp.sum(jnp.where(own[:, None, :], seg_off[None, :, :], 0), axis=2)
    len_b = jnp.sum(jnp.where(own[:, None, :], seg_len[None, :, :], 0), axis=2)
    tile_lo = jnp.sum(off_b + len_b <= lo[:, None], axis=1)
    tile_hi = jnp.sum(off_b < lo[:, None] + BLOCK_GROUPS, axis=1) - 1
    valid = jnp.clip(pick(first_block * BLOCK_GROUPS + exp_len) - lo, 0, BLOCK_GROUPS)
    flat = lambda a: i32(a.reshape(-1))
    sched = (i32(block_expert), i32(next_expert), i32(n_used.reshape(1)), i32(tile_lo), i32(tile_hi),
             i32(valid))
    segs = (flat(seg_len), flat(seg_start), flat(seg_off))
    tile = jnp.arange(n_tiles, dtype=seg_off.dtype)
    tile_in_array = jnp.where(tile < n_prompt_tiles, tile, tile - n_prompt_tiles)
    seg_src = tile_in_array[:, None] * TILE_GROUPS + seg_start - seg_off
    pad1 = lambda a: jnp.concatenate([flat(a.T), jnp.zeros((1,), jnp.int32)])
    segs_by_expert = (pad1(seg_len), pad1(seg_src), pad1(seg_off))
    return sched, segs, segs_by_expert, i32(jnp.sum(seg_len, axis=1)), n_blocks * ROW_BLOCK


def _prep_weights(w_in, b_in, conv_w, conv_b, lru_w_a, lru_b_a, lru_w_i, lru_b_i, lru_lambda,
                  w_out, b_out, ln1_g, ln1_b, router_w, router_b):
    def regroup_heads(a):
        rest = a.shape[1:]
        q = a[:Q_COLS].reshape(N_KV_HEADS, HEAD_GROUPS, HEAD_DIM, *rest)
        q = jnp.swapaxes(q, 0, 1).reshape(Q_COLS, *rest)
        return jnp.concatenate([q, a[Q_COLS:]], axis=0)

    def diag_tiles(w):
        per = MXU_DIM // (LRU_WIDTH // LRU_BLOCKS)
        w4 = w.reshape(LRU_BLOCKS // per, per, LRU_WIDTH // LRU_BLOCKS, LRU_WIDTH // LRU_BLOCKS)
        t = jnp.einsum("taij,ab->taibj", w4, jnp.eye(per, dtype=w.dtype))
        return t.reshape(LRU_BLOCKS // per, MXU_DIM, MXU_DIM).astype(BF16)

    return (
        regroup_heads(w_in[0].T).T.astype(BF16), regroup_heads(b_in[0])[None],
        conv_w[0], conv_b[0][None],
        diag_tiles(lru_w_a[0]), lru_b_a[0].reshape(1, LRU_WIDTH),
        diag_tiles(lru_w_i[0]), lru_b_i[0].reshape(1, LRU_WIDTH),
        lru_lambda[0][None],
        regroup_heads(w_out[0]).astype(BF16), b_out[0][None],
        ln1_g[0][None], ln1_b[0][None],
        router_w[0].T.astype(BF16), router_b[0][:, None],
    )


def kernel(x_prompt, x_sample, cache_win_k, cache_win_v, state_lru_h, state_conv, w_in, b_in, attn_sinks, conv_w, conv_b, lru_w_a, lru_b_a, lru_w_i, lru_b_i, lru_lambda, w_out, b_out, ln1_g, ln1_b, router_w, router_b, w_gate_up, b_gate_up, w_down, b_down, ln2_g, ln2_b):
    assert w_in.shape[0] == 1, "single-layer step"
    B, S, _ = x_prompt.shape
    nseq = x_sample.shape[0]
    assert x_sample.shape[1] == 1 and S % TILE == 0
    wts = _prep_weights(w_in, b_in, conv_w, conv_b, lru_w_a, lru_b_a, lru_w_i, lru_b_i, lru_lambda,
                        w_out, b_out, ln1_g, ln1_b, router_w, router_b)
    sinks = attn_sinks[0]

    h1p, xt_p, dest_p, tg_p, cnt_p, pk, pv, ph, pc = _mixer_prompt(x_prompt, sinks, wts)
    h1s, xt_s, dest_s, tg_s, cnt_s, sk, sv, sh, sc = _mixer_sample(
        x_sample.reshape(nseq, D_MODEL),
        cache_win_k[0].reshape(nseq, WINDOW, KV_COLS), cache_win_v[0].reshape(nseq, WINDOW, KV_COLS),
        state_lru_h[0], jnp.transpose(state_conv[0], (1, 0, 2)), sinks, wts)

    per_tile = lambda a: a.reshape(-1, *a.shape[2:])
    dest = jnp.concatenate([per_tile(dest_p), dest_s], axis=0)
    tg = jnp.concatenate([per_tile(tg_p), tg_s], axis=0)
    cnt = jnp.concatenate([per_tile(cnt_p), cnt_s], axis=0)[:, :, 0]
    sched, segs, segs_by_expert, tile_groups, n_rows = _moe_layout(cnt, xt_p.shape[0])

    end_to_end = lambda a: a.reshape(-1, SEG_ALIGN, D_MODEL)
    yb = _experts(sched, segs_by_expert, end_to_end(xt_p), end_to_end(xt_s),
                  w_gate_up[0], b_gate_up[0], w_down[0], b_down[0], n_rows)
    yp, ys = _combine(segs, tile_groups, B, h1p.reshape(B * S, D_MODEL), h1s, dest, tg,
                      ln2_g[0][None], ln2_b[0][None], yb)

    kv_shape = (N_KV_HEADS, HEAD_DIM)
    return (
        yp.reshape(B, S, D_MODEL), ys.reshape(nseq, 1, D_MODEL),
        pk.reshape(1, B, WINDOW, *kv_shape), pv.reshape(1, B, WINDOW, *kv_shape),
        ph.reshape(1, B, LRU_WIDTH), pc[None],
        sk.reshape(1, nseq, WINDOW, *kv_shape), sv.reshape(1, nseq, WINDOW, *kv_shape),
        sh[None], jnp.transpose(sc, (1, 0, 2))[None],
    )
```

```python
import functools

import jax
import jax.numpy as jnp
import numpy as np
from jax import lax
from jax.experimental import pallas as pl
from jax.experimental.pallas import tpu as pltpu

F32 = jnp.float32
BF16 = jnp.bfloat16

D_MODEL = 1024
N_Q_HEADS = 8
N_KV_HEADS = 2
HEAD_DIM = 64
WINDOW = 128
Q_COLS = N_Q_HEADS * HEAD_DIM
KV_COLS = N_KV_HEADS * HEAD_DIM
LRU_WIDTH = 512
LRU_BLOCKS = 8
CONV_WIDTH = 4
RG_C = 8.0
N_EXPERTS = 32
TOP_K = 4
D_FF = 1024
SWIGLU_LIMIT = 7.0
SWIGLU_ALPHA = 1.702
LN_EPS = 1e-5
DN_ALPHA = 2.0 ** 0.25
NEG_INF = -1e30
LOG2E = 1.4426950408889634
D_IN = Q_COLS + 2 * KV_COLS + 2 * LRU_WIDTH

LANES = 128
SUBLANES = 8
MXU_DIM = 256

TILE = 256
SEG_ALIGN = SUBLANES
ROW_BLOCK = 512
HEAD_GROUPS = N_Q_HEADS // N_KV_HEADS
VMEM_LIMIT = 56 * 1024 * 1024

TILE_ROWS = 1280
assert TILE_ROWS >= TOP_K * TILE + N_EXPERTS * (SEG_ALIGN - 1) and TILE_ROWS % LANES == 0


def _layer_norm(z, g, b):
    mu = jnp.mean(z, axis=-1, keepdims=True)
    zc = z - mu
    var = jnp.mean(zc * zc, axis=-1, keepdims=True)
    return zc * lax.rsqrt(var + LN_EPS) * g + b


def _sigmoid(x):
    return 0.5 + 0.5 * jnp.tanh(0.5 * x)


def _softplus(x):
    return jnp.maximum(x, 0.0) + jnp.log1p(jnp.exp(-jnp.abs(x)))


def _gelu_tanh(x):
    c = np.float32(np.sqrt(2.0 / np.pi))
    return 0.5 * x * (1.0 + jnp.tanh(c * (x + 0.044715 * (x * x * x))))


def _rglru_gates(xc, wa_ref, ba_ref, wi_ref, bi_ref, lam_ref):
    xcb = xc.astype(BF16)
    half = LRU_WIDTH // 2
    pre_a = jnp.concatenate(
        [jnp.dot(xcb[:, :half], wa_ref[0], preferred_element_type=F32),
         jnp.dot(xcb[:, half:], wa_ref[1], preferred_element_type=F32)], axis=1)
    pre_i = jnp.concatenate(
        [jnp.dot(xcb[:, :half], wi_ref[0], preferred_element_type=F32),
         jnp.dot(xcb[:, half:], wi_ref[1], preferred_element_type=F32)], axis=1)
    r = _sigmoid(pre_a + ba_ref[...])
    gi = _sigmoid(pre_i + bi_ref[...])
    log_a = (-RG_C) * r * _softplus(-lam_ref[...])
    a = jnp.exp(log_a)
    t = jnp.tanh(log_a)
    mult = jnp.sqrt(jnp.maximum(-2.0 * t / (1.0 - t), 0.0))
    return a, mult * (gi * xc)


def _route(h1, rw_ref, rb_ref):
    n = h1.shape[0]
    logits = lax.dot_general(rw_ref[...], h1.astype(BF16), (((1,), (1,)), ((), ())),
                             preferred_element_type=F32) + rb_ref[...]
    eidx = lax.broadcasted_iota(jnp.int32, (N_EXPERTS, n), 0)
    vals = logits
    top_v, top_e = [], []
    for _ in range(TOP_K):
        m = jnp.max(vals, axis=0, keepdims=True)
        idx = jnp.min(jnp.where(vals == m, eidx, N_EXPERTS), axis=0, keepdims=True)
        top_v.append(m)
        top_e.append(idx)
        vals = jnp.where(eidx == idx, -jnp.inf, vals)
    ex = [jnp.exp(v - top_v[0]) for v in top_v]
    den = ex[0] + ex[1] + ex[2] + ex[3]
    gates = jnp.concatenate([e / den for e in ex], axis=0)
    te = jnp.concatenate(top_e, axis=0)
    mem = jnp.zeros((N_EXPERTS, n), jnp.int32)
    for idx in top_e:
        mem = mem + jnp.where(eidx == idx, 1, 0)
    return te, gates, jnp.sum(mem, axis=1, keepdims=True)


def _compact_tile(h1, te, cnt):
    eidx = lax.broadcasted_iota(jnp.int32, (N_EXPERTS, TILE), 0)
    mem = jnp.zeros((N_EXPERTS, TILE), F32)
    for k in range(TOP_K):
        mem = mem + jnp.where(eidx == te[k:k + 1, :], 1.0, 0.0)
    tr = lax.broadcasted_iota(jnp.int32, (TILE, TILE), 0)
    tc = lax.broadcasted_iota(jnp.int32, (TILE, TILE), 1)
    before = jnp.where(tr < tc, 1.0, 0.0).astype(BF16)
    pos = jnp.dot(mem.astype(BF16), before, preferred_element_type=F32)
    seg_rows = ((cnt + (SEG_ALIGN - 1)) // SEG_ALIGN * SEG_ALIGN).astype(F32)
    er = lax.broadcasted_iota(jnp.int32, (N_EXPERTS, N_EXPERTS), 0)
    ec = lax.broadcasted_iota(jnp.int32, (N_EXPERTS, N_EXPERTS), 1)
    earlier = jnp.where(ec < er, 1.0, 0.0).astype(BF16)
    seg_start = jnp.dot(earlier, jnp.broadcast_to(seg_rows, (N_EXPERTS, LANES)).astype(BF16),
                        preferred_element_type=F32)[:, 0:1]
    base = (pos + seg_start).astype(jnp.int32)
    dests = []
    for k in range(TOP_K):
        d = jnp.sum(jnp.where(eidx == te[k:k + 1, :], base, 0), axis=0, keepdims=True)
        dests.append(jnp.where(te[k:k + 1, :] >= 0, d, -1))
    rid = lax.broadcasted_iota(jnp.int32, (TILE_ROWS, TILE), 0)
    onehot = jnp.zeros((TILE_ROWS, TILE), F32)
    for k in range(TOP_K):
        onehot = jnp.where(rid == dests[k], 1.0, onehot)
    picked = jnp.dot(onehot.astype(BF16), h1.astype(BF16), preferred_element_type=F32)
    return picked, jnp.concatenate(dests, axis=0)


def _post_mix(x, mix_in, w_out_ref, b_out_ref, g_ref, b_ref):
    mix = jnp.dot(mix_in.astype(BF16), w_out_ref[...], preferred_element_type=F32) + b_out_ref[...]
    return _layer_norm(DN_ALPHA * x + mix, g_ref[...], b_ref[...])


SEQ_PAIR = 2
PHASE_LAG = 0


def _interleave(chains, lag):
    results = [None] * len(chains)
    live = list(range(len(chains)))
    rnd = 0
    while live:
        for k in list(live):
            if rnd < lag * k:
                continue
            try:
                next(chains[k])
            except StopIteration as stop:
                results[k] = stop.value
                live.remove(k)
        rnd += 1
    return results


def _bias_tables(bias_s):
    row2 = lax.broadcasted_iota(jnp.int32, (2 * WINDOW, 2 * WINDOW), 0)
    col2 = lax.broadcasted_iota(jnp.int32, (2 * WINDOW, 2 * WINDOW), 1)
    dist = (row2 & (WINDOW - 1)) - col2 + WINDOW
    valid = (dist >= 0) & (dist <= WINDOW)
    distf = dist.astype(F32)
    for g in range(HEAD_GROUPS):
        slope = jnp.where(row2 >= WINDOW, 2.0 ** -(g + HEAD_GROUPS + 1), 2.0 ** -(g + 1))
        bias = jnp.where(valid, (-LOG2E) * (slope * distf), NEG_INF)
        bias_s[g, 0] = bias
        bias_s[g, 1] = jnp.where(col2 >= WINDOW, bias, NEG_INF)


def _mixer_prompt_kernel(sinks_ref, x_ref, *rest):
    wts = rest[:15]
    h1_ref, xt_ref, dest_ref, tg_ref, cnt_ref, wk_ref, wv_ref, lh_ref, cv_ref = rest[15:24]
    kext, vext, uext, hcar, attn_buf, bias_s = rest[24:]
    j = pl.program_id(1)
    nj = pl.num_programs(1)
    R = TILE
    one = lambda ref, bb: ref.at[pl.ds(bb, 1)]

    @pl.when(j == 0)
    def _():
        for bb in range(SEQ_PAIR):
            kext[bb, 0:WINDOW, :] = jnp.zeros((WINDOW, KV_COLS), F32)
            vext[bb, 0:WINDOW, :] = jnp.zeros((WINDOW, KV_COLS), F32)
            uext[bb, 0:SUBLANES, :] = jnp.zeros((SUBLANES, LRU_WIDTH), F32)
            hcar[bb] = jnp.zeros((1, LRU_WIDTH), F32)

    @pl.when((pl.program_id(0) == 0) & (j == 0))
    def _():
        _bias_tables(bias_s)

    per_tile = lambda ref, bb: ref.at[0, pl.ds(bb, 1)]
    h_last = _interleave([
        _mixer_tile(j, sinks_ref, one(x_ref, bb), wts, one(h1_ref, bb), one(xt_ref, bb),
                    per_tile(dest_ref, bb), per_tile(tg_ref, bb), per_tile(cnt_ref, bb),
                    kext.at[bb], vext.at[bb], uext.at[bb], hcar.at[bb], attn_buf.at[bb], bias_s)
        for bb in range(SEQ_PAIR)], PHASE_LAG)

    @pl.when(j == nj - 1)
    def _():
        for bb in range(SEQ_PAIR):
            wk_ref[bb] = kext[bb, R:R + WINDOW, :]
            wv_ref[bb] = vext[bb, R:R + WINDOW, :]
            lh_ref[bb] = h_last[bb]
            cv_ref[bb] = uext[bb, SUBLANES + R - (CONV_WIDTH - 1):SUBLANES + R, :]

    for bb in range(SEQ_PAIR):
        kext[bb, 0:WINDOW, :] = kext[bb, R:R + WINDOW, :]
        vext[bb, 0:WINDOW, :] = vext[bb, R:R + WINDOW, :]
        uext[bb, 0:SUBLANES, :] = uext[bb, R:R + SUBLANES, :]


def _mixer_tile(j, sinks_ref, x_ref, wts, h1_ref, xt_ref, dest_ref, tg_ref, cnt_ref,
                kext, vext, uext, hcar, attn_buf, bias_s):
    (w_in_ref, b_in_ref, conv_w_ref, conv_b_ref, wa_ref, ba_ref, wi_ref, bi_ref, lam_ref,
     w_out_ref, b_out_ref, ln_g_ref, ln_b_ref, rw_ref, rb_ref) = wts
    R = TILE
    x = x_ref[0]
    proj = jnp.dot(x.astype(BF16), w_in_ref[...], preferred_element_type=F32) + b_in_ref[...]
    q = proj[:, :Q_COLS]
    kext[WINDOW:WINDOW + R, :] = proj[:, Q_COLS:Q_COLS + KV_COLS]
    vext[WINDOW:WINDOW + R, :] = proj[:, Q_COLS + KV_COLS:Q_COLS + 2 * KV_COLS]
    uext[SUBLANES:SUBLANES + R, :] = proj[:, Q_COLS + 2 * KV_COLS:Q_COLS + 2 * KV_COLS + LRU_WIDTH]
    ug = proj[:, Q_COLS + 2 * KV_COLS + LRU_WIDTH:]
    yield

    upper = lax.broadcasted_iota(jnp.int32, (2 * WINDOW, 1), 0) >= WINDOW
    lo_lane = lax.broadcasted_iota(jnp.int32, (WINDOW, LANES), 1) < HEAD_DIM
    qs = q * (HEAD_DIM ** -0.5 * LOG2E)
    for s in range(R // WINDOW):
        kk = kext[s * WINDOW:s * WINDOW + 2 * WINDOW, :].astype(BF16)
        vv = vext[s * WINDOW:s * WINDOW + 2 * WINDOW, :].astype(BF16)
        table = jnp.where(j == 0, 1, 0) if s == 0 else 0
        for g in range(HEAD_GROUPS):
            qg = qs[s * WINDOW:(s + 1) * WINDOW, g * LANES:(g + 1) * LANES]
            q2 = jnp.concatenate([jnp.where(lo_lane, qg, 0.0), jnp.where(lo_lane, 0.0, qg)], axis=0)
            sc = lax.dot_general(q2.astype(BF16), kk, (((1,), (1,)), ((), ())),
                                 preferred_element_type=F32) + bias_s[g, table]
            sink = jnp.where(upper, sinks_ref[g + HEAD_GROUPS], sinks_ref[g]) * LOG2E
            m = jnp.maximum(jnp.max(sc, axis=-1, keepdims=True), sink)
            p = jnp.exp2(sc - m)
            den = jnp.sum(p, axis=-1, keepdims=True) + jnp.exp2(sink - m)
            o = jnp.dot(p.astype(BF16), vv, preferred_element_type=F32) / den
            attn_buf[s * WINDOW:(s + 1) * WINDOW, g * LANES:(g + 1) * LANES] = jnp.where(
                lo_lane, o[:WINDOW], o[WINDOW:])
            yield

    xc = conv_b_ref[...]
    for tap in range(CONV_WIDTH):
        off = SUBLANES - (CONV_WIDTH - 1) + tap
        xc = xc + uext[off:off + R, :] * conv_w_ref[tap:tap + 1, :]
    a, bv = _rglru_gates(xc, wa_ref, ba_ref, wi_ref, bi_ref, lam_ref)
    yield
    groups = R // SUBLANES
    a3 = a.reshape(groups, SUBLANES, LRU_WIDTH)
    b3 = bv.reshape(groups, SUBLANES, LRU_WIDTH)
    sub = lax.broadcasted_iota(jnp.int32, (groups, SUBLANES, LRU_WIDTH), 1)
    d = 1
    while d < SUBLANES:
        keep = sub >= d
        a_prev = jnp.where(keep, pltpu.roll(a3, d, 1), 1.0)
        b_prev = jnp.where(keep, pltpu.roll(b3, d, 1), 0.0)
        b3 = a3 * b_prev + b3
        a3 = a3 * a_prev
        d *= 2
    h_prev = hcar[...]
    h_groups = []
    for c in range(groups):
        hc = a3[c] * h_prev + b3[c]
        h_groups.append(hc)
        h_prev = hc[SUBLANES - 1:SUBLANES, :]
    h = jnp.concatenate(h_groups, axis=0)
    hcar[...] = h_prev
    rnn = h * _gelu_tanh(ug)
    yield

    mix_in = jnp.concatenate([attn_buf[...], rnn], axis=1)
    h1 = _post_mix(x, mix_in, w_out_ref, b_out_ref, ln_g_ref, ln_b_ref)
    h1_ref[0] = h1
    yield
    te, tg, cnt = _route(h1, rw_ref, rb_ref)
    tg_ref[0] = tg
    cnt_ref[0] = cnt
    yield
    picked, dest = _compact_tile(h1, te, cnt)
    xt_ref[0] = picked
    dest_ref[0] = dest
    return h_prev


def _const_spec(shape):
    return pl.BlockSpec(shape, lambda *_: (0,) * len(shape))


def _mixer_prompt(x, sinks, wts):
    B, S, _ = x.shape
    assert B % SEQ_PAIR == 0 and S % TILE == 0
    nj = S // TILE
    P = SEQ_PAIR
    tile_idx = lambda b, j, *_: (j, b, 0, 0)
    batch_idx = lambda b, j, *_: (b, 0, 0)
    in_specs = [pl.BlockSpec((P, TILE, D_MODEL), lambda b, j, *_: (b, j, 0))]
    in_specs += [_const_spec(w.shape) for w in wts]
    out_shape = (
        jax.ShapeDtypeStruct((B, S, D_MODEL), F32),
        jax.ShapeDtypeStruct((B * nj, TILE_ROWS, D_MODEL), F32),
        jax.ShapeDtypeStruct((nj, B, TOP_K, TILE), jnp.int32),
        jax.ShapeDtypeStruct((nj, B, TOP_K, TILE), F32),
        jax.ShapeDtypeStruct((nj, B, N_EXPERTS, 1), jnp.int32),
        jax.ShapeDtypeStruct((B, WINDOW, KV_COLS), F32),
        jax.ShapeDtypeStruct((B, WINDOW, KV_COLS), F32),
        jax.ShapeDtypeStruct((B, 1, LRU_WIDTH), F32),
        jax.ShapeDtypeStruct((B, CONV_WIDTH - 1, LRU_WIDTH), F32),
    )
    out_specs = (
        pl.BlockSpec((P, TILE, D_MODEL), lambda b, j, *_: (b, j, 0)),
        pl.BlockSpec((P, TILE_ROWS, D_MODEL), lambda b, j, *_: (j * (B // P) + b, 0, 0)),
        pl.BlockSpec((1, P, TOP_K, TILE), tile_idx),
        pl.BlockSpec((1, P, TOP_K, TILE), tile_idx),
        pl.BlockSpec((1, P, N_EXPERTS, 1), tile_idx),
        pl.BlockSpec((P, WINDOW, KV_COLS), batch_idx),
        pl.BlockSpec((P, WINDOW, KV_COLS), batch_idx),
        pl.BlockSpec((P, 1, LRU_WIDTH), batch_idx),
        pl.BlockSpec((P, CONV_WIDTH - 1, LRU_WIDTH), batch_idx),
    )
    scratch = [
        pltpu.VMEM((P, TILE + WINDOW, KV_COLS), F32),
        pltpu.VMEM((P, TILE + WINDOW, KV_COLS), F32),
        pltpu.VMEM((P, TILE + SUBLANES, LRU_WIDTH), F32),
        pltpu.VMEM((P, 1, LRU_WIDTH), F32),
        pltpu.VMEM((P, TILE, Q_COLS), F32),
        pltpu.VMEM((HEAD_GROUPS, 2, 2 * WINDOW, 2 * WINDOW), F32),
    ]
    return pl.pallas_call(
        _mixer_prompt_kernel,
        grid_spec=pltpu.PrefetchScalarGridSpec(
            num_scalar_prefetch=1, grid=(B // P, nj), in_specs=in_specs, out_specs=out_specs,
            scratch_shapes=scratch),
        out_shape=out_shape,
        compiler_params=pltpu.CompilerParams(
            dimension_semantics=("arbitrary", "arbitrary"), vmem_limit_bytes=VMEM_LIMIT),
        name="mixer_prompt",
    )(sinks, x, *wts)


SEQ_CHUNK = 16


def _mixer_sample_kernel(sinks_ref, x_ref, ck_ref, cv_ref, h0_ref, cprev_ref,
                         w_in_ref, b_in_ref, conv_w_ref, conv_b_ref,
                         wa_ref, ba_ref, wi_ref, bi_ref, lam_ref, w_out_ref, b_out_ref,
                         ln_g_ref, ln_b_ref, rw_ref, rb_ref,
                         h1_ref, xt_ref, dest_ref, tg_ref, cnt_ref, wk_ref, wv_ref, lh_ref, cnew_ref,
                         proj_s, attn_s, attn_c):
    c = pl.program_id(0)
    nc = pl.num_programs(0)
    nseq = x_ref.shape[0]

    @pl.when(c == 0)
    def _():
        proj_s[...] = jnp.dot(x_ref[...].astype(BF16), w_in_ref[...],
                              preferred_element_type=F32) + b_in_ref[...]

    sub = lax.broadcasted_iota(jnp.int32, (N_Q_HEADS, LANES), 0)
    lane = lax.broadcasted_iota(jnp.int32, (N_Q_HEADS, LANES), 1)
    own_half = (lane < HEAD_DIM) == (sub < HEAD_GROUPS)
    sub1 = sub[:, 0:1]
    slope = jnp.zeros((N_Q_HEADS, 1), F32)
    sink = jnp.zeros((N_Q_HEADS, 1), F32)
    for hd in range(N_Q_HEADS):
        slope = jnp.where(sub1 == hd, 2.0 ** -(hd + 1), slope)
        sink = jnp.where(sub1 == hd, sinks_ref[hd], sink)
    dist = (WINDOW - lax.broadcasted_iota(jnp.int32, (1, WINDOW), 1)).astype(F32)
    lo_row = lax.broadcasted_iota(jnp.int32, (1, LANES), 1) < HEAD_DIM
    scale = HEAD_DIM ** -0.5

    for i in range(SEQ_CHUNK):
        b = c * SEQ_CHUNK + i
        prow = proj_s[pl.ds(b, 1), :]
        q8 = jnp.zeros((N_Q_HEADS, LANES), F32)
        for g in range(HEAD_GROUPS):
            qg = jnp.broadcast_to(prow[:, g * LANES:(g + 1) * LANES], (N_Q_HEADS, LANES))
            q8 = jnp.where(((sub & (HEAD_GROUPS - 1)) == g) & own_half, qg, q8)
        k_new = prow[:, Q_COLS:Q_COLS + KV_COLS]
        v_new = prow[:, Q_COLS + KV_COLS:Q_COLS + 2 * KV_COLS]
        kb = ck_ref[i]
        vb = cv_ref[i]
        sc = lax.dot_general(q8.astype(BF16), kb.astype(BF16), (((1,), (1,)), ((), ())),
                             preferred_element_type=F32) * scale - slope * dist
        sc_new = jnp.sum(q8 * k_new, axis=-1, keepdims=True) * scale
        m = jnp.maximum(jnp.maximum(jnp.max(sc, axis=-1, keepdims=True), sc_new), sink)
        p = jnp.exp(sc - m)
        p_new = jnp.exp(sc_new - m)
        den = jnp.sum(p, axis=-1, keepdims=True) + p_new + jnp.exp(sink - m)
        o = (jnp.dot(p.astype(BF16), vb.astype(BF16), preferred_element_type=F32)
             + p_new * v_new) / den
        for g in range(HEAD_GROUPS):
            attn_c[i:i + 1, g * LANES:(g + 1) * LANES] = jnp.where(
                lo_row, o[g:g + 1, :], o[g + HEAD_GROUPS:g + HEAD_GROUPS + 1, :])
        wk_ref[i, 0:WINDOW - 1, :] = ck_ref[i, 1:WINDOW, :]
        wk_ref[i, WINDOW - 1:WINDOW, :] = k_new
        wv_ref[i, 0:WINDOW - 1, :] = cv_ref[i, 1:WINDOW, :]
        wv_ref[i, WINDOW - 1:WINDOW, :] = v_new
    attn_s[pl.ds(pl.multiple_of(c * SEQ_CHUNK, SEQ_CHUNK), SEQ_CHUNK), :] = attn_c[...]

    @pl.when(c == nc - 1)
    def _():
        x = x_ref[...]
        ux = proj_s[:, Q_COLS + 2 * KV_COLS:Q_COLS + 2 * KV_COLS + LRU_WIDTH]
        ug = proj_s[:, Q_COLS + 2 * KV_COLS + LRU_WIDTH:]
        xc = conv_b_ref[...]
        for tap in range(CONV_WIDTH - 1):
            xc = xc + cprev_ref[tap] * conv_w_ref[tap:tap + 1, :]
        xc = xc + ux * conv_w_ref[CONV_WIDTH - 1:CONV_WIDTH, :]
        a, bv = _rglru_gates(xc, wa_ref, ba_ref, wi_ref, bi_ref, lam_ref)
        h = a * h0_ref[...] + bv
        rnn = h * _gelu_tanh(ug)
        mix_in = jnp.concatenate([attn_s[...], rnn], axis=1)
        h1 = _post_mix(x, mix_in, w_out_ref, b_out_ref, ln_g_ref, ln_b_ref)
        h1_ref[...] = h1
        te, tg, cnt = _route(h1, rw_ref, rb_ref)
        cnt_ref[0] = cnt
        tg_ref[0] = jnp.concatenate([tg, jnp.zeros((TOP_K, TILE - nseq), F32)], axis=1)
        te_tile = jnp.concatenate([te, jnp.full((TOP_K, TILE - nseq), -1, jnp.int32)], axis=1)
        h1_tile = jnp.concatenate([h1, jnp.zeros((TILE - nseq, D_MODEL), F32)], axis=0)
        picked, dest = _compact_tile(h1_tile, te_tile, cnt)
        xt_ref[0] = picked
        dest_ref[0] = dest
        lh_ref[...] = h
        for tap in range(1, CONV_WIDTH - 1):
            cnew_ref[tap - 1] = cprev_ref[tap]
        cnew_ref[CONV_WIDTH - 2] = ux


def _mixer_sample(x, ck, cv, h0, cprev, sinks, wts):
    nseq = x.shape[0]
    assert nseq % SEQ_CHUNK == 0 and nseq <= TILE and nseq % LANES == 0
    nc = nseq // SEQ_CHUNK
    chunk_idx = lambda c, *_: (c, 0, 0)
    in_specs = [
        _const_spec((nseq, D_MODEL)),
        pl.BlockSpec((SEQ_CHUNK, WINDOW, KV_COLS), chunk_idx),
        pl.BlockSpec((SEQ_CHUNK, WINDOW, KV_COLS), chunk_idx),
        _const_spec((nseq, LRU_WIDTH)),
        _const_spec((CONV_WIDTH - 1, nseq, LRU_WIDTH)),
    ] + [_const_spec(w.shape) for w in wts]
    out_shape = (
        jax.ShapeDtypeStruct((nseq, D_MODEL), F32),
        jax.ShapeDtypeStruct((1, TILE_ROWS, D_MODEL), F32),
        jax.ShapeDtypeStruct((1, TOP_K, TILE), jnp.int32),
        jax.ShapeDtypeStruct((1, TOP_K, TILE), F32),
        jax.ShapeDtypeStruct((1, N_EXPERTS, 1), jnp.int32),
        jax.ShapeDtypeStruct((nseq, WINDOW, KV_COLS), F32),
        jax.ShapeDtypeStruct((nseq, WINDOW, KV_COLS), F32),
        jax.ShapeDtypeStruct((nseq, LRU_WIDTH), F32),
        jax.ShapeDtypeStruct((CONV_WIDTH - 1, nseq, LRU_WIDTH), F32),
    )
    out_specs = (
        _const_spec((nseq, D_MODEL)),
        _const_spec((1, TILE_ROWS, D_MODEL)),
        _const_spec((1, TOP_K, TILE)),
        _const_spec((1, TOP_K, TILE)),
        _const_spec((1, N_EXPERTS, 1)),
        pl.BlockSpec((SEQ_CHUNK, WINDOW, KV_COLS), chunk_idx),
        pl.BlockSpec((SEQ_CHUNK, WINDOW, KV_COLS), chunk_idx),
        _const_spec((nseq, LRU_WIDTH)),
        _const_spec((CONV_WIDTH - 1, nseq, LRU_WIDTH)),
    )
    scratch = [pltpu.VMEM((nseq, D_IN), F32), pltpu.VMEM((nseq, Q_COLS), F32),
               pltpu.VMEM((SEQ_CHUNK, Q_COLS), F32)]
    return pl.pallas_call(
        _mixer_sample_kernel,
        grid_spec=pltpu.PrefetchScalarGridSpec(
            num_scalar_prefetch=1, grid=(nc,), in_specs=in_specs, out_specs=out_specs,
            scratch_shapes=scratch),
        out_shape=out_shape,
        compiler_params=pltpu.CompilerParams(
            dimension_semantics=("arbitrary",), vmem_limit_bytes=VMEM_LIMIT),
        name="mixer_sample",
    )(sinks, x, ck, cv, h0, cprev, *wts)


def _grouped(rows):
    assert rows % SEG_ALIGN == 0
    return (rows // SEG_ALIGN, SEG_ALIGN, D_MODEL)


BLOCK_GROUPS = ROW_BLOCK // SEG_ALIGN
TILE_GROUPS = TILE_ROWS // SEG_ALIGN


def _experts_kernel(be_ref, nxt_ref, nb_ref, tlo_ref, thi_ref, valid_ref,
                    seg_len_ref, seg_src_ref, seg_off_ref,
                    xtp_ref, xts_ref, wgu_ref, bgu_ref, wdn_ref, bdn_ref, yb_ref,
                    xbuf, wgu_f, wdn_f, wgu_s, wdn_s, wsem, xsem, *, n_tiles, n_prompt_tiles):
    b = pl.program_id(0)
    slot = b % 2

    def weight_copies(e):
        return (pltpu.make_async_copy(wgu_ref.at[e], wgu_f, wsem.at[0]),
                pltpu.make_async_copy(wdn_ref.at[e], wdn_f, wsem.at[1]))

    def start_gather(blk, s):
        lo_b = blk * BLOCK_GROUPS
        first = be_ref[blk] * n_tiles

        t_lo = tlo_ref[blk]
        t_hi = thi_ref[blk]

        def piece(t):
            off = seg_off_ref[first + t]
            lo = jnp.maximum(off, lo_b)
            n = jnp.minimum(off + seg_len_ref[first + t], lo_b + BLOCK_GROUPS) - lo

            src = pl.ds(seg_src_ref[first + t] + lo, n)
            dst = xbuf.at[s, pl.ds(lo - lo_b, n)]
            wanted = (n > 0) & (t <= t_hi)

            @pl.when(wanted & (t < n_prompt_tiles))
            def _():
                pltpu.make_async_copy(xtp_ref.at[src], dst, xsem.at[s]).start()

            @pl.when(wanted & (t >= n_prompt_tiles))
            def _():
                pltpu.make_async_copy(xts_ref.at[src], dst, xsem.at[s]).start()

        def two_pieces(i, carry):
            piece(t_lo + 2 * i)
            piece(t_lo + 2 * i + 1)
            return carry

        lax.fori_loop(0, (t_hi - t_lo + 2) // 2, two_pieces, 0)

    @pl.when(b == 0)
    def _():
        xbuf[...] = jnp.zeros(xbuf.shape, F32)
        for cp in weight_copies(be_ref[0]):
            cp.start()
        start_gather(b, slot)

    @pl.when(b + 1 < nb_ref[0])
    def _():
        start_gather(b + 1, 1 - slot)

    @pl.when(b < nb_ref[0])
    def _():
        n = valid_ref[b]
        pltpu.make_async_copy(xtp_ref.at[pl.ds(0, n)], xbuf.at[slot, pl.ds(0, n)], xsem.at[slot]).wait()

        @pl.when((b == 0) | (be_ref[b] != be_ref[jnp.maximum(b - 1, 0)]))
        def _():
            for cp in weight_copies(be_ref[b]):
                cp.wait()
            wgu_s[...] = wgu_f[...].astype(BF16)
            wdn_s[...] = wdn_f[...].astype(BF16)

            @pl.when(nxt_ref[b] >= 0)
            def _():
                for cp in weight_copies(nxt_ref[b]):
                    cp.start()

        def expert_rows(rows):
            x = xbuf[slot, pl.ds(0, rows // SEG_ALIGN)].reshape(rows, D_MODEL).astype(BF16)
            hgu = jnp.dot(x, wgu_s[...], preferred_element_type=F32) + bgu_ref[0]
            glu = jnp.minimum(hgu[:, :D_FF], SWIGLU_LIMIT)
            lin = jnp.clip(hgu[:, D_FF:], -SWIGLU_LIMIT, SWIGLU_LIMIT)
            act = glu * jax.nn.sigmoid(SWIGLU_ALPHA * glu) * (lin + 1.0)
            yb_ref[0:rows, :] = jnp.dot(act.astype(BF16), wdn_s[...], preferred_element_type=F32) + bdn_ref[0]

        half = ROW_BLOCK // 2

        @pl.when(n > half // SEG_ALIGN)
        def _():
            expert_rows(ROW_BLOCK)

        @pl.when(n <= half // SEG_ALIGN)
        def _():
            expert_rows(half)
            yb_ref[half:, :] = jnp.zeros((ROW_BLOCK - half, D_MODEL), F32)

    @pl.when(b >= nb_ref[0])
    def _():
        yb_ref[...] = jnp.zeros(yb_ref.shape, F32)


def _experts(sched, segs, xt_prompt, xt_sample, w_gu, b_gu, w_dn, b_dn, n_rows):
    n_blocks = n_rows // ROW_BLOCK
    n_prompt_tiles = xt_prompt.shape[0] // TILE_GROUPS
    n_tiles = n_prompt_tiles + xt_sample.shape[0] // TILE_GROUPS
    n_prefetch = len(sched) + len(segs)
    exp_idx = lambda b, be, *_: (be[b], 0, 0)
    in_specs = [
        pl.BlockSpec(memory_space=pl.ANY),
        pl.BlockSpec(memory_space=pl.ANY),
        pl.BlockSpec(memory_space=pl.ANY),
        pl.BlockSpec((1, 1, 2 * D_FF), exp_idx),
        pl.BlockSpec(memory_space=pl.ANY),
        pl.BlockSpec((1, 1, D_MODEL), exp_idx),
    ]
    scratch = [pltpu.VMEM((2,) + _grouped(ROW_BLOCK), F32),
               pltpu.VMEM((D_MODEL, 2 * D_FF), F32), pltpu.VMEM((D_FF, D_MODEL), F32),
               pltpu.VMEM((D_MODEL, 2 * D_FF), BF16), pltpu.VMEM((D_FF, D_MODEL), BF16),
               pltpu.SemaphoreType.DMA((2,)), pltpu.SemaphoreType.DMA((2,))]
    return pl.pallas_call(
        functools.partial(_experts_kernel, n_tiles=n_tiles, n_prompt_tiles=n_prompt_tiles),
        grid_spec=pltpu.PrefetchScalarGridSpec(
            num_scalar_prefetch=n_prefetch, grid=(n_blocks,), in_specs=in_specs,
            out_specs=pl.BlockSpec((ROW_BLOCK, D_MODEL), lambda b, *_: (b, 0)),
            scratch_shapes=scratch),
        out_shape=jax.ShapeDtypeStruct((n_rows, D_MODEL), F32),
        compiler_params=pltpu.CompilerParams(
            dimension_semantics=("arbitrary",), vmem_limit_bytes=VMEM_LIMIT),
        name="moe_experts",
    )(*sched, *segs, xt_prompt, xt_sample, w_gu, b_gu[:, None, :], w_dn, b_dn[:, None, :])


def _tile_tokens(i, n_prompt_tiles, hp_ref, hs_ref):
    hs = hs_ref[...]
    hs_tile = jnp.concatenate([hs, jnp.zeros((TILE - hs.shape[0], D_MODEL), F32)], axis=0)
    return jnp.where(i < n_prompt_tiles, hp_ref[...], hs_tile)


COMBINE_SLOTS = 4


def _combine_kernel(seg_len_ref, seg_start_ref, seg_off_ref, tile_groups_ref,
                    hp_ref, hs_ref, dest_ref, gate_ref, g_ref, b_ref, yb_ref,
                    yp_ref, ys_ref, ybuf, sem, *, n_prompt_tiles):
    i = pl.program_id(0)
    n = pl.num_programs(0)
    slot = i % COMBINE_SLOTS

    def start_gather(tile):
        s = tile % COMBINE_SLOTS

        def segment(e, carry):
            idx = tile * N_EXPERTS + e
            groups = seg_len_ref[idx]

            @pl.when(groups > 0)
            def _():
                pltpu.make_async_copy(yb_ref.at[pl.ds(seg_off_ref[idx], groups)],
                                      ybuf.at[s, pl.ds(seg_start_ref[idx], groups)], sem.at[s]).start()
            return carry
        lax.fori_loop(0, N_EXPERTS, segment, 0)

    @pl.when(i == 0)
    def _():
        ybuf[...] = jnp.zeros(ybuf.shape, F32)
        for ahead in range(COMBINE_SLOTS - 1):
            @pl.when(ahead < n)
            def _():
                start_gather(i + ahead)

    @pl.when(i + COMBINE_SLOTS - 1 < n)
    def _():
        start_gather(i + COMBINE_SLOTS - 1)

    total = tile_groups_ref[i]

    @pl.when(total > 0)
    def _():
        pltpu.make_async_copy(yb_ref.at[pl.ds(0, total)], ybuf.at[slot, pl.ds(0, total)], sem.at[slot]).wait()

    dest = dest_ref[0]
    gate = gate_ref[0]
    rid = lax.broadcasted_iota(jnp.int32, (TILE_ROWS, TILE), 0)
    weights = jnp.zeros((TILE_ROWS, TILE), F32)
    for k in range(TOP_K):
        weights = jnp.where(rid == dest[k:k + 1, :], gate[k:k + 1, :], weights)
    rows = ybuf[slot].reshape(TILE_ROWS, D_MODEL).astype(BF16)
    ff = lax.dot_general(weights.astype(BF16), rows, (((0,), (0,)), ((), ())),
                         preferred_element_type=F32)
    h1 = _tile_tokens(i, n_prompt_tiles, hp_ref, hs_ref)
    y = _layer_norm(DN_ALPHA * h1 + ff, g_ref[...], b_ref[...])

    @pl.when(i < n_prompt_tiles)
    def _():
        yp_ref[...] = y

    @pl.when(i >= n_prompt_tiles)
    def _():
        ys_ref[...] = y[:ys_ref.shape[0]]


def _combine(segs, tile_groups, n_seq, h1p, h1s, dest, gate, ln_g, ln_b, yb):
    n_tiles = dest.shape[0]
    n_prompt_tiles = h1p.shape[0] // TILE
    blocks_per_seq = n_prompt_tiles // n_seq

    def prompt_idx(i, *_):
        t = jnp.minimum(i, n_prompt_tiles - 1)
        return ((t % n_seq) * blocks_per_seq + t // n_seq, 0)
    in_specs = [
        pl.BlockSpec((TILE, D_MODEL), prompt_idx),
        _const_spec(h1s.shape),
        pl.BlockSpec((1, TOP_K, TILE), lambda i, *_: (i, 0, 0)),
        pl.BlockSpec((1, TOP_K, TILE), lambda i, *_: (i, 0, 0)),
        _const_spec(ln_g.shape),
        _const_spec(ln_b.shape),
        pl.BlockSpec(memory_space=pl.ANY),
    ]
    out_shape = (jax.ShapeDtypeStruct(h1p.shape, F32), jax.ShapeDtypeStruct(h1s.shape, F32))
    out_specs = (pl.BlockSpec((TILE, D_MODEL), prompt_idx), _const_spec(h1s.shape))
    return pl.pallas_call(
        functools.partial(_combine_kernel, n_prompt_tiles=n_prompt_tiles),
        grid_spec=pltpu.PrefetchScalarGridSpec(
            num_scalar_prefetch=len(segs) + 1, grid=(n_tiles,), in_specs=in_specs, out_specs=out_specs,
            scratch_shapes=[pltpu.VMEM((COMBINE_SLOTS,) + _grouped(TILE_ROWS), F32),
                            pltpu.SemaphoreType.DMA((COMBINE_SLOTS,))]),
        out_shape=out_shape,
        compiler_params=pltpu.CompilerParams(
            dimension_semantics=("arbitrary",), vmem_limit_bytes=VMEM_LIMIT),
        name="moe_combine",
    )(*segs, tile_groups, h1p, h1s, dest, gate, ln_g, ln_b, yb.reshape(_grouped(yb.shape[0])))


def _moe_layout(cnt, n_prompt_tiles):
    n_tiles = cnt.shape[0]
    i32 = lambda a: a.astype(jnp.int32)
    seg_len = (cnt + SEG_ALIGN - 1) // SEG_ALIGN
    seg_start = jnp.cumsum(seg_len, axis=1) - seg_len
    exp_len = jnp.sum(seg_len, axis=0)
    exp_blocks = (exp_len + BLOCK_GROUPS - 1) // BLOCK_GROUPS
    blocks_end = jnp.cumsum(exp_blocks)
    first_block = blocks_end - exp_blocks
    seg_off = first_block[None, :] * BLOCK_GROUPS + jnp.cumsum(seg_len, axis=0) - seg_len
    max_rows = n_tiles * (TOP_K * TILE + N_EXPERTS * (SEG_ALIGN - 1)) + N_EXPERTS * (ROW_BLOCK - SEG_ALIGN)
    n_blocks = -(-max_rows // ROW_BLOCK)
    n_used = blocks_end[-1]
    blk = jnp.minimum(jnp.arange(n_blocks, dtype=jnp.int32), n_used - 1)
    eid = jnp.arange(N_EXPERTS, dtype=jnp.int32)
    block_expert = jnp.minimum(jnp.sum(blocks_end[None, :] <= blk[:, None], axis=1), N_EXPERTS - 1)
    own = block_expert[:, None] == eid[None, :]
    pick = lambda per_expert: jnp.sum(jnp.where(own, per_expert[None, :], 0), axis=1)
    later_used = (eid[None, :] > eid[:, None]) & (exp_blocks[None, :] > 0)
    next_of = jnp.min(jnp.where(later_used, eid[None, :], N_EXPERTS), axis=1)
    next_expert = pick(jnp.where(next_of < N_EXPERTS, next_of, -1))
    lo = blk * BLOCK_GROUPS
    off_b = jnp.sum(jnp.where(own[:, None, :], seg_off[None, :, :], 0), axis=2)
    len_b = jnp.sum(jnp.where(own[:, None, :], seg_len[None, :, :], 0), axis=2)
    tile_lo = jnp.sum(off_b + len_b <= lo[:, None], axis=1)
    tile_hi = jnp.sum(off_b < lo[:, None] + BLOCK_GROUPS, axis=1) - 1
    valid = jnp.clip(pick(first_block * BLOCK_GROUPS + exp_len) - lo, 0, BLOCK_GROUPS)
    flat = lambda a: i32(a.reshape(-1))
    sched = (i32(block_expert), i32(next_expert), i32(n_used.reshape(1)), i32(tile_lo), i32(tile_hi),
             i32(valid))
    segs = (flat(seg_len), flat(seg_start), flat(seg_off))
    tile = jnp.arange(n_tiles, dtype=seg_off.dtype)
    tile_in_array = jnp.where(tile < n_prompt_tiles, tile, tile - n_prompt_tiles)
    seg_src = tile_in_array[:, None] * TILE_GROUPS + seg_start - seg_off
    pad1 = lambda a: jnp.concatenate([flat(a.T), jnp.zeros((1,), jnp.int32)])
    segs_by_expert = (pad1(seg_len), pad1(seg_src), pad1(seg_off))
    return sched, segs, segs_by_expert, i32(jnp.sum(seg_len, axis=1)), n_blocks * ROW_BLOCK


def _prep_weights(w_in, b_in, conv_w, conv_b, lru_w_a, lru_b_a, lru_w_i, lru_b_i, lru_lambda,
                  w_out, b_out, ln1_g, ln1_b, router_w, router_b):
    def regroup_heads(a):
        rest = a.shape[1:]
        q = a[:Q_COLS].reshape(N_KV_HEADS, HEAD_GROUPS, HEAD_DIM, *rest)
        q = jnp.swapaxes(q, 0, 1).reshape(Q_COLS, *rest)
        return jnp.concatenate([q, a[Q_COLS:]], axis=0)

    def diag_tiles(w):
        per = MXU_DIM // (LRU_WIDTH // LRU_BLOCKS)
        w4 = w.reshape(LRU_BLOCKS // per, per, LRU_WIDTH // LRU_BLOCKS, LRU_WIDTH // LRU_BLOCKS)
        t = jnp.einsum("taij,ab->taibj", w4, jnp.eye(per, dtype=w.dtype))
        return t.reshape(LRU_BLOCKS // per, MXU_DIM, MXU_DIM).astype(BF16)

    return (
        regroup_heads(w_in[0].T).T.astype(BF16), regroup_heads(b_in[0])[None],
        conv_w[0], conv_b[0][None],
        diag_tiles(lru_w_a[0]), lru_b_a[0].reshape(1, LRU_WIDTH),
        diag_tiles(lru_w_i[0]), lru_b_i[0].reshape(1, LRU_WIDTH),
        lru_lambda[0][None],
        regroup_heads(w_out[0]).astype(BF16), b_out[0][None],
        ln1_g[0][None], ln1_b[0][None],
        router_w[0].T.astype(BF16), router_b[0][:, None],
    )


def kernel(x_prompt, x_sample, cache_win_k, cache_win_v, state_lru_h, state_conv, w_in, b_in, attn_sinks, conv_w, conv_b, lru_w_a, lru_b_a, lru_w_i, lru_b_i, lru_lambda, w_out, b_out, ln1_g, ln1_b, router_w, router_b, w_gate_up, b_gate_up, w_down, b_down, ln2_g, ln2_b):
    assert w_in.shape[0] == 1, "single-layer step"
    B, S, _ = x_prompt.shape
    nseq = x_sample.shape[0]
    assert x_sample.shape[1] == 1 and S % TILE == 0
    wts = _prep_weights(w_in, b_in, conv_w, conv_b, lru_w_a, lru_b_a, lru_w_i, lru_b_i, lru_lambda,
                        w_out, b_out, ln1_g, ln1_b, router_w, router_b)
    sinks = attn_sinks[0]

    h1p, xt_p, dest_p, tg_p, cnt_p, pk, pv, ph, pc = _mixer_prompt(x_prompt, sinks, wts)
    h1s, xt_s, dest_s, tg_s, cnt_s, sk, sv, sh, sc = _mixer_sample(
        x_sample.reshape(nseq, D_MODEL),
        cache_win_k[0].reshape(nseq, WINDOW, KV_COLS), cache_win_v[0].reshape(nseq, WINDOW, KV_COLS),
        state_lru_h[0], jnp.transpose(state_conv[0], (1, 0, 2)), sinks, wts)

    per_tile = lambda a: a.reshape(-1, *a.shape[2:])
    dest = jnp.concatenate([per_tile(dest_p), dest_s], axis=0)
    tg = jnp.concatenate([per_tile(tg_p), tg_s], axis=0)
    cnt = jnp.concatenate([per_tile(cnt_p), cnt_s], axis=0)[:, :, 0]
    sched, segs, segs_by_expert, tile_groups, n_rows = _moe_layout(cnt, xt_p.shape[0])

    end_to_end = lambda a: a.reshape(-1, SEG_ALIGN, D_MODEL)
    yb = _experts(sched, segs_by_expert, end_to_end(xt_p), end_to_end(xt_s),
                  w_gate_up[0], b_gate_up[0], w_down[0], b_down[0], n_rows)
    yp, ys = _combine(segs, tile_groups, B, h1p.reshape(B * S, D_MODEL), h1s, dest, tg,
                      ln2_g[0][None], ln2_b[0][None], yb)

    kv_shape = (N_KV_HEADS, HEAD_DIM)
    return (
        yp.reshape(B, S, D_MODEL), ys.reshape(nseq, 1, D_MODEL),
        pk.reshape(1, B, WINDOW, *kv_shape), pv.reshape(1, B, WINDOW, *kv_shape),
        ph.reshape(1, B, LRU_WIDTH), pc[None],
        sk.reshape(1, nseq, WINDOW, *kv_shape), sv.reshape(1, nseq, WINDOW, *kv_shape),
        sh[None], jnp.transpose(sc, (1, 0, 2))[None],
    )
```

```python
import functools

import jax
import jax.numpy as jnp
import numpy as np
from jax import lax
from jax.experimental import pallas as pl
from jax.experimental.pallas import tpu as pltpu

F32 = jnp.float32
BF16 = jnp.bfloat16

D_MODEL = 1024
N_Q_HEADS = 8
N_KV_HEADS = 2
HEAD_DIM = 64
WINDOW = 128
Q_COLS = N_Q_HEADS * HEAD_DIM
KV_COLS = N_KV_HEADS * HEAD_DIM
LRU_WIDTH = 512
LRU_BLOCKS = 8
CONV_WIDTH = 4
RG_C = 8.0
N_EXPERTS = 32
TOP_K = 4
D_FF = 1024
SWIGLU_LIMIT = 7.0
SWIGLU_ALPHA = 1.702
LN_EPS = 1e-5
DN_ALPHA = 2.0 ** 0.25
NEG_INF = -1e30
LOG2E = 1.4426950408889634
D_IN = Q_COLS + 2 * KV_COLS + 2 * LRU_WIDTH

LANES = 128
SUBLANES = 8
MXU_DIM = 256

TILE = 256
SEG_ALIGN = SUBLANES
ROW_BLOCK = 512
HEAD_GROUPS = N_Q_HEADS // N_KV_HEADS
VMEM_LIMIT = 56 * 1024 * 1024

TILE_ROWS = 1280
assert TILE_ROWS >= TOP_K * TILE + N_EXPERTS * (SEG_ALIGN - 1) and TILE_ROWS % LANES == 0


def _layer_norm(z, g, b):
    mu = jnp.mean(z, axis=-1, keepdims=True)
    zc = z - mu
    var = jnp.mean(zc * zc, axis=-1, keepdims=True)
    return zc * lax.rsqrt(var + LN_EPS) * g + b


def _sigmoid(x):
    return 0.5 + 0.5 * jnp.tanh(0.5 * x)


def _softplus(x):
    return jnp.maximum(x, 0.0) + jnp.log1p(jnp.exp(-jnp.abs(x)))


def _gelu_tanh(x):
    c = np.float32(np.sqrt(2.0 / np.pi))
    return 0.5 * x * (1.0 + jnp.tanh(c * (x + 0.044715 * (x * x * x))))


def _rglru_gates(xc, wa_ref, ba_ref, wi_ref, bi_ref, lam_ref):
    xcb = xc.astype(BF16)
    half = LRU_WIDTH // 2
    pre_a = jnp.concatenate(
        [jnp.dot(xcb[:, :half], wa_ref[0], preferred_element_type=F32),
         jnp.dot(xcb[:, half:], wa_ref[1], preferred_element_type=F32)], axis=1)
    pre_i = jnp.concatenate(
        [jnp.dot(xcb[:, :half], wi_ref[0], preferred_element_type=F32),
         jnp.dot(xcb[:, half:], wi_ref[1], preferred_element_type=F32)], axis=1)
    r = _sigmoid(pre_a + ba_ref[...])
    gi = _sigmoid(pre_i + bi_ref[...])
    log_a = (-RG_C) * r * _softplus(-lam_ref[...])
    a = jnp.exp(log_a)
    t = jnp.tanh(log_a)
    mult = jnp.sqrt(jnp.maximum(-2.0 * t / (1.0 - t), 0.0))
    return a, mult * (gi * xc)


def _route(h1, rw_ref, rb_ref):
    n = h1.shape[0]
    logits = lax.dot_general(rw_ref[...], h1.astype(BF16), (((1,), (1,)), ((), ())),
                             preferred_element_type=F32) + rb_ref[...]
    eidx = lax.broadcasted_iota(jnp.int32, (N_EXPERTS, n), 0)
    vals = logits
    top_v, top_e = [], []
    for _ in range(TOP_K):
        m = jnp.max(vals, axis=0, keepdims=True)
        idx = jnp.min(jnp.where(vals == m, eidx, N_EXPERTS), axis=0, keepdims=True)
        top_v.append(m)
        top_e.append(idx)
        vals = jnp.where(eidx == idx, -jnp.inf, vals)
    ex = [jnp.exp(v - top_v[0]) for v in top_v]
    den = ex[0] + ex[1] + ex[2] + ex[3]
    gates = jnp.concatenate([e / den for e in ex], axis=0)
    te = jnp.concatenate(top_e, axis=0)
    mem = jnp.zeros((N_EXPERTS, n), jnp.int32)
    for idx in top_e:
        mem = mem + jnp.where(eidx == idx, 1, 0)
    return te, gates, jnp.sum(mem, axis=1, keepdims=True)


def _compact_tile(h1, te, cnt):
    eidx = lax.broadcasted_iota(jnp.int32, (N_EXPERTS, TILE), 0)
    mem = jnp.zeros((N_EXPERTS, TILE), F32)
    for k in range(TOP_K):
        mem = mem + jnp.where(eidx == te[k:k + 1, :], 1.0, 0.0)
    tr = lax.broadcasted_iota(jnp.int32, (TILE, TILE), 0)
    tc = lax.broadcasted_iota(jnp.int32, (TILE, TILE), 1)
    before = jnp.where(tr < tc, 1.0, 0.0).astype(BF16)
    pos = jnp.dot(mem.astype(BF16), before, preferred_element_type=F32)
    seg_rows = ((cnt + (SEG_ALIGN - 1)) // SEG_ALIGN * SEG_ALIGN).astype(F32)
    er = lax.broadcasted_iota(jnp.int32, (N_EXPERTS, N_EXPERTS), 0)
    ec = lax.broadcasted_iota(jnp.int32, (N_EXPERTS, N_EXPERTS), 1)
    earlier = jnp.where(ec < er, 1.0, 0.0).astype(BF16)
    seg_start = jnp.dot(earlier, jnp.broadcast_to(seg_rows, (N_EXPERTS, LANES)).astype(BF16),
                        preferred_element_type=F32)[:, 0:1]
    base = (pos + seg_start).astype(jnp.int32)
    dests = []
    for k in range(TOP_K):
        d = jnp.sum(jnp.where(eidx == te[k:k + 1, :], base, 0), axis=0, keepdims=True)
        dests.append(jnp.where(te[k:k + 1, :] >= 0, d, -1))
    rid = lax.broadcasted_iota(jnp.int32, (TILE_ROWS, TILE), 0)
    onehot = jnp.zeros((TILE_ROWS, TILE), F32)
    for k in range(TOP_K):
        onehot = jnp.where(rid == dests[k], 1.0, onehot)
    picked = jnp.dot(onehot.astype(BF16), h1.astype(BF16), preferred_element_type=F32)
    return picked, jnp.concatenate(dests, axis=0)


def _post_mix(x, mix_in, w_out_ref, b_out_ref, g_ref, b_ref):
    mix = jnp.dot(mix_in.astype(BF16), w_out_ref[...], preferred_element_type=F32) + b_out_ref[...]
    return _layer_norm(DN_ALPHA * x + mix, g_ref[...], b_ref[...])


SEQ_PAIR = 2
PHASE_LAG = 0


def _interleave(chains, lag):
    results = [None] * len(chains)
    live = list(range(len(chains)))
    rnd = 0
    while live:
        for k in list(live):
            if rnd < lag * k:
                continue
            try:
                next(chains[k])
            except StopIteration as stop:
                results[k] = stop.value
                live.remove(k)
        rnd += 1
    return results


def _bias_tables(bias_s):
    row2 = lax.broadcasted_iota(jnp.int32, (2 * WINDOW, 2 * WINDOW), 0)
    col2 = lax.broadcasted_iota(jnp.int32, (2 * WINDOW, 2 * WINDOW), 1)
    dist = (row2 & (WINDOW - 1)) - col2 + WINDOW
    valid = (dist >= 0) & (dist <= WINDOW)
    distf = dist.astype(F32)
    for g in range(HEAD_GROUPS):
        slope = jnp.where(row2 >= WINDOW, 2.0 ** -(g + HEAD_GROUPS + 1), 2.0 ** -(g + 1))
        bias = jnp.where(valid, (-LOG2E) * (slope * distf), NEG_INF)
        bias_s[g, 0] = bias
        bias_s[g, 1] = jnp.where(col2 >= WINDOW, bias, NEG_INF)


def _mixer_prompt_kernel(sinks_ref, x_ref, *rest):
    wts = rest[:15]
    h1_ref, xt_ref, dest_ref, tg_ref, cnt_ref, wk_ref, wv_ref, lh_ref, cv_ref = rest[15:24]
    kext, vext, uext, hcar, attn_buf, bias_s = rest[24:]
    j = pl.program_id(1)
    nj = pl.num_programs(1)
    R = TILE
    one = lambda ref, bb: ref.at[pl.ds(bb, 1)]

    @pl.when(j == 0)
    def _():
        for bb in range(SEQ_PAIR):
            kext[bb, 0:WINDOW, :] = jnp.zeros((WINDOW, KV_COLS), F32)
            vext[bb, 0:WINDOW, :] = jnp.zeros((WINDOW, KV_COLS), F32)
            uext[bb, 0:SUBLANES, :] = jnp.zeros((SUBLANES, LRU_WIDTH), F32)
            hcar[bb] = jnp.zeros((1, LRU_WIDTH), F32)

    @pl.when((pl.program_id(0) == 0) & (j == 0))
    def _():
        _bias_tables(bias_s)

    per_tile = lambda ref, bb: ref.at[0, pl.ds(bb, 1)]
    h_last = _interleave([
        _mixer_tile(j, sinks_ref, one(x_ref, bb), wts, one(h1_ref, bb), one(xt_ref, bb),
                    per_tile(dest_ref, bb), per_tile(tg_ref, bb), per_tile(cnt_ref, bb),
                    kext.at[bb], vext.at[bb], uext.at[bb], hcar.at[bb], attn_buf.at[bb], bias_s)
        for bb in range(SEQ_PAIR)], PHASE_LAG)

    @pl.when(j == nj - 1)
    def _():
        for bb in range(SEQ_PAIR):
            wk_ref[bb] = kext[bb, R:R + WINDOW, :]
            wv_ref[bb] = vext[bb, R:R + WINDOW, :]
            lh_ref[bb] = h_last[bb]
            cv_ref[bb] = uext[bb, SUBLANES + R - (CONV_WIDTH - 1):SUBLANES + R, :]

    for bb in range(SEQ_PAIR):
        kext[bb, 0:WINDOW, :] = kext[bb, R:R + WINDOW, :]
        vext[bb, 0:WINDOW, :] = vext[bb, R:R + WINDOW, :]
        uext[bb, 0:SUBLANES, :] = uext[bb, R:R + SUBLANES, :]


def _mixer_tile(j, sinks_ref, x_ref, wts, h1_ref, xt_ref, dest_ref, tg_ref, cnt_ref,
                kext, vext, uext, hcar, attn_buf, bias_s):
    (w_in_ref, b_in_ref, conv_w_ref, conv_b_ref, wa_ref, ba_ref, wi_ref, bi_ref, lam_ref,
     w_out_ref, b_out_ref, ln_g_ref, ln_b_ref, rw_ref, rb_ref) = wts
    R = TILE
    x = x_ref[0]
    proj = jnp.dot(x.astype(BF16), w_in_ref[...], preferred_element_type=F32) + b_in_ref[...]
    q = proj[:, :Q_COLS]
    kext[WINDOW:WINDOW + R, :] = proj[:, Q_COLS:Q_COLS + KV_COLS]
    vext[WINDOW:WINDOW + R, :] = proj[:, Q_COLS + KV_COLS:Q_COLS + 2 * KV_COLS]
    uext[SUBLANES:SUBLANES + R, :] = proj[:, Q_COLS + 2 * KV_COLS:Q_COLS + 2 * KV_COLS + LRU_WIDTH]
    ug = proj[:, Q_COLS + 2 * KV_COLS + LRU_WIDTH:]
    yield

    upper = lax.broadcasted_iota(jnp.int32, (2 * WINDOW, 1), 0) >= WINDOW
    lo_lane = lax.broadcasted_iota(jnp.int32, (WINDOW, LANES), 1) < HEAD_DIM
    qs = q * (HEAD_DIM ** -0.5 * LOG2E)
    for s in range(R // WINDOW):
        kk = kext[s * WINDOW:s * WINDOW + 2 * WINDOW, :].astype(BF16)
        vv = vext[s * WINDOW:s * WINDOW + 2 * WINDOW, :].astype(BF16)
        table = jnp.where(j == 0, 1, 0) if s == 0 else 0
        for g in range(HEAD_GROUPS):
            qg = qs[s * WINDOW:(s + 1) * WINDOW, g * LANES:(g + 1) * LANES]
            q2 = jnp.concatenate([jnp.where(lo_lane, qg, 0.0), jnp.where(lo_lane, 0.0, qg)], axis=0)
            sc = lax.dot_general(q2.astype(BF16), kk, (((1,), (1,)), ((), ())),
                                 preferred_element_type=F32) + bias_s[g, table]
            sink = jnp.where(upper, sinks_ref[g + HEAD_GROUPS], sinks_ref[g]) * LOG2E
            m = jnp.maximum(jnp.max(sc, axis=-1, keepdims=True), sink)
            p = jnp.exp2(sc - m)
            den = jnp.sum(p, axis=-1, keepdims=True) + jnp.exp2(sink - m)
            o = jnp.dot(p.astype(BF16), vv, preferred_element_type=F32) / den
            attn_buf[s * WINDOW:(s + 1) * WINDOW, g * LANES:(g + 1) * LANES] = jnp.where(
                lo_lane, o[:WINDOW], o[WINDOW:])
            yield

    xc = conv_b_ref[...]
    for tap in range(CONV_WIDTH):
        off = SUBLANES - (CONV_WIDTH - 1) + tap
        xc = xc + uext[off:off + R, :] * conv_w_ref[tap:tap + 1, :]
    a, bv = _rglru_gates(xc, wa_ref, ba_ref, wi_ref, bi_ref, lam_ref)
    yield
    groups = R // SUBLANES
    a3 = a.reshape(groups, SUBLANES, LRU_WIDTH)
    b3 = bv.reshape(groups, SUBLANES, LRU_WIDTH)
    sub = lax.broadcasted_iota(jnp.int32, (groups, SUBLANES, LRU_WIDTH), 1)
    d = 1
    while d < SUBLANES:
        keep = sub >= d
        a_prev = jnp.where(keep, pltpu.roll(a3, d, 1), 1.0)
        b_prev = jnp.where(keep, pltpu.roll(b3, d, 1), 0.0)
        b3 = a3 * b_prev + b3
        a3 = a3 * a_prev
        d *= 2
    h_prev = hcar[...]
    h_groups = []
    for c in range(groups):
        hc = a3[c] * h_prev + b3[c]
        h_groups.append(hc)
        h_prev = hc[SUBLANES - 1:SUBLANES, :]
    h = jnp.concatenate(h_groups, axis=0)
    hcar[...] = h_prev
    rnn = h * _gelu_tanh(ug)
    yield

    mix_in = jnp.concatenate([attn_buf[...], rnn], axis=1)
    h1 = _post_mix(x, mix_in, w_out_ref, b_out_ref, ln_g_ref, ln_b_ref)
    h1_ref[0] = h1
    yield
    te, tg, cnt = _route(h1, rw_ref, rb_ref)
    tg_ref[0] = tg
    cnt_ref[0] = cnt
    yield
    picked, dest = _compact_tile(h1, te, cnt)
    xt_ref[0] = picked
    dest_ref[0] = dest
    return h_prev


def _const_spec(shape):
    return pl.BlockSpec(shape, lambda *_: (0,) * len(shape))


def _mixer_prompt(x, sinks, wts):
    B, S, _ = x.shape
    assert B % SEQ_PAIR == 0 and S % TILE == 0
    nj = S // TILE
    P = SEQ_PAIR
    tile_idx = lambda b, j, *_: (j, b, 0, 0)
    batch_idx = lambda b, j, *_: (b, 0, 0)
    in_specs = [pl.BlockSpec((P, TILE, D_MODEL), lambda b, j, *_: (b, j, 0))]
    in_specs += [_const_spec(w.shape) for w in wts]
    out_shape = (
        jax.ShapeDtypeStruct((B, S, D_MODEL), F32),
        jax.ShapeDtypeStruct((B * nj, TILE_ROWS, D_MODEL), F32),
        jax.ShapeDtypeStruct((nj, B, TOP_K, TILE), jnp.int32),
        jax.ShapeDtypeStruct((nj, B, TOP_K, TILE), F32),
        jax.ShapeDtypeStruct((nj, B, N_EXPERTS, 1), jnp.int32),
        jax.ShapeDtypeStruct((B, WINDOW, KV_COLS), F32),
        jax.ShapeDtypeStruct((B, WINDOW, KV_COLS), F32),
        jax.ShapeDtypeStruct((B, 1, LRU_WIDTH), F32),
        jax.ShapeDtypeStruct((B, CONV_WIDTH - 1, LRU_WIDTH), F32),
    )
    out_specs = (
        pl.BlockSpec((P, TILE, D_MODEL), lambda b, j, *_: (b, j, 0)),
        pl.BlockSpec((P, TILE_ROWS, D_MODEL), lambda b, j, *_: (j * (B // P) + b, 0, 0)),
        pl.BlockSpec((1, P, TOP_K, TILE), tile_idx),
        pl.BlockSpec((1, P, TOP_K, TILE), tile_idx),
        pl.BlockSpec((1, P, N_EXPERTS, 1), tile_idx),
        pl.BlockSpec((P, WINDOW, KV_COLS), batch_idx),
        pl.BlockSpec((P, WINDOW, KV_COLS), batch_idx),
        pl.BlockSpec((P, 1, LRU_WIDTH), batch_idx),
        pl.BlockSpec((P, CONV_WIDTH - 1, LRU_WIDTH), batch_idx),
    )
    scratch = [
        pltpu.VMEM((P, TILE + WINDOW, KV_COLS), F32),
        pltpu.VMEM((P, TILE + WINDOW, KV_COLS), F32),
        pltpu.VMEM((P, TILE + SUBLANES, LRU_WIDTH), F32),
        pltpu.VMEM((P, 1, LRU_WIDTH), F32),
        pltpu.VMEM((P, TILE, Q_COLS), F32),
        pltpu.VMEM((HEAD_GROUPS, 2, 2 * WINDOW, 2 * WINDOW), F32),
    ]
    return pl.pallas_call(
        _mixer_prompt_kernel,
        grid_spec=pltpu.PrefetchScalarGridSpec(
            num_scalar_prefetch=1, grid=(B // P, nj), in_specs=in_specs, out_specs=out_specs,
            scratch_shapes=scratch),
        out_shape=out_shape,
        compiler_params=pltpu.CompilerParams(
            dimension_semantics=("arbitrary", "arbitrary"), vmem_limit_bytes=VMEM_LIMIT),
        name="mixer_prompt",
    )(sinks, x, *wts)


SEQ_CHUNK = 16


def _mixer_sample_kernel(sinks_ref, x_ref, ck_ref, cv_ref, h0_ref, cprev_ref,
                         w_in_ref, b_in_ref, conv_w_ref, conv_b_ref,
                         wa_ref, ba_ref, wi_ref, bi_ref, lam_ref, w_out_ref, b_out_ref,
                         ln_g_ref, ln_b_ref, rw_ref, rb_ref,
                         h1_ref, xt_ref, dest_ref, tg_ref, cnt_ref, wk_ref, wv_ref, lh_ref, cnew_ref,
                         proj_s, attn_s, attn_c):
    c = pl.program_id(0)
    nc = pl.num_programs(0)
    nseq = x_ref.shape[0]

    @pl.when(c == 0)
    def _():
        proj_s[...] = jnp.dot(x_ref[...].astype(BF16), w_in_ref[...],
                              preferred_element_type=F32) + b_in_ref[...]

    sub = lax.broadcasted_iota(jnp.int32, (N_Q_HEADS, LANES), 0)
    lane = lax.broadcasted_iota(jnp.int32, (N_Q_HEADS, LANES), 1)
    own_half = (lane < HEAD_DIM) == (sub < HEAD_GROUPS)
    sub1 = sub[:, 0:1]
    slope = jnp.zeros((N_Q_HEADS, 1), F32)
    sink = jnp.zeros((N_Q_HEADS, 1), F32)
    for hd in range(N_Q_HEADS):
        slope = jnp.where(sub1 == hd, 2.0 ** -(hd + 1), slope)
        sink = jnp.where(sub1 == hd, sinks_ref[hd], sink)
    dist = (WINDOW - lax.broadcasted_iota(jnp.int32, (1, WINDOW), 1)).astype(F32)
    lo_row = lax.broadcasted_iota(jnp.int32, (1, LANES), 1) < HEAD_DIM
    scale = HEAD_DIM ** -0.5

    for i in range(SEQ_CHUNK):
        b = c * SEQ_CHUNK + i
        prow = proj_s[pl.ds(b, 1), :]
        q8 = jnp.zeros((N_Q_HEADS, LANES), F32)
        for g in range(HEAD_GROUPS):
            qg = jnp.broadcast_to(prow[:, g * LANES:(g + 1) * LANES], (N_Q_HEADS, LANES))
            q8 = jnp.where(((sub & (HEAD_GROUPS - 1)) == g) & own_half, qg, q8)
        k_new = prow[:, Q_COLS:Q_COLS + KV_COLS]
        v_new = prow[:, Q_COLS + KV_COLS:Q_COLS + 2 * KV_COLS]
        kb = ck_ref[i]
        vb = cv_ref[i]
        sc = lax.dot_general(q8.astype(BF16), kb.astype(BF16), (((1,), (1,)), ((), ())),
                             preferred_element_type=F32) * scale - slope * dist
        sc_new = jnp.sum(q8 * k_new, axis=-1, keepdims=True) * scale
        m = jnp.maximum(jnp.maximum(jnp.max(sc, axis=-1, keepdims=True), sc_new), sink)
        p = jnp.exp(sc - m)
        p_new = jnp.exp(sc_new - m)
        den = jnp.sum(p, axis=-1, keepdims=True) + p_new + jnp.exp(sink - m)
        o = (jnp.dot(p.astype(BF16), vb.astype(BF16), preferred_element_type=F32)
             + p_new * v_new) / den
        for g in range(HEAD_GROUPS):
            attn_c[i:i + 1, g * LANES:(g + 1) * LANES] = jnp.where(
                lo_row, o[g:g + 1, :], o[g + HEAD_GROUPS:g + HEAD_GROUPS + 1, :])
        wk_ref[i, 0:WINDOW - 1, :] = ck_ref[i, 1:WINDOW, :]
        wk_ref[i, WINDOW - 1:WINDOW, :] = k_new
        wv_ref[i, 0:WINDOW - 1, :] = cv_ref[i, 1:WINDOW, :]
        wv_ref[i, WINDOW - 1:WINDOW, :] = v_new
    attn_s[pl.ds(pl.multiple_of(c * SEQ_CHUNK, SEQ_CHUNK), SEQ_CHUNK), :] = attn_c[...]

    @pl.when(c == nc - 1)
    def _():
        x = x_ref[...]
        ux = proj_s[:, Q_COLS + 2 * KV_COLS:Q_COLS + 2 * KV_COLS + LRU_WIDTH]
        ug = proj_s[:, Q_COLS + 2 * KV_COLS + LRU_WIDTH:]
        xc = conv_b_ref[...]
        for tap in range(CONV_WIDTH - 1):
            xc = xc + cprev_ref[tap] * conv_w_ref[tap:tap + 1, :]
        xc = xc + ux * conv_w_ref[CONV_WIDTH - 1:CONV_WIDTH, :]
        a, bv = _rglru_gates(xc, wa_ref, ba_ref, wi_ref, bi_ref, lam_ref)
        h = a * h0_ref[...] + bv
        rnn = h * _gelu_tanh(ug)
        mix_in = jnp.concatenate([attn_s[...], rnn], axis=1)
        h1 = _post_mix(x, mix_in, w_out_ref, b_out_ref, ln_g_ref, ln_b_ref)
        h1_ref[...] = h1
        te, tg, cnt = _route(h1, rw_ref, rb_ref)
        cnt_ref[0] = cnt
        tg_ref[0] = jnp.concatenate([tg, jnp.zeros((TOP_K, TILE - nseq), F32)], axis=1)
        te_tile = jnp.concatenate([te, jnp.full((TOP_K, TILE - nseq), -1, jnp.int32)], axis=1)
        h1_tile = jnp.concatenate([h1, jnp.zeros((TILE - nseq, D_MODEL), F32)], axis=0)
        picked, dest = _compact_tile(h1_tile, te_tile, cnt)
        xt_ref[0] = picked
        dest_ref[0] = dest
        lh_ref[...] = h
        for tap in range(1, CONV_WIDTH - 1):
            cnew_ref[tap - 1] = cprev_ref[tap]
        cnew_ref[CONV_WIDTH - 2] = ux


def _mixer_sample(x, ck, cv, h0, cprev, sinks, wts):
    nseq = x.shape[0]
    assert nseq % SEQ_CHUNK == 0 and nseq <= TILE and nseq % LANES == 0
    nc = nseq // SEQ_CHUNK
    chunk_idx = lambda c, *_: (c, 0, 0)
    in_specs = [
        _const_spec((nseq, D_MODEL)),
        pl.BlockSpec((SEQ_CHUNK, WINDOW, KV_COLS), chunk_idx),
        pl.BlockSpec((SEQ_CHUNK, WINDOW, KV_COLS), chunk_idx),
        _const_spec((nseq, LRU_WIDTH)),
        _const_spec((CONV_WIDTH - 1, nseq, LRU_WIDTH)),
    ] + [_const_spec(w.shape) for w in wts]
    out_shape = (
        jax.ShapeDtypeStruct((nseq, D_MODEL), F32),
        jax.ShapeDtypeStruct((1, TILE_ROWS, D_MODEL), F32),
        jax.ShapeDtypeStruct((1, TOP_K, TILE), jnp.int32),
        jax.ShapeDtypeStruct((1, TOP_K, TILE), F32),
        jax.ShapeDtypeStruct((1, N_EXPERTS, 1), jnp.int32),
        jax.ShapeDtypeStruct((nseq, WINDOW, KV_COLS), F32),
        jax.ShapeDtypeStruct((nseq, WINDOW, KV_COLS), F32),
        jax.ShapeDtypeStruct((nseq, LRU_WIDTH), F32),
        jax.ShapeDtypeStruct((CONV_WIDTH - 1, nseq, LRU_WIDTH), F32),
    )
    out_specs = (
        _const_spec((nseq, D_MODEL)),
        _const_spec((1, TILE_ROWS, D_MODEL)),
        _const_spec((1, TOP_K, TILE)),
        _const_spec((1, TOP_K, TILE)),
        _const_spec((1, N_EXPERTS, 1)),
        pl.BlockSpec((SEQ_CHUNK, WINDOW, KV_COLS), chunk_idx),
        pl.BlockSpec((SEQ_CHUNK, WINDOW, KV_COLS), chunk_idx),
        _const_spec((nseq, LRU_WIDTH)),
        _const_spec((CONV_WIDTH - 1, nseq, LRU_WIDTH)),
    )
    scratch = [pltpu.VMEM((nseq, D_IN), F32), pltpu.VMEM((nseq, Q_COLS), F32),
               pltpu.VMEM((SEQ_CHUNK, Q_COLS), F32)]
    return pl.pallas_call(
        _mixer_sample_kernel,
        grid_spec=pltpu.PrefetchScalarGridSpec(
            num_scalar_prefetch=1, grid=(nc,), in_specs=in_specs, out_specs=out_specs,
            scratch_shapes=scratch),
        out_shape=out_shape,
        compiler_params=pltpu.CompilerParams(
            dimension_semantics=("arbitrary",), vmem_limit_bytes=VMEM_LIMIT),
        name="mixer_sample",
    )(sinks, x, ck, cv, h0, cprev, *wts)


def _grouped(rows):
    assert rows % SEG_ALIGN == 0
    return (rows // SEG_ALIGN, SEG_ALIGN, D_MODEL)


BLOCK_GROUPS = ROW_BLOCK // SEG_ALIGN
TILE_GROUPS = TILE_ROWS // SEG_ALIGN


def _experts_kernel(be_ref, nxt_ref, nb_ref, tlo_ref, thi_ref, valid_ref,
                    seg_len_ref, seg_src_ref, seg_off_ref,
                    xtp_ref, xts_ref, wgu_ref, bgu_ref, wdn_ref, bdn_ref, yb_ref,
                    xbuf, wgu_f, wdn_f, wgu_s, wdn_s, wsem, xsem, *, n_tiles, n_prompt_tiles):
    b = pl.program_id(0)
    slot = b % 2

    def weight_copies(e):
        return (pltpu.make_async_copy(wgu_ref.at[e], wgu_f, wsem.at[0]),
                pltpu.make_async_copy(wdn_ref.at[e], wdn_f, wsem.at[1]))

    def start_gather(blk, s):
        lo_b = blk * BLOCK_GROUPS
        first = be_ref[blk] * n_tiles

        t_lo = tlo_ref[blk]
        t_hi = thi_ref[blk]

        def piece(t):
            off = seg_off_ref[first + t]
            lo = jnp.maximum(off, lo_b)
            n = jnp.minimum(off + seg_len_ref[first + t], lo_b + BLOCK_GROUPS) - lo

            src = pl.ds(seg_src_ref[first + t] + lo, n)
            dst = xbuf.at[s, pl.ds(lo - lo_b, n)]
            wanted = (n > 0) & (t <= t_hi)

            @pl.when(wanted & (t < n_prompt_tiles))
            def _():
                pltpu.make_async_copy(xtp_ref.at[src], dst, xsem.at[s]).start()

            @pl.when(wanted & (t >= n_prompt_tiles))
            def _():
                pltpu.make_async_copy(xts_ref.at[src], dst, xsem.at[s]).start()

        def two_pieces(i, carry):
            piece(t_lo + 2 * i)
            piece(t_lo + 2 * i + 1)
            return carry

        lax.fori_loop(0, (t_hi - t_lo + 2) // 2, two_pieces, 0)

    @pl.when(b == 0)
    def _():
        xbuf[...] = jnp.zeros(xbuf.shape, F32)
        for cp in weight_copies(be_ref[0]):
            cp.start()
        start_gather(b, slot)

    @pl.when(b + 1 < nb_ref[0])
    def _():
        start_gather(b + 1, 1 - slot)

    @pl.when(b < nb_ref[0])
    def _():
        n = valid_ref[b]
        pltpu.make_async_copy(xtp_ref.at[pl.ds(0, n)], xbuf.at[slot, pl.ds(0, n)], xsem.at[slot]).wait()

        @pl.when((b == 0) | (be_ref[b] != be_ref[jnp.maximum(b - 1, 0)]))
        def _():
            for cp in weight_copies(be_ref[b]):
                cp.wait()
            wgu_s[...] = wgu_f[...].astype(BF16)
            wdn_s[...] = wdn_f[...].astype(BF16)

            @pl.when(nxt_ref[b] >= 0)
            def _():
                for cp in weight_copies(nxt_ref[b]):
                    cp.start()

        def expert_rows(rows):
            x = xbuf[slot, pl.ds(0, rows // SEG_ALIGN)].reshape(rows, D_MODEL).astype(BF16)
            hgu = jnp.dot(x, wgu_s[...], preferred_element_type=F32) + bgu_ref[0]
            glu = jnp.minimum(hgu[:, :D_FF], SWIGLU_LIMIT)
            lin = jnp.clip(hgu[:, D_FF:], -SWIGLU_LIMIT, SWIGLU_LIMIT)
            act = glu * jax.nn.sigmoid(SWIGLU_ALPHA * glu) * (lin + 1.0)
            yb_ref[0:rows, :] = jnp.dot(act.astype(BF16), wdn_s[...], preferred_element_type=F32) + bdn_ref[0]

        quarter = ROW_BLOCK // 4
        quarters = (n * SEG_ALIGN + quarter - 1) // quarter
        for used in range(1, 5):
            @pl.when(quarters == used)
            def _(rows=used * quarter):
                expert_rows(rows)
                if rows < ROW_BLOCK:
                    yb_ref[rows:, :] = jnp.zeros((ROW_BLOCK - rows, D_MODEL), F32)

    @pl.when(b >= nb_ref[0])
    def _():
        yb_ref[...] = jnp.zeros(yb_ref.shape, F32)


def _experts(sched, segs, xt_prompt, xt_sample, w_gu, b_gu, w_dn, b_dn, n_rows):
    n_blocks = n_rows // ROW_BLOCK
    n_prompt_tiles = xt_prompt.shape[0] // TILE_GROUPS
    n_tiles = n_prompt_tiles + xt_sample.shape[0] // TILE_GROUPS
    n_prefetch = len(sched) + len(segs)
    exp_idx = lambda b, be, *_: (be[b], 0, 0)
    in_specs = [
        pl.BlockSpec(memory_space=pl.ANY),
        pl.BlockSpec(memory_space=pl.ANY),
        pl.BlockSpec(memory_space=pl.ANY),
        pl.BlockSpec((1, 1, 2 * D_FF), exp_idx),
        pl.BlockSpec(memory_space=pl.ANY),
        pl.BlockSpec((1, 1, D_MODEL), exp_idx),
    ]
    scratch = [pltpu.VMEM((2,) + _grouped(ROW_BLOCK), F32),
               pltpu.VMEM((D_MODEL, 2 * D_FF), F32), pltpu.VMEM((D_FF, D_MODEL), F32),
               pltpu.VMEM((D_MODEL, 2 * D_FF), BF16), pltpu.VMEM((D_FF, D_MODEL), BF16),
               pltpu.SemaphoreType.DMA((2,)), pltpu.SemaphoreType.DMA((2,))]
    return pl.pallas_call(
        functools.partial(_experts_kernel, n_tiles=n_tiles, n_prompt_tiles=n_prompt_tiles),
        grid_spec=pltpu.PrefetchScalarGridSpec(
            num_scalar_prefetch=n_prefetch, grid=(n_blocks,), in_specs=in_specs,
            out_specs=pl.BlockSpec((ROW_BLOCK, D_MODEL), lambda b, *_: (b, 0)),
            scratch_shapes=scratch),
        out_shape=jax.ShapeDtypeStruct((n_rows, D_MODEL), F32),
        compiler_params=pltpu.CompilerParams(
            dimension_semantics=("arbitrary",), vmem_limit_bytes=VMEM_LIMIT),
        name="moe_experts",
    )(*sched, *segs, xt_prompt, xt_sample, w_gu, b_gu[:, None, :], w_dn, b_dn[:, None, :])


def _tile_tokens(i, n_prompt_tiles, hp_ref, hs_ref):
    hs = hs_ref[...]
    hs_tile = jnp.concatenate([hs, jnp.zeros((TILE - hs.shape[0], D_MODEL), F32)], axis=0)
    return jnp.where(i < n_prompt_tiles, hp_ref[...], hs_tile)


COMBINE_SLOTS = 3


def _combine_kernel(seg_len_ref, seg_start_ref, seg_off_ref, tile_groups_ref,
                    hp_ref, hs_ref, dest_ref, gate_ref, g_ref, b_ref, yb_ref,
                    yp_ref, ys_ref, ybuf, sem, *, n_prompt_tiles):
    i = pl.program_id(0)
    n = pl.num_programs(0)
    slot = i % COMBINE_SLOTS

    def start_gather(tile):
        s = tile % COMBINE_SLOTS

        def segment(e, carry):
            idx = tile * N_EXPERTS + e
            groups = seg_len_ref[idx]

            @pl.when(groups > 0)
            def _():
                pltpu.make_async_copy(yb_ref.at[pl.ds(seg_off_ref[idx], groups)],
                                      ybuf.at[s, pl.ds(seg_start_ref[idx], groups)], sem.at[s]).start()
            return carry
        lax.fori_loop(0, N_EXPERTS, segment, 0)

    @pl.when(i == 0)
    def _():
        ybuf[...] = jnp.zeros(ybuf.shape, F32)
        for ahead in range(COMBINE_SLOTS - 1):
            @pl.when(ahead < n)
            def _():
                start_gather(i + ahead)

    @pl.when(i + COMBINE_SLOTS - 1 < n)
    def _():
        start_gather(i + COMBINE_SLOTS - 1)

    total = tile_groups_ref[i]

    @pl.when(total > 0)
    def _():
        pltpu.make_async_copy(yb_ref.at[pl.ds(0, total)], ybuf.at[slot, pl.ds(0, total)], sem.at[slot]).wait()

    dest = dest_ref[0]
    gate = gate_ref[0]
    rid = lax.broadcasted_iota(jnp.int32, (TILE_ROWS, TILE), 0)
    weights = jnp.zeros((TILE_ROWS, TILE), F32)
    for k in range(TOP_K):
        weights = jnp.where(rid == dest[k:k + 1, :], gate[k:k + 1, :], weights)
    rows = ybuf[slot].reshape(TILE_ROWS, D_MODEL).astype(BF16)
    ff = lax.dot_general(weights.astype(BF16), rows, (((0,), (0,)), ((), ())),
                         preferred_element_type=F32)
    h1 = _tile_tokens(i, n_prompt_tiles, hp_ref, hs_ref)
    y = _layer_norm(DN_ALPHA * h1 + ff, g_ref[...], b_ref[...])

    @pl.when(i < n_prompt_tiles)
    def _():
        yp_ref[...] = y

    @pl.when(i >= n_prompt_tiles)
    def _():
        ys_ref[...] = y[:ys_ref.shape[0]]


def _combine(segs, tile_groups, n_seq, h1p, h1s, dest, gate, ln_g, ln_b, yb):
    n_tiles = dest.shape[0]
    n_prompt_tiles = h1p.shape[0] // TILE
    blocks_per_seq = n_prompt_tiles // n_seq

    def prompt_idx(i, *_):
        t = jnp.minimum(i, n_prompt_tiles - 1)
        return ((t % n_seq) * blocks_per_seq + t // n_seq, 0)
    in_specs = [
        pl.BlockSpec((TILE, D_MODEL), prompt_idx),
        _const_spec(h1s.shape),
        pl.BlockSpec((1, TOP_K, TILE), lambda i, *_: (i, 0, 0)),
        pl.BlockSpec((1, TOP_K, TILE), lambda i, *_: (i, 0, 0)),
        _const_spec(ln_g.shape),
        _const_spec(ln_b.shape),
        pl.BlockSpec(memory_space=pl.ANY),
    ]
    out_shape = (jax.ShapeDtypeStruct(h1p.shape, F32), jax.ShapeDtypeStruct(h1s.shape, F32))
    out_specs = (pl.BlockSpec((TILE, D_MODEL), prompt_idx), _const_spec(h1s.shape))
    return pl.pallas_call(
        functools.partial(_combine_kernel, n_prompt_tiles=n_prompt_tiles),
        grid_spec=pltpu.PrefetchScalarGridSpec(
            num_scalar_prefetch=len(segs) + 1, grid=(n_tiles,), in_specs=in_specs, out_specs=out_specs,
            scratch_shapes=[pltpu.VMEM((COMBINE_SLOTS,) + _grouped(TILE_ROWS), F32),
                            pltpu.SemaphoreType.DMA((COMBINE_SLOTS,))]),
        out_shape=out_shape,
        compiler_params=pltpu.CompilerParams(
            dimension_semantics=("arbitrary",), vmem_limit_bytes=VMEM_LIMIT),
        name="moe_combine",
    )(*segs, tile_groups, h1p, h1s, dest, gate, ln_g, ln_b, yb.reshape(_grouped(yb.shape[0])))


def _moe_layout(cnt, n_prompt_tiles):
    n_tiles = cnt.shape[0]
    i32 = lambda a: a.astype(jnp.int32)
    seg_len = (cnt + SEG_ALIGN - 1) // SEG_ALIGN
    seg_start = jnp.cumsum(seg_len, axis=1) - seg_len
    exp_len = jnp.sum(seg_len, axis=0)
    exp_blocks = (exp_len + BLOCK_GROUPS - 1) // BLOCK_GROUPS
    blocks_end = jnp.cumsum(exp_blocks)
    first_block = blocks_end - exp_blocks
    seg_off = first_block[None, :] * BLOCK_GROUPS + jnp.cumsum(seg_len, axis=0) - seg_len
    max_rows = n_tiles * (TOP_K * TILE + N_EXPERTS * (SEG_ALIGN - 1)) + N_EXPERTS * (ROW_BLOCK - SEG_ALIGN)
    n_blocks = -(-max_rows // ROW_BLOCK)
    n_used = blocks_end[-1]
    blk = jnp.minimum(jnp.arange(n_blocks, dtype=jnp.int32), n_used - 1)
    eid = jnp.arange(N_EXPERTS, dtype=jnp.int32)
    block_expert = jnp.minimum(jnp.sum(blocks_end[None, :] <= blk[:, None], axis=1), N_EXPERTS - 1)
    own = block_expert[:, None] == eid[None, :]
    pick = lambda per_expert: jnp.sum(jnp.where(own, per_expert[None, :], 0), axis=1)
    later_used = (eid[None, :] > eid[:, None]) & (exp_blocks[None, :] > 0)
    next_of = jnp.min(jnp.where(later_used, eid[None, :], N_EXPERTS), axis=1)
    next_expert = pick(jnp.where(next_of < N_EXPERTS, next_of, -1))
    lo = blk * BLOCK_GROUPS
    off_b = jnp.sum(jnp.where(own[:, None, :], seg_off[None, :, :], 0), axis=2)
    len_b = jnp.sum(jnp.where(own[:, None, :], seg_len[None, :, :], 0), axis=2)
    tile_lo = jnp.sum(off_b + len_b <= lo[:, None], axis=1)
    tile_hi = jnp.sum(off_b < lo[:, None] + BLOCK_GROUPS, axis=1) - 1
    valid = jnp.clip(pick(first_block * BLOCK_GROUPS + exp_len) - lo, 0, BLOCK_GROUPS)
    flat = lambda a: i32(a.reshape(-1))
    sched = (i32(block_expert), i32(next_expert), i32(n_used.reshape(1)), i32(tile_lo), i32(tile_hi),
             i32(valid))
    segs = (flat(seg_len), flat(seg_start), flat(seg_off))
    tile = jnp.arange(n_tiles, dtype=seg_off.dtype)
    tile_in_array = jnp.where(tile < n_prompt_tiles, tile, tile - n_prompt_tiles)
    seg_src = tile_in_array[:, None] * TILE_GROUPS + seg_start - seg_off
    pad1 = lambda a: jnp.concatenate([flat(a.T), jnp.zeros((1,), jnp.int32)])
    segs_by_expert = (pad1(seg_len), pad1(seg_src), pad1(seg_off))
    return sched, segs, segs_by_expert, i32(jnp.sum(seg_len, axis=1)), n_blocks * ROW_BLOCK


def _prep_weights(w_in, b_in, conv_w, conv_b, lru_w_a, lru_b_a, lru_w_i, lru_b_i, lru_lambda,
                  w_out, b_out, ln1_g, ln1_b, router_w, router_b):
    def regroup_heads(a):
        rest = a.shape[1:]
        q = a[:Q_COLS].reshape(N_KV_HEADS, HEAD_GROUPS, HEAD_DIM, *rest)
        q = jnp.swapaxes(q, 0, 1).reshape(Q_COLS, *rest)
        return jnp.concatenate([q, a[Q_COLS:]], axis=0)

    def diag_tiles(w):
        per = MXU_DIM // (LRU_WIDTH // LRU_BLOCKS)
        w4 = w.reshape(LRU_BLOCKS // per, per, LRU_WIDTH // LRU_BLOCKS, LRU_WIDTH // LRU_BLOCKS)
        t = jnp.einsum("taij,ab->taibj", w4, jnp.eye(per, dtype=w.dtype))
        return t.reshape(LRU_BLOCKS // per, MXU_DIM, MXU_DIM).astype(BF16)

    return (
        regroup_heads(w_in[0].T).T.astype(BF16), regroup_heads(b_in[0])[None],
        conv_w[0], conv_b[0][None],
        diag_tiles(lru_w_a[0]), lru_b_a[0].reshape(1, LRU_WIDTH),
        diag_tiles(lru_w_i[0]), lru_b_i[0].reshape(1, LRU_WIDTH),
        lru_lambda[0][None],
        regroup_heads(w_out[0]).astype(BF16), b_out[0][None],
        ln1_g[0][None], ln1_b[0][None],
        router_w[0].T.astype(BF16), router_b[0][:, None],
    )


def kernel(x_prompt, x_sample, cache_win_k, cache_win_v, state_lru_h, state_conv, w_in, b_in, attn_sinks, conv_w, conv_b, lru_w_a, lru_b_a, lru_w_i, lru_b_i, lru_lambda, w_out, b_out, ln1_g, ln1_b, router_w, router_b, w_gate_up, b_gate_up, w_down, b_down, ln2_g, ln2_b):
    assert w_in.shape[0] == 1, "single-layer step"
    B, S, _ = x_prompt.shape
    nseq = x_sample.shape[0]
    assert x_sample.shape[1] == 1 and S % TILE == 0
    wts = _prep_weights(w_in, b_in, conv_w, conv_b, lru_w_a, lru_b_a, lru_w_i, lru_b_i, lru_lambda,
                        w_out, b_out, ln1_g, ln1_b, router_w, router_b)
    sinks = attn_sinks[0]

    h1p, xt_p, dest_p, tg_p, cnt_p, pk, pv, ph, pc = _mixer_prompt(x_prompt, sinks, wts)
    h1s, xt_s, dest_s, tg_s, cnt_s, sk, sv, sh, sc = _mixer_sample(
        x_sample.reshape(nseq, D_MODEL),
        cache_win_k[0].reshape(nseq, WINDOW, KV_COLS), cache_win_v[0].reshape(nseq, WINDOW, KV_COLS),
        state_lru_h[0], jnp.transpose(state_conv[0], (1, 0, 2)), sinks, wts)

    per_tile = lambda a: a.reshape(-1, *a.shape[2:])
    dest = jnp.concatenate([per_tile(dest_p), dest_s], axis=0)
    tg = jnp.concatenate([per_tile(tg_p), tg_s], axis=0)
    cnt = jnp.concatenate([per_tile(cnt_p), cnt_s], axis=0)[:, :, 0]
    sched, segs, segs_by_expert, tile_groups, n_rows = _moe_layout(cnt, xt_p.shape[0])

    end_to_end = lambda a: a.reshape(-1, SEG_ALIGN, D_MODEL)
    yb = _experts(sched, segs_by_expert, end_to_end(xt_p), end_to_end(xt_s),
                  w_gate_up[0], b_gate_up[0], w_down[0], b_down[0], n_rows)
    yp, ys = _combine(segs, tile_groups, B, h1p.reshape(B * S, D_MODEL), h1s, dest, tg,
                      ln2_g[0][None], ln2_b[0][None], yb)

    kv_shape = (N_KV_HEADS, HEAD_DIM)
    return (
        yp.reshape(B, S, D_MODEL), ys.reshape(nseq, 1, D_MODEL),
        pk.reshape(1, B, WINDOW, *kv_shape), pv.reshape(1, B, WINDOW, *kv_shape),
        ph.reshape(1, B, LRU_WIDTH), pc[None],
        sk.reshape(1, nseq, WINDOW, *kv_shape), sv.reshape(1, nseq, WINDOW, *kv_shape),
        sh[None], jnp.transpose(sc, (1, 0, 2))[None],
    )
```

```python
import functools

import jax
import jax.numpy as jnp
import numpy as np
from jax import lax
from jax.experimental import pallas as pl
from jax.experimental.pallas import tpu as pltpu

F32 = jnp.float32
BF16 = jnp.bfloat16

D_MODEL = 1024
N_Q_HEADS = 8
N_KV_HEADS = 2
HEAD_DIM = 64
WINDOW = 128
Q_COLS = N_Q_HEADS * HEAD_DIM
KV_COLS = N_KV_HEADS * HEAD_DIM
LRU_WIDTH = 512
LRU_BLOCKS = 8
CONV_WIDTH = 4
RG_C = 8.0
N_EXPERTS = 32
TOP_K = 4
D_FF = 1024
SWIGLU_LIMIT = 7.0
SWIGLU_ALPHA = 1.702
LN_EPS = 1e-5
DN_ALPHA = 2.0 ** 0.25
NEG_INF = -1e30
LOG2E = 1.4426950408889634
D_IN = Q_COLS + 2 * KV_COLS + 2 * LRU_WIDTH

LANES = 128
SUBLANES = 8
MXU_DIM = 256

TILE = 256
SEG_ALIGN = SUBLANES
ROW_BLOCK = 512
HEAD_GROUPS = N_Q_HEADS // N_KV_HEADS
VMEM_LIMIT = 56 * 1024 * 1024

TILE_ROWS = 1280
assert TILE_ROWS >= TOP_K * TILE + N_EXPERTS * (SEG_ALIGN - 1) and TILE_ROWS % LANES == 0


def _layer_norm(z, g, b):
    mu = jnp.mean(z, axis=-1, keepdims=True)
    zc = z - mu
    var = jnp.mean(zc * zc, axis=-1, keepdims=True)
    return zc * lax.rsqrt(var + LN_EPS) * g + b


def _sigmoid(x):
    return 0.5 + 0.5 * jnp.tanh(0.5 * x)


def _softplus(x):
    return jnp.maximum(x, 0.0) + jnp.log1p(jnp.exp(-jnp.abs(x)))


def _gelu_tanh(x):
    c = np.float32(np.sqrt(2.0 / np.pi))
    return 0.5 * x * (1.0 + jnp.tanh(c * (x + 0.044715 * (x * x * x))))


def _rglru_gates(xc, wa_ref, ba_ref, wi_ref, bi_ref, lam_ref):
    xcb = xc.astype(BF16)
    half = LRU_WIDTH // 2
    pre_a = jnp.concatenate(
        [jnp.dot(xcb[:, :half], wa_ref[0], preferred_element_type=F32),
         jnp.dot(xcb[:, half:], wa_ref[1], preferred_element_type=F32)], axis=1)
    pre_i = jnp.concatenate(
        [jnp.dot(xcb[:, :half], wi_ref[0], preferred_element_type=F32),
         jnp.dot(xcb[:, half:], wi_ref[1], preferred_element_type=F32)], axis=1)
    r = _sigmoid(pre_a + ba_ref[...])
    gi = _sigmoid(pre_i + bi_ref[...])
    log_a = (-RG_C) * r * _softplus(-lam_ref[...])
    a = jnp.exp(log_a)
    t = jnp.tanh(log_a)
    mult = jnp.sqrt(jnp.maximum(-2.0 * t / (1.0 - t), 0.0))
    return a, mult * (gi * xc)


def _route(h1, rw_ref, rb_ref):
    n = h1.shape[0]
    logits = lax.dot_general(rw_ref[...], h1.astype(BF16), (((1,), (1,)), ((), ())),
                             preferred_element_type=F32) + rb_ref[...]
    eidx = lax.broadcasted_iota(jnp.int32, (N_EXPERTS, n), 0)
    vals = logits
    top_v, top_e = [], []
    for _ in range(TOP_K):
        m = jnp.max(vals, axis=0, keepdims=True)
        idx = jnp.min(jnp.where(vals == m, eidx, N_EXPERTS), axis=0, keepdims=True)
        top_v.append(m)
        top_e.append(idx)
        vals = jnp.where(eidx == idx, -jnp.inf, vals)
    ex = [jnp.exp(v - top_v[0]) for v in top_v]
    den = ex[0] + ex[1] + ex[2] + ex[3]
    gates = jnp.concatenate([e / den for e in ex], axis=0)
    te = jnp.concatenate(top_e, axis=0)
    mem = jnp.zeros((N_EXPERTS, n), jnp.int32)
    for idx in top_e:
        mem = mem + jnp.where(eidx == idx, 1, 0)
    return te, gates, jnp.sum(mem, axis=1, keepdims=True)


def _compact_tile(h1, te, cnt):
    eidx = lax.broadcasted_iota(jnp.int32, (N_EXPERTS, TILE), 0)
    mem = jnp.zeros((N_EXPERTS, TILE), F32)
    for k in range(TOP_K):
        mem = mem + jnp.where(eidx == te[k:k + 1, :], 1.0, 0.0)
    tr = lax.broadcasted_iota(jnp.int32, (TILE, TILE), 0)
    tc = lax.broadcasted_iota(jnp.int32, (TILE, TILE), 1)
    before = jnp.where(tr < tc, 1.0, 0.0).astype(BF16)
    pos = jnp.dot(mem.astype(BF16), before, preferred_element_type=F32)
    seg_rows = ((cnt + (SEG_ALIGN - 1)) // SEG_ALIGN * SEG_ALIGN).astype(F32)
    er = lax.broadcasted_iota(jnp.int32, (N_EXPERTS, N_EXPERTS), 0)
    ec = lax.broadcasted_iota(jnp.int32, (N_EXPERTS, N_EXPERTS), 1)
    earlier = jnp.where(ec < er, 1.0, 0.0).astype(BF16)
    seg_start = jnp.dot(earlier, jnp.broadcast_to(seg_rows, (N_EXPERTS, LANES)).astype(BF16),
                        preferred_element_type=F32)[:, 0:1]
    base = (pos + seg_start).astype(jnp.int32)
    dests = []
    for k in range(TOP_K):
        d = jnp.sum(jnp.where(eidx == te[k:k + 1, :], base, 0), axis=0, keepdims=True)
        dests.append(jnp.where(te[k:k + 1, :] >= 0, d, -1))
    rid = lax.broadcasted_iota(jnp.int32, (TILE_ROWS, TILE), 0)
    onehot = jnp.zeros((TILE_ROWS, TILE), F32)
    for k in range(TOP_K):
        onehot = jnp.where(rid == dests[k], 1.0, onehot)
    picked = jnp.dot(onehot.astype(BF16), h1.astype(BF16), preferred_element_type=F32)
    return picked, jnp.concatenate(dests, axis=0)


def _post_mix(x, mix_in, w_out_ref, b_out_ref, g_ref, b_ref):
    mix = jnp.dot(mix_in.astype(BF16), w_out_ref[...], preferred_element_type=F32) + b_out_ref[...]
    return _layer_norm(DN_ALPHA * x + mix, g_ref[...], b_ref[...])


SEQ_PAIR = 2


def _interleave(chains):
    results = [None] * len(chains)
    live = list(range(len(chains)))
    while live:
        for k in list(live):
            try:
                next(chains[k])
            except StopIteration as stop:
                results[k] = stop.value
                live.remove(k)
    return results


def _bias_tables(bias_s):
    row2 = lax.broadcasted_iota(jnp.int32, (2 * WINDOW, 2 * WINDOW), 0)
    col2 = lax.broadcasted_iota(jnp.int32, (2 * WINDOW, 2 * WINDOW), 1)
    dist = (row2 & (WINDOW - 1)) - col2 + WINDOW
    valid = (dist >= 0) & (dist <= WINDOW)
    distf = dist.astype(F32)
    for g in range(HEAD_GROUPS):
        slope = jnp.where(row2 >= WINDOW, 2.0 ** -(g + HEAD_GROUPS + 1), 2.0 ** -(g + 1))
        bias = jnp.where(valid, (-LOG2E) * (slope * distf), NEG_INF)
        bias_s[g, 0] = bias
        bias_s[g, 1] = jnp.where(col2 >= WINDOW, bias, NEG_INF)


def _mixer_prompt_kernel(sinks_ref, x_ref, *rest):
    wts = rest[:15]
    h1_ref, xt_ref, dest_ref, tg_ref, cnt_ref, wk_ref, wv_ref, lh_ref, cv_ref = rest[15:24]
    kext, vext, uext, hcar, attn_buf, bias_s = rest[24:]
    j = pl.program_id(1)
    nj = pl.num_programs(1)
    R = TILE
    one = lambda ref, bb: ref.at[pl.ds(bb, 1)]

    @pl.when(j == 0)
    def _():
        for bb in range(SEQ_PAIR):
            kext[bb, 0:WINDOW, :] = jnp.zeros((WINDOW, KV_COLS), F32)
            vext[bb, 0:WINDOW, :] = jnp.zeros((WINDOW, KV_COLS), F32)
            uext[bb, 0:SUBLANES, :] = jnp.zeros((SUBLANES, LRU_WIDTH), F32)
            hcar[bb] = jnp.zeros((1, LRU_WIDTH), F32)

    @pl.when((pl.program_id(0) == 0) & (j == 0))
    def _():
        _bias_tables(bias_s)

    per_tile = lambda ref, bb: ref.at[0, pl.ds(bb, 1)]
    h_last = _interleave([
        _mixer_tile(j, sinks_ref, one(x_ref, bb), wts, one(h1_ref, bb), one(xt_ref, bb),
                    per_tile(dest_ref, bb), per_tile(tg_ref, bb), per_tile(cnt_ref, bb),
                    kext.at[bb], vext.at[bb], uext.at[bb], hcar.at[bb], attn_buf.at[bb], bias_s)
        for bb in range(SEQ_PAIR)])

    @pl.when(j == nj - 1)
    def _():
        for bb in range(SEQ_PAIR):
            wk_ref[bb] = kext[bb, R:R + WINDOW, :]
            wv_ref[bb] = vext[bb, R:R + WINDOW, :]
            lh_ref[bb] = h_last[bb]
            cv_ref[bb] = uext[bb, SUBLANES + R - (CONV_WIDTH - 1):SUBLANES + R, :]

    for bb in range(SEQ_PAIR):
        kext[bb, 0:WINDOW, :] = kext[bb, R:R + WINDOW, :]
        vext[bb, 0:WINDOW, :] = vext[bb, R:R + WINDOW, :]
        uext[bb, 0:SUBLANES, :] = uext[bb, R:R + SUBLANES, :]


def _mixer_tile(j, sinks_ref, x_ref, wts, h1_ref, xt_ref, dest_ref, tg_ref, cnt_ref,
                kext, vext, uext, hcar, attn_buf, bias_s):
    (w_in_ref, b_in_ref, conv_w_ref, conv_b_ref, wa_ref, ba_ref, wi_ref, bi_ref, lam_ref,
     w_out_ref, b_out_ref, ln_g_ref, ln_b_ref, rw_ref, rb_ref) = wts
    R = TILE
    x = x_ref[0]
    proj = jnp.dot(x.astype(BF16), w_in_ref[...], preferred_element_type=F32) + b_in_ref[...]
    q = proj[:, :Q_COLS]
    kext[WINDOW:WINDOW + R, :] = proj[:, Q_COLS:Q_COLS + KV_COLS]
    vext[WINDOW:WINDOW + R, :] = proj[:, Q_COLS + KV_COLS:Q_COLS + 2 * KV_COLS]
    uext[SUBLANES:SUBLANES + R, :] = proj[:, Q_COLS + 2 * KV_COLS:Q_COLS + 2 * KV_COLS + LRU_WIDTH]
    ug = proj[:, Q_COLS + 2 * KV_COLS + LRU_WIDTH:]
    yield

    upper = lax.broadcasted_iota(jnp.int32, (2 * WINDOW, 1), 0) >= WINDOW
    lo_lane = lax.broadcasted_iota(jnp.int32, (WINDOW, LANES), 1) < HEAD_DIM
    qs = q * (HEAD_DIM ** -0.5 * LOG2E)
    for s in range(R // WINDOW):
        kk = kext[s * WINDOW:s * WINDOW + 2 * WINDOW, :].astype(BF16)
        vv = vext[s * WINDOW:s * WINDOW + 2 * WINDOW, :].astype(BF16)
        table = jnp.where(j == 0, 1, 0) if s == 0 else 0
        for g in range(HEAD_GROUPS):
            qg = qs[s * WINDOW:(s + 1) * WINDOW, g * LANES:(g + 1) * LANES]
            q2 = jnp.concatenate([jnp.where(lo_lane, qg, 0.0), jnp.where(lo_lane, 0.0, qg)], axis=0)
            sc = lax.dot_general(q2.astype(BF16), kk, (((1,), (1,)), ((), ())),
                                 preferred_element_type=F32) + bias_s[g, table]
            sink = jnp.where(upper, sinks_ref[g + HEAD_GROUPS], sinks_ref[g]) * LOG2E
            m = jnp.maximum(jnp.max(sc, axis=-1, keepdims=True), sink)
            p = jnp.exp2(sc - m)
            den = jnp.sum(p, axis=-1, keepdims=True) + jnp.exp2(sink - m)
            o = jnp.dot(p.astype(BF16), vv, preferred_element_type=F32) / den
            attn_buf[s * WINDOW:(s + 1) * WINDOW, g * LANES:(g + 1) * LANES] = jnp.where(
                lo_lane, o[:WINDOW], o[WINDOW:])
            yield

    xc = conv_b_ref[...]
    for tap in range(CONV_WIDTH):
        off = SUBLANES - (CONV_WIDTH - 1) + tap
        xc = xc + uext[off:off + R, :] * conv_w_ref[tap:tap + 1, :]
    a, bv = _rglru_gates(xc, wa_ref, ba_ref, wi_ref, bi_ref, lam_ref)
    yield
    groups = R // SUBLANES
    a3 = a.reshape(groups, SUBLANES, LRU_WIDTH)
    b3 = bv.reshape(groups, SUBLANES, LRU_WIDTH)
    sub = lax.broadcasted_iota(jnp.int32, (groups, SUBLANES, LRU_WIDTH), 1)
    d = 1
    while d < SUBLANES:
        keep = sub >= d
        a_prev = jnp.where(keep, pltpu.roll(a3, d, 1), 1.0)
        b_prev = jnp.where(keep, pltpu.roll(b3, d, 1), 0.0)
        b3 = a3 * b_prev + b3
        a3 = a3 * a_prev
        d *= 2
    h_prev = hcar[...]
    h_groups = []
    for c in range(groups):
        hc = a3[c] * h_prev + b3[c]
        h_groups.append(hc)
        h_prev = hc[SUBLANES - 1:SUBLANES, :]
    h = jnp.concatenate(h_groups, axis=0)
    hcar[...] = h_prev
    rnn = h * _gelu_tanh(ug)
    yield

    mix_in = jnp.concatenate([attn_buf[...], rnn], axis=1)
    h1 = _post_mix(x, mix_in, w_out_ref, b_out_ref, ln_g_ref, ln_b_ref)
    h1_ref[0] = h1
    yield
    te, tg, cnt = _route(h1, rw_ref, rb_ref)
    tg_ref[0] = tg
    cnt_ref[0] = cnt
    yield
    picked, dest = _compact_tile(h1, te, cnt)
    xt_ref[0] = picked
    dest_ref[0] = dest
    return h_prev


def _const_spec(shape):
    return pl.BlockSpec(shape, lambda *_: (0,) * len(shape))


def _mixer_prompt(x, sinks, wts):
    B, S, _ = x.shape
    assert B % SEQ_PAIR == 0 and S % TILE == 0
    nj = S // TILE
    P = SEQ_PAIR
    tile_idx = lambda b, j, *_: (j, b, 0, 0)
    batch_idx = lambda b, j, *_: (b, 0, 0)
    in_specs = [pl.BlockSpec((P, TILE, D_MODEL), lambda b, j, *_: (b, j, 0))]
    in_specs += [_const_spec(w.shape) for w in wts]
    out_shape = (
        jax.ShapeDtypeStruct((B, S, D_MODEL), F32),
        jax.ShapeDtypeStruct((B * nj, TILE_ROWS, D_MODEL), F32),
        jax.ShapeDtypeStruct((nj, B, TOP_K, TILE), jnp.int32),
        jax.ShapeDtypeStruct((nj, B, TOP_K, TILE), F32),
        jax.ShapeDtypeStruct((nj, B, N_EXPERTS, 1), jnp.int32),
        jax.ShapeDtypeStruct((B, WINDOW, KV_COLS), F32),
        jax.ShapeDtypeStruct((B, WINDOW, KV_COLS), F32),
        jax.ShapeDtypeStruct((B, 1, LRU_WIDTH), F32),
        jax.ShapeDtypeStruct((B, CONV_WIDTH - 1, LRU_WIDTH), F32),
    )
    out_specs = (
        pl.BlockSpec((P, TILE, D_MODEL), lambda b, j, *_: (b, j, 0)),
        pl.BlockSpec((P, TILE_ROWS, D_MODEL), lambda b, j, *_: (j * (B // P) + b, 0, 0)),
        pl.BlockSpec((1, P, TOP_K, TILE), tile_idx),
        pl.BlockSpec((1, P, TOP_K, TILE), tile_idx),
        pl.BlockSpec((1, P, N_EXPERTS, 1), tile_idx),
        pl.BlockSpec((P, WINDOW, KV_COLS), batch_idx),
        pl.BlockSpec((P, WINDOW, KV_COLS), batch_idx),
        pl.BlockSpec((P, 1, LRU_WIDTH), batch_idx),
        pl.BlockSpec((P, CONV_WIDTH - 1, LRU_WIDTH), batch_idx),
    )
    scratch = [
        pltpu.VMEM((P, TILE + WINDOW, KV_COLS), F32),
        pltpu.VMEM((P, TILE + WINDOW, KV_COLS), F32),
        pltpu.VMEM((P, TILE + SUBLANES, LRU_WIDTH), F32),
        pltpu.VMEM((P, 1, LRU_WIDTH), F32),
        pltpu.VMEM((P, TILE, Q_COLS), F32),
        pltpu.VMEM((HEAD_GROUPS, 2, 2 * WINDOW, 2 * WINDOW), F32),
    ]
    return pl.pallas_call(
        _mixer_prompt_kernel,
        grid_spec=pltpu.PrefetchScalarGridSpec(
            num_scalar_prefetch=1, grid=(B // P, nj), in_specs=in_specs, out_specs=out_specs,
            scratch_shapes=scratch),
        out_shape=out_shape,
        compiler_params=pltpu.CompilerParams(
            dimension_semantics=("arbitrary", "arbitrary"), vmem_limit_bytes=VMEM_LIMIT),
        name="mixer_prompt",
    )(sinks, x, *wts)


SEQ_CHUNK = 16


def _mixer_sample_kernel(sinks_ref, x_ref, ck_ref, cv_ref, h0_ref, cprev_ref,
                         w_in_ref, b_in_ref, conv_w_ref, conv_b_ref,
                         wa_ref, ba_ref, wi_ref, bi_ref, lam_ref, w_out_ref, b_out_ref,
                         ln_g_ref, ln_b_ref, rw_ref, rb_ref,
                         h1_ref, xt_ref, dest_ref, tg_ref, cnt_ref, wk_ref, wv_ref, lh_ref, cnew_ref,
                         proj_s, attn_s, attn_c):
    c = pl.program_id(0)
    nc = pl.num_programs(0)
    nseq = x_ref.shape[0]

    @pl.when(c == 0)
    def _():
        proj_s[...] = jnp.dot(x_ref[...].astype(BF16), w_in_ref[...],
                              preferred_element_type=F32) + b_in_ref[...]

    sub = lax.broadcasted_iota(jnp.int32, (N_Q_HEADS, LANES), 0)
    lane = lax.broadcasted_iota(jnp.int32, (N_Q_HEADS, LANES), 1)
    own_half = (lane < HEAD_DIM) == (sub < HEAD_GROUPS)
    sub1 = sub[:, 0:1]
    slope = jnp.zeros((N_Q_HEADS, 1), F32)
    sink = jnp.zeros((N_Q_HEADS, 1), F32)
    for hd in range(N_Q_HEADS):
        slope = jnp.where(sub1 == hd, 2.0 ** -(hd + 1), slope)
        sink = jnp.where(sub1 == hd, sinks_ref[hd], sink)
    dist = (WINDOW - lax.broadcasted_iota(jnp.int32, (1, WINDOW), 1)).astype(F32)
    lo_row = lax.broadcasted_iota(jnp.int32, (1, LANES), 1) < HEAD_DIM
    scale = HEAD_DIM ** -0.5

    for i in range(SEQ_CHUNK):
        b = c * SEQ_CHUNK + i
        prow = proj_s[pl.ds(b, 1), :]
        q8 = jnp.zeros((N_Q_HEADS, LANES), F32)
        for g in range(HEAD_GROUPS):
            qg = jnp.broadcast_to(prow[:, g * LANES:(g + 1) * LANES], (N_Q_HEADS, LANES))
            q8 = jnp.where(((sub & (HEAD_GROUPS - 1)) == g) & own_half, qg, q8)
        k_new = prow[:, Q_COLS:Q_COLS + KV_COLS]
        v_new = prow[:, Q_COLS + KV_COLS:Q_COLS + 2 * KV_COLS]
        kb = ck_ref[i]
        vb = cv_ref[i]
        sc = lax.dot_general(q8.astype(BF16), kb.astype(BF16), (((1,), (1,)), ((), ())),
                             preferred_element_type=F32) * scale - slope * dist
        sc_new = jnp.sum(q8 * k_new, axis=-1, keepdims=True) * scale
        m = jnp.maximum(jnp.maximum(jnp.max(sc, axis=-1, keepdims=True), sc_new), sink)
        p = jnp.exp(sc - m)
        p_new = jnp.exp(sc_new - m)
        den = jnp.sum(p, axis=-1, keepdims=True) + p_new + jnp.exp(sink - m)
        o = (jnp.dot(p.astype(BF16), vb.astype(BF16), preferred_element_type=F32)
             + p_new * v_new) / den
        for g in range(HEAD_GROUPS):
            attn_c[i:i + 1, g * LANES:(g + 1) * LANES] = jnp.where(
                lo_row, o[g:g + 1, :], o[g + HEAD_GROUPS:g + HEAD_GROUPS + 1, :])
        wk_ref[i, 0:WINDOW - 1, :] = ck_ref[i, 1:WINDOW, :]
        wk_ref[i, WINDOW - 1:WINDOW, :] = k_new
        wv_ref[i, 0:WINDOW - 1, :] = cv_ref[i, 1:WINDOW, :]
        wv_ref[i, WINDOW - 1:WINDOW, :] = v_new
    attn_s[pl.ds(pl.multiple_of(c * SEQ_CHUNK, SEQ_CHUNK), SEQ_CHUNK), :] = attn_c[...]

    @pl.when(c == nc - 1)
    def _():
        x = x_ref[...]
        ux = proj_s[:, Q_COLS + 2 * KV_COLS:Q_COLS + 2 * KV_COLS + LRU_WIDTH]
        ug = proj_s[:, Q_COLS + 2 * KV_COLS + LRU_WIDTH:]
        xc = conv_b_ref[...]
        for tap in range(CONV_WIDTH - 1):
            xc = xc + cprev_ref[tap] * conv_w_ref[tap:tap + 1, :]
        xc = xc + ux * conv_w_ref[CONV_WIDTH - 1:CONV_WIDTH, :]
        a, bv = _rglru_gates(xc, wa_ref, ba_ref, wi_ref, bi_ref, lam_ref)
        h = a * h0_ref[...] + bv
        rnn = h * _gelu_tanh(ug)
        mix_in = jnp.concatenate([attn_s[...], rnn], axis=1)
        h1 = _post_mix(x, mix_in, w_out_ref, b_out_ref, ln_g_ref, ln_b_ref)
        h1_ref[...] = h1
        te, tg, cnt = _route(h1, rw_ref, rb_ref)
        cnt_ref[0] = cnt
        tg_ref[0] = jnp.concatenate([tg, jnp.zeros((TOP_K, TILE - nseq), F32)], axis=1)
        te_tile = jnp.concatenate([te, jnp.full((TOP_K, TILE - nseq), -1, jnp.int32)], axis=1)
        h1_tile = jnp.concatenate([h1, jnp.zeros((TILE - nseq, D_MODEL), F32)], axis=0)
        picked, dest = _compact_tile(h1_tile, te_tile, cnt)
        xt_ref[0] = picked
        dest_ref[0] = dest
        lh_ref[...] = h
        for tap in range(1, CONV_WIDTH - 1):
            cnew_ref[tap - 1] = cprev_ref[tap]
        cnew_ref[CONV_WIDTH - 2] = ux


def _mixer_sample(x, ck, cv, h0, cprev, sinks, wts):
    nseq = x.shape[0]
    assert nseq % SEQ_CHUNK == 0 and nseq <= TILE and nseq % LANES == 0
    nc = nseq // SEQ_CHUNK
    chunk_idx = lambda c, *_: (c, 0, 0)
    in_specs = [
        _const_spec((nseq, D_MODEL)),
        pl.BlockSpec((SEQ_CHUNK, WINDOW, KV_COLS), chunk_idx),
        pl.BlockSpec((SEQ_CHUNK, WINDOW, KV_COLS), chunk_idx),
        _const_spec((nseq, LRU_WIDTH)),
        _const_spec((CONV_WIDTH - 1, nseq, LRU_WIDTH)),
    ] + [_const_spec(w.shape) for w in wts]
    out_shape = (
        jax.ShapeDtypeStruct((nseq, D_MODEL), F32),
        jax.ShapeDtypeStruct((1, TILE_ROWS, D_MODEL), F32),
        jax.ShapeDtypeStruct((1, TOP_K, TILE), jnp.int32),
        jax.ShapeDtypeStruct((1, TOP_K, TILE), F32),
        jax.ShapeDtypeStruct((1, N_EXPERTS, 1), jnp.int32),
        jax.ShapeDtypeStruct((nseq, WINDOW, KV_COLS), F32),
        jax.ShapeDtypeStruct((nseq, WINDOW, KV_COLS), F32),
        jax.ShapeDtypeStruct((nseq, LRU_WIDTH), F32),
        jax.ShapeDtypeStruct((CONV_WIDTH - 1, nseq, LRU_WIDTH), F32),
    )
    out_specs = (
        _const_spec((nseq, D_MODEL)),
        _const_spec((1, TILE_ROWS, D_MODEL)),
        _const_spec((1, TOP_K, TILE)),
        _const_spec((1, TOP_K, TILE)),
        _const_spec((1, N_EXPERTS, 1)),
        pl.BlockSpec((SEQ_CHUNK, WINDOW, KV_COLS), chunk_idx),
        pl.BlockSpec((SEQ_CHUNK, WINDOW, KV_COLS), chunk_idx),
        _const_spec((nseq, LRU_WIDTH)),
        _const_spec((CONV_WIDTH - 1, nseq, LRU_WIDTH)),
    )
    scratch = [pltpu.VMEM((nseq, D_IN), F32), pltpu.VMEM((nseq, Q_COLS), F32),
               pltpu.VMEM((SEQ_CHUNK, Q_COLS), F32)]
    return pl.pallas_call(
        _mixer_sample_kernel,
        grid_spec=pltpu.PrefetchScalarGridSpec(
            num_scalar_prefetch=1, grid=(nc,), in_specs=in_specs, out_specs=out_specs,
            scratch_shapes=scratch),
        out_shape=out_shape,
        compiler_params=pltpu.CompilerParams(
            dimension_semantics=("arbitrary",), vmem_limit_bytes=VMEM_LIMIT),
        name="mixer_sample",
    )(sinks, x, ck, cv, h0, cprev, *wts)


def _grouped(rows):
    assert rows % SEG_ALIGN == 0
    return (rows // SEG_ALIGN, SEG_ALIGN, D_MODEL)


BLOCK_GROUPS = ROW_BLOCK // SEG_ALIGN
TILE_GROUPS = TILE_ROWS // SEG_ALIGN


def _experts_kernel(be_ref, nxt_ref, nb_ref, tlo_ref, thi_ref, valid_ref,
                    seg_len_ref, seg_src_ref, seg_off_ref,
                    xtp_ref, xts_ref, wgu_ref, bgu_ref, wdn_ref, bdn_ref, yb_ref,
                    xbuf, wgu_f, wdn_f, wgu_s, wdn_s, wsem, xsem, *, n_tiles, n_prompt_tiles):
    b = pl.program_id(0)
    slot = b % 2

    def weight_copies(e):
        return (pltpu.make_async_copy(wgu_ref.at[e], wgu_f, wsem.at[0]),
                pltpu.make_async_copy(wdn_ref.at[e], wdn_f, wsem.at[1]))

    def start_gather(blk, s):
        lo_b = blk * BLOCK_GROUPS
        first = be_ref[blk] * n_tiles

        t_lo = tlo_ref[blk]
        t_hi = thi_ref[blk]

        def piece(t):
            off = seg_off_ref[first + t]
            lo = jnp.maximum(off, lo_b)
            n = jnp.minimum(off + seg_len_ref[first + t], lo_b + BLOCK_GROUPS) - lo

            src = pl.ds(seg_src_ref[first + t] + lo, n)
            dst = xbuf.at[s, pl.ds(lo - lo_b, n)]
            wanted = (n > 0) & (t <= t_hi)

            @pl.when(wanted & (t < n_prompt_tiles))
            def _():
                pltpu.make_async_copy(xtp_ref.at[src], dst, xsem.at[s]).start()

            @pl.when(wanted & (t >= n_prompt_tiles))
            def _():
                pltpu.make_async_copy(xts_ref.at[src], dst, xsem.at[s]).start()

        def two_pieces(i, carry):
            piece(t_lo + 2 * i)
            piece(t_lo + 2 * i + 1)
            return carry

        lax.fori_loop(0, (t_hi - t_lo + 2) // 2, two_pieces, 0)

    @pl.when(b == 0)
    def _():
        xbuf[...] = jnp.zeros(xbuf.shape, F32)
        for cp in weight_copies(be_ref[0]):
            cp.start()
        start_gather(b, slot)

    @pl.when(b + 1 < nb_ref[0])
    def _():
        start_gather(b + 1, 1 - slot)

    @pl.when(b < nb_ref[0])
    def _():
        n = valid_ref[b]
        pltpu.make_async_copy(xtp_ref.at[pl.ds(0, n)], xbuf.at[slot, pl.ds(0, n)], xsem.at[slot]).wait()

        @pl.when((b == 0) | (be_ref[b] != be_ref[jnp.maximum(b - 1, 0)]))
        def _():
            for cp in weight_copies(be_ref[b]):
                cp.wait()
            wgu_s[...] = wgu_f[...].astype(BF16)
            wdn_s[...] = wdn_f[...].astype(BF16)

            @pl.when(nxt_ref[b] >= 0)
            def _():
                for cp in weight_copies(nxt_ref[b]):
                    cp.start()

        def expert_rows(rows):
            x = xbuf[slot, pl.ds(0, rows // SEG_ALIGN)].reshape(rows, D_MODEL).astype(BF16)
            hgu = jnp.dot(x, wgu_s[...], preferred_element_type=F32) + bgu_ref[0]
            glu = jnp.minimum(hgu[:, :D_FF], SWIGLU_LIMIT)
            lin = jnp.clip(hgu[:, D_FF:], -SWIGLU_LIMIT, SWIGLU_LIMIT)
            act = glu * jax.nn.sigmoid(SWIGLU_ALPHA * glu) * (lin + 1.0)
            yb_ref[0:rows, :] = jnp.dot(act.astype(BF16), wdn_s[...], preferred_element_type=F32) + bdn_ref[0]

        quarter = ROW_BLOCK // 4
        quarters = (n * SEG_ALIGN + quarter - 1) // quarter
        for used in range(1, 5):
            @pl.when(quarters == used)
            def _(rows=used * quarter):
                expert_rows(rows)
                if rows < ROW_BLOCK:
                    yb_ref[rows:, :] = jnp.zeros((ROW_BLOCK - rows, D_MODEL), F32)

    @pl.when(b >= nb_ref[0])
    def _():
        yb_ref[...] = jnp.zeros(yb_ref.shape, F32)


def _experts(sched, segs, xt_prompt, xt_sample, w_gu, b_gu, w_dn, b_dn, n_rows):
    n_blocks = n_rows // ROW_BLOCK
    n_prompt_tiles = xt_prompt.shape[0] // TILE_GROUPS
    n_tiles = n_prompt_tiles + xt_sample.shape[0] // TILE_GROUPS
    n_prefetch = len(sched) + len(segs)
    exp_idx = lambda b, be, *_: (be[b], 0, 0)
    in_specs = [
        pl.BlockSpec(memory_space=pl.ANY),
        pl.BlockSpec(memory_space=pl.ANY),
        pl.BlockSpec(memory_space=pl.ANY),
        pl.BlockSpec((1, 1, 2 * D_FF), exp_idx),
        pl.BlockSpec(memory_space=pl.ANY),
        pl.BlockSpec((1, 1, D_MODEL), exp_idx),
    ]
    scratch = [pltpu.VMEM((2,) + _grouped(ROW_BLOCK), F32),
               pltpu.VMEM((D_MODEL, 2 * D_FF), F32), pltpu.VMEM((D_FF, D_MODEL), F32),
               pltpu.VMEM((D_MODEL, 2 * D_FF), BF16), pltpu.VMEM((D_FF, D_MODEL), BF16),
               pltpu.SemaphoreType.DMA((2,)), pltpu.SemaphoreType.DMA((2,))]
    return pl.pallas_call(
        functools.partial(_experts_kernel, n_tiles=n_tiles, n_prompt_tiles=n_prompt_tiles),
        grid_spec=pltpu.PrefetchScalarGridSpec(
            num_scalar_prefetch=n_prefetch, grid=(n_blocks,), in_specs=in_specs,
            out_specs=pl.BlockSpec((ROW_BLOCK, D_MODEL), lambda b, *_: (b, 0)),
            scratch_shapes=scratch),
        out_shape=jax.ShapeDtypeStruct((n_rows, D_MODEL), F32),
        compiler_params=pltpu.CompilerParams(
            dimension_semantics=("arbitrary",), vmem_limit_bytes=VMEM_LIMIT),
        name="moe_experts",
    )(*sched, *segs, xt_prompt, xt_sample, w_gu, b_gu[:, None, :], w_dn, b_dn[:, None, :])


def _tile_tokens(i, n_prompt_tiles, hp_ref, hs_ref):
    hs = hs_ref[...]
    hs_tile = jnp.concatenate([hs, jnp.zeros((TILE - hs.shape[0], D_MODEL), F32)], axis=0)
    return jnp.where(i < n_prompt_tiles, hp_ref[...], hs_tile)


COMBINE_SLOTS = 3


def _combine_kernel(seg_len_ref, seg_start_ref, seg_off_ref, tile_groups_ref,
                    hp_ref, hs_ref, dest_ref, gate_ref, g_ref, b_ref, yb_ref,
                    yp_ref, ys_ref, ybuf, sem, *, n_prompt_tiles):
    i = pl.program_id(0)
    n = pl.num_programs(0)
    slot = i % COMBINE_SLOTS

    def start_gather(tile):
        s = tile % COMBINE_SLOTS

        def segment(e, carry):
            idx = tile * N_EXPERTS + e
            groups = seg_len_ref[idx]

            @pl.when(groups > 0)
            def _():
                pltpu.make_async_copy(yb_ref.at[pl.ds(seg_off_ref[idx], groups)],
                                      ybuf.at[s, pl.ds(seg_start_ref[idx], groups)], sem.at[s]).start()
            return carry
        lax.fori_loop(0, N_EXPERTS, segment, 0)

    @pl.when(i == 0)
    def _():
        ybuf[...] = jnp.zeros(ybuf.shape, F32)
        for ahead in range(COMBINE_SLOTS - 1):
            @pl.when(ahead < n)
            def _():
                start_gather(i + ahead)

    @pl.when(i + COMBINE_SLOTS - 1 < n)
    def _():
        start_gather(i + COMBINE_SLOTS - 1)

    total = tile_groups_ref[i]

    @pl.when(total > 0)
    def _():
        pltpu.make_async_copy(yb_ref.at[pl.ds(0, total)], ybuf.at[slot, pl.ds(0, total)], sem.at[slot]).wait()

    dest = dest_ref[0]
    gate = gate_ref[0]
    rid = lax.broadcasted_iota(jnp.int32, (TILE_ROWS, TILE), 0)
    weights = jnp.zeros((TILE_ROWS, TILE), F32)
    for k in range(TOP_K):
        weights = jnp.where(rid == dest[k:k + 1, :], gate[k:k + 1, :], weights)
    rows = ybuf[slot].reshape(TILE_ROWS, D_MODEL).astype(BF16)
    ff = lax.dot_general(weights.astype(BF16), rows, (((0,), (0,)), ((), ())),
                         preferred_element_type=F32)
    h1 = _tile_tokens(i, n_prompt_tiles, hp_ref, hs_ref)
    y = _layer_norm(DN_ALPHA * h1 + ff, g_ref[...], b_ref[...])

    @pl.when(i < n_prompt_tiles)
    def _():
        yp_ref[...] = y

    @pl.when(i >= n_prompt_tiles)
    def _():
        ys_ref[...] = y[:ys_ref.shape[0]]


def _combine(segs, tile_groups, n_seq, h1p, h1s, dest, gate, ln_g, ln_b, yb):
    n_tiles = dest.shape[0]
    n_prompt_tiles = h1p.shape[0] // TILE
    blocks_per_seq = n_prompt_tiles // n_seq

    def prompt_idx(i, *_):
        t = jnp.minimum(i, n_prompt_tiles - 1)
        return ((t % n_seq) * blocks_per_seq + t // n_seq, 0)
    in_specs = [
        pl.BlockSpec((TILE, D_MODEL), prompt_idx),
        _const_spec(h1s.shape),
        pl.BlockSpec((1, TOP_K, TILE), lambda i, *_: (i, 0, 0)),
        pl.BlockSpec((1, TOP_K, TILE), lambda i, *_: (i, 0, 0)),
        _const_spec(ln_g.shape),
        _const_spec(ln_b.shape),
        pl.BlockSpec(memory_space=pl.ANY),
    ]
    out_shape = (jax.ShapeDtypeStruct(h1p.shape, F32), jax.ShapeDtypeStruct(h1s.shape, F32))
    out_specs = (pl.BlockSpec((TILE, D_MODEL), prompt_idx), _const_spec(h1s.shape))
    return pl.pallas_call(
        functools.partial(_combine_kernel, n_prompt_tiles=n_prompt_tiles),
        grid_spec=pltpu.PrefetchScalarGridSpec(
            num_scalar_prefetch=len(segs) + 1, grid=(n_tiles,), in_specs=in_specs, out_specs=out_specs,
            scratch_shapes=[pltpu.VMEM((COMBINE_SLOTS,) + _grouped(TILE_ROWS), F32),
                            pltpu.SemaphoreType.DMA((COMBINE_SLOTS,))]),
        out_shape=out_shape,
        compiler_params=pltpu.CompilerParams(
            dimension_semantics=("arbitrary",), vmem_limit_bytes=VMEM_LIMIT),
        name="moe_combine",
    )(*segs, tile_groups, h1p, h1s, dest, gate, ln_g, ln_b, yb.reshape(_grouped(yb.shape[0])))


def _moe_layout(cnt, n_prompt_tiles):
    n_tiles = cnt.shape[0]
    i32 = lambda a: a.astype(jnp.int32)
    seg_len = (cnt + SEG_ALIGN - 1) // SEG_ALIGN
    seg_start = jnp.cumsum(seg_len, axis=1) - seg_len
    exp_len = jnp.sum(seg_len, axis=0)
    exp_blocks = (exp_len + BLOCK_GROUPS - 1) // BLOCK_GROUPS
    blocks_end = jnp.cumsum(exp_blocks)
    first_block = blocks_end - exp_blocks
    seg_off = first_block[None, :] * BLOCK_GROUPS + jnp.cumsum(seg_len, axis=0) - seg_len
    max_rows = n_tiles * (TOP_K * TILE + N_EXPERTS * (SEG_ALIGN - 1)) + N_EXPERTS * (ROW_BLOCK - SEG_ALIGN)
    n_blocks = -(-max_rows // ROW_BLOCK)
    n_used = blocks_end[-1]
    blk = jnp.minimum(jnp.arange(n_blocks, dtype=jnp.int32), n_used - 1)
    eid = jnp.arange(N_EXPERTS, dtype=jnp.int32)
    block_expert = jnp.minimum(jnp.sum(blocks_end[None, :] <= blk[:, None], axis=1), N_EXPERTS - 1)
    own = block_expert[:, None] == eid[None, :]
    pick = lambda per_expert: jnp.sum(jnp.where(own, per_expert[None, :], 0), axis=1)
    later_used = (eid[None, :] > eid[:, None]) & (exp_blocks[None, :] > 0)
    next_of = jnp.min(jnp.where(later_used, eid[None, :], N_EXPERTS), axis=1)
    next_expert = pick(jnp.where(next_of < N_EXPERTS, next_of, -1))
    lo = blk * BLOCK_GROUPS
    off_b = jnp.sum(jnp.where(own[:, None, :], seg_off[None, :, :], 0), axis=2)
    len_b = jnp.sum(jnp.where(own[:, None, :], seg_len[None, :, :], 0), axis=2)
    tile_lo = jnp.sum(off_b + len_b <= lo[:, None], axis=1)
    tile_hi = jnp.sum(off_b < lo[:, None] + BLOCK_GROUPS, axis=1) - 1
    valid = jnp.clip(pick(first_block * BLOCK_GROUPS + exp_len) - lo, 0, BLOCK_GROUPS)
    flat = lambda a: i32(a.reshape(-1))
    sched = (i32(block_expert), i32(next_expert), i32(n_used.reshape(1)), i32(tile_lo), i32(tile_hi),
             i32(valid))
    segs = (flat(seg_len), flat(seg_start), flat(seg_off))
    tile = jnp.arange(n_tiles, dtype=seg_off.dtype)
    tile_in_array = jnp.where(tile < n_prompt_tiles, tile, tile - n_prompt_tiles)
    seg_src = tile_in_array[:, None] * TILE_GROUPS + seg_start - seg_off
    pad1 = lambda a: jnp.concatenate([flat(a.T), jnp.zeros((1,), jnp.int32)])
    segs_by_expert = (pad1(seg_len), pad1(seg_src), pad1(seg_off))
    return sched, segs, segs_by_expert, i32(jnp.sum(seg_len, axis=1)), n_blocks * ROW_BLOCK


def _prep_weights(w_in, b_in, conv_w, conv_b, lru_w_a, lru_b_a, lru_w_i, lru_b_i, lru_lambda,
                  w_out, b_out, ln1_g, ln1_b, router_w, router_b):
    def regroup_heads(a):
        rest = a.shape[1:]
        q = a[:Q_COLS].reshape(N_KV_HEADS, HEAD_GROUPS, HEAD_DIM, *rest)
        q = jnp.swapaxes(q, 0, 1).reshape(Q_COLS, *rest)
        return jnp.concatenate([q, a[Q_COLS:]], axis=0)

    def diag_tiles(w):
        per = MXU_DIM // (LRU_WIDTH // LRU_BLOCKS)
        w4 = w.reshape(LRU_BLOCKS // per, per, LRU_WIDTH // LRU_BLOCKS, LRU_WIDTH // LRU_BLOCKS)
        t = jnp.einsum("taij,ab->taibj", w4, jnp.eye(per, dtype=w.dtype))
        return t.reshape(LRU_BLOCKS // per, MXU_DIM, MXU_DIM).astype(BF16)

    return (
        regroup_heads(w_in[0].T).T.astype(BF16), regroup_heads(b_in[0])[None],
        conv_w[0], conv_b[0][None],
        diag_tiles(lru_w_a[0]), lru_b_a[0].reshape(1, LRU_WIDTH),
        diag_tiles(lru_w_i[0]), lru_b_i[0].reshape(1, LRU_WIDTH),
        lru_lambda[0][None],
        regroup_heads(w_out[0]).astype(BF16), b_out[0][None],
        ln1_g[0][None], ln1_b[0][None],
        router_w[0].T.astype(BF16), router_b[0][:, None],
    )


def kernel(x_prompt, x_sample, cache_win_k, cache_win_v, state_lru_h, state_conv, w_in, b_in, attn_sinks, conv_w, conv_b, lru_w_a, lru_b_a, lru_w_i, lru_b_i, lru_lambda, w_out, b_out, ln1_g, ln1_b, router_w, router_b, w_gate_up, b_gate_up, w_down, b_down, ln2_g, ln2_b):
    assert w_in.shape[0] == 1, "single-layer step"
    B, S, _ = x_prompt.shape
    nseq = x_sample.shape[0]
    assert x_sample.shape[1] == 1 and S % TILE == 0
    wts = _prep_weights(w_in, b_in, conv_w, conv_b, lru_w_a, lru_b_a, lru_w_i, lru_b_i, lru_lambda,
                        w_out, b_out, ln1_g, ln1_b, router_w, router_b)
    sinks = attn_sinks[0]

    h1p, xt_p, dest_p, tg_p, cnt_p, pk, pv, ph, pc = _mixer_prompt(x_prompt, sinks, wts)
    h1s, xt_s, dest_s, tg_s, cnt_s, sk, sv, sh, sc = _mixer_sample(
        x_sample.reshape(nseq, D_MODEL),
        cache_win_k[0].reshape(nseq, WINDOW, KV_COLS), cache_win_v[0].reshape(nseq, WINDOW, KV_COLS),
        state_lru_h[0], jnp.transpose(state_conv[0], (1, 0, 2)), sinks, wts)

    per_tile = lambda a: a.reshape(-1, *a.shape[2:])
    dest = jnp.concatenate([per_tile(dest_p), dest_s], axis=0)
    tg = jnp.concatenate([per_tile(tg_p), tg_s], axis=0)
    cnt = jnp.concatenate([per_tile(cnt_p), cnt_s], axis=0)[:, :, 0]
    sched, segs, segs_by_expert, tile_groups, n_rows = _moe_layout(cnt, xt_p.shape[0])

    end_to_end = lambda a: a.reshape(-1, SEG_ALIGN, D_MODEL)
    yb = _experts(sched, segs_by_expert, end_to_end(xt_p), end_to_end(xt_s),
                  w_gate_up[0], b_gate_up[0], w_down[0], b_down[0], n_rows)
    yp, ys = _combine(segs, tile_groups, B, h1p.reshape(B * S, D_MODEL), h1s, dest, tg,
                      ln2_g[0][None], ln2_b[0][None], yb)

    kv_shape = (N_KV_HEADS, HEAD_DIM)
    return (
        yp.reshape(B, S, D_MODEL), ys.reshape(nseq, 1, D_MODEL),
        pk.reshape(1, B, WINDOW, *kv_shape), pv.reshape(1, B, WINDOW, *kv_shape),
        ph.reshape(1, B, LRU_WIDTH), pc[None],
        sk.reshape(1, nseq, WINDOW, *kv_shape), sv.reshape(1, nseq, WINDOW, *kv_shape),
        sh[None], jnp.transpose(sc, (1, 0, 2))[None],
    )
```

```python
import functools

import jax
import jax.numpy as jnp
import numpy as np
from jax import lax
from jax.experimental import pallas as pl
from jax.experimental.pallas import tpu as pltpu

F32 = jnp.float32
BF16 = jnp.bfloat16

D_MODEL = 1024
N_Q_HEADS = 8
N_KV_HEADS = 2
HEAD_DIM = 64
WINDOW = 128
Q_COLS = N_Q_HEADS * HEAD_DIM
KV_COLS = N_KV_HEADS * HEAD_DIM
LRU_WIDTH = 512
LRU_BLOCKS = 8
CONV_WIDTH = 4
RG_C = 8.0
N_EXPERTS = 32
TOP_K = 4
D_FF = 1024
SWIGLU_LIMIT = 7.0
SWIGLU_ALPHA = 1.702
LN_EPS = 1e-5
DN_ALPHA = 2.0 ** 0.25
NEG_INF = -1e30
LOG2E = 1.4426950408889634
D_IN = Q_COLS + 2 * KV_COLS + 2 * LRU_WIDTH

LANES = 128
SUBLANES = 8
MXU_DIM = 256

TILE = 256
SEG_ALIGN = SUBLANES
ROW_BLOCK = 1024
HEAD_GROUPS = N_Q_HEADS // N_KV_HEADS
VMEM_LIMIT = 56 * 1024 * 1024

TILE_ROWS = 1280
assert TILE_ROWS >= TOP_K * TILE + N_EXPERTS * (SEG_ALIGN - 1) and TILE_ROWS % LANES == 0


def _layer_norm(z, g, b):
    mu = jnp.mean(z, axis=-1, keepdims=True)
    zc = z - mu
    var = jnp.mean(zc * zc, axis=-1, keepdims=True)
    return zc * lax.rsqrt(var + LN_EPS) * g + b


def _sigmoid(x):
    return 0.5 + 0.5 * jnp.tanh(0.5 * x)


def _softplus(x):
    return jnp.maximum(x, 0.0) + jnp.log1p(jnp.exp(-jnp.abs(x)))


def _gelu_tanh(x):
    c = np.float32(np.sqrt(2.0 / np.pi))
    return 0.5 * x * (1.0 + jnp.tanh(c * (x + 0.044715 * (x * x * x))))


def _rglru_gates(xc, wa_ref, ba_ref, wi_ref, bi_ref, lam_ref):
    xcb = xc.astype(BF16)
    half = LRU_WIDTH // 2
    pre_a = jnp.concatenate(
        [jnp.dot(xcb[:, :half], wa_ref[0], preferred_element_type=F32),
         jnp.dot(xcb[:, half:], wa_ref[1], preferred_element_type=F32)], axis=1)
    pre_i = jnp.concatenate(
        [jnp.dot(xcb[:, :half], wi_ref[0], preferred_element_type=F32),
         jnp.dot(xcb[:, half:], wi_ref[1], preferred_element_type=F32)], axis=1)
    r = _sigmoid(pre_a + ba_ref[...])
    gi = _sigmoid(pre_i + bi_ref[...])
    log_a = (-RG_C) * r * _softplus(-lam_ref[...])
    a = jnp.exp(log_a)
    t = jnp.tanh(log_a)
    mult = jnp.sqrt(jnp.maximum(-2.0 * t / (1.0 - t), 0.0))
    return a, mult * (gi * xc)


def _route(h1, rw_ref, rb_ref):
    n = h1.shape[0]
    logits = lax.dot_general(rw_ref[...], h1.astype(BF16), (((1,), (1,)), ((), ())),
                             preferred_element_type=F32) + rb_ref[...]
    eidx = lax.broadcasted_iota(jnp.int32, (N_EXPERTS, n), 0)
    vals = logits
    top_v, top_e = [], []
    for _ in range(TOP_K):
        m = jnp.max(vals, axis=0, keepdims=True)
        idx = jnp.min(jnp.where(vals == m, eidx, N_EXPERTS), axis=0, keepdims=True)
        top_v.append(m)
        top_e.append(idx)
        vals = jnp.where(eidx == idx, -jnp.inf, vals)
    ex = [jnp.exp(v - top_v[0]) for v in top_v]
    den = ex[0] + ex[1] + ex[2] + ex[3]
    gates = jnp.concatenate([e / den for e in ex], axis=0)
    te = jnp.concatenate(top_e, axis=0)
    mem = jnp.zeros((N_EXPERTS, n), jnp.int32)
    for idx in top_e:
        mem = mem + jnp.where(eidx == idx, 1, 0)
    return te, gates, jnp.sum(mem, axis=1, keepdims=True)


def _compact_tile(h1, te, cnt):
    eidx = lax.broadcasted_iota(jnp.int32, (N_EXPERTS, TILE), 0)
    mem = jnp.zeros((N_EXPERTS, TILE), F32)
    for k in range(TOP_K):
        mem = mem + jnp.where(eidx == te[k:k + 1, :], 1.0, 0.0)
    tr = lax.broadcasted_iota(jnp.int32, (TILE, TILE), 0)
    tc = lax.broadcasted_iota(jnp.int32, (TILE, TILE), 1)
    before = jnp.where(tr < tc, 1.0, 0.0).astype(BF16)
    pos = jnp.dot(mem.astype(BF16), before, preferred_element_type=F32)
    seg_rows = ((cnt + (SEG_ALIGN - 1)) // SEG_ALIGN * SEG_ALIGN).astype(F32)
    er = lax.broadcasted_iota(jnp.int32, (N_EXPERTS, N_EXPERTS), 0)
    ec = lax.broadcasted_iota(jnp.int32, (N_EXPERTS, N_EXPERTS), 1)
    earlier = jnp.where(ec < er, 1.0, 0.0).astype(BF16)
    seg_start = jnp.dot(earlier, jnp.broadcast_to(seg_rows, (N_EXPERTS, LANES)).astype(BF16),
                        preferred_element_type=F32)[:, 0:1]
    base = (pos + seg_start).astype(jnp.int32)
    dests = []
    for k in range(TOP_K):
        d = jnp.sum(jnp.where(eidx == te[k:k + 1, :], base, 0), axis=0, keepdims=True)
        dests.append(jnp.where(te[k:k + 1, :] >= 0, d, -1))
    rid = lax.broadcasted_iota(jnp.int32, (TILE_ROWS, TILE), 0)
    onehot = jnp.zeros((TILE_ROWS, TILE), F32)
    for k in range(TOP_K):
        onehot = jnp.where(rid == dests[k], 1.0, onehot)
    picked = jnp.dot(onehot.astype(BF16), h1.astype(BF16), preferred_element_type=F32)
    return picked, jnp.concatenate(dests, axis=0)


def _post_mix(x, mix_in, w_out_ref, b_out_ref, g_ref, b_ref):
    mix = jnp.dot(mix_in.astype(BF16), w_out_ref[...], preferred_element_type=F32) + b_out_ref[...]
    return _layer_norm(DN_ALPHA * x + mix, g_ref[...], b_ref[...])


SEQ_PAIR = 2


def _interleave(chains):
    results = [None] * len(chains)
    live = list(range(len(chains)))
    while live:
        for k in list(live):
            try:
                next(chains[k])
            except StopIteration as stop:
                results[k] = stop.value
                live.remove(k)
    return results


def _bias_tables(bias_s):
    row2 = lax.broadcasted_iota(jnp.int32, (2 * WINDOW, 2 * WINDOW), 0)
    col2 = lax.broadcasted_iota(jnp.int32, (2 * WINDOW, 2 * WINDOW), 1)
    dist = (row2 & (WINDOW - 1)) - col2 + WINDOW
    valid = (dist >= 0) & (dist <= WINDOW)
    distf = dist.astype(F32)
    for g in range(HEAD_GROUPS):
        slope = jnp.where(row2 >= WINDOW, 2.0 ** -(g + HEAD_GROUPS + 1), 2.0 ** -(g + 1))
        bias = jnp.where(valid, (-LOG2E) * (slope * distf), NEG_INF)
        bias_s[g, 0] = bias
        bias_s[g, 1] = jnp.where(col2 >= WINDOW, bias, NEG_INF)


def _mixer_prompt_kernel(sinks_ref, x_ref, *rest):
    wts = rest[:15]
    h1_ref, xt_ref, dest_ref, tg_ref, cnt_ref, wk_ref, wv_ref, lh_ref, cv_ref = rest[15:24]
    kext, vext, uext, hcar, attn_buf, bias_s = rest[24:]
    j = pl.program_id(1)
    nj = pl.num_programs(1)
    R = TILE
    one = lambda ref, bb: ref.at[pl.ds(bb, 1)]

    @pl.when(j == 0)
    def _():
        for bb in range(SEQ_PAIR):
            kext[bb, 0:WINDOW, :] = jnp.zeros((WINDOW, KV_COLS), F32)
            vext[bb, 0:WINDOW, :] = jnp.zeros((WINDOW, KV_COLS), F32)
            uext[bb, 0:SUBLANES, :] = jnp.zeros((SUBLANES, LRU_WIDTH), F32)
            hcar[bb] = jnp.zeros((1, LRU_WIDTH), F32)

    @pl.when((pl.program_id(0) == 0) & (j == 0))
    def _():
        _bias_tables(bias_s)

    per_tile = lambda ref, bb: ref.at[0, pl.ds(bb, 1)]
    h_last = _interleave([
        _mixer_tile(j, sinks_ref, one(x_ref, bb), wts, one(h1_ref, bb), one(xt_ref, bb),
                    per_tile(dest_ref, bb), per_tile(tg_ref, bb), per_tile(cnt_ref, bb),
                    kext.at[bb], vext.at[bb], uext.at[bb], hcar.at[bb], attn_buf.at[bb], bias_s)
        for bb in range(SEQ_PAIR)])

    @pl.when(j == nj - 1)
    def _():
        for bb in range(SEQ_PAIR):
            wk_ref[bb] = kext[bb, R:R + WINDOW, :]
            wv_ref[bb] = vext[bb, R:R + WINDOW, :]
            lh_ref[bb] = h_last[bb]
            cv_ref[bb] = uext[bb, SUBLANES + R - (CONV_WIDTH - 1):SUBLANES + R, :]

    for bb in range(SEQ_PAIR):
        kext[bb, 0:WINDOW, :] = kext[bb, R:R + WINDOW, :]
        vext[bb, 0:WINDOW, :] = vext[bb, R:R + WINDOW, :]
        uext[bb, 0:SUBLANES, :] = uext[bb, R:R + SUBLANES, :]


def _mixer_tile(j, sinks_ref, x_ref, wts, h1_ref, xt_ref, dest_ref, tg_ref, cnt_ref,
                kext, vext, uext, hcar, attn_buf, bias_s):
    (w_in_ref, b_in_ref, conv_w_ref, conv_b_ref, wa_ref, ba_ref, wi_ref, bi_ref, lam_ref,
     w_out_ref, b_out_ref, ln_g_ref, ln_b_ref, rw_ref, rb_ref) = wts
    R = TILE
    x = x_ref[0]
    proj = jnp.dot(x.astype(BF16), w_in_ref[...], preferred_element_type=F32) + b_in_ref[...]
    q = proj[:, :Q_COLS]
    kext[WINDOW:WINDOW + R, :] = proj[:, Q_COLS:Q_COLS + KV_COLS]
    vext[WINDOW:WINDOW + R, :] = proj[:, Q_COLS + KV_COLS:Q_COLS + 2 * KV_COLS]
    uext[SUBLANES:SUBLANES + R, :] = proj[:, Q_COLS + 2 * KV_COLS:Q_COLS + 2 * KV_COLS + LRU_WIDTH]
    ug = proj[:, Q_COLS + 2 * KV_COLS + LRU_WIDTH:]
    yield

    upper = lax.broadcasted_iota(jnp.int32, (2 * WINDOW, 1), 0) >= WINDOW
    lo_lane = lax.broadcasted_iota(jnp.int32, (WINDOW, LANES), 1) < HEAD_DIM
    qs = q * (HEAD_DIM ** -0.5 * LOG2E)
    for s in range(R // WINDOW):
        kk = kext[s * WINDOW:s * WINDOW + 2 * WINDOW, :].astype(BF16)
        vv = vext[s * WINDOW:s * WINDOW + 2 * WINDOW, :].astype(BF16)
        table = jnp.where(j == 0, 1, 0) if s == 0 else 0
        for g in range(HEAD_GROUPS):
            qg = qs[s * WINDOW:(s + 1) * WINDOW, g * LANES:(g + 1) * LANES]
            q2 = jnp.concatenate([jnp.where(lo_lane, qg, 0.0), jnp.where(lo_lane, 0.0, qg)], axis=0)
            sc = lax.dot_general(q2.astype(BF16), kk, (((1,), (1,)), ((), ())),
                                 preferred_element_type=F32) + bias_s[g, table]
            sink = jnp.where(upper, sinks_ref[g + HEAD_GROUPS], sinks_ref[g]) * LOG2E
            m = jnp.maximum(jnp.max(sc, axis=-1, keepdims=True), sink)
            p = jnp.exp2(sc - m)
            den = jnp.sum(p, axis=-1, keepdims=True) + jnp.exp2(sink - m)
            o = jnp.dot(p.astype(BF16), vv, preferred_element_type=F32) / den
            attn_buf[s * WINDOW:(s + 1) * WINDOW, g * LANES:(g + 1) * LANES] = jnp.where(
                lo_lane, o[:WINDOW], o[WINDOW:])
            yield

    xc = conv_b_ref[...]
    for tap in range(CONV_WIDTH):
        off = SUBLANES - (CONV_WIDTH - 1) + tap
        xc = xc + uext[off:off + R, :] * conv_w_ref[tap:tap + 1, :]
    a, bv = _rglru_gates(xc, wa_ref, ba_ref, wi_ref, bi_ref, lam_ref)
    yield
    groups = R // SUBLANES
    a3 = a.reshape(groups, SUBLANES, LRU_WIDTH)
    b3 = bv.reshape(groups, SUBLANES, LRU_WIDTH)
    sub = lax.broadcasted_iota(jnp.int32, (groups, SUBLANES, LRU_WIDTH), 1)
    d = 1
    while d < SUBLANES:
        keep = sub >= d
        a_prev = jnp.where(keep, pltpu.roll(a3, d, 1), 1.0)
        b_prev = jnp.where(keep, pltpu.roll(b3, d, 1), 0.0)
        b3 = a3 * b_prev + b3
        a3 = a3 * a_prev
        d *= 2
    h_prev = hcar[...]
    h_groups = []
    for c in range(groups):
        hc = a3[c] * h_prev + b3[c]
        h_groups.append(hc)
        h_prev = hc[SUBLANES - 1:SUBLANES, :]
    h = jnp.concatenate(h_groups, axis=0)
    hcar[...] = h_prev
    rnn = h * _gelu_tanh(ug)
    yield

    mix_in = jnp.concatenate([attn_buf[...], rnn], axis=1)
    h1 = _post_mix(x, mix_in, w_out_ref, b_out_ref, ln_g_ref, ln_b_ref)
    h1_ref[0] = h1
    yield
    te, tg, cnt = _route(h1, rw_ref, rb_ref)
    tg_ref[0] = tg
    cnt_ref[0] = cnt
    yield
    picked, dest = _compact_tile(h1, te, cnt)
    xt_ref[0] = picked
    dest_ref[0] = dest
    return h_prev


def _const_spec(shape):
    return pl.BlockSpec(shape, lambda *_: (0,) * len(shape))


def _mixer_prompt(x, sinks, wts):
    B, S, _ = x.shape
    assert B % SEQ_PAIR == 0 and S % TILE == 0
    nj = S // TILE
    P = SEQ_PAIR
    tile_idx = lambda b, j, *_: (j, b, 0, 0)
    batch_idx = lambda b, j, *_: (b, 0, 0)
    in_specs = [pl.BlockSpec((P, TILE, D_MODEL), lambda b, j, *_: (b, j, 0))]
    in_specs += [_const_spec(w.shape) for w in wts]
    out_shape = (
        jax.ShapeDtypeStruct((B, S, D_MODEL), F32),
        jax.ShapeDtypeStruct((B * nj, TILE_ROWS, D_MODEL), F32),
        jax.ShapeDtypeStruct((nj, B, TOP_K, TILE), jnp.int32),
        jax.ShapeDtypeStruct((nj, B, TOP_K, TILE), F32),
        jax.ShapeDtypeStruct((nj, B, N_EXPERTS, 1), jnp.int32),
        jax.ShapeDtypeStruct((B, WINDOW, KV_COLS), F32),
        jax.ShapeDtypeStruct((B, WINDOW, KV_COLS), F32),
        jax.ShapeDtypeStruct((B, 1, LRU_WIDTH), F32),
        jax.ShapeDtypeStruct((B, CONV_WIDTH - 1, LRU_WIDTH), F32),
    )
    out_specs = (
        pl.BlockSpec((P, TILE, D_MODEL), lambda b, j, *_: (b, j, 0)),
        pl.BlockSpec((P, TILE_ROWS, D_MODEL), lambda b, j, *_: (j * (B // P) + b, 0, 0)),
        pl.BlockSpec((1, P, TOP_K, TILE), tile_idx),
        pl.BlockSpec((1, P, TOP_K, TILE), tile_idx),
        pl.BlockSpec((1, P, N_EXPERTS, 1), tile_idx),
        pl.BlockSpec((P, WINDOW, KV_COLS), batch_idx),
        pl.BlockSpec((P, WINDOW, KV_COLS), batch_idx),
        pl.BlockSpec((P, 1, LRU_WIDTH), batch_idx),
        pl.BlockSpec((P, CONV_WIDTH - 1, LRU_WIDTH), batch_idx),
    )
    scratch = [
        pltpu.VMEM((P, TILE + WINDOW, KV_COLS), F32),
        pltpu.VMEM((P, TILE + WINDOW, KV_COLS), F32),
        pltpu.VMEM((P, TILE + SUBLANES, LRU_WIDTH), F32),
        pltpu.VMEM((P, 1, LRU_WIDTH), F32),
        pltpu.VMEM((P, TILE, Q_COLS), F32),
        pltpu.VMEM((HEAD_GROUPS, 2, 2 * WINDOW, 2 * WINDOW), F32),
    ]
    return pl.pallas_call(
        _mixer_prompt_kernel,
        grid_spec=pltpu.PrefetchScalarGridSpec(
            num_scalar_prefetch=1, grid=(B // P, nj), in_specs=in_specs, out_specs=out_specs,
            scratch_shapes=scratch),
        out_shape=out_shape,
        compiler_params=pltpu.CompilerParams(
            dimension_semantics=("arbitrary", "arbitrary"), vmem_limit_bytes=VMEM_LIMIT),
        name="mixer_prompt",
    )(sinks, x, *wts)


SEQ_CHUNK = 16


def _mixer_sample_kernel(sinks_ref, x_ref, ck_ref, cv_ref, h0_ref, cprev_ref,
                         w_in_ref, b_in_ref, conv_w_ref, conv_b_ref,
                         wa_ref, ba_ref, wi_ref, bi_ref, lam_ref, w_out_ref, b_out_ref,
                         ln_g_ref, ln_b_ref, rw_ref, rb_ref,
                         h1_ref, xt_ref, dest_ref, tg_ref, cnt_ref, wk_ref, wv_ref, lh_ref, cnew_ref,
                         proj_s, attn_s, attn_c):
    c = pl.program_id(0)
    nc = pl.num_programs(0)
    nseq = x_ref.shape[0]

    @pl.when(c == 0)
    def _():
        proj_s[...] = jnp.dot(x_ref[...].astype(BF16), w_in_ref[...],
                              preferred_element_type=F32) + b_in_ref[...]

    sub = lax.broadcasted_iota(jnp.int32, (N_Q_HEADS, LANES), 0)
    lane = lax.broadcasted_iota(jnp.int32, (N_Q_HEADS, LANES), 1)
    own_half = (lane < HEAD_DIM) == (sub < HEAD_GROUPS)
    sub1 = sub[:, 0:1]
    slope = jnp.zeros((N_Q_HEADS, 1), F32)
    sink = jnp.zeros((N_Q_HEADS, 1), F32)
    for hd in range(N_Q_HEADS):
        slope = jnp.where(sub1 == hd, 2.0 ** -(hd + 1), slope)
        sink = jnp.where(sub1 == hd, sinks_ref[hd], sink)
    dist = (WINDOW - lax.broadcasted_iota(jnp.int32, (1, WINDOW), 1)).astype(F32)
    lo_row = lax.broadcasted_iota(jnp.int32, (1, LANES), 1) < HEAD_DIM
    scale = HEAD_DIM ** -0.5

    def one_sequence(i):
        b = c * SEQ_CHUNK + i
        prow = proj_s[pl.ds(b, 1), :]
        q8 = jnp.zeros((N_Q_HEADS, LANES), F32)
        for g in range(HEAD_GROUPS):
            qg = jnp.broadcast_to(prow[:, g * LANES:(g + 1) * LANES], (N_Q_HEADS, LANES))
            q8 = jnp.where(((sub & (HEAD_GROUPS - 1)) == g) & own_half, qg, q8)
        k_new = prow[:, Q_COLS:Q_COLS + KV_COLS]
        v_new = prow[:, Q_COLS + KV_COLS:Q_COLS + 2 * KV_COLS]
        kb = ck_ref[i]
        vb = cv_ref[i]
        sc = lax.dot_general(q8.astype(BF16), kb.astype(BF16), (((1,), (1,)), ((), ())),
                             preferred_element_type=F32) * scale - slope * dist
        sc_new = jnp.sum(q8 * k_new, axis=-1, keepdims=True) * scale
        yield
        m = jnp.maximum(jnp.maximum(jnp.max(sc, axis=-1, keepdims=True), sc_new), sink)
        p = jnp.exp(sc - m)
        p_new = jnp.exp(sc_new - m)
        den = jnp.sum(p, axis=-1, keepdims=True) + p_new + jnp.exp(sink - m)
        yield
        o = (jnp.dot(p.astype(BF16), vb.astype(BF16), preferred_element_type=F32)
             + p_new * v_new) / den
        yield
        for g in range(HEAD_GROUPS):
            attn_c[i:i + 1, g * LANES:(g + 1) * LANES] = jnp.where(
                lo_row, o[g:g + 1, :], o[g + HEAD_GROUPS:g + HEAD_GROUPS + 1, :])
        wk_ref[i, 0:WINDOW - 1, :] = ck_ref[i, 1:WINDOW, :]
        wk_ref[i, WINDOW - 1:WINDOW, :] = k_new
        wv_ref[i, 0:WINDOW - 1, :] = cv_ref[i, 1:WINDOW, :]
        wv_ref[i, WINDOW - 1:WINDOW, :] = v_new

    _interleave([one_sequence(i) for i in range(SEQ_CHUNK)])
    attn_s[pl.ds(pl.multiple_of(c * SEQ_CHUNK, SEQ_CHUNK), SEQ_CHUNK), :] = attn_c[...]

    @pl.when(c == nc - 1)
    def _():
        x = x_ref[...]
        ux = proj_s[:, Q_COLS + 2 * KV_COLS:Q_COLS + 2 * KV_COLS + LRU_WIDTH]
        ug = proj_s[:, Q_COLS + 2 * KV_COLS + LRU_WIDTH:]
        xc = conv_b_ref[...]
        for tap in range(CONV_WIDTH - 1):
            xc = xc + cprev_ref[tap] * conv_w_ref[tap:tap + 1, :]
        xc = xc + ux * conv_w_ref[CONV_WIDTH - 1:CONV_WIDTH, :]
        a, bv = _rglru_gates(xc, wa_ref, ba_ref, wi_ref, bi_ref, lam_ref)
        h = a * h0_ref[...] + bv
        rnn = h * _gelu_tanh(ug)
        mix_in = jnp.concatenate([attn_s[...], rnn], axis=1)
        h1 = _post_mix(x, mix_in, w_out_ref, b_out_ref, ln_g_ref, ln_b_ref)
        h1_ref[...] = h1
        te, tg, cnt = _route(h1, rw_ref, rb_ref)
        cnt_ref[0] = cnt
        tg_ref[0] = jnp.concatenate([tg, jnp.zeros((TOP_K, TILE - nseq), F32)], axis=1)
        te_tile = jnp.concatenate([te, jnp.full((TOP_K, TILE - nseq), -1, jnp.int32)], axis=1)
        h1_tile = jnp.concatenate([h1, jnp.zeros((TILE - nseq, D_MODEL), F32)], axis=0)
        picked, dest = _compact_tile(h1_tile, te_tile, cnt)
        xt_ref[0] = picked
        dest_ref[0] = dest
        lh_ref[...] = h
        for tap in range(1, CONV_WIDTH - 1):
            cnew_ref[tap - 1] = cprev_ref[tap]
        cnew_ref[CONV_WIDTH - 2] = ux


def _mixer_sample(x, ck, cv, h0, cprev, sinks, wts):
    nseq = x.shape[0]
    assert nseq % SEQ_CHUNK == 0 and nseq <= TILE and nseq % LANES == 0
    nc = nseq // SEQ_CHUNK
    chunk_idx = lambda c, *_: (c, 0, 0)
    in_specs = [
        _const_spec((nseq, D_MODEL)),
        pl.BlockSpec((SEQ_CHUNK, WINDOW, KV_COLS), chunk_idx),
        pl.BlockSpec((SEQ_CHUNK, WINDOW, KV_COLS), chunk_idx),
        _const_spec((nseq, LRU_WIDTH)),
        _const_spec((CONV_WIDTH - 1, nseq, LRU_WIDTH)),
    ] + [_const_spec(w.shape) for w in wts]
    out_shape = (
        jax.ShapeDtypeStruct((nseq, D_MODEL), F32),
        jax.ShapeDtypeStruct((1, TILE_ROWS, D_MODEL), F32),
        jax.ShapeDtypeStruct((1, TOP_K, TILE), jnp.int32),
        jax.ShapeDtypeStruct((1, TOP_K, TILE), F32),
        jax.ShapeDtypeStruct((1, N_EXPERTS, 1), jnp.int32),
        jax.ShapeDtypeStruct((nseq, WINDOW, KV_COLS), F32),
        jax.ShapeDtypeStruct((nseq, WINDOW, KV_COLS), F32),
        jax.ShapeDtypeStruct((nseq, LRU_WIDTH), F32),
        jax.ShapeDtypeStruct((CONV_WIDTH - 1, nseq, LRU_WIDTH), F32),
    )
    out_specs = (
        _const_spec((nseq, D_MODEL)),
        _const_spec((1, TILE_ROWS, D_MODEL)),
        _const_spec((1, TOP_K, TILE)),
        _const_spec((1, TOP_K, TILE)),
        _const_spec((1, N_EXPERTS, 1)),
        pl.BlockSpec((SEQ_CHUNK, WINDOW, KV_COLS), chunk_idx),
        pl.BlockSpec((SEQ_CHUNK, WINDOW, KV_COLS), chunk_idx),
        _const_spec((nseq, LRU_WIDTH)),
        _const_spec((CONV_WIDTH - 1, nseq, LRU_WIDTH)),
    )
    scratch = [pltpu.VMEM((nseq, D_IN), F32), pltpu.VMEM((nseq, Q_COLS), F32),
               pltpu.VMEM((SEQ_CHUNK, Q_COLS), F32)]
    return pl.pallas_call(
        _mixer_sample_kernel,
        grid_spec=pltpu.PrefetchScalarGridSpec(
            num_scalar_prefetch=1, grid=(nc,), in_specs=in_specs, out_specs=out_specs,
            scratch_shapes=scratch),
        out_shape=out_shape,
        compiler_params=pltpu.CompilerParams(
            dimension_semantics=("arbitrary",), vmem_limit_bytes=VMEM_LIMIT),
        name="mixer_sample",
    )(sinks, x, ck, cv, h0, cprev, *wts)


def _grouped(rows):
    assert rows % SEG_ALIGN == 0
    return (rows // SEG_ALIGN, SEG_ALIGN, D_MODEL)


BLOCK_GROUPS = ROW_BLOCK // SEG_ALIGN
TILE_GROUPS = TILE_ROWS // SEG_ALIGN


def _experts_kernel(be_ref, nxt_ref, nb_ref, tlo_ref, thi_ref, valid_ref,
                    seg_len_ref, seg_src_ref, seg_off_ref,
                    xtp_ref, xts_ref, wgu_ref, bgu_ref, wdn_ref, bdn_ref, yb_ref,
                    xbuf, wgu_f, wdn_f, wgu_s, wdn_s, wsem, xsem, *, n_tiles, n_prompt_tiles):
    b = pl.program_id(0)
    slot = b % 2

    def weight_copies(e):
        return (pltpu.make_async_copy(wgu_ref.at[e], wgu_f, wsem.at[0]),
                pltpu.make_async_copy(wdn_ref.at[e], wdn_f, wsem.at[1]))

    def start_gather(blk, s):
        lo_b = blk * BLOCK_GROUPS
        first = be_ref[blk] * n_tiles

        t_lo = tlo_ref[blk]
        t_hi = thi_ref[blk]

        def piece(t):
            off = seg_off_ref[first + t]
            lo = jnp.maximum(off, lo_b)
            n = jnp.minimum(off + seg_len_ref[first + t], lo_b + BLOCK_GROUPS) - lo

            src = pl.ds(seg_src_ref[first + t] + lo, n)
            dst = xbuf.at[s, pl.ds(lo - lo_b, n)]
            wanted = (n > 0) & (t <= t_hi)

            @pl.when(wanted & (t < n_prompt_tiles))
            def _():
                pltpu.make_async_copy(xtp_ref.at[src], dst, xsem.at[s]).start()

            @pl.when(wanted & (t >= n_prompt_tiles))
            def _():
                pltpu.make_async_copy(xts_ref.at[src], dst, xsem.at[s]).start()

        def two_pieces(i, carry):
            piece(t_lo + 2 * i)
            piece(t_lo + 2 * i + 1)
            return carry

        lax.fori_loop(0, (t_hi - t_lo + 2) // 2, two_pieces, 0)

    @pl.when(b == 0)
    def _():
        xbuf[...] = jnp.zeros(xbuf.shape, F32)
        for cp in weight_copies(be_ref[0]):
            cp.start()
        start_gather(b, slot)

    @pl.when(b + 1 < nb_ref[0])
    def _():
        start_gather(b + 1, 1 - slot)

    @pl.when(b < nb_ref[0])
    def _():
        n = valid_ref[b]
        pltpu.make_async_copy(xtp_ref.at[pl.ds(0, n)], xbuf.at[slot, pl.ds(0, n)], xsem.at[slot]).wait()

        @pl.when((b == 0) | (be_ref[b] != be_ref[jnp.maximum(b - 1, 0)]))
        def _():
            for cp in weight_copies(be_ref[b]):
                cp.wait()
            wgu_s[...] = wgu_f[...].astype(BF16)
            wdn_s[...] = wdn_f[...].astype(BF16)

            @pl.when(nxt_ref[b] >= 0)
            def _():
                for cp in weight_copies(nxt_ref[b]):
                    cp.start()

        def expert_rows(rows):
            x = xbuf[slot, pl.ds(0, rows // SEG_ALIGN)].reshape(rows, D_MODEL).astype(BF16)
            hgu = jnp.dot(x, wgu_s[...], preferred_element_type=F32) + bgu_ref[0]
            glu = jnp.minimum(hgu[:, :D_FF], SWIGLU_LIMIT)
            lin = jnp.clip(hgu[:, D_FF:], -SWIGLU_LIMIT, SWIGLU_LIMIT)
            act = glu * jax.nn.sigmoid(SWIGLU_ALPHA * glu) * (lin + 1.0)
            yb_ref[0:rows, :] = jnp.dot(act.astype(BF16), wdn_s[...], preferred_element_type=F32) + bdn_ref[0]

        quarter = ROW_BLOCK // 4
        quarters = (n * SEG_ALIGN + quarter - 1) // quarter
        for used in range(1, 5):
            @pl.when(quarters == used)
            def _(rows=used * quarter):
                expert_rows(rows)
                if rows < ROW_BLOCK:
                    yb_ref[rows:, :] = jnp.zeros((ROW_BLOCK - rows, D_MODEL), F32)

    @pl.when(b >= nb_ref[0])
    def _():
        yb_ref[...] = jnp.zeros(yb_ref.shape, F32)


def _experts(sched, segs, xt_prompt, xt_sample, w_gu, b_gu, w_dn, b_dn, n_rows):
    n_blocks = n_rows // ROW_BLOCK
    n_prompt_tiles = xt_prompt.shape[0] // TILE_GROUPS
    n_tiles = n_prompt_tiles + xt_sample.shape[0] // TILE_GROUPS
    n_prefetch = len(sched) + len(segs)
    exp_idx = lambda b, be, *_: (be[b], 0, 0)
    in_specs = [
        pl.BlockSpec(memory_space=pl.ANY),
        pl.BlockSpec(memory_space=pl.ANY),
        pl.BlockSpec(memory_space=pl.ANY),
        pl.BlockSpec((1, 1, 2 * D_FF), exp_idx),
        pl.BlockSpec(memory_space=pl.ANY),
        pl.BlockSpec((1, 1, D_MODEL), exp_idx),
    ]
    scratch = [pltpu.VMEM((2,) + _grouped(ROW_BLOCK), F32),
               pltpu.VMEM((D_MODEL, 2 * D_FF), F32), pltpu.VMEM((D_FF, D_MODEL), F32),
               pltpu.VMEM((D_MODEL, 2 * D_FF), BF16), pltpu.VMEM((D_FF, D_MODEL), BF16),
               pltpu.SemaphoreType.DMA((2,)), pltpu.SemaphoreType.DMA((2,))]
    return pl.pallas_call(
        functools.partial(_experts_kernel, n_tiles=n_tiles, n_prompt_tiles=n_prompt_tiles),
        grid_spec=pltpu.PrefetchScalarGridSpec(
            num_scalar_prefetch=n_prefetch, grid=(n_blocks,), in_specs=in_specs,
            out_specs=pl.BlockSpec((ROW_BLOCK, D_MODEL), lambda b, *_: (b, 0)),
            scratch_shapes=scratch),
        out_shape=jax.ShapeDtypeStruct((n_rows, D_MODEL), F32),
        compiler_params=pltpu.CompilerParams(
            dimension_semantics=("arbitrary",), vmem_limit_bytes=VMEM_LIMIT),
        name="moe_experts",
    )(*sched, *segs, xt_prompt, xt_sample, w_gu, b_gu[:, None, :], w_dn, b_dn[:, None, :])


def _tile_tokens(i, n_prompt_tiles, hp_ref, hs_ref):
    hs = hs_ref[...]
    hs_tile = jnp.concatenate([hs, jnp.zeros((TILE - hs.shape[0], D_MODEL), F32)], axis=0)
    return jnp.where(i < n_prompt_tiles, hp_ref[...], hs_tile)


COMBINE_SLOTS = 3


def _combine_kernel(seg_len_ref, seg_start_ref, seg_off_ref, tile_groups_ref,
                    hp_ref, hs_ref, dest_ref, gate_ref, g_ref, b_ref, yb_ref,
                    yp_ref, ys_ref, ybuf, sem, *, n_prompt_tiles):
    i = pl.program_id(0)
    n = pl.num_programs(0)
    slot = i % COMBINE_SLOTS

    def start_gather(tile):
        s = tile % COMBINE_SLOTS

        def segment(e, carry):
            idx = tile * N_EXPERTS + e
            groups = seg_len_ref[idx]

            @pl.when(groups > 0)
            def _():
                pltpu.make_async_copy(yb_ref.at[pl.ds(seg_off_ref[idx], groups)],
                                      ybuf.at[s, pl.ds(seg_start_ref[idx], groups)], sem.at[s]).start()
            return carry
        lax.fori_loop(0, N_EXPERTS, segment, 0)

    @pl.when(i == 0)
    def _():
        ybuf[...] = jnp.zeros(ybuf.shape, F32)
        for ahead in range(COMBINE_SLOTS - 1):
            @pl.when(ahead < n)
            def _():
                start_gather(i + ahead)

    @pl.when(i + COMBINE_SLOTS - 1 < n)
    def _():
        start_gather(i + COMBINE_SLOTS - 1)

    total = tile_groups_ref[i]

    @pl.when(total > 0)
    def _():
        pltpu.make_async_copy(yb_ref.at[pl.ds(0, total)], ybuf.at[slot, pl.ds(0, total)], sem.at[slot]).wait()

    dest = dest_ref[0]
    gate = gate_ref[0]
    rid = lax.broadcasted_iota(jnp.int32, (TILE_ROWS, TILE), 0)
    weights = jnp.zeros((TILE_ROWS, TILE), F32)
    for k in range(TOP_K):
        weights = jnp.where(rid == dest[k:k + 1, :], gate[k:k + 1, :], weights)
    rows = ybuf[slot].reshape(TILE_ROWS, D_MODEL).astype(BF16)
    ff = lax.dot_general(weights.astype(BF16), rows, (((0,), (0,)), ((), ())),
                         preferred_element_type=F32)
    h1 = _tile_tokens(i, n_prompt_tiles, hp_ref, hs_ref)
    y = _layer_norm(DN_ALPHA * h1 + ff, g_ref[...], b_ref[...])

    @pl.when(i < n_prompt_tiles)
    def _():
        yp_ref[...] = y

    @pl.when(i >= n_prompt_tiles)
    def _():
        ys_ref[...] = y[:ys_ref.shape[0]]


def _combine(segs, tile_groups, n_seq, h1p, h1s, dest, gate, ln_g, ln_b, yb):
    n_tiles = dest.shape[0]
    n_prompt_tiles = h1p.shape[0] // TILE
    blocks_per_seq = n_prompt_tiles // n_seq

    def prompt_idx(i, *_):
        t = jnp.minimum(i, n_prompt_tiles - 1)
        return ((t % n_seq) * blocks_per_seq + t // n_seq, 0)
    in_specs = [
        pl.BlockSpec((TILE, D_MODEL), prompt_idx),
        _const_spec(h1s.shape),
        pl.BlockSpec((1, TOP_K, TILE), lambda i, *_: (i, 0, 0)),
        pl.BlockSpec((1, TOP_K, TILE), lambda i, *_: (i, 0, 0)),
        _const_spec(ln_g.shape),
        _const_spec(ln_b.shape),
        pl.BlockSpec(memory_space=pl.ANY),
    ]
    out_shape = (jax.ShapeDtypeStruct(h1p.shape, F32), jax.ShapeDtypeStruct(h1s.shape, F32))
    out_specs = (pl.BlockSpec((TILE, D_MODEL), prompt_idx), _const_spec(h1s.shape))
    return pl.pallas_call(
        functools.partial(_combine_kernel, n_prompt_tiles=n_prompt_tiles),
        grid_spec=pltpu.PrefetchScalarGridSpec(
            num_scalar_prefetch=len(segs) + 1, grid=(n_tiles,), in_specs=in_specs, out_specs=out_specs,
            scratch_shapes=[pltpu.VMEM((COMBINE_SLOTS,) + _grouped(TILE_ROWS), F32),
                            pltpu.SemaphoreType.DMA((COMBINE_SLOTS,))]),
        out_shape=out_shape,
        compiler_params=pltpu.CompilerParams(
            dimension_semantics=("arbitrary",), vmem_limit_bytes=VMEM_LIMIT),
        name="moe_combine",
    )(*segs, tile_groups, h1p, h1s, dest, gate, ln_g, ln_b, yb.reshape(_grouped(yb.shape[0])))


def _moe_layout(cnt, n_prompt_tiles):
    n_tiles = cnt.shape[0]
    i32 = lambda a: a.astype(jnp.int32)
    seg_len = (cnt + SEG_ALIGN - 1) // SEG_ALIGN
    seg_start = jnp.cumsum(seg_len, axis=1) - seg_len
    exp_len = jnp.sum(seg_len, axis=0)
    exp_blocks = (exp_len + BLOCK_GROUPS - 1) // BLOCK_GROUPS
    blocks_end = jnp.cumsum(exp_blocks)
    first_block = blocks_end - exp_blocks
    seg_off = first_block[None, :] * BLOCK_GROUPS + jnp.cumsum(seg_len, axis=0) - seg_len
    max_rows = n_tiles * (TOP_K * TILE + N_EXPERTS * (SEG_ALIGN - 1)) + N_EXPERTS * (ROW_BLOCK - SEG_ALIGN)
    n_blocks = -(-max_rows // ROW_BLOCK)
    n_used = blocks_end[-1]
    blk = jnp.minimum(jnp.arange(n_blocks, dtype=jnp.int32), n_used - 1)
    eid = jnp.arange(N_EXPERTS, dtype=jnp.int32)
    block_expert = jnp.minimum(jnp.sum(blocks_end[None, :] <= blk[:, None], axis=1), N_EXPERTS - 1)
    own = block_expert[:, None] == eid[None, :]
    pick = lambda per_expert: jnp.sum(jnp.where(own, per_expert[None, :], 0), axis=1)
    later_used = (eid[None, :] > eid[:, None]) & (exp_blocks[None, :] > 0)
    next_of = jnp.min(jnp.where(later_used, eid[None, :], N_EXPERTS), axis=1)
    next_expert = pick(jnp.where(next_of < N_EXPERTS, next_of, -1))
    lo = blk * BLOCK_GROUPS
    off_b = jnp.sum(jnp.where(own[:, None, :], seg_off[None, :, :], 0), axis=2)
    len_b = jnp.sum(jnp.where(own[:, None, :], seg_len[None, :, :], 0), axis=2)
    tile_lo = jnp.sum(off_b + len_b <= lo[:, None], axis=1)
    tile_hi = jnp.sum(off_b < lo[:, None] + BLOCK_GROUPS, axis=1) - 1
    valid = jnp.clip(pick(first_block * BLOCK_GROUPS + exp_len) - lo, 0, BLOCK_GROUPS)
    flat = lambda a: i32(a.reshape(-1))
    sched = (i32(block_expert), i32(next_expert), i32(n_used.reshape(1)), i32(tile_lo), i32(tile_hi),
             i32(valid))
    segs = (flat(seg_len), flat(seg_start), flat(seg_off))
    tile = jnp.arange(n_tiles, dtype=seg_off.dtype)
    tile_in_array = jnp.where(tile < n_prompt_tiles, tile, tile - n_prompt_tiles)
    seg_src = tile_in_array[:, None] * TILE_GROUPS + seg_start - seg_off
    pad1 = lambda a: jnp.concatenate([flat(a.T), jnp.zeros((1,), jnp.int32)])
    segs_by_expert = (pad1(seg_len), pad1(seg_src), pad1(seg_off))
    return sched, segs, segs_by_expert, i32(jnp.sum(seg_len, axis=1)), n_blocks * ROW_BLOCK


def _prep_weights(w_in, b_in, conv_w, conv_b, lru_w_a, lru_b_a, lru_w_i, lru_b_i, lru_lambda,
                  w_out, b_out, ln1_g, ln1_b, router_w, router_b):
    def regroup_heads(a):
        rest = a.shape[1:]
        q = a[:Q_COLS].reshape(N_KV_HEADS, HEAD_GROUPS, HEAD_DIM, *rest)
        q = jnp.swapaxes(q, 0, 1).reshape(Q_COLS, *rest)
        return jnp.concatenate([q, a[Q_COLS:]], axis=0)

    def diag_tiles(w):
        per = MXU_DIM // (LRU_WIDTH // LRU_BLOCKS)
        w4 = w.reshape(LRU_BLOCKS // per, per, LRU_WIDTH // LRU_BLOCKS, LRU_WIDTH // LRU_BLOCKS)
        t = jnp.einsum("taij,ab->taibj", w4, jnp.eye(per, dtype=w.dtype))
        return t.reshape(LRU_BLOCKS // per, MXU_DIM, MXU_DIM).astype(BF16)

    return (
        regroup_heads(w_in[0].T).T.astype(BF16), regroup_heads(b_in[0])[None],
        conv_w[0], conv_b[0][None],
        diag_tiles(lru_w_a[0]), lru_b_a[0].reshape(1, LRU_WIDTH),
        diag_tiles(lru_w_i[0]), lru_b_i[0].reshape(1, LRU_WIDTH),
        lru_lambda[0][None],
        regroup_heads(w_out[0]).astype(BF16), b_out[0][None],
        ln1_g[0][None], ln1_b[0][None],
        router_w[0].T.astype(BF16), router_b[0][:, None],
    )


def kernel(x_prompt, x_sample, cache_win_k, cache_win_v, state_lru_h, state_conv, w_in, b_in, attn_sinks, conv_w, conv_b, lru_w_a, lru_b_a, lru_w_i, lru_b_i, lru_lambda, w_out, b_out, ln1_g, ln1_b, router_w, router_b, w_gate_up, b_gate_up, w_down, b_down, ln2_g, ln2_b):
    assert w_in.shape[0] == 1, "single-layer step"
    B, S, _ = x_prompt.shape
    nseq = x_sample.shape[0]
    assert x_sample.shape[1] == 1 and S % TILE == 0
    wts = _prep_weights(w_in, b_in, conv_w, conv_b, lru_w_a, lru_b_a, lru_w_i, lru_b_i, lru_lambda,
                        w_out, b_out, ln1_g, ln1_b, router_w, router_b)
    sinks = attn_sinks[0]

    h1p, xt_p, dest_p, tg_p, cnt_p, pk, pv, ph, pc = _mixer_prompt(x_prompt, sinks, wts)
    h1s, xt_s, dest_s, tg_s, cnt_s, sk, sv, sh, sc = _mixer_sample(
        x_sample.reshape(nseq, D_MODEL),
        cache_win_k[0].reshape(nseq, WINDOW, KV_COLS), cache_win_v[0].reshape(nseq, WINDOW, KV_COLS),
        state_lru_h[0], jnp.transpose(state_conv[0], (1, 0, 2)), sinks, wts)

    per_tile = lambda a: a.reshape(-1, *a.shape[2:])
    dest = jnp.concatenate([per_tile(dest_p), dest_s], axis=0)
    tg = jnp.concatenate([per_tile(tg_p), tg_s], axis=0)
    cnt = jnp.concatenate([per_tile(cnt_p), cnt_s], axis=0)[:, :, 0]
    sched, segs, segs_by_expert, tile_groups, n_rows = _moe_layout(cnt, xt_p.shape[0])

    end_to_end = lambda a: a.reshape(-1, SEG_ALIGN, D_MODEL)
    yb = _experts(sched, segs_by_expert, end_to_end(xt_p), end_to_end(xt_s),
                  w_gate_up[0], b_gate_up[0], w_down[0], b_down[0], n_rows)
    yp, ys = _combine(segs, tile_groups, B, h1p.reshape(B * S, D_MODEL), h1s, dest, tg,
                      ln2_g[0][None], ln2_b[0][None], yb)

    kv_shape = (N_KV_HEADS, HEAD_DIM)
    return (
        yp.reshape(B, S, D_MODEL), ys.reshape(nseq, 1, D_MODEL),
        pk.reshape(1, B, WINDOW, *kv_shape), pv.reshape(1, B, WINDOW, *kv_shape),
        ph.reshape(1, B, LRU_WIDTH), pc[None],
        sk.reshape(1, nseq, WINDOW, *kv_shape), sv.reshape(1, nseq, WINDOW, *kv_shape),
        sh[None], jnp.transpose(sc, (1, 0, 2))[None],
    )
```

```python
import functools

import jax
import jax.numpy as jnp
import numpy as np
from jax import lax
from jax.experimental import pallas as pl
from jax.experimental.pallas import tpu as pltpu

F32 = jnp.float32
BF16 = jnp.bfloat16

D_MODEL = 1024
N_Q_HEADS = 8
N_KV_HEADS = 2
HEAD_DIM = 64
WINDOW = 128
Q_COLS = N_Q_HEADS * HEAD_DIM
KV_COLS = N_KV_HEADS * HEAD_DIM
LRU_WIDTH = 512
LRU_BLOCKS = 8
CONV_WIDTH = 4
RG_C = 8.0
N_EXPERTS = 32
TOP_K = 4
D_FF = 1024
SWIGLU_LIMIT = 7.0
SWIGLU_ALPHA = 1.702
LN_EPS = 1e-5
DN_ALPHA = 2.0 ** 0.25
NEG_INF = -1e30
LOG2E = 1.4426950408889634
D_IN = Q_COLS + 2 * KV_COLS + 2 * LRU_WIDTH

LANES = 128
SUBLANES = 8
MXU_DIM = 256

TILE = 256
SEG_ALIGN = SUBLANES
ROW_BLOCK = 1024
HEAD_GROUPS = N_Q_HEADS // N_KV_HEADS
VMEM_LIMIT = 56 * 1024 * 1024

TILE_ROWS = 1280
assert TILE_ROWS >= TOP_K * TILE + N_EXPERTS * (SEG_ALIGN - 1) and TILE_ROWS % LANES == 0


def _layer_norm(z, g, b):
    mu = jnp.mean(z, axis=-1, keepdims=True)
    zc = z - mu
    var = jnp.mean(zc * zc, axis=-1, keepdims=True)
    return zc * lax.rsqrt(var + LN_EPS) * g + b


def _sigmoid(x):
    return 0.5 + 0.5 * jnp.tanh(0.5 * x)


def _softplus(x):
    return jnp.maximum(x, 0.0) + jnp.log1p(jnp.exp(-jnp.abs(x)))


def _gelu_tanh(x):
    c = np.float32(np.sqrt(2.0 / np.pi))
    return 0.5 * x * (1.0 + jnp.tanh(c * (x + 0.044715 * (x * x * x))))


def _rglru_gates(xc, wa_ref, ba_ref, wi_ref, bi_ref, lam_ref):
    xcb = xc.astype(BF16)
    half = LRU_WIDTH // 2
    pre_a = jnp.concatenate(
        [jnp.dot(xcb[:, :half], wa_ref[0], preferred_element_type=F32),
         jnp.dot(xcb[:, half:], wa_ref[1], preferred_element_type=F32)], axis=1)
    pre_i = jnp.concatenate(
        [jnp.dot(xcb[:, :half], wi_ref[0], preferred_element_type=F32),
         jnp.dot(xcb[:, half:], wi_ref[1], preferred_element_type=F32)], axis=1)
    r = _sigmoid(pre_a + ba_ref[...])
    gi = _sigmoid(pre_i + bi_ref[...])
    log_a = (-RG_C) * r * _softplus(-lam_ref[...])
    a = jnp.exp(log_a)
    t = jnp.tanh(log_a)
    mult = jnp.sqrt(jnp.maximum(-2.0 * t / (1.0 - t), 0.0))
    return a, mult * (gi * xc)


def _route(h1, rw_ref, rb_ref):
    n = h1.shape[0]
    logits = lax.dot_general(rw_ref[...], h1.astype(BF16), (((1,), (1,)), ((), ())),
                             preferred_element_type=F32) + rb_ref[...]
    eidx = lax.broadcasted_iota(jnp.int32, (N_EXPERTS, n), 0)
    vals = logits
    top_v, top_e = [], []
    for _ in range(TOP_K):
        m = jnp.max(vals, axis=0, keepdims=True)
        idx = jnp.min(jnp.where(vals == m, eidx, N_EXPERTS), axis=0, keepdims=True)
        top_v.append(m)
        top_e.append(idx)
        vals = jnp.where(eidx == idx, -jnp.inf, vals)
    ex = [jnp.exp(v - top_v[0]) for v in top_v]
    den = ex[0] + ex[1] + ex[2] + ex[3]
    gates = jnp.concatenate([e / den for e in ex], axis=0)
    te = jnp.concatenate(top_e, axis=0)
    mem = jnp.zeros((N_EXPERTS, n), jnp.int32)
    for idx in top_e:
        mem = mem + jnp.where(eidx == idx, 1, 0)
    return te, gates, jnp.sum(mem, axis=1, keepdims=True)


def _compact_tile(h1, te, cnt):
    eidx = lax.broadcasted_iota(jnp.int32, (N_EXPERTS, TILE), 0)
    mem = jnp.zeros((N_EXPERTS, TILE), F32)
    for k in range(TOP_K):
        mem = mem + jnp.where(eidx == te[k:k + 1, :], 1.0, 0.0)
    tr = lax.broadcasted_iota(jnp.int32, (TILE, TILE), 0)
    tc = lax.broadcasted_iota(jnp.int32, (TILE, TILE), 1)
    before = jnp.where(tr < tc, 1.0, 0.0).astype(BF16)
    pos = jnp.dot(mem.astype(BF16), before, preferred_element_type=F32)
    seg_rows = ((cnt + (SEG_ALIGN - 1)) // SEG_ALIGN * SEG_ALIGN).astype(F32)
    er = lax.broadcasted_iota(jnp.int32, (N_EXPERTS, N_EXPERTS), 0)
    ec = lax.broadcasted_iota(jnp.int32, (N_EXPERTS, N_EXPERTS), 1)
    earlier = jnp.where(ec < er, 1.0, 0.0).astype(BF16)
    seg_start = jnp.dot(earlier, jnp.broadcast_to(seg_rows, (N_EXPERTS, LANES)).astype(BF16),
                        preferred_element_type=F32)[:, 0:1]
    base = (pos + seg_start).astype(jnp.int32)
    dests = []
    for k in range(TOP_K):
        d = jnp.sum(jnp.where(eidx == te[k:k + 1, :], base, 0), axis=0, keepdims=True)
        dests.append(jnp.where(te[k:k + 1, :] >= 0, d, -1))
    rid = lax.broadcasted_iota(jnp.int32, (TILE_ROWS, TILE), 0)
    onehot = jnp.zeros((TILE_ROWS, TILE), F32)
    for k in range(TOP_K):
        onehot = jnp.where(rid == dests[k], 1.0, onehot)
    picked = jnp.dot(onehot.astype(BF16), h1.astype(BF16), preferred_element_type=F32)
    return picked, jnp.concatenate(dests, axis=0)


def _post_mix(x, mix_in, w_out_ref, b_out_ref, g_ref, b_ref):
    mix = jnp.dot(mix_in.astype(BF16), w_out_ref[...], preferred_element_type=F32) + b_out_ref[...]
    return _layer_norm(DN_ALPHA * x + mix, g_ref[...], b_ref[...])


SEQ_PAIR = 2


def _interleave(chains):
    results = [None] * len(chains)
    live = list(range(len(chains)))
    while live:
        for k in list(live):
            try:
                next(chains[k])
            except StopIteration as stop:
                results[k] = stop.value
                live.remove(k)
    return results


def _bias_tables(bias_s):
    row2 = lax.broadcasted_iota(jnp.int32, (2 * WINDOW, 2 * WINDOW), 0)
    col2 = lax.broadcasted_iota(jnp.int32, (2 * WINDOW, 2 * WINDOW), 1)
    dist = (row2 & (WINDOW - 1)) - col2 + WINDOW
    valid = (dist >= 0) & (dist <= WINDOW)
    distf = dist.astype(F32)
    for g in range(HEAD_GROUPS):
        slope = jnp.where(row2 >= WINDOW, 2.0 ** -(g + HEAD_GROUPS + 1), 2.0 ** -(g + 1))
        bias = jnp.where(valid, (-LOG2E) * (slope * distf), NEG_INF)
        bias_s[g, 0] = bias
        bias_s[g, 1] = jnp.where(col2 >= WINDOW, bias, NEG_INF)


def _mixer_prompt_kernel(sinks_ref, x_ref, *rest):
    wts = rest[:15]
    h1_ref, xt_ref, dest_ref, tg_ref, cnt_ref, wk_ref, wv_ref, lh_ref, cv_ref = rest[15:24]
    kext, vext, uext, hcar, attn_buf, bias_s = rest[24:]
    j = pl.program_id(1)
    nj = pl.num_programs(1)
    R = TILE
    one = lambda ref, bb: ref.at[pl.ds(bb, 1)]

    @pl.when(j == 0)
    def _():
        for bb in range(SEQ_PAIR):
            kext[bb, 0:WINDOW, :] = jnp.zeros((WINDOW, KV_COLS), F32)
            vext[bb, 0:WINDOW, :] = jnp.zeros((WINDOW, KV_COLS), F32)
            uext[bb, 0:SUBLANES, :] = jnp.zeros((SUBLANES, LRU_WIDTH), F32)
            hcar[bb] = jnp.zeros((1, LRU_WIDTH), F32)

    @pl.when((pl.program_id(0) == 0) & (j == 0))
    def _():
        _bias_tables(bias_s)

    per_tile = lambda ref, bb: ref.at[0, pl.ds(bb, 1)]
    h_last = _interleave([
        _mixer_tile(j, sinks_ref, one(x_ref, bb), wts, one(h1_ref, bb), one(xt_ref, bb),
                    per_tile(dest_ref, bb), per_tile(tg_ref, bb), per_tile(cnt_ref, bb),
                    kext.at[bb], vext.at[bb], uext.at[bb], hcar.at[bb], attn_buf.at[bb], bias_s)
        for bb in range(SEQ_PAIR)])

    @pl.when(j == nj - 1)
    def _():
        for bb in range(SEQ_PAIR):
            wk_ref[bb] = kext[bb, R:R + WINDOW, :]
            wv_ref[bb] = vext[bb, R:R + WINDOW, :]
            lh_ref[bb] = h_last[bb]
            cv_ref[bb] = uext[bb, SUBLANES + R - (CONV_WIDTH - 1):SUBLANES + R, :]

    for bb in range(SEQ_PAIR):
        kext[bb, 0:WINDOW, :] = kext[bb, R:R + WINDOW, :]
        vext[bb, 0:WINDOW, :] = vext[bb, R:R + WINDOW, :]
        uext[bb, 0:SUBLANES, :] = uext[bb, R:R + SUBLANES, :]


def _mixer_tile(j, sinks_ref, x_ref, wts, h1_ref, xt_ref, dest_ref, tg_ref, cnt_ref,
                kext, vext, uext, hcar, attn_buf, bias_s):
    (w_in_ref, b_in_ref, conv_w_ref, conv_b_ref, wa_ref, ba_ref, wi_ref, bi_ref, lam_ref,
     w_out_ref, b_out_ref, ln_g_ref, ln_b_ref, rw_ref, rb_ref) = wts
    R = TILE
    x = x_ref[0]
    proj = jnp.dot(x.astype(BF16), w_in_ref[...], preferred_element_type=F32) + b_in_ref[...]
    q = proj[:, :Q_COLS]
    kext[WINDOW:WINDOW + R, :] = proj[:, Q_COLS:Q_COLS + KV_COLS]
    vext[WINDOW:WINDOW + R, :] = proj[:, Q_COLS + KV_COLS:Q_COLS + 2 * KV_COLS]
    uext[SUBLANES:SUBLANES + R, :] = proj[:, Q_COLS + 2 * KV_COLS:Q_COLS + 2 * KV_COLS + LRU_WIDTH]
    ug = proj[:, Q_COLS + 2 * KV_COLS + LRU_WIDTH:]
    yield

    upper = lax.broadcasted_iota(jnp.int32, (2 * WINDOW, 1), 0) >= WINDOW
    lo_lane = lax.broadcasted_iota(jnp.int32, (WINDOW, LANES), 1) < HEAD_DIM
    qs = q * (HEAD_DIM ** -0.5 * LOG2E)
    for s in range(R // WINDOW):
        kk = kext[s * WINDOW:s * WINDOW + 2 * WINDOW, :].astype(BF16)
        vv = vext[s * WINDOW:s * WINDOW + 2 * WINDOW, :].astype(BF16)
        table = jnp.where(j == 0, 1, 0) if s == 0 else 0
        for g in range(HEAD_GROUPS):
            qg = qs[s * WINDOW:(s + 1) * WINDOW, g * LANES:(g + 1) * LANES]
            q2 = jnp.concatenate([jnp.where(lo_lane, qg, 0.0), jnp.where(lo_lane, 0.0, qg)], axis=0)
            sc = lax.dot_general(q2.astype(BF16), kk, (((1,), (1,)), ((), ())),
                                 preferred_element_type=F32) + bias_s[g, table]
            sink = jnp.where(upper, sinks_ref[g + HEAD_GROUPS], sinks_ref[g]) * LOG2E
            m = jnp.maximum(jnp.max(sc, axis=-1, keepdims=True), sink)
            p = jnp.exp2(sc - m)
            den = jnp.sum(p, axis=-1, keepdims=True) + jnp.exp2(sink - m)
            o = jnp.dot(p.astype(BF16), vv, preferred_element_type=F32) / den
            attn_buf[s * WINDOW:(s + 1) * WINDOW, g * LANES:(g + 1) * LANES] = jnp.where(
                lo_lane, o[:WINDOW], o[WINDOW:])
            yield

    xc = conv_b_ref[...]
    for tap in range(CONV_WIDTH):
        off = SUBLANES - (CONV_WIDTH - 1) + tap
        xc = xc + uext[off:off + R, :] * conv_w_ref[tap:tap + 1, :]
    a, bv = _rglru_gates(xc, wa_ref, ba_ref, wi_ref, bi_ref, lam_ref)
    yield
    groups = R // SUBLANES
    a3 = a.reshape(groups, SUBLANES, LRU_WIDTH)
    b3 = bv.reshape(groups, SUBLANES, LRU_WIDTH)
    sub = lax.broadcasted_iota(jnp.int32, (groups, SUBLANES, LRU_WIDTH), 1)
    d = 1
    while d < SUBLANES:
        keep = sub >= d
        a_prev = jnp.where(keep, pltpu.roll(a3, d, 1), 1.0)
        b_prev = jnp.where(keep, pltpu.roll(b3, d, 1), 0.0)
        b3 = a3 * b_prev + b3
        a3 = a3 * a_prev
        d *= 2
    h_prev = hcar[...]
    h_groups = []
    for c in range(groups):
        hc = a3[c] * h_prev + b3[c]
        h_groups.append(hc)
        h_prev = hc[SUBLANES - 1:SUBLANES, :]
    h = jnp.concatenate(h_groups, axis=0)
    hcar[...] = h_prev
    rnn = h * _gelu_tanh(ug)
    yield

    mix_in = jnp.concatenate([attn_buf[...], rnn], axis=1)
    h1 = _post_mix(x, mix_in, w_out_ref, b_out_ref, ln_g_ref, ln_b_ref)
    h1_ref[0] = h1
    yield
    te, tg, cnt = _route(h1, rw_ref, rb_ref)
    tg_ref[0] = tg
    cnt_ref[0] = cnt
    yield
    picked, dest = _compact_tile(h1, te, cnt)
    xt_ref[0] = picked
    dest_ref[0] = dest
    return h_prev


def _const_spec(shape):
    return pl.BlockSpec(shape, lambda *_: (0,) * len(shape))


def _mixer_prompt(x, sinks, wts):
    B, S, _ = x.shape
    assert B % SEQ_PAIR == 0 and S % TILE == 0
    nj = S // TILE
    P = SEQ_PAIR
    tile_idx = lambda b, j, *_: (j, b, 0, 0)
    batch_idx = lambda b, j, *_: (b, 0, 0)
    in_specs = [pl.BlockSpec((P, TILE, D_MODEL), lambda b, j, *_: (b, j, 0))]
    in_specs += [_const_spec(w.shape) for w in wts]
    out_shape = (
        jax.ShapeDtypeStruct((B, S, D_MODEL), F32),
        jax.ShapeDtypeStruct((B * nj, TILE_ROWS, D_MODEL), F32),
        jax.ShapeDtypeStruct((nj, B, TOP_K, TILE), jnp.int32),
        jax.ShapeDtypeStruct((nj, B, TOP_K, TILE), F32),
        jax.ShapeDtypeStruct((nj, B, N_EXPERTS, 1), jnp.int32),
        jax.ShapeDtypeStruct((B, WINDOW, KV_COLS), F32),
        jax.ShapeDtypeStruct((B, WINDOW, KV_COLS), F32),
        jax.ShapeDtypeStruct((B, 1, LRU_WIDTH), F32),
        jax.ShapeDtypeStruct((B, CONV_WIDTH - 1, LRU_WIDTH), F32),
    )
    out_specs = (
        pl.BlockSpec((P, TILE, D_MODEL), lambda b, j, *_: (b, j, 0)),
        pl.BlockSpec((P, TILE_ROWS, D_MODEL), lambda b, j, *_: (j * (B // P) + b, 0, 0)),
        pl.BlockSpec((1, P, TOP_K, TILE), tile_idx),
        pl.BlockSpec((1, P, TOP_K, TILE), tile_idx),
        pl.BlockSpec((1, P, N_EXPERTS, 1), tile_idx),
        pl.BlockSpec((P, WINDOW, KV_COLS), batch_idx),
        pl.BlockSpec((P, WINDOW, KV_COLS), batch_idx),
        pl.BlockSpec((P, 1, LRU_WIDTH), batch_idx),
        pl.BlockSpec((P, CONV_WIDTH - 1, LRU_WIDTH), batch_idx),
    )
    scratch = [
        pltpu.VMEM((P, TILE + WINDOW, KV_COLS), F32),
        pltpu.VMEM((P, TILE + WINDOW, KV_COLS), F32),
        pltpu.VMEM((P, TILE + SUBLANES, LRU_WIDTH), F32),
        pltpu.VMEM((P, 1, LRU_WIDTH), F32),
        pltpu.VMEM((P, TILE, Q_COLS), F32),
        pltpu.VMEM((HEAD_GROUPS, 2, 2 * WINDOW, 2 * WINDOW), F32),
    ]
    return pl.pallas_call(
        _mixer_prompt_kernel,
        grid_spec=pltpu.PrefetchScalarGridSpec(
            num_scalar_prefetch=1, grid=(B // P, nj), in_specs=in_specs, out_specs=out_specs,
            scratch_shapes=scratch),
        out_shape=out_shape,
        compiler_params=pltpu.CompilerParams(
            dimension_semantics=("arbitrary", "arbitrary"), vmem_limit_bytes=VMEM_LIMIT),
        name="mixer_prompt",
    )(sinks, x, *wts)


SEQ_CHUNK = 16


def _mixer_sample_kernel(sinks_ref, x_ref, ck_ref, cv_ref, h0_ref, cprev_ref,
                         w_in_ref, b_in_ref, conv_w_ref, conv_b_ref,
                         wa_ref, ba_ref, wi_ref, bi_ref, lam_ref, w_out_ref, b_out_ref,
                         ln_g_ref, ln_b_ref, rw_ref, rb_ref,
                         h1_ref, xt_ref, dest_ref, tg_ref, cnt_ref, wk_ref, wv_ref, lh_ref, cnew_ref,
                         proj_s, attn_s, attn_c):
    c = pl.program_id(0)
    nc = pl.num_programs(0)
    nseq = x_ref.shape[0]

    @pl.when(c == 0)
    def _():
        proj_s[...] = jnp.dot(x_ref[...].astype(BF16), w_in_ref[...],
                              preferred_element_type=F32) + b_in_ref[...]

    sub = lax.broadcasted_iota(jnp.int32, (N_Q_HEADS, LANES), 0)
    lane = lax.broadcasted_iota(jnp.int32, (N_Q_HEADS, LANES), 1)
    own_half = (lane < HEAD_DIM) == (sub < HEAD_GROUPS)
    sub1 = sub[:, 0:1]
    slope = jnp.zeros((N_Q_HEADS, 1), F32)
    sink = jnp.zeros((N_Q_HEADS, 1), F32)
    for hd in range(N_Q_HEADS):
        slope = jnp.where(sub1 == hd, 2.0 ** -(hd + 1), slope)
        sink = jnp.where(sub1 == hd, sinks_ref[hd], sink)
    dist = (WINDOW - lax.broadcasted_iota(jnp.int32, (1, WINDOW), 1)).astype(F32)
    lo_row = lax.broadcasted_iota(jnp.int32, (1, LANES), 1) < HEAD_DIM
    scale = HEAD_DIM ** -0.5

    def one_sequence(i):
        b = c * SEQ_CHUNK + i
        prow = proj_s[pl.ds(b, 1), :]
        q8 = jnp.zeros((N_Q_HEADS, LANES), F32)
        for g in range(HEAD_GROUPS):
            qg = jnp.broadcast_to(prow[:, g * LANES:(g + 1) * LANES], (N_Q_HEADS, LANES))
            q8 = jnp.where(((sub & (HEAD_GROUPS - 1)) == g) & own_half, qg, q8)
        k_new = prow[:, Q_COLS:Q_COLS + KV_COLS]
        v_new = prow[:, Q_COLS + KV_COLS:Q_COLS + 2 * KV_COLS]
        kb = ck_ref[i]
        vb = cv_ref[i]
        sc = lax.dot_general(q8.astype(BF16), kb.astype(BF16), (((1,), (1,)), ((), ())),
                             preferred_element_type=F32) * scale - slope * dist
        sc_new = jnp.sum(q8 * k_new, axis=-1, keepdims=True) * scale
        yield
        m = jnp.maximum(jnp.maximum(jnp.max(sc, axis=-1, keepdims=True), sc_new), sink)
        p = jnp.exp(sc - m)
        p_new = jnp.exp(sc_new - m)
        den = jnp.sum(p, axis=-1, keepdims=True) + p_new + jnp.exp(sink - m)
        yield
        o = (jnp.dot(p.astype(BF16), vb.astype(BF16), preferred_element_type=F32)
             + p_new * v_new) / den
        yield
        for g in range(HEAD_GROUPS):
            attn_c[i:i + 1, g * LANES:(g + 1) * LANES] = jnp.where(
                lo_row, o[g:g + 1, :], o[g + HEAD_GROUPS:g + HEAD_GROUPS + 1, :])
        wk_ref[i, 0:WINDOW - 1, :] = ck_ref[i, 1:WINDOW, :]
        wk_ref[i, WINDOW - 1:WINDOW, :] = k_new
        wv_ref[i, 0:WINDOW - 1, :] = cv_ref[i, 1:WINDOW, :]
        wv_ref[i, WINDOW - 1:WINDOW, :] = v_new

    _interleave([one_sequence(i) for i in range(SEQ_CHUNK)])
    attn_s[pl.ds(pl.multiple_of(c * SEQ_CHUNK, SEQ_CHUNK), SEQ_CHUNK), :] = attn_c[...]

    @pl.when(c == nc - 1)
    def _():
        x = x_ref[...]
        ux = proj_s[:, Q_COLS + 2 * KV_COLS:Q_COLS + 2 * KV_COLS + LRU_WIDTH]
        ug = proj_s[:, Q_COLS + 2 * KV_COLS + LRU_WIDTH:]
        xc = conv_b_ref[...]
        for tap in range(CONV_WIDTH - 1):
            xc = xc + cprev_ref[tap] * conv_w_ref[tap:tap + 1, :]
        xc = xc + ux * conv_w_ref[CONV_WIDTH - 1:CONV_WIDTH, :]
        a, bv = _rglru_gates(xc, wa_ref, ba_ref, wi_ref, bi_ref, lam_ref)
        h = a * h0_ref[...] + bv
        rnn = h * _gelu_tanh(ug)
        mix_in = jnp.concatenate([attn_s[...], rnn], axis=1)
        h1 = _post_mix(x, mix_in, w_out_ref, b_out_ref, ln_g_ref, ln_b_ref)
        h1_ref[...] = h1
        te, tg, cnt = _route(h1, rw_ref, rb_ref)
        cnt_ref[0] = cnt
        tg_ref[0] = jnp.concatenate([tg, jnp.zeros((TOP_K, TILE - nseq), F32)], axis=1)
        te_tile = jnp.concatenate([te, jnp.full((TOP_K, TILE - nseq), -1, jnp.int32)], axis=1)
        h1_tile = jnp.concatenate([h1, jnp.zeros((TILE - nseq, D_MODEL), F32)], axis=0)
        picked, dest = _compact_tile(h1_tile, te_tile, cnt)
        xt_ref[0] = picked
        dest_ref[0] = dest
        lh_ref[...] = h
        for tap in range(1, CONV_WIDTH - 1):
            cnew_ref[tap - 1] = cprev_ref[tap]
        cnew_ref[CONV_WIDTH - 2] = ux


def _mixer_sample(x, ck, cv, h0, cprev, sinks, wts):
    nseq = x.shape[0]
    assert nseq % SEQ_CHUNK == 0 and nseq <= TILE and nseq % LANES == 0
    nc = nseq // SEQ_CHUNK
    chunk_idx = lambda c, *_: (c, 0, 0)
    in_specs = [
        _const_spec((nseq, D_MODEL)),
        pl.BlockSpec((SEQ_CHUNK, WINDOW, KV_COLS), chunk_idx),
        pl.BlockSpec((SEQ_CHUNK, WINDOW, KV_COLS), chunk_idx),
        _const_spec((nseq, LRU_WIDTH)),
        _const_spec((CONV_WIDTH - 1, nseq, LRU_WIDTH)),
    ] + [_const_spec(w.shape) for w in wts]
    out_shape = (
        jax.ShapeDtypeStruct((nseq, D_MODEL), F32),
        jax.ShapeDtypeStruct((1, TILE_ROWS, D_MODEL), F32),
        jax.ShapeDtypeStruct((1, TOP_K, TILE), jnp.int32),
        jax.ShapeDtypeStruct((1, TOP_K, TILE), F32),
        jax.ShapeDtypeStruct((1, N_EXPERTS, 1), jnp.int32),
        jax.ShapeDtypeStruct((nseq, WINDOW, KV_COLS), F32),
        jax.ShapeDtypeStruct((nseq, WINDOW, KV_COLS), F32),
        jax.ShapeDtypeStruct((nseq, LRU_WIDTH), F32),
        jax.ShapeDtypeStruct((CONV_WIDTH - 1, nseq, LRU_WIDTH), F32),
    )
    out_specs = (
        _const_spec((nseq, D_MODEL)),
        _const_spec((1, TILE_ROWS, D_MODEL)),
        _const_spec((1, TOP_K, TILE)),
        _const_spec((1, TOP_K, TILE)),
        _const_spec((1, N_EXPERTS, 1)),
        pl.BlockSpec((SEQ_CHUNK, WINDOW, KV_COLS), chunk_idx),
        pl.BlockSpec((SEQ_CHUNK, WINDOW, KV_COLS), chunk_idx),
        _const_spec((nseq, LRU_WIDTH)),
        _const_spec((CONV_WIDTH - 1, nseq, LRU_WIDTH)),
    )
    scratch = [pltpu.VMEM((nseq, D_IN), F32), pltpu.VMEM((nseq, Q_COLS), F32),
               pltpu.VMEM((SEQ_CHUNK, Q_COLS), F32)]
    return pl.pallas_call(
        _mixer_sample_kernel,
        grid_spec=pltpu.PrefetchScalarGridSpec(
            num_scalar_prefetch=1, grid=(nc,), in_specs=in_specs, out_specs=out_specs,
            scratch_shapes=scratch),
        out_shape=out_shape,
        compiler_params=pltpu.CompilerParams(
            dimension_semantics=("arbitrary",), vmem_limit_bytes=VMEM_LIMIT),
        name="mixer_sample",
    )(sinks, x, ck, cv, h0, cprev, *wts)


def _grouped(rows):
    assert rows % SEG_ALIGN == 0
    return (rows // SEG_ALIGN, SEG_ALIGN, D_MODEL)


BLOCK_GROUPS = ROW_BLOCK // SEG_ALIGN
WEIGHT_DMA_PRIORITY = 1
TILE_GROUPS = TILE_ROWS // SEG_ALIGN


def _experts_kernel(be_ref, nxt_ref, nb_ref, tlo_ref, thi_ref, valid_ref,
                    seg_len_ref, seg_src_ref, seg_off_ref,
                    xtp_ref, xts_ref, wgu_ref, bgu_ref, wdn_ref, bdn_ref, yb_ref,
                    xbuf, wgu_f, wdn_f, wgu_s, wdn_s, wsem, xsem, *, n_tiles, n_prompt_tiles):
    b = pl.program_id(0)
    slot = b % 2

    def weight_copies(e):
        return (pltpu.make_async_copy(wgu_ref.at[e], wgu_f, wsem.at[0]),
                pltpu.make_async_copy(wdn_ref.at[e], wdn_f, wsem.at[1]))

    def start_gather(blk, s):
        lo_b = blk * BLOCK_GROUPS
        first = be_ref[blk] * n_tiles

        t_lo = tlo_ref[blk]
        t_hi = thi_ref[blk]

        def piece(t):
            off = seg_off_ref[first + t]
            lo = jnp.maximum(off, lo_b)
            n = jnp.minimum(off + seg_len_ref[first + t], lo_b + BLOCK_GROUPS) - lo

            src = pl.ds(seg_src_ref[first + t] + lo, n)
            dst = xbuf.at[s, pl.ds(lo - lo_b, n)]
            wanted = (n > 0) & (t <= t_hi)

            @pl.when(wanted & (t < n_prompt_tiles))
            def _():
                pltpu.make_async_copy(xtp_ref.at[src], dst, xsem.at[s]).start()

            @pl.when(wanted & (t >= n_prompt_tiles))
            def _():
                pltpu.make_async_copy(xts_ref.at[src], dst, xsem.at[s]).start()

        def two_pieces(i, carry):
            piece(t_lo + 2 * i)
            piece(t_lo + 2 * i + 1)
            return carry

        lax.fori_loop(0, (t_hi - t_lo + 2) // 2, two_pieces, 0)

    @pl.when(b == 0)
    def _():
        xbuf[...] = jnp.zeros(xbuf.shape, F32)
        for cp in weight_copies(be_ref[0]):
            cp.start(priority=WEIGHT_DMA_PRIORITY)
        start_gather(b, slot)

    @pl.when(b + 1 < nb_ref[0])
    def _():
        start_gather(b + 1, 1 - slot)

    @pl.when(b < nb_ref[0])
    def _():
        n = valid_ref[b]
        pltpu.make_async_copy(xtp_ref.at[pl.ds(0, n)], xbuf.at[slot, pl.ds(0, n)], xsem.at[slot]).wait()

        @pl.when((b == 0) | (be_ref[b] != be_ref[jnp.maximum(b - 1, 0)]))
        def _():
            for cp in weight_copies(be_ref[b]):
                cp.wait()
            wgu_s[...] = wgu_f[...].astype(BF16)
            wdn_s[...] = wdn_f[...].astype(BF16)

            @pl.when(nxt_ref[b] >= 0)
            def _():
                for cp in weight_copies(nxt_ref[b]):
                    cp.start(priority=WEIGHT_DMA_PRIORITY)

        def expert_rows(rows):
            x = xbuf[slot, pl.ds(0, rows // SEG_ALIGN)].reshape(rows, D_MODEL).astype(BF16)
            hgu = jnp.dot(x, wgu_s[...], preferred_element_type=F32) + bgu_ref[0]
            glu = jnp.minimum(hgu[:, :D_FF], SWIGLU_LIMIT)
            lin = jnp.clip(hgu[:, D_FF:], -SWIGLU_LIMIT, SWIGLU_LIMIT)
            act = glu * jax.nn.sigmoid(SWIGLU_ALPHA * glu) * (lin + 1.0)
            yb_ref[0:rows, :] = jnp.dot(act.astype(BF16), wdn_s[...], preferred_element_type=F32) + bdn_ref[0]

        quarter = ROW_BLOCK // 4
        quarters = (n * SEG_ALIGN + quarter - 1) // quarter
        for used in range(1, 5):
            @pl.when(quarters == used)
            def _(rows=used * quarter):
                expert_rows(rows)
                if rows < ROW_BLOCK:
                    yb_ref[rows:, :] = jnp.zeros((ROW_BLOCK - rows, D_MODEL), F32)

    @pl.when(b >= nb_ref[0])
    def _():
        yb_ref[...] = jnp.zeros(yb_ref.shape, F32)


def _experts(sched, segs, xt_prompt, xt_sample, w_gu, b_gu, w_dn, b_dn, n_rows):
    n_blocks = n_rows // ROW_BLOCK
    n_prompt_tiles = xt_prompt.shape[0] // TILE_GROUPS
    n_tiles = n_prompt_tiles + xt_sample.shape[0] // TILE_GROUPS
    n_prefetch = len(sched) + len(segs)
    exp_idx = lambda b, be, *_: (be[b], 0, 0)
    in_specs = [
        pl.BlockSpec(memory_space=pl.ANY),
        pl.BlockSpec(memory_space=pl.ANY),
        pl.BlockSpec(memory_space=pl.ANY),
        pl.BlockSpec((1, 1, 2 * D_FF), exp_idx),
        pl.BlockSpec(memory_space=pl.ANY),
        pl.BlockSpec((1, 1, D_MODEL), exp_idx),
    ]
    scratch = [pltpu.VMEM((2,) + _grouped(ROW_BLOCK), F32),
               pltpu.VMEM((D_MODEL, 2 * D_FF), F32), pltpu.VMEM((D_FF, D_MODEL), F32),
               pltpu.VMEM((D_MODEL, 2 * D_FF), BF16), pltpu.VMEM((D_FF, D_MODEL), BF16),
               pltpu.SemaphoreType.DMA((2,)), pltpu.SemaphoreType.DMA((2,))]
    return pl.pallas_call(
        functools.partial(_experts_kernel, n_tiles=n_tiles, n_prompt_tiles=n_prompt_tiles),
        grid_spec=pltpu.PrefetchScalarGridSpec(
            num_scalar_prefetch=n_prefetch, grid=(n_blocks,), in_specs=in_specs,
            out_specs=pl.BlockSpec((ROW_BLOCK, D_MODEL), lambda b, *_: (b, 0)),
            scratch_shapes=scratch),
        out_shape=jax.ShapeDtypeStruct((n_rows, D_MODEL), F32),
        compiler_params=pltpu.CompilerParams(
            dimension_semantics=("arbitrary",), vmem_limit_bytes=VMEM_LIMIT),
        name="moe_experts",
    )(*sched, *segs, xt_prompt, xt_sample, w_gu, b_gu[:, None, :], w_dn, b_dn[:, None, :])


def _tile_tokens(i, n_prompt_tiles, hp_ref, hs_ref):
    hs = hs_ref[...]
    hs_tile = jnp.concatenate([hs, jnp.zeros((TILE - hs.shape[0], D_MODEL), F32)], axis=0)
    return jnp.where(i < n_prompt_tiles, hp_ref[...], hs_tile)


COMBINE_SLOTS = 3


def _combine_kernel(seg_len_ref, seg_start_ref, seg_off_ref, tile_groups_ref,
                    hp_ref, hs_ref, dest_ref, gate_ref, g_ref, b_ref, yb_ref,
                    yp_ref, ys_ref, ybuf, sem, *, n_prompt_tiles):
    i = pl.program_id(0)
    n = pl.num_programs(0)
    slot = i % COMBINE_SLOTS

    def start_gather(tile):
        s = tile % COMBINE_SLOTS

        def segment(e, priority):
            idx = tile * N_EXPERTS + e
            groups = seg_len_ref[idx]

            @pl.when(groups > 0)
            def _():
                pltpu.make_async_copy(yb_ref.at[pl.ds(seg_off_ref[idx], groups)],
                                      ybuf.at[s, pl.ds(seg_start_ref[idx], groups)],
                                      sem.at[s]).start(priority=priority)

        def two_segments(k, carry):
            segment(2 * k, 0)
            segment(2 * k + 1, 1)
            return carry
        lax.fori_loop(0, N_EXPERTS // 2, two_segments, 0)

    @pl.when(i == 0)
    def _():
        ybuf[...] = jnp.zeros(ybuf.shape, F32)
        for ahead in range(COMBINE_SLOTS - 1):
            @pl.when(ahead < n)
            def _():
                start_gather(i + ahead)

    @pl.when(i + COMBINE_SLOTS - 1 < n)
    def _():
        start_gather(i + COMBINE_SLOTS - 1)

    total = tile_groups_ref[i]

    @pl.when(total > 0)
    def _():
        pltpu.make_async_copy(yb_ref.at[pl.ds(0, total)], ybuf.at[slot, pl.ds(0, total)], sem.at[slot]).wait()

    dest = dest_ref[0]
    gate = gate_ref[0]
    rid = lax.broadcasted_iota(jnp.int32, (TILE_ROWS, TILE), 0)
    weights = jnp.zeros((TILE_ROWS, TILE), F32)
    for k in range(TOP_K):
        weights = jnp.where(rid == dest[k:k + 1, :], gate[k:k + 1, :], weights)
    rows = ybuf[slot].reshape(TILE_ROWS, D_MODEL).astype(BF16)
    ff = lax.dot_general(weights.astype(BF16), rows, (((0,), (0,)), ((), ())),
                         preferred_element_type=F32)
    h1 = _tile_tokens(i, n_prompt_tiles, hp_ref, hs_ref)
    y = _layer_norm(DN_ALPHA * h1 + ff, g_ref[...], b_ref[...])

    @pl.when(i < n_prompt_tiles)
    def _():
        yp_ref[...] = y

    @pl.when(i >= n_prompt_tiles)
    def _():
        ys_ref[...] = y[:ys_ref.shape[0]]


def _combine(segs, tile_groups, n_seq, h1p, h1s, dest, gate, ln_g, ln_b, yb):
    n_tiles = dest.shape[0]
    n_prompt_tiles = h1p.shape[0] // TILE
    blocks_per_seq = n_prompt_tiles // n_seq

    def prompt_idx(i, *_):
        t = jnp.minimum(i, n_prompt_tiles - 1)
        return ((t % n_seq) * blocks_per_seq + t // n_seq, 0)
    in_specs = [
        pl.BlockSpec((TILE, D_MODEL), prompt_idx),
        _const_spec(h1s.shape),
        pl.BlockSpec((1, TOP_K, TILE), lambda i, *_: (i, 0, 0)),
        pl.BlockSpec((1, TOP_K, TILE), lambda i, *_: (i, 0, 0)),
        _const_spec(ln_g.shape),
        _const_spec(ln_b.shape),
        pl.BlockSpec(memory_space=pl.ANY),
    ]
    out_shape = (jax.ShapeDtypeStruct(h1p.shape, F32), jax.ShapeDtypeStruct(h1s.shape, F32))
    out_specs = (pl.BlockSpec((TILE, D_MODEL), prompt_idx), _const_spec(h1s.shape))
    return pl.pallas_call(
        functools.partial(_combine_kernel, n_prompt_tiles=n_prompt_tiles),
        grid_spec=pltpu.PrefetchScalarGridSpec(
            num_scalar_prefetch=len(segs) + 1, grid=(n_tiles,), in_specs=in_specs, out_specs=out_specs,
            scratch_shapes=[pltpu.VMEM((COMBINE_SLOTS,) + _grouped(TILE_ROWS), F32),
                            pltpu.SemaphoreType.DMA((COMBINE_SLOTS,))]),
        out_shape=out_shape,
        compiler_params=pltpu.CompilerParams(
            dimension_semantics=("arbitrary",), vmem_limit_bytes=VMEM_LIMIT),
        name="moe_combine",
    )(*segs, tile_groups, h1p, h1s, dest, gate, ln_g, ln_b, yb.reshape(_grouped(yb.shape[0])))


def _moe_layout(cnt, n_prompt_tiles):
    n_tiles = cnt.shape[0]
    i32 = lambda a: a.astype(jnp.int32)
    seg_len = (cnt + SEG_ALIGN - 1) // SEG_ALIGN
    seg_start = jnp.cumsum(seg_len, axis=1) - seg_len
    exp_len = jnp.sum(seg_len, axis=0)
    exp_blocks = (exp_len + BLOCK_GROUPS - 1) // BLOCK_GROUPS
    blocks_end = jnp.cumsum(exp_blocks)
    first_block = blocks_end - exp_blocks
    seg_off = first_block[None, :] * BLOCK_GROUPS + jnp.cumsum(seg_len, axis=0) - seg_len
    max_rows = n_tiles * (TOP_K * TILE + N_EXPERTS * (SEG_ALIGN - 1)) + N_EXPERTS * (ROW_BLOCK - SEG_ALIGN)
    n_blocks = -(-max_rows // ROW_BLOCK)
    n_used = blocks_end[-1]
    blk = jnp.minimum(jnp.arange(n_blocks, dtype=jnp.int32), n_used - 1)
    eid = jnp.arange(N_EXPERTS, dtype=jnp.int32)
    block_expert = jnp.minimum(jnp.sum(blocks_end[None, :] <= blk[:, None], axis=1), N_EXPERTS - 1)
    own = block_expert[:, None] == eid[None, :]
    pick = lambda per_expert: jnp.sum(jnp.where(own, per_expert[None, :], 0), axis=1)
    later_used = (eid[None, :] > eid[:, None]) & (exp_blocks[None, :] > 0)
    next_of = jnp.min(jnp.where(later_used, eid[None, :], N_EXPERTS), axis=1)
    next_expert = pick(jnp.where(next_of < N_EXPERTS, next_of, -1))
    lo = blk * BLOCK_GROUPS
    off_b = jnp.sum(jnp.where(own[:, None, :], seg_off[None, :, :], 0), axis=2)
    len_b = jnp.sum(jnp.where(own[:, None, :], seg_len[None, :, :], 0), axis=2)
    tile_lo = jnp.sum(off_b + len_b <= lo[:, None], axis=1)
    tile_hi = jnp.sum(off_b < lo[:, None] + BLOCK_GROUPS, axis=1) - 1
    valid = jnp.clip(pick(first_block * BLOCK_GROUPS + exp_len) - lo, 0, BLOCK_GROUPS)
    flat = lambda a: i32(a.reshape(-1))
    sched = (i32(block_expert), i32(next_expert), i32(n_used.reshape(1)), i32(tile_lo), i32(tile_hi),
             i32(valid))
    segs = (flat(seg_len), flat(seg_start), flat(seg_off))
    tile = jnp.arange(n_tiles, dtype=seg_off.dtype)
    tile_in_array = jnp.where(tile < n_prompt_tiles, tile, tile - n_prompt_tiles)
    seg_src = tile_in_array[:, None] * TILE_GROUPS + seg_start - seg_off
    pad1 = lambda a: jnp.concatenate([flat(a.T), jnp.zeros((1,), jnp.int32)])
    segs_by_expert = (pad1(seg_len), pad1(seg_src), pad1(seg_off))
    return sched, segs, segs_by_expert, i32(jnp.sum(seg_len, axis=1)), n_blocks * ROW_BLOCK


def _prep_weights(w_in, b_in, conv_w, conv_b, lru_w_a, lru_b_a, lru_w_i, lru_b_i, lru_lambda,
                  w_out, b_out, ln1_g, ln1_b, router_w, router_b):
    def regroup_heads(a):
        rest = a.shape[1:]
        q = a[:Q_COLS].reshape(N_KV_HEADS, HEAD_GROUPS, HEAD_DIM, *rest)
        q = jnp.swapaxes(q, 0, 1).reshape(Q_COLS, *rest)
        return jnp.concatenate([q, a[Q_COLS:]], axis=0)

    def diag_tiles(w):
        per = MXU_DIM // (LRU_WIDTH // LRU_BLOCKS)
        w4 = w.reshape(LRU_BLOCKS // per, per, LRU_WIDTH // LRU_BLOCKS, LRU_WIDTH // LRU_BLOCKS)
        t = jnp.einsum("taij,ab->taibj", w4, jnp.eye(per, dtype=w.dtype))
        return t.reshape(LRU_BLOCKS // per, MXU_DIM, MXU_DIM).astype(BF16)

    return (
        regroup_heads(w_in[0].T).T.astype(BF16), regroup_heads(b_in[0])[None],
        conv_w[0], conv_b[0][None],
        diag_tiles(lru_w_a[0]), lru_b_a[0].reshape(1, LRU_WIDTH),
        diag_tiles(lru_w_i[0]), lru_b_i[0].reshape(1, LRU_WIDTH),
        lru_lambda[0][None],
        regroup_heads(w_out[0]).astype(BF16), b_out[0][None],
        ln1_g[0][None], ln1_b[0][None],
        router_w[0].T.astype(BF16), router_b[0][:, None],
    )


def kernel(x_prompt, x_sample, cache_win_k, cache_win_v, state_lru_h, state_conv, w_in, b_in, attn_sinks, conv_w, conv_b, lru_w_a, lru_b_a, lru_w_i, lru_b_i, lru_lambda, w_out, b_out, ln1_g, ln1_b, router_w, router_b, w_gate_up, b_gate_up, w_down, b_down, ln2_g, ln2_b):
    assert w_in.shape[0] == 1, "single-layer step"
    B, S, _ = x_prompt.shape
    nseq = x_sample.shape[0]
    assert x_sample.shape[1] == 1 and S % TILE == 0
    wts = _prep_weights(w_in, b_in, conv_w, conv_b, lru_w_a, lru_b_a, lru_w_i, lru_b_i, lru_lambda,
                        w_out, b_out, ln1_g, ln1_b, router_w, router_b)
    sinks = attn_sinks[0]

    h1p, xt_p, dest_p, tg_p, cnt_p, pk, pv, ph, pc = _mixer_prompt(x_prompt, sinks, wts)
    h1s, xt_s, dest_s, tg_s, cnt_s, sk, sv, sh, sc = _mixer_sample(
        x_sample.reshape(nseq, D_MODEL),
        cache_win_k[0].reshape(nseq, WINDOW, KV_COLS), cache_win_v[0].reshape(nseq, WINDOW, KV_COLS),
        state_lru_h[0], jnp.transpose(state_conv[0], (1, 0, 2)), sinks, wts)

    per_tile = lambda a: a.reshape(-1, *a.shape[2:])
    dest = jnp.concatenate([per_tile(dest_p), dest_s], axis=0)
    tg = jnp.concatenate([per_tile(tg_p), tg_s], axis=0)
    cnt = jnp.concatenate([per_tile(cnt_p), cnt_s], axis=0)[:, :, 0]
    sched, segs, segs_by_expert, tile_groups, n_rows = _moe_layout(cnt, xt_p.shape[0])

    end_to_end = lambda a: a.reshape(-1, SEG_ALIGN, D_MODEL)
    yb = _experts(sched, segs_by_expert, end_to_end(xt_p), end_to_end(xt_s),
                  w_gate_up[0], b_gate_up[0], w_down[0], b_down[0], n_rows)
    yp, ys = _combine(segs, tile_groups, B, h1p.reshape(B * S, D_MODEL), h1s, dest, tg,
                      ln2_g[0][None], ln2_b[0][None], yb)

    kv_shape = (N_KV_HEADS, HEAD_DIM)
    return (
        yp.reshape(B, S, D_MODEL), ys.reshape(nseq, 1, D_MODEL),
        pk.reshape(1, B, WINDOW, *kv_shape), pv.reshape(1, B, WINDOW, *kv_shape),
        ph.reshape(1, B, LRU_WIDTH), pc[None],
        sk.reshape(1, nseq, WINDOW, *kv_shape), sv.reshape(1, nseq, WINDOW, *kv_shape),
        sh[None], jnp.transpose(sc, (1, 0, 2))[None],
    )
```

```python
import functools

import jax
import jax.numpy as jnp
import numpy as np
from jax import lax
from jax.experimental import pallas as pl
from jax.experimental.pallas import tpu as pltpu

F32 = jnp.float32
BF16 = jnp.bfloat16

D_MODEL = 1024
N_Q_HEADS = 8
N_KV_HEADS = 2
HEAD_DIM = 64
WINDOW = 128
Q_COLS = N_Q_HEADS * HEAD_DIM
KV_COLS = N_KV_HEADS * HEAD_DIM
LRU_WIDTH = 512
LRU_BLOCKS = 8
CONV_WIDTH = 4
RG_C = 8.0
N_EXPERTS = 32
TOP_K = 4
D_FF = 1024
SWIGLU_LIMIT = 7.0
SWIGLU_ALPHA = 1.702
LN_EPS = 1e-5
DN_ALPHA = 2.0 ** 0.25
NEG_INF = -1e30
LOG2E = 1.4426950408889634
D_IN = Q_COLS + 2 * KV_COLS + 2 * LRU_WIDTH

LANES = 128
SUBLANES = 8
MXU_DIM = 256

TILE = 256
SEG_ALIGN = SUBLANES
ROW_BLOCK = 1024
HEAD_GROUPS = N_Q_HEADS // N_KV_HEADS
VMEM_LIMIT = 56 * 1024 * 1024

TILE_ROWS = 1280
assert TILE_ROWS >= TOP_K * TILE + N_EXPERTS * (SEG_ALIGN - 1) and TILE_ROWS % LANES == 0


def _layer_norm(z, g, b):
    mu = jnp.mean(z, axis=-1, keepdims=True)
    zc = z - mu
    var = jnp.mean(zc * zc, axis=-1, keepdims=True)
    return zc * lax.rsqrt(var + LN_EPS) * g + b


def _sigmoid(x):
    return 0.5 + 0.5 * jnp.tanh(0.5 * x)


def _softplus(x):
    return jnp.maximum(x, 0.0) + jnp.log1p(jnp.exp(-jnp.abs(x)))


def _gelu_tanh(x):
    c = np.float32(np.sqrt(2.0 / np.pi))
    return 0.5 * x * (1.0 + jnp.tanh(c * (x + 0.044715 * (x * x * x))))


def _rglru_gates(xc, wa_ref, ba_ref, wi_ref, bi_ref, lam_ref):
    xcb = xc.astype(BF16)
    half = LRU_WIDTH // 2
    pre_a = jnp.concatenate(
        [jnp.dot(xcb[:, :half], wa_ref[0], preferred_element_type=F32),
         jnp.dot(xcb[:, half:], wa_ref[1], preferred_element_type=F32)], axis=1)
    pre_i = jnp.concatenate(
        [jnp.dot(xcb[:, :half], wi_ref[0], preferred_element_type=F32),
         jnp.dot(xcb[:, half:], wi_ref[1], preferred_element_type=F32)], axis=1)
    r = _sigmoid(pre_a + ba_ref[...])
    gi = _sigmoid(pre_i + bi_ref[...])
    log_a = (-RG_C) * r * _softplus(-lam_ref[...])
    a = jnp.exp(log_a)
    t = jnp.tanh(log_a)
    mult = jnp.sqrt(jnp.maximum(-2.0 * t / (1.0 - t), 0.0))
    return a, mult * (gi * xc)


def _route(h1, rw_ref, rb_ref):
    n = h1.shape[0]
    logits = lax.dot_general(rw_ref[...], h1.astype(BF16), (((1,), (1,)), ((), ())),
                             preferred_element_type=F32) + rb_ref[...]
    eidx = lax.broadcasted_iota(jnp.int32, (N_EXPERTS, n), 0)
    vals = logits
    top_v, top_e = [], []
    for _ in range(TOP_K):
        m = jnp.max(vals, axis=0, keepdims=True)
        idx = jnp.min(jnp.where(vals == m, eidx, N_EXPERTS), axis=0, keepdims=True)
        top_v.append(m)
        top_e.append(idx)
        vals = jnp.where(eidx == idx, -jnp.inf, vals)
    ex = [jnp.exp(v - top_v[0]) for v in top_v]
    den = ex[0] + ex[1] + ex[2] + ex[3]
    gates = jnp.concatenate([e / den for e in ex], axis=0)
    te = jnp.concatenate(top_e, axis=0)
    mem = jnp.zeros((N_EXPERTS, n), jnp.int32)
    for idx in top_e:
        mem = mem + jnp.where(eidx == idx, 1, 0)
    return te, gates, jnp.sum(mem, axis=1, keepdims=True)


def _compact_tile(h1, te, cnt):
    eidx = lax.broadcasted_iota(jnp.int32, (N_EXPERTS, TILE), 0)
    mem = jnp.zeros((N_EXPERTS, TILE), F32)
    for k in range(TOP_K):
        mem = mem + jnp.where(eidx == te[k:k + 1, :], 1.0, 0.0)
    tr = lax.broadcasted_iota(jnp.int32, (TILE, TILE), 0)
    tc = lax.broadcasted_iota(jnp.int32, (TILE, TILE), 1)
    before = jnp.where(tr < tc, 1.0, 0.0).astype(BF16)
    pos = jnp.dot(mem.astype(BF16), before, preferred_element_type=F32)
    seg_rows = ((cnt + (SEG_ALIGN - 1)) // SEG_ALIGN * SEG_ALIGN).astype(F32)
    er = lax.broadcasted_iota(jnp.int32, (N_EXPERTS, N_EXPERTS), 0)
    ec = lax.broadcasted_iota(jnp.int32, (N_EXPERTS, N_EXPERTS), 1)
    earlier = jnp.where(ec < er, 1.0, 0.0).astype(BF16)
    seg_start = jnp.dot(earlier, jnp.broadcast_to(seg_rows, (N_EXPERTS, LANES)).astype(BF16),
                        preferred_element_type=F32)[:, 0:1]
    base = (pos + seg_start).astype(jnp.int32)
    dests = []
    for k in range(TOP_K):
        d = jnp.sum(jnp.where(eidx == te[k:k + 1, :], base, 0), axis=0, keepdims=True)
        dests.append(jnp.where(te[k:k + 1, :] >= 0, d, -1))
    rid = lax.broadcasted_iota(jnp.int16, (TILE_ROWS, TILE), 0)
    onehot = jnp.zeros((TILE_ROWS, TILE), BF16)
    for k in range(TOP_K):
        onehot = jnp.where(rid == dests[k].astype(jnp.int16), jnp.ones((), BF16), onehot)
    picked = jnp.dot(onehot, h1.astype(BF16), preferred_element_type=F32)
    return picked, jnp.concatenate(dests, axis=0)


def _post_mix(x, mix_in, w_out_ref, b_out_ref, g_ref, b_ref):
    mix = jnp.dot(mix_in.astype(BF16), w_out_ref[...], preferred_element_type=F32) + b_out_ref[...]
    return _layer_norm(DN_ALPHA * x + mix, g_ref[...], b_ref[...])


SEQ_PAIR = 2


def _interleave(chains):
    results = [None] * len(chains)
    live = list(range(len(chains)))
    while live:
        for k in list(live):
            try:
                next(chains[k])
            except StopIteration as stop:
                results[k] = stop.value
                live.remove(k)
    return results


def _bias_tables(bias_s):
    row2 = lax.broadcasted_iota(jnp.int32, (2 * WINDOW, 2 * WINDOW), 0)
    col2 = lax.broadcasted_iota(jnp.int32, (2 * WINDOW, 2 * WINDOW), 1)
    dist = (row2 & (WINDOW - 1)) - col2 + WINDOW
    valid = (dist >= 0) & (dist <= WINDOW)
    distf = dist.astype(F32)
    for g in range(HEAD_GROUPS):
        slope = jnp.where(row2 >= WINDOW, 2.0 ** -(g + HEAD_GROUPS + 1), 2.0 ** -(g + 1))
        bias = jnp.where(valid, (-LOG2E) * (slope * distf), NEG_INF)
        bias_s[g, 0] = bias
        bias_s[g, 1] = jnp.where(col2 >= WINDOW, bias, NEG_INF)


def _mixer_prompt_kernel(sinks_ref, x_ref, *rest):
    wts = rest[:15]
    h1_ref, xt_ref, dest_ref, tg_ref, cnt_ref, wk_ref, wv_ref, lh_ref, cv_ref = rest[15:24]
    kext, vext, uext, hcar, attn_buf, bias_s = rest[24:]
    j = pl.program_id(1)
    nj = pl.num_programs(1)
    R = TILE
    one = lambda ref, bb: ref.at[pl.ds(bb, 1)]

    @pl.when(j == 0)
    def _():
        for bb in range(SEQ_PAIR):
            kext[bb, 0:WINDOW, :] = jnp.zeros((WINDOW, KV_COLS), F32)
            vext[bb, 0:WINDOW, :] = jnp.zeros((WINDOW, KV_COLS), F32)
            uext[bb, 0:SUBLANES, :] = jnp.zeros((SUBLANES, LRU_WIDTH), F32)
            hcar[bb] = jnp.zeros((1, LRU_WIDTH), F32)

    @pl.when((pl.program_id(0) == 0) & (j == 0))
    def _():
        _bias_tables(bias_s)

    per_tile = lambda ref, bb: ref.at[0, pl.ds(bb, 1)]
    h_last = _interleave([
        _mixer_tile(j, sinks_ref, one(x_ref, bb), wts, one(h1_ref, bb), one(xt_ref, bb),
                    per_tile(dest_ref, bb), per_tile(tg_ref, bb), per_tile(cnt_ref, bb),
                    kext.at[bb], vext.at[bb], uext.at[bb], hcar.at[bb], attn_buf.at[bb], bias_s)
        for bb in range(SEQ_PAIR)])

    @pl.when(j == nj - 1)
    def _():
        for bb in range(SEQ_PAIR):
            wk_ref[bb] = kext[bb, R:R + WINDOW, :]
            wv_ref[bb] = vext[bb, R:R + WINDOW, :]
            lh_ref[bb] = h_last[bb]
            cv_ref[bb] = uext[bb, SUBLANES + R - (CONV_WIDTH - 1):SUBLANES + R, :]

    for bb in range(SEQ_PAIR):
        kext[bb, 0:WINDOW, :] = kext[bb, R:R + WINDOW, :]
        vext[bb, 0:WINDOW, :] = vext[bb, R:R + WINDOW, :]
        uext[bb, 0:SUBLANES, :] = uext[bb, R:R + SUBLANES, :]


def _mixer_tile(j, sinks_ref, x_ref, wts, h1_ref, xt_ref, dest_ref, tg_ref, cnt_ref,
                kext, vext, uext, hcar, attn_buf, bias_s):
    (w_in_ref, b_in_ref, conv_w_ref, conv_b_ref, wa_ref, ba_ref, wi_ref, bi_ref, lam_ref,
     w_out_ref, b_out_ref, ln_g_ref, ln_b_ref, rw_ref, rb_ref) = wts
    R = TILE
    x = x_ref[0]
    proj = jnp.dot(x.astype(BF16), w_in_ref[...], preferred_element_type=F32) + b_in_ref[...]
    q = proj[:, :Q_COLS]
    kext[WINDOW:WINDOW + R, :] = proj[:, Q_COLS:Q_COLS + KV_COLS]
    vext[WINDOW:WINDOW + R, :] = proj[:, Q_COLS + KV_COLS:Q_COLS + 2 * KV_COLS]
    uext[SUBLANES:SUBLANES + R, :] = proj[:, Q_COLS + 2 * KV_COLS:Q_COLS + 2 * KV_COLS + LRU_WIDTH]
    ug = proj[:, Q_COLS + 2 * KV_COLS + LRU_WIDTH:]
    yield

    upper = lax.broadcasted_iota(jnp.int32, (2 * WINDOW, 1), 0) >= WINDOW
    lo_lane = lax.broadcasted_iota(jnp.int32, (WINDOW, LANES), 1) < HEAD_DIM
    qs = q * (HEAD_DIM ** -0.5 * LOG2E)
    for s in range(R // WINDOW):
        kk = kext[s * WINDOW:s * WINDOW + 2 * WINDOW, :].astype(BF16)
        vv = vext[s * WINDOW:s * WINDOW + 2 * WINDOW, :].astype(BF16)
        table = jnp.where(j == 0, 1, 0) if s == 0 else 0
        for g in range(HEAD_GROUPS):
            qg = qs[s * WINDOW:(s + 1) * WINDOW, g * LANES:(g + 1) * LANES]
            q2 = jnp.concatenate([jnp.where(lo_lane, qg, 0.0), jnp.where(lo_lane, 0.0, qg)], axis=0)
            sc = lax.dot_general(q2.astype(BF16), kk, (((1,), (1,)), ((), ())),
                                 preferred_element_type=F32) + bias_s[g, table]
            sink = jnp.where(upper, sinks_ref[g + HEAD_GROUPS], sinks_ref[g]) * LOG2E
            m = jnp.maximum(jnp.max(sc, axis=-1, keepdims=True), sink)
            p = jnp.exp2(sc - m)
            den = jnp.sum(p, axis=-1, keepdims=True) + jnp.exp2(sink - m)
            o = jnp.dot(p.astype(BF16), vv, preferred_element_type=F32) / den
            attn_buf[s * WINDOW:(s + 1) * WINDOW, g * LANES:(g + 1) * LANES] = jnp.where(
                lo_lane, o[:WINDOW], o[WINDOW:])
            yield

    xc = conv_b_ref[...]
    for tap in range(CONV_WIDTH):
        off = SUBLANES - (CONV_WIDTH - 1) + tap
        xc = xc + uext[off:off + R, :] * conv_w_ref[tap:tap + 1, :]
    a, bv = _rglru_gates(xc, wa_ref, ba_ref, wi_ref, bi_ref, lam_ref)
    yield
    groups = R // SUBLANES
    a3 = a.reshape(groups, SUBLANES, LRU_WIDTH)
    b3 = bv.reshape(groups, SUBLANES, LRU_WIDTH)
    sub = lax.broadcasted_iota(jnp.int32, (groups, SUBLANES, LRU_WIDTH), 1)
    d = 1
    while d < SUBLANES:
        keep = sub >= d
        a_prev = jnp.where(keep, pltpu.roll(a3, d, 1), 1.0)
        b_prev = jnp.where(keep, pltpu.roll(b3, d, 1), 0.0)
        b3 = a3 * b_prev + b3
        a3 = a3 * a_prev
        d *= 2
    h_prev = hcar[...]
    h_groups = []
    for c in range(groups):
        hc = a3[c] * h_prev + b3[c]
        h_groups.append(hc)
        h_prev = hc[SUBLANES - 1:SUBLANES, :]
    h = jnp.concatenate(h_groups, axis=0)
    hcar[...] = h_prev
    rnn = h * _gelu_tanh(ug)
    yield

    mix_in = jnp.concatenate([attn_buf[...], rnn], axis=1)
    h1 = _post_mix(x, mix_in, w_out_ref, b_out_ref, ln_g_ref, ln_b_ref)
    h1_ref[0] = h1
    yield
    te, tg, cnt = _route(h1, rw_ref, rb_ref)
    tg_ref[0] = tg
    cnt_ref[0] = cnt
    yield
    picked, dest = _compact_tile(h1, te, cnt)
    xt_ref[0] = picked
    dest_ref[0] = dest
    return h_prev


def _const_spec(shape):
    return pl.BlockSpec(shape, lambda *_: (0,) * len(shape))


def _mixer_prompt(x, sinks, wts):
    B, S, _ = x.shape
    assert B % SEQ_PAIR == 0 and S % TILE == 0
    nj = S // TILE
    P = SEQ_PAIR
    tile_idx = lambda b, j, *_: (j, b, 0, 0)
    batch_idx = lambda b, j, *_: (b, 0, 0)
    in_specs = [pl.BlockSpec((P, TILE, D_MODEL), lambda b, j, *_: (b, j, 0))]
    in_specs += [_const_spec(w.shape) for w in wts]
    out_shape = (
        jax.ShapeDtypeStruct((B, S, D_MODEL), F32),
        jax.ShapeDtypeStruct((B * nj, TILE_ROWS, D_MODEL), F32),
        jax.ShapeDtypeStruct((nj, B, TOP_K, TILE), jnp.int32),
        jax.ShapeDtypeStruct((nj, B, TOP_K, TILE), F32),
        jax.ShapeDtypeStruct((nj, B, N_EXPERTS, 1), jnp.int32),
        jax.ShapeDtypeStruct((B, WINDOW, KV_COLS), F32),
        jax.ShapeDtypeStruct((B, WINDOW, KV_COLS), F32),
        jax.ShapeDtypeStruct((B, 1, LRU_WIDTH), F32),
        jax.ShapeDtypeStruct((B, CONV_WIDTH - 1, LRU_WIDTH), F32),
    )
    out_specs = (
        pl.BlockSpec((P, TILE, D_MODEL), lambda b, j, *_: (b, j, 0)),
        pl.BlockSpec((P, TILE_ROWS, D_MODEL), lambda b, j, *_: (j * (B // P) + b, 0, 0)),
        pl.BlockSpec((1, P, TOP_K, TILE), tile_idx),
        pl.BlockSpec((1, P, TOP_K, TILE), tile_idx),
        pl.BlockSpec((1, P, N_EXPERTS, 1), tile_idx),
        pl.BlockSpec((P, WINDOW, KV_COLS), batch_idx),
        pl.BlockSpec((P, WINDOW, KV_COLS), batch_idx),
        pl.BlockSpec((P, 1, LRU_WIDTH), batch_idx),
        pl.BlockSpec((P, CONV_WIDTH - 1, LRU_WIDTH), batch_idx),
    )
    scratch = [
        pltpu.VMEM((P, TILE + WINDOW, KV_COLS), F32),
        pltpu.VMEM((P, TILE + WINDOW, KV_COLS), F32),
        pltpu.VMEM((P, TILE + SUBLANES, LRU_WIDTH), F32),
        pltpu.VMEM((P, 1, LRU_WIDTH), F32),
        pltpu.VMEM((P, TILE, Q_COLS), F32),
        pltpu.VMEM((HEAD_GROUPS, 2, 2 * WINDOW, 2 * WINDOW), F32),
    ]
    return pl.pallas_call(
        _mixer_prompt_kernel,
        grid_spec=pltpu.PrefetchScalarGridSpec(
            num_scalar_prefetch=1, grid=(B // P, nj), in_specs=in_specs, out_specs=out_specs,
            scratch_shapes=scratch),
        out_shape=out_shape,
        compiler_params=pltpu.CompilerParams(
            dimension_semantics=("arbitrary", "arbitrary"), vmem_limit_bytes=VMEM_LIMIT),
        name="mixer_prompt",
    )(sinks, x, *wts)


SEQ_CHUNK = 32


def _mixer_sample_kernel(sinks_ref, x_ref, ck_ref, cv_ref, h0_ref, cprev_ref,
                         w_in_ref, b_in_ref, conv_w_ref, conv_b_ref,
                         wa_ref, ba_ref, wi_ref, bi_ref, lam_ref, w_out_ref, b_out_ref,
                         ln_g_ref, ln_b_ref, rw_ref, rb_ref,
                         h1_ref, xt_ref, dest_ref, tg_ref, cnt_ref, wk_ref, wv_ref, lh_ref, cnew_ref,
                         proj_s, attn_s, attn_c):
    c = pl.program_id(0)
    nc = pl.num_programs(0)
    nseq = x_ref.shape[0]

    @pl.when(c == 0)
    def _():
        proj_s[...] = jnp.dot(x_ref[...].astype(BF16), w_in_ref[...],
                              preferred_element_type=F32) + b_in_ref[...]

    sub = lax.broadcasted_iota(jnp.int32, (N_Q_HEADS, LANES), 0)
    lane = lax.broadcasted_iota(jnp.int32, (N_Q_HEADS, LANES), 1)
    own_half = (lane < HEAD_DIM) == (sub < HEAD_GROUPS)
    sub1 = sub[:, 0:1]
    slope = jnp.zeros((N_Q_HEADS, 1), F32)
    sink = jnp.zeros((N_Q_HEADS, 1), F32)
    for hd in range(N_Q_HEADS):
        slope = jnp.where(sub1 == hd, 2.0 ** -(hd + 1), slope)
        sink = jnp.where(sub1 == hd, sinks_ref[hd], sink)
    dist = (WINDOW - lax.broadcasted_iota(jnp.int32, (1, WINDOW), 1)).astype(F32)
    lo_row = lax.broadcasted_iota(jnp.int32, (1, LANES), 1) < HEAD_DIM
    scale = HEAD_DIM ** -0.5

    def one_sequence(i):
        b = c * SEQ_CHUNK + i
        prow = proj_s[pl.ds(b, 1), :]
        q8 = jnp.zeros((N_Q_HEADS, LANES), F32)
        for g in range(HEAD_GROUPS):
            qg = jnp.broadcast_to(prow[:, g * LANES:(g + 1) * LANES], (N_Q_HEADS, LANES))
            q8 = jnp.where(((sub & (HEAD_GROUPS - 1)) == g) & own_half, qg, q8)
        k_new = prow[:, Q_COLS:Q_COLS + KV_COLS]
        v_new = prow[:, Q_COLS + KV_COLS:Q_COLS + 2 * KV_COLS]
        kb = ck_ref[i]
        vb = cv_ref[i]
        sc = lax.dot_general(q8.astype(BF16), kb.astype(BF16), (((1,), (1,)), ((), ())),
                             preferred_element_type=F32) * scale - slope * dist
        sc_new = jnp.sum(q8 * k_new, axis=-1, keepdims=True) * scale
        yield
        m = jnp.maximum(jnp.maximum(jnp.max(sc, axis=-1, keepdims=True), sc_new), sink)
        p = jnp.exp(sc - m)
        p_new = jnp.exp(sc_new - m)
        den = jnp.sum(p, axis=-1, keepdims=True) + p_new + jnp.exp(sink - m)
        yield
        o = (jnp.dot(p.astype(BF16), vb.astype(BF16), preferred_element_type=F32)
             + p_new * v_new) / den
        yield
        for g in range(HEAD_GROUPS):
            attn_c[i:i + 1, g * LANES:(g + 1) * LANES] = jnp.where(
                lo_row, o[g:g + 1, :], o[g + HEAD_GROUPS:g + HEAD_GROUPS + 1, :])
        wk_ref[i, 0:WINDOW - 1, :] = ck_ref[i, 1:WINDOW, :]
        wk_ref[i, WINDOW - 1:WINDOW, :] = k_new
        wv_ref[i, 0:WINDOW - 1, :] = cv_ref[i, 1:WINDOW, :]
        wv_ref[i, WINDOW - 1:WINDOW, :] = v_new

    _interleave([one_sequence(i) for i in range(SEQ_CHUNK)])
    attn_s[pl.ds(pl.multiple_of(c * SEQ_CHUNK, SEQ_CHUNK), SEQ_CHUNK), :] = attn_c[...]

    @pl.when(c == nc - 1)
    def _():
        x = x_ref[...]
        ux = proj_s[:, Q_COLS + 2 * KV_COLS:Q_COLS + 2 * KV_COLS + LRU_WIDTH]
        ug = proj_s[:, Q_COLS + 2 * KV_COLS + LRU_WIDTH:]
        xc = conv_b_ref[...]
        for tap in range(CONV_WIDTH - 1):
            xc = xc + cprev_ref[tap] * conv_w_ref[tap:tap + 1, :]
        xc = xc + ux * conv_w_ref[CONV_WIDTH - 1:CONV_WIDTH, :]
        a, bv = _rglru_gates(xc, wa_ref, ba_ref, wi_ref, bi_ref, lam_ref)
        h = a * h0_ref[...] + bv
        rnn = h * _gelu_tanh(ug)
        mix_in = jnp.concatenate([attn_s[...], rnn], axis=1)
        h1 = _post_mix(x, mix_in, w_out_ref, b_out_ref, ln_g_ref, ln_b_ref)
        h1_ref[...] = h1
        te, tg, cnt = _route(h1, rw_ref, rb_ref)
        cnt_ref[0] = cnt
        tg_ref[0] = jnp.concatenate([tg, jnp.zeros((TOP_K, TILE - nseq), F32)], axis=1)
        te_tile = jnp.concatenate([te, jnp.full((TOP_K, TILE - nseq), -1, jnp.int32)], axis=1)
        h1_tile = jnp.concatenate([h1, jnp.zeros((TILE - nseq, D_MODEL), F32)], axis=0)
        picked, dest = _compact_tile(h1_tile, te_tile, cnt)
        xt_ref[0] = picked
        dest_ref[0] = dest
        lh_ref[...] = h
        for tap in range(1, CONV_WIDTH - 1):
            cnew_ref[tap - 1] = cprev_ref[tap]
        cnew_ref[CONV_WIDTH - 2] = ux


def _mixer_sample(x, ck, cv, h0, cprev, sinks, wts):
    nseq = x.shape[0]
    assert nseq % SEQ_CHUNK == 0 and nseq <= TILE and nseq % LANES == 0
    nc = nseq // SEQ_CHUNK
    chunk_idx = lambda c, *_: (c, 0, 0)
    in_specs = [
        _const_spec((nseq, D_MODEL)),
        pl.BlockSpec((SEQ_CHUNK, WINDOW, KV_COLS), chunk_idx),
        pl.BlockSpec((SEQ_CHUNK, WINDOW, KV_COLS), chunk_idx),
        _const_spec((nseq, LRU_WIDTH)),
        _const_spec((CONV_WIDTH - 1, nseq, LRU_WIDTH)),
    ] + [_const_spec(w.shape) for w in wts]
    out_shape = (
        jax.ShapeDtypeStruct((nseq, D_MODEL), F32),
        jax.ShapeDtypeStruct((1, TILE_ROWS, D_MODEL), F32),
        jax.ShapeDtypeStruct((1, TOP_K, TILE), jnp.int32),
        jax.ShapeDtypeStruct((1, TOP_K, TILE), F32),
        jax.ShapeDtypeStruct((1, N_EXPERTS, 1), jnp.int32),
        jax.ShapeDtypeStruct((nseq, WINDOW, KV_COLS), F32),
        jax.ShapeDtypeStruct((nseq, WINDOW, KV_COLS), F32),
        jax.ShapeDtypeStruct((nseq, LRU_WIDTH), F32),
        jax.ShapeDtypeStruct((CONV_WIDTH - 1, nseq, LRU_WIDTH), F32),
    )
    out_specs = (
        _const_spec((nseq, D_MODEL)),
        _const_spec((1, TILE_ROWS, D_MODEL)),
        _const_spec((1, TOP_K, TILE)),
        _const_spec((1, TOP_K, TILE)),
        _const_spec((1, N_EXPERTS, 1)),
        pl.BlockSpec((SEQ_CHUNK, WINDOW, KV_COLS), chunk_idx),
        pl.BlockSpec((SEQ_CHUNK, WINDOW, KV_COLS), chunk_idx),
        _const_spec((nseq, LRU_WIDTH)),
        _const_spec((CONV_WIDTH - 1, nseq, LRU_WIDTH)),
    )
    scratch = [pltpu.VMEM((nseq, D_IN), F32), pltpu.VMEM((nseq, Q_COLS), F32),
               pltpu.VMEM((SEQ_CHUNK, Q_COLS), F32)]
    return pl.pallas_call(
        _mixer_sample_kernel,
        grid_spec=pltpu.PrefetchScalarGridSpec(
            num_scalar_prefetch=1, grid=(nc,), in_specs=in_specs, out_specs=out_specs,
            scratch_shapes=scratch),
        out_shape=out_shape,
        compiler_params=pltpu.CompilerParams(
            dimension_semantics=("arbitrary",), vmem_limit_bytes=VMEM_LIMIT),
        name="mixer_sample",
    )(sinks, x, ck, cv, h0, cprev, *wts)


def _grouped(rows):
    assert rows % SEG_ALIGN == 0
    return (rows // SEG_ALIGN, SEG_ALIGN, D_MODEL)


BLOCK_GROUPS = ROW_BLOCK // SEG_ALIGN
TILE_GROUPS = TILE_ROWS // SEG_ALIGN


def _experts_kernel(be_ref, nxt_ref, nb_ref, tlo_ref, thi_ref, valid_ref,
                    seg_len_ref, seg_src_ref, seg_off_ref,
                    xtp_ref, xts_ref, wgu_ref, bgu_ref, wdn_ref, bdn_ref, yb_ref,
                    xbuf, wgu_f, wdn_f, wgu_s, wdn_s, wsem, xsem, *, n_tiles, n_prompt_tiles):
    b = pl.program_id(0)
    slot = b % 2

    def weight_copies(e):
        return (pltpu.make_async_copy(wgu_ref.at[e], wgu_f, wsem.at[0]),
                pltpu.make_async_copy(wdn_ref.at[e], wdn_f, wsem.at[1]))

    def start_gather(blk, s):
        lo_b = blk * BLOCK_GROUPS
        first = be_ref[blk] * n_tiles

        t_lo = tlo_ref[blk]
        t_hi = thi_ref[blk]

        def piece(t):
            off = seg_off_ref[first + t]
            lo = jnp.maximum(off, lo_b)
            n = jnp.minimum(off + seg_len_ref[first + t], lo_b + BLOCK_GROUPS) - lo

            src = pl.ds(seg_src_ref[first + t] + lo, n)
            dst = xbuf.at[s, pl.ds(lo - lo_b, n)]
            wanted = (n > 0) & (t <= t_hi)

            @pl.when(wanted & (t < n_prompt_tiles))
            def _():
                pltpu.make_async_copy(xtp_ref.at[src], dst, xsem.at[s]).start()

            @pl.when(wanted & (t >= n_prompt_tiles))
            def _():
                pltpu.make_async_copy(xts_ref.at[src], dst, xsem.at[s]).start()

        def two_pieces(i, carry):
            piece(t_lo + 2 * i)
            piece(t_lo + 2 * i + 1)
            return carry

        lax.fori_loop(0, (t_hi - t_lo + 2) // 2, two_pieces, 0)

    @pl.when(b == 0)
    def _():
        xbuf[...] = jnp.zeros(xbuf.shape, F32)
        for cp in weight_copies(be_ref[0]):
            cp.start()
        start_gather(b, slot)

    @pl.when(b + 1 < nb_ref[0])
    def _():
        start_gather(b + 1, 1 - slot)

    @pl.when(b < nb_ref[0])
    def _():
        n = valid_ref[b]
        pltpu.make_async_copy(xtp_ref.at[pl.ds(0, n)], xbuf.at[slot, pl.ds(0, n)], xsem.at[slot]).wait()

        @pl.when((b == 0) | (be_ref[b] != be_ref[jnp.maximum(b - 1, 0)]))
        def _():
            for cp in weight_copies(be_ref[b]):
                cp.wait()
            wgu_s[...] = wgu_f[...].astype(BF16)
            wdn_s[...] = wdn_f[...].astype(BF16)

            @pl.when(nxt_ref[b] >= 0)
            def _():
                for cp in weight_copies(nxt_ref[b]):
                    cp.start()

        def expert_rows(rows):
            x = xbuf[slot, pl.ds(0, rows // SEG_ALIGN)].reshape(rows, D_MODEL).astype(BF16)
            hgu = jnp.dot(x, wgu_s[...], preferred_element_type=F32) + bgu_ref[0]
            glu = jnp.minimum(hgu[:, :D_FF], SWIGLU_LIMIT)
            lin = jnp.clip(hgu[:, D_FF:], -SWIGLU_LIMIT, SWIGLU_LIMIT)
            act = glu * jax.nn.sigmoid(SWIGLU_ALPHA * glu) * (lin + 1.0)
            yb_ref[0:rows, :] = jnp.dot(act.astype(BF16), wdn_s[...], preferred_element_type=F32) + bdn_ref[0]

        quarter = ROW_BLOCK // 4
        quarters = (n * SEG_ALIGN + quarter - 1) // quarter
        for used in range(1, 5):
            @pl.when(quarters == used)
            def _(rows=used * quarter):
                expert_rows(rows)
                if rows < ROW_BLOCK:
                    yb_ref[rows:, :] = jnp.zeros((ROW_BLOCK - rows, D_MODEL), F32)

    @pl.when(b >= nb_ref[0])
    def _():
        yb_ref[...] = jnp.zeros(yb_ref.shape, F32)


def _experts(sched, segs, xt_prompt, xt_sample, w_gu, b_gu, w_dn, b_dn, n_rows):
    n_blocks = n_rows // ROW_BLOCK
    n_prompt_tiles = xt_prompt.shape[0] // TILE_GROUPS
    n_tiles = n_prompt_tiles + xt_sample.shape[0] // TILE_GROUPS
    n_prefetch = len(sched) + len(segs)
    exp_idx = lambda b, be, *_: (be[b], 0, 0)
    in_specs = [
        pl.BlockSpec(memory_space=pl.ANY),
        pl.BlockSpec(memory_space=pl.ANY),
        pl.BlockSpec(memory_space=pl.ANY),
        pl.BlockSpec((1, 1, 2 * D_FF), exp_idx),
        pl.BlockSpec(memory_space=pl.ANY),
        pl.BlockSpec((1, 1, D_MODEL), exp_idx),
    ]
    scratch = [pltpu.VMEM((2,) + _grouped(ROW_BLOCK), F32),
               pltpu.VMEM((D_MODEL, 2 * D_FF), F32), pltpu.VMEM((D_FF, D_MODEL), F32),
               pltpu.VMEM((D_MODEL, 2 * D_FF), BF16), pltpu.VMEM((D_FF, D_MODEL), BF16),
               pltpu.SemaphoreType.DMA((2,)), pltpu.SemaphoreType.DMA((2,))]
    return pl.pallas_call(
        functools.partial(_experts_kernel, n_tiles=n_tiles, n_prompt_tiles=n_prompt_tiles),
        grid_spec=pltpu.PrefetchScalarGridSpec(
            num_scalar_prefetch=n_prefetch, grid=(n_blocks,), in_specs=in_specs,
            out_specs=pl.BlockSpec((ROW_BLOCK, D_MODEL), lambda b, *_: (b, 0)),
            scratch_shapes=scratch),
        out_shape=jax.ShapeDtypeStruct((n_rows, D_MODEL), F32),
        compiler_params=pltpu.CompilerParams(
            dimension_semantics=("arbitrary",), vmem_limit_bytes=VMEM_LIMIT),
        name="moe_experts",
    )(*sched, *segs, xt_prompt, xt_sample, w_gu, b_gu[:, None, :], w_dn, b_dn[:, None, :])


def _tile_tokens(i, n_prompt_tiles, hp_ref, hs_ref):
    hs = hs_ref[...]
    hs_tile = jnp.concatenate([hs, jnp.zeros((TILE - hs.shape[0], D_MODEL), F32)], axis=0)
    return jnp.where(i < n_prompt_tiles, hp_ref[...], hs_tile)


COMBINE_SLOTS = 3


def _combine_kernel(seg_len_ref, seg_start_ref, seg_off_ref, tile_groups_ref,
                    hp_ref, hs_ref, dest_ref, gate_ref, g_ref, b_ref, yb_ref,
                    yp_ref, ys_ref, ybuf, sem, *, n_prompt_tiles):
    i = pl.program_id(0)
    n = pl.num_programs(0)
    slot = i % COMBINE_SLOTS

    def start_gather(tile):
        s = tile % COMBINE_SLOTS

        def segment(e, carry):
            idx = tile * N_EXPERTS + e
            groups = seg_len_ref[idx]

            @pl.when(groups > 0)
            def _():
                pltpu.make_async_copy(yb_ref.at[pl.ds(seg_off_ref[idx], groups)],
                                      ybuf.at[s, pl.ds(seg_start_ref[idx], groups)], sem.at[s]).start()
            return carry
        lax.fori_loop(0, N_EXPERTS, segment, 0)

    @pl.when(i == 0)
    def _():
        ybuf[...] = jnp.zeros(ybuf.shape, F32)
        for ahead in range(COMBINE_SLOTS - 1):
            @pl.when(ahead < n)
            def _():
                start_gather(i + ahead)

    @pl.when(i + COMBINE_SLOTS - 1 < n)
    def _():
        start_gather(i + COMBINE_SLOTS - 1)

    total = tile_groups_ref[i]

    @pl.when(total > 0)
    def _():
        pltpu.make_async_copy(yb_ref.at[pl.ds(0, total)], ybuf.at[slot, pl.ds(0, total)], sem.at[slot]).wait()

    dest = dest_ref[0]
    gate = gate_ref[0]
    rid = lax.broadcasted_iota(jnp.int32, (TILE_ROWS, TILE), 0)
    weights = jnp.zeros((TILE_ROWS, TILE), F32)
    for k in range(TOP_K):
        weights = jnp.where(rid == dest[k:k + 1, :], gate[k:k + 1, :], weights)
    rows = ybuf[slot].reshape(TILE_ROWS, D_MODEL).astype(BF16)
    ff = lax.dot_general(weights.astype(BF16), rows, (((0,), (0,)), ((), ())),
                         preferred_element_type=F32)
    h1 = _tile_tokens(i, n_prompt_tiles, hp_ref, hs_ref)
    y = _layer_norm(DN_ALPHA * h1 + ff, g_ref[...], b_ref[...])

    @pl.when(i < n_prompt_tiles)
    def _():
        yp_ref[...] = y

    @pl.when(i >= n_prompt_tiles)
    def _():
        ys_ref[...] = y[:ys_ref.shape[0]]


def _combine(segs, tile_groups, n_seq, h1p, h1s, dest, gate, ln_g, ln_b, yb):
    n_tiles = dest.shape[0]
    n_prompt_tiles = h1p.shape[0] // TILE
    blocks_per_seq = n_prompt_tiles // n_seq

    def prompt_idx(i, *_):
        t = jnp.minimum(i, n_prompt_tiles - 1)
        return ((t % n_seq) * blocks_per_seq + t // n_seq, 0)
    in_specs = [
        pl.BlockSpec((TILE, D_MODEL), prompt_idx),
        _const_spec(h1s.shape),
        pl.BlockSpec((1, TOP_K, TILE), lambda i, *_: (i, 0, 0)),
        pl.BlockSpec((1, TOP_K, TILE), lambda i, *_: (i, 0, 0)),
        _const_spec(ln_g.shape),
        _const_spec(ln_b.shape),
        pl.BlockSpec(memory_space=pl.ANY),
    ]
    out_shape = (jax.ShapeDtypeStruct(h1p.shape, F32), jax.ShapeDtypeStruct(h1s.shape, F32))
    out_specs = (pl.BlockSpec((TILE, D_MODEL), prompt_idx), _const_spec(h1s.shape))
    return pl.pallas_call(
        functools.partial(_combine_kernel, n_prompt_tiles=n_prompt_tiles),
        grid_spec=pltpu.PrefetchScalarGridSpec(
            num_scalar_prefetch=len(segs) + 1, grid=(n_tiles,), in_specs=in_specs, out_specs=out_specs,
            scratch_shapes=[pltpu.VMEM((COMBINE_SLOTS,) + _grouped(TILE_ROWS), F32),
                            pltpu.SemaphoreType.DMA((COMBINE_SLOTS,))]),
        out_shape=out_shape,
        compiler_params=pltpu.CompilerParams(
            dimension_semantics=("arbitrary",), vmem_limit_bytes=VMEM_LIMIT),
        name="moe_combine",
    )(*segs, tile_groups, h1p, h1s, dest, gate, ln_g, ln_b, yb.reshape(_grouped(yb.shape[0])))


def _moe_layout(cnt, n_prompt_tiles):
    n_tiles = cnt.shape[0]
    i32 = lambda a: a.astype(jnp.int32)
    seg_len = (cnt + SEG_ALIGN - 1) // SEG_ALIGN
    seg_start = jnp.cumsum(seg_len, axis=1) - seg_len
    exp_len = jnp.sum(seg_len, axis=0)
    exp_blocks = (exp_len + BLOCK_GROUPS - 1) // BLOCK_GROUPS
    blocks_end = jnp.cumsum(exp_blocks)
    first_block = blocks_end - exp_blocks
    seg_off = first_block[None, :] * BLOCK_GROUPS + jnp.cumsum(seg_len, axis=0) - seg_len
    max_rows = n_tiles * (TOP_K * TILE + N_EXPERTS * (SEG_ALIGN - 1)) + N_EXPERTS * (ROW_BLOCK - SEG_ALIGN)
    n_blocks = -(-max_rows // ROW_BLOCK)
    n_used = blocks_end[-1]
    blk = jnp.minimum(jnp.arange(n_blocks, dtype=jnp.int32), n_used - 1)
    eid = jnp.arange(N_EXPERTS, dtype=jnp.int32)
    block_expert = jnp.minimum(jnp.sum(blocks_end[None, :] <= blk[:, None], axis=1), N_EXPERTS - 1)
    own = block_expert[:, None] == eid[None, :]
    pick = lambda per_expert: jnp.sum(jnp.where(own, per_expert[None, :], 0), axis=1)
    later_used = (eid[None, :] > eid[:, None]) & (exp_blocks[None, :] > 0)
    next_of = jnp.min(jnp.where(later_used, eid[None, :], N_EXPERTS), axis=1)
    next_expert = pick(jnp.where(next_of < N_EXPERTS, next_of, -1))
    lo = blk * BLOCK_GROUPS
    off_b = jnp.sum(jnp.where(own[:, None, :], seg_off[None, :, :], 0), axis=2)
    len_b = jnp.sum(jnp.where(own[:, None, :], seg_len[None, :, :], 0), axis=2)
    tile_lo = jnp.sum(off_b + len_b <= lo[:, None], axis=1)
    tile_hi = jnp.sum(off_b < lo[:, None] + BLOCK_GROUPS, axis=1) - 1
    valid = jnp.clip(pick(first_block * BLOCK_GROUPS + exp_len) - lo, 0, BLOCK_GROUPS)
    flat = lambda a: i32(a.reshape(-1))
    sched = (i32(block_expert), i32(next_expert), i32(n_used.reshape(1)), i32(tile_lo), i32(tile_hi),
             i32(valid))
    segs = (flat(seg_len), flat(seg_start), flat(seg_off))
    tile = jnp.arange(n_tiles, dtype=seg_off.dtype)
    tile_in_array = jnp.where(tile < n_prompt_tiles, tile, tile - n_prompt_tiles)
    seg_src = tile_in_array[:, None] * TILE_GROUPS + seg_start - seg_off
    pad1 = lambda a: jnp.concatenate([flat(a.T), jnp.zeros((1,), jnp.int32)])
    segs_by_expert = (pad1(seg_len), pad1(seg_src), pad1(seg_off))
    return sched, segs, segs_by_expert, i32(jnp.sum(seg_len, axis=1)), n_blocks * ROW_BLOCK


def _prep_weights(w_in, b_in, conv_w, conv_b, lru_w_a, lru_b_a, lru_w_i, lru_b_i, lru_lambda,
                  w_out, b_out, ln1_g, ln1_b, router_w, router_b):
    def regroup_heads(a):
        rest = a.shape[1:]
        q = a[:Q_COLS].reshape(N_KV_HEADS, HEAD_GROUPS, HEAD_DIM, *rest)
        q = jnp.swapaxes(q, 0, 1).reshape(Q_COLS, *rest)
        return jnp.concatenate([q, a[Q_COLS:]], axis=0)

    def diag_tiles(w):
        per = MXU_DIM // (LRU_WIDTH // LRU_BLOCKS)
        w4 = w.reshape(LRU_BLOCKS // per, per, LRU_WIDTH // LRU_BLOCKS, LRU_WIDTH // LRU_BLOCKS)
        t = jnp.einsum("taij,ab->taibj", w4, jnp.eye(per, dtype=w.dtype))
        return t.reshape(LRU_BLOCKS // per, MXU_DIM, MXU_DIM).astype(BF16)

    return (
        regroup_heads(w_in[0].T).T.astype(BF16), regroup_heads(b_in[0])[None],
        conv_w[0], conv_b[0][None],
        diag_tiles(lru_w_a[0]), lru_b_a[0].reshape(1, LRU_WIDTH),
        diag_tiles(lru_w_i[0]), lru_b_i[0].reshape(1, LRU_WIDTH),
        lru_lambda[0][None],
        regroup_heads(w_out[0]).astype(BF16), b_out[0][None],
        ln1_g[0][None], ln1_b[0][None],
        router_w[0].T.astype(BF16), router_b[0][:, None],
    )


def kernel(x_prompt, x_sample, cache_win_k, cache_win_v, state_lru_h, state_conv, w_in, b_in, attn_sinks, conv_w, conv_b, lru_w_a, lru_b_a, lru_w_i, lru_b_i, lru_lambda, w_out, b_out, ln1_g, ln1_b, router_w, router_b, w_gate_up, b_gate_up, w_down, b_down, ln2_g, ln2_b):
    assert w_in.shape[0] == 1, "single-layer step"
    B, S, _ = x_prompt.shape
    nseq = x_sample.shape[0]
    assert x_sample.shape[1] == 1 and S % TILE == 0
    wts = _prep_weights(w_in, b_in, conv_w, conv_b, lru_w_a, lru_b_a, lru_w_i, lru_b_i, lru_lambda,
                        w_out, b_out, ln1_g, ln1_b, router_w, router_b)
    sinks = attn_sinks[0]

    h1p, xt_p, dest_p, tg_p, cnt_p, pk, pv, ph, pc = _mixer_prompt(x_prompt, sinks, wts)
    h1s, xt_s, dest_s, tg_s, cnt_s, sk, sv, sh, sc = _mixer_sample(
        x_sample.reshape(nseq, D_MODEL),
        cache_win_k[0].reshape(nseq, WINDOW, KV_COLS), cache_win_v[0].reshape(nseq, WINDOW, KV_COLS),
        state_lru_h[0], jnp.transpose(state_conv[0], (1, 0, 2)), sinks, wts)

    per_tile = lambda a: a.reshape(-1, *a.shape[2:])
    dest = jnp.concatenate([per_tile(dest_p), dest_s], axis=0)
    tg = jnp.concatenate([per_tile(tg_p), tg_s], axis=0)
    cnt = jnp.concatenate([per_tile(cnt_p), cnt_s], axis=0)[:, :, 0]
    sched, segs, segs_by_expert, tile_groups, n_rows = _moe_layout(cnt, xt_p.shape[0])

    end_to_end = lambda a: a.reshape(-1, SEG_ALIGN, D_MODEL)
    yb = _experts(sched, segs_by_expert, end_to_end(xt_p), end_to_end(xt_s),
                  w_gate_up[0], b_gate_up[0], w_down[0], b_down[0], n_rows)
    yp, ys = _combine(segs, tile_groups, B, h1p.reshape(B * S, D_MODEL), h1s, dest, tg,
                      ln2_g[0][None], ln2_b[0][None], yb)

    kv_shape = (N_KV_HEADS, HEAD_DIM)
    return (
        yp.reshape(B, S, D_MODEL), ys.reshape(nseq, 1, D_MODEL),
        pk.reshape(1, B, WINDOW, *kv_shape), pv.reshape(1, B, WINDOW, *kv_shape),
        ph.reshape(1, B, LRU_WIDTH), pc[None],
        sk.reshape(1, nseq, WINDOW, *kv_shape), sv.reshape(1, nseq, WINDOW, *kv_shape),
        sh[None], jnp.transpose(sc, (1, 0, 2))[None],
    )
```
